```python
import math
import numpy as np
import jax
import jax.numpy as jnp
from jax import lax

D_MODEL = 2048
BATCH = 4
SEQ = 4096
DEPTH = 2

HEAD_DIM = 64
BLK = 128
ROPE_THETA = 10000.0
PLE_DIM = 256
LN_EPS = 1e-5

NSA_HQ = 8
NSA_HKV = 2
NSA_G = NSA_HQ // NSA_HKV
NSA_CMP_L = 32
NSA_CMP_D = 16
NSA_SEL_L = 64
NSA_TOPN = 16
NSA_WIN = 512
NSA_CMP_HIDDEN = 256
NSA_FORCE = 1e9
SWA_HQ = 8
SWA_HKV = 2
SWA_G = SWA_HQ // SWA_HKV
SWA_WIN = 128
FOX_H = 8
DIFF_H = 8
DIFF_SUB = HEAD_DIM // 2

N_BRANCH = 4
BRANCH_W = 8 * HEAD_DIM

D_FF = 5504
N_EXPERTS = 8
TOP_K = 2
D_FF_EXPERT = 7168
N_DENSE = (DEPTH + 1) // 2
N_MOE = DEPTH // 2

ALPHA = (2.0 * DEPTH) ** 0.25
BETA = (8.0 * DEPTH) ** -0.25

IN_SPLITS = (
    ("a_q", NSA_HQ * HEAD_DIM), ("a_kc", NSA_HKV * HEAD_DIM), ("a_vc", NSA_HKV * HEAD_DIM),
    ("a_ks", NSA_HKV * HEAD_DIM), ("a_vs", NSA_HKV * HEAD_DIM),
    ("a_kw", NSA_HKV * HEAD_DIM), ("a_vw", NSA_HKV * HEAD_DIM), ("a_g", NSA_HQ * 3),
    ("b_q", SWA_HQ * HEAD_DIM), ("b_k", SWA_HKV * HEAD_DIM), ("b_v", SWA_HKV * HEAD_DIM),
    ("c_q", FOX_H * HEAD_DIM), ("c_k", FOX_H * HEAD_DIM), ("c_v", FOX_H * HEAD_DIM), ("c_f", FOX_H),
    ("d_q", DIFF_H * HEAD_DIM), ("d_k", DIFF_H * HEAD_DIM), ("d_v", DIFF_H * HEAD_DIM),
    ("merge_gate", N_BRANCH * D_MODEL),
)
IN_COLS = sum(w for _, w in IN_SPLITS)

kernel_name = "hybrid_nsa_swa_fox_diff_deepnorm_moe"


def split_columns(proj):
    out = {}
    off = 0
    for name, width in IN_SPLITS:
        out[name] = proj[..., off:off + width]
        off += width
    return out


def layer_norm(x, g, b):
    xf = x.astype(jnp.float32)
    mu = jnp.mean(xf, axis=-1, keepdims=True)
    var = jnp.mean(jnp.square(xf - mu), axis=-1, keepdims=True)
    return ((xf - mu) * lax.rsqrt(var + LN_EPS) * g + b).astype(x.dtype)


def rms_norm(x, g):
    xf = x.astype(jnp.float32)
    return (xf * lax.rsqrt(jnp.mean(jnp.square(xf), axis=-1, keepdims=True) + LN_EPS) * g).astype(x.dtype)


def rope_tables(positions, dim):
    inv = 1.0 / (ROPE_THETA ** (jnp.arange(0, dim, 2, dtype=jnp.float32) / dim))
    ang = positions.astype(jnp.float32)[..., None] * inv
    return jnp.cos(ang), jnp.sin(ang)


def apply_rope(x, cos, sin):
    x1, x2 = jnp.split(x, 2, axis=-1)
    c = cos[:, :, None, :].astype(x.dtype)
    s = sin[:, :, None, :].astype(x.dtype)
    return jnp.concatenate([x1 * c - x2 * s, x2 * c + x1 * s], axis=-1)


def banded_attention(q, k, v, window, sinks=None):
    B, S, Hkv, G, D = q.shape
    nb = S // BLK
    npre = window // BLK
    pad = ((0, 0), (npre * BLK, 0), (0, 0), (0, 0))
    kb = jnp.pad(k, pad).reshape(B, nb + npre, BLK, Hkv, D)
    vb = jnp.pad(v, pad).reshape(B, nb + npre, BLK, Hkv, D)
    kband = jnp.concatenate([kb[:, i:i + nb] for i in range(npre + 1)], axis=2)
    vband = jnp.concatenate([vb[:, i:i + nb] for i in range(npre + 1)], axis=2)
    qb = q.reshape(B, nb, BLK, Hkv, G, D)
    s = jnp.einsum('bnqhgd,bnkhd->bnhgqk', qb, kband,
                   preferred_element_type=jnp.float32) * (D ** -0.5)
    nk = (npre + 1) * BLK
    qi = jnp.arange(BLK)[:, None]
    ki = jnp.arange(nk)[None, :]
    diff = npre * BLK + qi - ki
    s_pos = jnp.arange(nb)[:, None, None] * BLK - npre * BLK + ki[None]
    mask = (diff >= 0)[None] & (diff < window)[None] & (s_pos >= 0)
    s = jnp.where(mask[None, :, None, None], s, -jnp.inf)
    if sinks is None:
        p = jax.nn.softmax(s, axis=-1)
    else:
        sk = sinks.astype(jnp.float32)[:, :, None, None]
        m = jnp.maximum(jnp.max(s, axis=-1, keepdims=True), sk)
        e = jnp.exp(s - m)
        p = e / (jnp.sum(e, axis=-1, keepdims=True) + jnp.exp(sk - m))
    o = jnp.einsum('bnhgqk,bnkhd->bnqhgd', p.astype(v.dtype), vband)
    return o.reshape(B, S, Hkv, G, D)


def nsa_compress(x, pe, w1, b1, w2):
    B, S, H, D = x.shape
    n_cmp = (S - NSA_CMP_L) // NSA_CMP_D + 1
    idx = np.arange(n_cmp)[:, None] * NSA_CMP_D + np.arange(NSA_CMP_L)[None, :]
    blocks = x[:, idx] + pe[None, None, :, None, :]
    flat = blocks.transpose(0, 1, 3, 2, 4).reshape(B, n_cmp, H, NSA_CMP_L * D)
    return jax.nn.gelu(flat @ w1 + b1) @ w2


def selection_map(n_cmp, n_sel):
    ci = np.arange(n_cmp)[:, None] * NSA_CMP_D
    sj = np.arange(n_sel)[None, :] * NSA_SEL_L
    ov = np.clip(np.minimum(ci + NSA_CMP_L, sj + NSA_SEL_L) - np.maximum(ci, sj), 0, None)
    return (ov / NSA_CMP_D).astype(np.float32)


def nsa_attention(q, k_c, v_c, k_s, v_s, k_w, v_w, gates):
    B, S, Hkv, G, D = q.shape
    scale = D ** -0.5
    n_cmp = k_c.shape[1]
    n_sel = S // NSA_SEL_L
    topn = min(NSA_TOPN, n_sel)
    t = jnp.arange(S)
    s = jnp.einsum('bthgd,bihd->bhgti', q, k_c, preferred_element_type=jnp.float32) * scale
    cmask = (jnp.arange(n_cmp) * NSA_CMP_D + NSA_CMP_L - 1)[None, :] <= t[:, None]
    p_cmp = jnp.where(cmask, jax.nn.softmax(jnp.where(cmask, s, -1e30), axis=-1), 0.0)
    o_cmp = jnp.einsum('bhgti,bihd->bthgd', p_cmp.astype(v_c.dtype), v_c)
    imp = jnp.einsum('bhgti,ij->bhtj', p_cmp, jnp.asarray(selection_map(n_cmp, n_sel)))
    cur = (t // NSA_SEL_L)[:, None]
    j = jnp.arange(n_sel)[None, :]
    forced = (j == 0) | (j == cur) | (j == cur - 1)
    imp = jnp.where(forced, NSA_FORCE, jnp.where(j > cur, -NSA_FORCE, imp))
    _, sel_idx = lax.top_k(imp, topn)
    kblk = k_s.reshape(B, n_sel, NSA_SEL_L, Hkv, D).transpose(0, 3, 1, 2, 4)
    vblk = v_s.reshape(B, n_sel, NSA_SEL_L, Hkv, D).transpose(0, 3, 1, 2, 4)
    nb = S // BLK
    qb = q.reshape(B, nb, BLK, Hkv, G, D).swapaxes(0, 1)
    ib = sel_idx.reshape(B, Hkv, nb, BLK, topn).transpose(2, 0, 1, 3, 4)
    gather = jax.vmap(jax.vmap(lambda blocks, ix: blocks[ix]))

    def sel_block(args):
        n, qi, ix = args
        kg = gather(kblk, ix)
        vg = gather(vblk, ix)
        sc = jnp.einsum('bqhgd,bhqnld->bhgqnl', qi, kg, preferred_element_type=jnp.float32) * scale
        tq = n * BLK + jnp.arange(BLK)
        kpos = ix[..., None] * NSA_SEL_L + jnp.arange(NSA_SEL_L)
        mask = kpos <= tq[None, None, :, None, None]
        sc = jnp.where(mask[:, :, None], sc, -jnp.inf)
        p = jax.nn.softmax(sc.reshape(B, Hkv, G, BLK, topn * NSA_SEL_L), axis=-1)
        p = p.reshape(B, Hkv, G, BLK, topn, NSA_SEL_L)
        return jnp.einsum('bhgqnl,bhqnld->bqhgd', p.astype(vg.dtype), vg)

    o_slc = lax.map(sel_block, (jnp.arange(nb), qb, ib)).swapaxes(0, 1).reshape(B, S, Hkv, G, D)
    o_win = banded_attention(q, k_w, v_w, NSA_WIN)
    o = gates[..., 0:1] * o_cmp + gates[..., 1:2] * o_slc + gates[..., 2:3] * o_win
    return o.reshape(B, S, Hkv * G * D)


def fox_attention(q, k, v, f_logit):
    B, S, H, D = q.shape
    nb = S // BLK
    cum = jnp.cumsum(jax.nn.log_sigmoid(f_logit.astype(jnp.float32)), axis=1)
    cum_t = cum.transpose(0, 2, 1)
    kpos = jnp.arange(S)
    qb = q.reshape(B, nb, BLK, H, D).swapaxes(0, 1)
    cb = cum_t.reshape(B, H, nb, BLK).transpose(2, 0, 1, 3)

    def block(args):
        n, qi, ci = args
        s = jnp.einsum('bqhd,bkhd->bhqk', qi, k, preferred_element_type=jnp.float32) * (D ** -0.5)
        s = s + (ci[..., :, None] - cum_t[..., None, :])
        mask = kpos[None, :] <= (n * BLK + jnp.arange(BLK))[:, None]
        p = jax.nn.softmax(jnp.where(mask, s, -jnp.inf), axis=-1)
        return jnp.einsum('bhqk,bkhd->bqhd', p.astype(v.dtype), v)

    o = lax.map(block, (jnp.arange(nb), qb, cb))
    return o.swapaxes(0, 1).reshape(B, S, H * D)


def diff_attention(q, k, v, lam, lam_init, gain):
    B, S, H, _, Ds = q.shape
    Dv = v.shape[-1]
    nb = S // BLK
    kpos = jnp.arange(S)
    qb = q.reshape(B, nb, BLK, H, 2, Ds).swapaxes(0, 1)

    def block(args):
        n, qi = args
        s = jnp.einsum('bqhcd,bkhcd->bchqk', qi, k, preferred_element_type=jnp.float32) * (Ds ** -0.5)
        mask = kpos[None, :] <= (n * BLK + jnp.arange(BLK))[:, None]
        p = jax.nn.softmax(jnp.where(mask, s, -jnp.inf), axis=-1)
        a = p[:, 0] - lam * p[:, 1]
        return jnp.einsum('bhqk,bkhd->bqhd', a.astype(v.dtype), v)

    o = lax.map(block, (jnp.arange(nb), qb)).swapaxes(0, 1).reshape(B, S, H, Dv)
    o = rms_norm(o, gain) * (1.0 - lam_init)
    return o.reshape(B, S, H * Dv)


def token_mixer(h, cos, sin, cos_d, sin_d, w_in, cmp_pe, cmp_w1, cmp_b1, cmp_w2,
                sinks, fox_bf, diff_lambda, diff_gain, lam_init, w_branch, w_out):
    B, S, _ = h.shape
    c = split_columns(h @ w_in)

    def heads(t, n):
        return t.reshape(B, S, n, HEAD_DIM)

    qa = apply_rope(heads(c['a_q'], NSA_HQ), cos, sin).reshape(B, S, NSA_HKV, NSA_G, HEAD_DIM)
    kc = nsa_compress(apply_rope(heads(c['a_kc'], NSA_HKV), cos, sin), cmp_pe[0], cmp_w1[0], cmp_b1[0], cmp_w2[0])
    vc = nsa_compress(heads(c['a_vc'], NSA_HKV), cmp_pe[1], cmp_w1[1], cmp_b1[1], cmp_w2[1])
    ks = apply_rope(heads(c['a_ks'], NSA_HKV), cos, sin)
    kw = apply_rope(heads(c['a_kw'], NSA_HKV), cos, sin)
    ga = jax.nn.sigmoid(c['a_g'].reshape(B, S, NSA_HKV, NSA_G, 3))
    o_a = nsa_attention(qa, kc, vc, ks, heads(c['a_vs'], NSA_HKV), kw, heads(c['a_vw'], NSA_HKV), ga)
    qb = apply_rope(heads(c['b_q'], SWA_HQ), cos, sin).reshape(B, S, SWA_HKV, SWA_G, HEAD_DIM)
    kb = apply_rope(heads(c['b_k'], SWA_HKV), cos, sin)
    o_b = banded_attention(qb, kb, heads(c['b_v'], SWA_HKV), SWA_WIN,
                           sinks.reshape(SWA_HKV, SWA_G)).reshape(B, S, SWA_HQ * HEAD_DIM)
    o_c = fox_attention(heads(c['c_q'], FOX_H), heads(c['c_k'], FOX_H), heads(c['c_v'], FOX_H),
                        c['c_f'] + fox_bf)
    qd = apply_rope(c['d_q'].reshape(B, S, DIFF_H * 2, DIFF_SUB), cos_d, sin_d).reshape(B, S, DIFF_H, 2, DIFF_SUB)
    kd = apply_rope(c['d_k'].reshape(B, S, DIFF_H * 2, DIFF_SUB), cos_d, sin_d).reshape(B, S, DIFF_H, 2, DIFF_SUB)
    lf = diff_lambda.astype(jnp.float32)
    lam = jnp.exp(jnp.sum(lf[0] * lf[1])) - jnp.exp(jnp.sum(lf[2] * lf[3])) + lam_init
    o_d = diff_attention(qd, kd, heads(c['d_v'], DIFF_H), lam, lam_init, diff_gain)
    branches = jnp.stack([o_a, o_b, o_c, o_d], axis=2)
    up = jnp.einsum('bsnc,ncd->bsnd', branches, w_branch)
    gates = jax.nn.sigmoid(c['merge_gate'].reshape(B, S, N_BRANCH, D_MODEL))
    merged = jnp.sum(gates * up, axis=2)
    return merged @ w_out


def swiglu(h, wg, wu, wd):
    return (jax.nn.silu(h @ wg) * (h @ wu)) @ wd


def moe_ffn(h, w_router, b_router, wg, wu, wd):
    logits = (h @ w_router).astype(jnp.float32) + b_router.astype(jnp.float32)
    top_v, top_i = lax.top_k(logits, TOP_K)
    top_w = jax.nn.softmax(top_v, axis=-1)
    combine = jnp.sum(jax.nn.one_hot(top_i, N_EXPERTS, dtype=jnp.float32) * top_w[..., None], axis=-2)
    y = jnp.zeros_like(h)
    for e in range(N_EXPERTS):
        y = y + combine[..., e:e + 1].astype(h.dtype) * swiglu(h, wg[e], wu[e], wd[e])
    return y


def _nrm(k, shape, scale):
    return jax.random.normal(k, shape, jnp.float32) * scale


def setup_inputs(seed: int = 0) -> dict:
    key = jax.random.key(seed)
    ks = jax.random.split(key, 32)
    D = D_MODEL
    return {
        "x": _nrm(ks[0], (BATCH, SEQ, D), 1.0),
        "p": _nrm(ks[1], (DEPTH, BATCH, SEQ, PLE_DIM), 1.0),
        "positions": jax.random.randint(ks[2], (BATCH, 1), 0, 1024, dtype=jnp.int32)
                     + jnp.arange(SEQ, dtype=jnp.int32)[None, :],
        "w_in": _nrm(ks[3], (DEPTH, D, IN_COLS), D ** -0.5),
        "nsa_cmp_pe": _nrm(ks[4], (DEPTH, 2, NSA_CMP_L, HEAD_DIM), 0.02),
        "nsa_cmp_w1": _nrm(ks[5], (DEPTH, 2, NSA_CMP_L * HEAD_DIM, NSA_CMP_HIDDEN), (NSA_CMP_L * HEAD_DIM) ** -0.5),
        "nsa_cmp_b1": _nrm(ks[6], (DEPTH, 2, NSA_CMP_HIDDEN), 0.01),
        "nsa_cmp_w2": _nrm(ks[7], (DEPTH, 2, NSA_CMP_HIDDEN, HEAD_DIM), NSA_CMP_HIDDEN ** -0.5),
        "swa_sinks": _nrm(ks[8], (DEPTH, SWA_HQ), 0.5),
        "fox_bf": 3.0 + _nrm(ks[9], (DEPTH, FOX_H), 0.1),
        "diff_lambda": _nrm(ks[10], (DEPTH, 4, DIFF_SUB), 0.1),
        "diff_gain": 1.0 + _nrm(ks[11], (DEPTH, HEAD_DIM), 0.01),
        "w_branch": _nrm(ks[12], (DEPTH, N_BRANCH, BRANCH_W, D), BETA * BRANCH_W ** -0.5),
        "w_out": _nrm(ks[13], (DEPTH, D, D), BETA * D ** -0.5),
        "ln1_g": 1.0 + _nrm(ks[14], (DEPTH, D), 0.01),
        "ln1_b": _nrm(ks[15], (DEPTH, D), 0.01),
        "ffn_wg": _nrm(ks[16], (N_DENSE, D, D_FF), D ** -0.5),
        "ffn_wu": _nrm(ks[17], (N_DENSE, D, D_FF), D ** -0.5),
        "ffn_wd": _nrm(ks[18], (N_DENSE, D_FF, D), BETA * D_FF ** -0.5),
        "moe_router": _nrm(ks[19], (N_MOE, D, N_EXPERTS), D ** -0.5),
        "moe_router_b": _nrm(ks[20], (N_MOE, N_EXPERTS), 0.01),
        "moe_wg": _nrm(ks[21], (N_MOE, N_EXPERTS, D, D_FF_EXPERT), D ** -0.5),
        "moe_wu": _nrm(ks[22], (N_MOE, N_EXPERTS, D, D_FF_EXPERT), D ** -0.5),
        "moe_wd": _nrm(ks[23], (N_MOE, N_EXPERTS, D_FF_EXPERT, D), BETA * D_FF_EXPERT ** -0.5),
        "ple_proj": _nrm(ks[24], (DEPTH, PLE_DIM, D), BETA * PLE_DIM ** -0.5),
        "ple_gate": _nrm(ks[25], (DEPTH, D, D), D ** -0.5),
        "ln2_g": 1.0 + _nrm(ks[26], (DEPTH, D), 0.01),
        "ln2_b": _nrm(ks[27], (DEPTH, D), 0.01),
    }


def reference(x, p, positions, w_in, nsa_cmp_pe, nsa_cmp_w1, nsa_cmp_b1, nsa_cmp_w2,
              swa_sinks, fox_bf, diff_lambda, diff_gain, w_branch, w_out, ln1_g, ln1_b,
              ffn_wg, ffn_wu, ffn_wd, moe_router, moe_router_b, moe_wg, moe_wu, moe_wd,
              ple_proj, ple_gate, ln2_g, ln2_b):
    cos, sin = rope_tables(positions, HEAD_DIM)
    cos_d, sin_d = rope_tables(positions, DIFF_SUB)
    h = x
    for i in range(DEPTH):
        lam_init = 0.8 - 0.6 * math.exp(-0.3 * i)
        mix = token_mixer(h, cos, sin, cos_d, sin_d, w_in[i], nsa_cmp_pe[i], nsa_cmp_w1[i],
                          nsa_cmp_b1[i], nsa_cmp_w2[i], swa_sinks[i], fox_bf[i], diff_lambda[i],
                          diff_gain[i], lam_init, w_branch[i], w_out[i])
        h = layer_norm(ALPHA * h + mix, ln1_g[i], ln1_b[i])
        if i % 2 == 0:
            f = swiglu(h, ffn_wg[i // 2], ffn_wu[i // 2], ffn_wd[i // 2])
        else:
            f = moe_ffn(h, moe_router[i // 2], moe_router_b[i // 2], moe_wg[i // 2], moe_wu[i // 2], moe_wd[i // 2])
        ple = jax.nn.sigmoid(h @ ple_gate[i]) * (p[i] @ ple_proj[i])
        h = layer_norm(ALPHA * h + f + ple, ln2_g[i], ln2_b[i])
    return h
```

```python
import functools
import math

import numpy as np
import jax
import jax.numpy as jnp
from jax import lax
from jax.experimental import pallas as pl
from jax.experimental.pallas import tpu as pltpu

F32 = jnp.float32
BF16 = jnp.bfloat16

D_MODEL = 2048
DEPTH = 2
HEAD_DIM = 64
ROPE_THETA = 10000.0
PLE_DIM = 256
LN_EPS = 1e-5
NSA_CMP_L = 32
NSA_CMP_D = 16
NSA_SEL_L = 64
NSA_TOPN = 16
NSA_WIN = 512
NSA_CMP_HIDDEN = 256
NSA_FORCE = 1e9
SWA_WIN = 128
DIFF_SUB = HEAD_DIM // 2
N_BRANCH = 4
BRANCH_W = 8 * HEAD_DIM
D_FF = 5504
N_EXPERTS = 8
TOP_K = 2
D_FF_EXPERT = 7168
ALPHA = (2.0 * DEPTH) ** 0.25

IN_SPLITS = (
    ("a_q", 512), ("a_kc", 128), ("a_vc", 128), ("a_ks", 128), ("a_vs", 128),
    ("a_kw", 128), ("a_vw", 128), ("a_g", 24),
    ("b_q", 512), ("b_k", 128), ("b_v", 128),
    ("c_q", 512), ("c_k", 512), ("c_v", 512), ("c_f", 8),
    ("d_q", 512), ("d_k", 512), ("d_v", 512),
    ("merge_gate", N_BRANCH * D_MODEL),
)
SEG_ROPE64 = ("a_q", "a_kc", "a_ks", "a_kw", "b_q", "b_k")
SEG_ROPE32 = ("d_q", "d_k")
SEG_PLAIN = ("a_vc", "a_vs", "a_vw", "b_v", "c_q", "c_k", "c_v", "d_v", "merge_gate")
SEG_SMALL = ("a_g", "c_f")

LANES = 128
NEG = -1e30
VMEM_LIMIT = 56 * 1024 * 1024


def _cparams(sem):
    return pltpu.CompilerParams(dimension_semantics=sem, vmem_limit_bytes=VMEM_LIMIT)


def _sigmoid(x):
    return 1.0 / (1.0 + jnp.exp(-x))


def _dot(a, b):
    return jnp.dot(a, b, preferred_element_type=F32)


def _dot_nt(a, b):
    return lax.dot_general(a, b, (((1,), (1,)), ((), ())), preferred_element_type=F32)


def _split2(x):
    hi = x.astype(BF16)
    lo = (x - hi.astype(F32)).astype(BF16)
    return hi, lo


def _split3(x):
    hi = x.astype(BF16)
    r = x - hi.astype(F32)
    mid = r.astype(BF16)
    lo = (r - mid.astype(F32)).astype(BF16)
    return hi, mid, lo


def _proj_body(x_ref, w_ref, *rest, rope_half):
    if rope_half:
        cos_ref, sin_ref, o_ref = rest
    else:
        (o_ref,) = rest
    acc = _dot(x_ref[...], w_ref[...])
    if not rope_half:
        o_ref[...] = acc.astype(o_ref.dtype)
        return
    cos = cos_ref[...]
    sin = sin_ref[...]
    lane = lax.broadcasted_iota(jnp.int32, cos.shape, 1)
    first = (lane % (2 * rope_half)) < rope_half
    for c in range(acc.shape[1] // LANES):
        a = acc[:, c * LANES:(c + 1) * LANES]
        rot = jnp.where(first, pltpu.roll(a, LANES - rope_half, 1), pltpu.roll(a, rope_half, 1))
        o_ref[:, c * LANES:(c + 1) * LANES] = (a * cos + rot * sin).astype(o_ref.dtype)


def _proj(x, w, out_dtype, tm, tn, rope=None):
    m, k = x.shape
    n = w.shape[1]
    in_specs = [pl.BlockSpec((tm, k), lambda i, j: (i, 0)),
                pl.BlockSpec((k, tn), lambda i, j: (0, j))]
    args = [x, w]
    rope_half = 0
    if rope is not None:
        cos_tab, sin_tab, rope_half = rope
        in_specs += [pl.BlockSpec((tm, LANES), lambda i, j: (i, 0)),
                     pl.BlockSpec((tm, LANES), lambda i, j: (i, 0))]
        args += [cos_tab, sin_tab]
    return pl.pallas_call(
        functools.partial(_proj_body, rope_half=rope_half),
        grid=(m // tm, n // tn),
        in_specs=in_specs,
        out_specs=pl.BlockSpec((tm, tn), lambda i, j: (i, j)),
        out_shape=jax.ShapeDtypeStruct((m, n), out_dtype),
        compiler_params=_cparams(("parallel", "parallel")),
        name="proj_rope" if rope_half else "proj",
    )(*args)


def _rope_tabs(positions, dim):
    inv = 1.0 / (ROPE_THETA ** (jnp.arange(0, dim, 2, dtype=F32) / dim))
    ang = positions.astype(F32).reshape(-1)[:, None] * inv
    c, s = jnp.cos(ang), jnp.sin(ang)
    reps = LANES // dim
    return (jnp.tile(jnp.concatenate([c, c], -1), (1, reps)),
            jnp.tile(jnp.concatenate([-s, s], -1), (1, reps)))


def _online_step(s, v, m_ref, l_ref, acc_ref, idx):
    m_prev = m_ref[idx]
    m_new = jnp.maximum(m_prev, jnp.max(s, axis=-1, keepdims=True))
    alpha = jnp.exp(m_prev - m_new)
    p = jnp.exp(s - m_new)
    l_ref[idx] = alpha * l_ref[idx] + jnp.sum(p, axis=-1, keepdims=True)
    acc_ref[idx] = alpha * acc_ref[idx] + _dot(p.astype(BF16), v)
    m_ref[idx] = m_new


def _init_state(m_ref, l_ref, acc_ref):
    m_ref[...] = jnp.full(m_ref.shape, NEG, F32)
    l_ref[...] = jnp.zeros(l_ref.shape, F32)
    acc_ref[...] = jnp.zeros(acc_ref.shape, F32)


def _causal_pair_body(*refs, tq, nmaps, scale, kind):
    if kind == "fox":
        q_ref, k_ref, v_ref, ck_ref, o_ref, m_ref, l_ref, acc_ref = refs
    else:
        lam_ref, q_ref, k_ref, v_ref, gain_ref, o_ref, m_ref, l_ref, acc_ref = refs
    qi = pl.program_id(2)
    ki = pl.program_id(3)
    width = LANES // nmaps

    @pl.when(ki == 0)
    def _():
        _init_state(m_ref, l_ref, acc_ref)

    @pl.when(ki <= qi)
    def _():
        q = q_ref[0]
        k = k_ref[0]
        v = v_ref[0]
        lane = lax.broadcasted_iota(jnp.int32, q.shape, 1)
        row = qi * tq + lax.broadcasted_iota(jnp.int32, (tq, tq), 0)
        col = ki * tq + lax.broadcasted_iota(jnp.int32, (tq, tq), 1)
        causal = col <= row
        for mp in range(nmaps):
            qm = jnp.where((lane >= mp * width) & (lane < (mp + 1) * width), q, jnp.zeros_like(q))
            s = _dot_nt(qm, k) * scale
            if kind == "fox":
                s = s - ck_ref[0, mp]
            s = jnp.where(causal, s, NEG)
            _online_step(s, v, m_ref, l_ref, acc_ref, mp)

    @pl.when(ki == pl.num_programs(3) - 1)
    def _():
        lane = lax.broadcasted_iota(jnp.int32, (tq, LANES), 1)
        lo = lane < HEAD_DIM
        if kind == "fox":
            o = jnp.where(lo, acc_ref[0] / l_ref[0], acc_ref[1] / l_ref[1])
            o_ref[0] = o.astype(o_ref.dtype)
        else:
            lam = lam_ref[0]
            o0 = acc_ref[0] / l_ref[0] - lam * (acc_ref[1] / l_ref[1])
            o1 = acc_ref[2] / l_ref[2] - lam * (acc_ref[3] / l_ref[3])
            o = jnp.where(lo, o0, o1)
            sq = o * o
            ss_lo = jnp.sum(jnp.where(lo, sq, 0.0), axis=-1, keepdims=True)
            ss_hi = jnp.sum(jnp.where(lo, 0.0, sq), axis=-1, keepdims=True)
            ms = jnp.where(lo, ss_lo, ss_hi) * (1.0 / HEAD_DIM)
            o = o * lax.rsqrt(ms + LN_EPS) * gain_ref[...] * lam_ref[1]
            o_ref[0] = o.astype(o_ref.dtype)


def _causal_pair_attention(q_arr, q_off, k_arr, k_off, v_arr, v_off, *, tq, kind, scale,
                           ck=None, lam=None, gain=None):
    b, s, _ = q_arr.shape
    nq = s // tq
    npairs = 4
    nmaps = 2 if kind == "fox" else 4
    qspec = pl.BlockSpec((1, tq, LANES), lambda bi, j, qi, ki, *_: (bi, qi, q_off + j))
    kspec = pl.BlockSpec((1, tq, LANES), lambda bi, j, qi, ki, *_: (bi, jnp.minimum(ki, qi), k_off + j))
    vspec = pl.BlockSpec((1, tq, LANES), lambda bi, j, qi, ki, *_: (bi, jnp.minimum(ki, qi), v_off + j))
    ospec = pl.BlockSpec((1, tq, LANES), lambda bi, j, qi, ki, *_: (bi, qi, j))
    scratch = [pltpu.VMEM((nmaps, tq, 1), F32), pltpu.VMEM((nmaps, tq, 1), F32),
               pltpu.VMEM((nmaps, tq, LANES), F32)]
    body = functools.partial(_causal_pair_body, tq=tq, nmaps=nmaps, scale=scale, kind=kind)
    out_shape = jax.ShapeDtypeStruct((b, s, npairs * LANES), BF16)
    sem = ("parallel", "parallel", "parallel", "arbitrary")
    if kind == "fox":
        ckspec = pl.BlockSpec((1, 2, 1, tq), lambda bi, j, qi, ki: (bi, j, 0, jnp.minimum(ki, qi)))
        return pl.pallas_call(
            body, grid=(b, npairs, nq, nq),
            in_specs=[qspec, kspec, vspec, ckspec], out_specs=ospec,
            out_shape=out_shape, scratch_shapes=scratch,
            compiler_params=_cparams(sem), name="fox_attention",
        )(q_arr, k_arr, v_arr, ck)
    gspec = pl.BlockSpec((1, LANES), lambda bi, j, qi, ki, *_: (0, 0))
    return pl.pallas_call(
        body,
        grid_spec=pltpu.PrefetchScalarGridSpec(
            num_scalar_prefetch=1, grid=(b, npairs, nq, nq),
            in_specs=[qspec, kspec, vspec, gspec], out_specs=ospec,
            scratch_shapes=scratch),
        out_shape=out_shape,
        compiler_params=_cparams(sem), name="diff_attention",
    )(lam, q_arr, k_arr, v_arr, gain)


def _band_body(*refs, tq, window, nsteps, scale, use_sink):
    if use_sink:
        sink_ref, q_ref, k_ref, v_ref, o_ref, m_ref, l_ref, acc_ref = refs
    else:
        q_ref, k_ref, v_ref, o_ref, m_ref, l_ref, acc_ref = refs
    j = pl.program_id(1)
    qi = pl.program_id(2)
    st = pl.program_id(3)
    kt = qi - (nsteps - 1) + st
    lo_lane = (j // 2) * HEAD_DIM
    lane = lax.broadcasted_iota(jnp.int32, (tq, LANES), 1)
    in_group = (lane >= lo_lane) & (lane < lo_lane + HEAD_DIM)

    @pl.when(st == 0)
    def _():
        _init_state(m_ref, l_ref, acc_ref)

    @pl.when(kt >= 0)
    def _():
        q = q_ref[0].astype(F32)
        q_rolled = pltpu.roll(q, HEAD_DIM, 1)
        k = k_ref[0]
        v = v_ref[0]
        row = qi * tq + lax.broadcasted_iota(jnp.int32, (tq, tq), 0)
        col = kt * tq + lax.broadcasted_iota(jnp.int32, (tq, tq), 1)
        keep = (col <= row) & (row - col < window)
        for e in range(2):
            q_e = jnp.where(lo_lane == e * HEAD_DIM, q, q_rolled)
            qm = jnp.where(in_group, q_e, 0.0).astype(BF16)
            s = jnp.where(keep, _dot_nt(qm, k) * scale, NEG)
            _online_step(s, v, m_ref, l_ref, acc_ref, e)

    @pl.when(st == nsteps - 1)
    def _():
        outs = []
        for e in range(2):
            m = m_ref[e]
            l = l_ref[e]
            acc = acc_ref[e]
            if use_sink:
                sk = sink_ref[2 * j + e]
                m_f = jnp.maximum(m, sk)
                corr = jnp.exp(m - m_f)
                o = acc * corr / (l * corr + jnp.exp(sk - m_f))
            else:
                o = acc / l
            outs.append(jnp.where(lo_lane == e * HEAD_DIM, o, pltpu.roll(o, HEAD_DIM, 1)))
        o_ref[0] = jnp.where(lane < HEAD_DIM, outs[0], outs[1]).astype(o_ref.dtype)


def _band_attention(q_arr, q_off, k_arr, k_off, v_arr, v_off, *, tq, window, scale, sinks=None):
    b, s, _ = q_arr.shape
    nq = s // tq
    nsteps = -(-window // tq) + 1

    def kv_tile(qi, st):
        return jnp.maximum(qi - (nsteps - 1) + st, 0)

    qspec = pl.BlockSpec((1, tq, LANES), lambda bi, j, qi, st, *_: (bi, qi, q_off + j))
    kspec = pl.BlockSpec((1, tq, LANES), lambda bi, j, qi, st, *_: (bi, kv_tile(qi, st), k_off))
    vspec = pl.BlockSpec((1, tq, LANES), lambda bi, j, qi, st, *_: (bi, kv_tile(qi, st), v_off))
    ospec = pl.BlockSpec((1, tq, LANES), lambda bi, j, qi, st, *_: (bi, qi, j))
    scratch = [pltpu.VMEM((2, tq, 1), F32), pltpu.VMEM((2, tq, 1), F32),
               pltpu.VMEM((2, tq, LANES), F32)]
    body = functools.partial(_band_body, tq=tq, window=window, nsteps=nsteps, scale=scale,
                             use_sink=sinks is not None)
    nprefetch = 0 if sinks is None else 1
    args = ([] if sinks is None else [sinks]) + [q_arr, k_arr, v_arr]
    return pl.pallas_call(
        body,
        grid_spec=pltpu.PrefetchScalarGridSpec(
            num_scalar_prefetch=nprefetch, grid=(b, 4, nq, nsteps),
            in_specs=[qspec, kspec, vspec], out_specs=ospec, scratch_shapes=scratch),
        out_shape=jax.ShapeDtypeStruct((b, s, 4 * LANES), BF16),
        compiler_params=_cparams(("parallel", "parallel", "parallel", "arbitrary")),
        name="swa_attention" if sinks is not None else "nsa_window_attention",
    )(*args)


def _gelu_tanh(x):
    return 0.5 * x * (1.0 + jnp.tanh(math.sqrt(2.0 / math.pi) * (x + 0.044715 * (x * x * x))))


def _compress_body(x_ref, pe_ref, w1_ref, b1_ref, w2_ref, o_ref):
    x = (x_ref[0, 0].astype(F32) + pe_ref[0]).astype(BF16)
    hid = _gelu_tanh(_dot(x, w1_ref[0]) + b1_ref[0])
    o_ref[0, 0] = _dot(hid.astype(BF16), w2_ref[0])


def _nsa_compress(x, pe, w1, b1, w2):
    _, nb, ncp, ld = x.shape
    hid = w1.shape[-1]
    return pl.pallas_call(
        _compress_body,
        grid=(2, nb),
        in_specs=[pl.BlockSpec((1, 1, ncp, ld), lambda t, i: (t, i, 0, 0)),
                  pl.BlockSpec((1, 1, ld), lambda t, i: (t, 0, 0)),
                  pl.BlockSpec((1, ld, hid), lambda t, i: (t, 0, 0)),
                  pl.BlockSpec((1, 1, hid), lambda t, i: (t, 0, 0)),
                  pl.BlockSpec((1, hid, HEAD_DIM), lambda t, i: (t, 0, 0))],
        out_specs=pl.BlockSpec((1, 1, ncp, HEAD_DIM), lambda t, i: (t, i, 0, 0)),
        out_shape=jax.ShapeDtypeStruct((2, nb, ncp, HEAD_DIM), F32),
        compiler_params=_cparams(("parallel", "parallel")),
        name="nsa_compress",
    )(x, pe, w1, b1, w2)


def _cmp_topk_body(q_ref, kc_ref, vct_ref, selt_ref, o_ref, *, tq, ncp, nsel, topn, scale):
    qi = pl.program_id(2)
    q = q_ref[0]
    kc4 = kc_ref[0, 0]
    vct = vct_ref[0, 0]
    lane = lax.broadcasted_iota(jnp.int32, kc4.shape, 1)
    ci = lax.broadcasted_iota(jnp.int32, (ncp, tq), 0)
    tpos = qi * tq + lax.broadcasted_iota(jnp.int32, (ncp, tq), 1)
    cmask = ci * NSA_CMP_D + (NSA_CMP_L - 1) <= tpos
    psum = jnp.zeros((ncp, tq), F32)
    rows = []
    for a in range(4):
        kcm = jnp.where((lane >= a * HEAD_DIM) & (lane < (a + 1) * HEAD_DIM), kc4, jnp.zeros_like(kc4))
        st = jnp.where(cmask, _dot_nt(kcm, q) * scale, NEG)
        m = jnp.max(st, axis=0, keepdims=True)
        e = jnp.where(cmask, jnp.exp(st - m), 0.0)
        l = jnp.sum(e, axis=0, keepdims=True)
        p = e * jnp.where(l > 0.0, 1.0 / l, 0.0)
        psum = psum + p
        rows.append(_dot(vct, p.astype(BF16)))
    hi, lo = _split2(psum)
    selt = selt_ref[...]
    imp = _dot(selt, hi) + _dot(selt, lo)
    blk = lax.broadcasted_iota(jnp.int32, (nsel, tq), 0)
    cur = (qi * tq + lax.broadcasted_iota(jnp.int32, (nsel, tq), 1)) // NSA_SEL_L
    forced = (blk == 0) | (blk == cur) | (blk == cur - 1)
    imp = jnp.where(forced, NSA_FORCE, jnp.where(blk > cur, -NSA_FORCE, imp))
    cnt = jnp.zeros((nsel, tq), jnp.int32)
    for jp in range(nsel):
        v = imp[jp:jp + 1, :]
        tie = jnp.where(blk > jp, 1, 0)
        cnt = cnt + jnp.where(v > imp, 1, jnp.where(v == imp, tie, 0))
    mneg = jnp.where(cnt < topn, 0.0, NEG)
    pieces = rows + [mneg]
    if nsel < LANES:
        pieces.append(jnp.zeros((LANES - nsel, tq), F32))
    o_ref[0, 0] = jnp.concatenate(pieces, axis=0).T


def _nsa_cmp_topk(q_arr, q_off256, kc4, vct, selt, *, tq, topn, scale):
    b, s, _ = q_arr.shape
    ncp = kc4.shape[2]
    nsel = selt.shape[0]
    body = functools.partial(_cmp_topk_body, tq=tq, ncp=ncp, nsel=nsel, topn=topn, scale=scale)
    return pl.pallas_call(
        body,
        grid=(b, 2, s // tq),
        in_specs=[pl.BlockSpec((1, tq, 2 * LANES), lambda bi, g, qi: (bi, qi, q_off256 + g)),
                  pl.BlockSpec((1, 1, ncp, 2 * LANES), lambda bi, g, qi: (bi, g, 0, 0)),
                  pl.BlockSpec((1, 1, HEAD_DIM, ncp), lambda bi, g, qi: (bi, g, 0, 0)),
                  pl.BlockSpec((nsel, ncp), lambda bi, g, qi: (0, 0))],
        out_specs=pl.BlockSpec((1, 1, tq, 3 * LANES), lambda bi, g, qi: (bi, g, qi, 0)),
        out_shape=jax.ShapeDtypeStruct((b, 2, s, 3 * LANES), F32),
        compiler_params=_cparams(("parallel", "parallel", "parallel")),
        name="nsa_cmp_topk",
    )(q_arr, kc4, vct, selt)


def _nsa_sel_body(q_ref, k_ref, v_ref, ocmp_ref, owin_ref, sm_ref, e_ref, o_ref,
                  m_ref, l_ref, acc_ref, *, tq, scale):
    g = pl.program_id(1)
    qi = pl.program_id(2)
    ki = pl.program_id(3)

    @pl.when(ki == 0)
    def _():
        _init_state(m_ref, l_ref, acc_ref)

    @pl.when(ki <= qi)
    def _():
        qb = q_ref[0]
        qs = jnp.concatenate([qb[:, a * LANES:(a + 1) * LANES] for a in range(4)], axis=0)
        s = _dot_nt(qs, k_ref[0]) * scale
        row = qi * tq + (lax.broadcasted_iota(jnp.int32, (4 * tq, tq), 0) & (tq - 1))
        col = ki * tq + lax.broadcasted_iota(jnp.int32, (4 * tq, tq), 1)
        s = jnp.where(col <= row, s, NEG)
        _online_step(s, v_ref[0], m_ref, l_ref, acc_ref, 0)

    @pl.when(ki == pl.num_programs(3) - 1)
    def _():
        lane = lax.broadcasted_iota(jnp.int32, (tq, LANES), 1)
        o_all = acc_ref[0] / l_ref[0]
        heads = []
        for a in range(4):
            o = o_all[a * tq:(a + 1) * tq]
            heads.append(jnp.where(g == 0, o, pltpu.roll(o, HEAD_DIM, 1)))
        o_slc = jnp.concatenate(
            [jnp.where(lane < HEAD_DIM, heads[2 * p], pltpu.roll(heads[2 * p + 1], HEAD_DIM, 1))
             for p in range(2)], axis=1)
        hi, lo = _split2(sm_ref[0])
        gates = [_sigmoid(_dot(hi, e_ref[0, c]) + _dot(lo, e_ref[0, c])) for c in range(3)]
        o = (gates[0] * ocmp_ref[0, 0][:, :2 * LANES] + gates[1] * o_slc
             + gates[2] * owin_ref[0].astype(F32))
        o_ref[0] = o.astype(o_ref.dtype)


def _nsa_sel_attention(q_aug, k_aug, v_arr, v_off, cmp_out, o_win, small, e_mat, *, tq, scale):
    b, s, _ = q_aug.shape
    nq = s // tq
    kv = lambda bi, g, qi, ki: (bi, jnp.minimum(ki, qi), g)
    return pl.pallas_call(
        functools.partial(_nsa_sel_body, tq=tq, scale=scale),
        grid=(b, 2, nq, nq),
        in_specs=[pl.BlockSpec((1, tq, 4 * LANES), lambda bi, g, qi, ki: (bi, qi, g)),
                  pl.BlockSpec((1, tq, LANES), kv),
                  pl.BlockSpec((1, tq, LANES), lambda bi, g, qi, ki: (bi, jnp.minimum(ki, qi), v_off)),
                  pl.BlockSpec((1, 1, tq, 3 * LANES), lambda bi, g, qi, ki: (bi, g, qi, 0)),
                  pl.BlockSpec((1, tq, 2 * LANES), lambda bi, g, qi, ki: (bi, qi, g)),
                  pl.BlockSpec((1, tq, LANES), lambda bi, g, qi, ki: (bi, qi, 0)),
                  pl.BlockSpec((1, 3, LANES, 2 * LANES), lambda bi, g, qi, ki: (g, 0, 0, 0))],
        out_specs=pl.BlockSpec((1, tq, 2 * LANES), lambda bi, g, qi, ki: (bi, qi, g)),
        out_shape=jax.ShapeDtypeStruct((b, s, 4 * LANES), BF16),
        scratch_shapes=[pltpu.VMEM((1, 4 * tq, 1), F32), pltpu.VMEM((1, 4 * tq, 1), F32),
                        pltpu.VMEM((1, 4 * tq, LANES), F32)],
        compiler_params=_cparams(("parallel", "parallel", "parallel", "arbitrary")),
        name="nsa_selected_attention",
    )(q_aug, k_aug, v_arr, cmp_out, o_win, small, e_mat)


def _cumgate_body(x_ref, o_ref):
    x = x_ref[0]
    r = x.shape[0]
    ls = jnp.minimum(x, 0.0) - jnp.log1p(jnp.exp(-jnp.abs(x)))
    i0 = lax.broadcasted_iota(jnp.int32, (LANES, LANES), 0)
    i1 = lax.broadcasted_iota(jnp.int32, (LANES, LANES), 1)
    upper = jnp.where(i0 <= i1, 1.0, 0.0).astype(BF16)
    ones = jnp.ones((LANES, LANES), BF16)
    r0 = lax.broadcasted_iota(jnp.int32, (r, r), 0)
    r1 = lax.broadcasted_iota(jnp.int32, (r, r), 1)
    strict = jnp.where(r1 < r0, 1.0, 0.0).astype(BF16)
    parts = _split3(ls)
    intra = sum(_dot(pp, upper) for pp in parts)
    rowtot = sum(_dot(pp, ones) for pp in parts)
    off = sum(_dot(strict, pp) for pp in _split3(rowtot))
    o_ref[0] = intra + off


def _cum_log_forget(x):
    n, r, _ = x.shape
    return pl.pallas_call(
        _cumgate_body,
        grid=(n,),
        in_specs=[pl.BlockSpec((1, r, LANES), lambda i: (i, 0, 0))],
        out_specs=pl.BlockSpec((1, r, LANES), lambda i: (i, 0, 0)),
        out_shape=jax.ShapeDtypeStruct((n, r, LANES), F32),
        compiler_params=_cparams(("parallel",)),
        name="cum_log_forget",
    )(x)


def _merge_body(oa_ref, ob_ref, oc_ref, od_ref, g0_ref, g1_ref, g2_ref, g3_ref, wb_ref, o_ref):
    acc = None
    for n, (o_r, g_r) in enumerate(((oa_ref, g0_ref), (ob_ref, g1_ref), (oc_ref, g2_ref), (od_ref, g3_ref))):
        term = _sigmoid(g_r[...].astype(F32)) * _dot(o_r[...], wb_ref[n])
        acc = term if acc is None else acc + term
    o_ref[...] = acc.astype(o_ref.dtype)


def _merge(o_list, plain, gate_off, wb, *, tm, tn):
    t = plain.shape[0]
    d = wb.shape[-1]
    nj = d // tn
    ospec = pl.BlockSpec((tm, BRANCH_W), lambda i, j: (i, 0))
    gspecs = [pl.BlockSpec((tm, tn), functools.partial(lambda i, j, n: (i, gate_off // tn + n * nj + j), n=n))
              for n in range(N_BRANCH)]
    return pl.pallas_call(
        _merge_body,
        grid=(t // tm, nj),
        in_specs=[ospec] * 4 + gspecs + [pl.BlockSpec((N_BRANCH, BRANCH_W, tn), lambda i, j: (0, 0, j))],
        out_specs=pl.BlockSpec((tm, tn), lambda i, j: (i, j)),
        out_shape=jax.ShapeDtypeStruct((t, d), BF16),
        compiler_params=_cparams(("parallel", "parallel")),
        name="gated_merge",
    )(*o_list, plain, plain, plain, plain, wb)


def _layer_norm(y, g, b):
    mu = jnp.mean(y, axis=-1, keepdims=True)
    yc = y - mu
    var = jnp.mean(yc * yc, axis=-1, keepdims=True)
    return yc * lax.rsqrt(var + LN_EPS) * g + b


def _outproj_ln_body(mg_ref, wo_ref, h_ref, g_ref, b_ref, *rest, with_router):
    if with_router:
        wr_ref, o_ref, ob_ref, lg_ref = rest
    else:
        o_ref, ob_ref = rest
    y = ALPHA * h_ref[...] + _dot(mg_ref[...], wo_ref[...])
    out = _layer_norm(y, g_ref[...], b_ref[...])
    o_ref[...] = out
    ob_ref[...] = out.astype(BF16)
    if with_router:
        hi, lo = _split2(out)
        lg_ref[...] = _dot(hi, wr_ref[0]) + _dot(lo, wr_ref[0]) + _dot(hi, wr_ref[1])


def _outproj_ln(merged, w_out, h, g, b, w_router=None, *, tm):
    t, d = h.shape
    with_router = w_router is not None
    row = lambda i: (i, 0)
    fix = lambda i: (0, 0)
    in_specs = [pl.BlockSpec((tm, d), row), pl.BlockSpec((d, d), fix), pl.BlockSpec((tm, d), row),
                pl.BlockSpec((1, d), fix), pl.BlockSpec((1, d), fix)]
    out_specs = [pl.BlockSpec((tm, d), row), pl.BlockSpec((tm, d), row)]
    out_shape = [jax.ShapeDtypeStruct((t, d), F32), jax.ShapeDtypeStruct((t, d), BF16)]
    args = [merged, w_out, h, g, b]
    if with_router:
        in_specs.append(pl.BlockSpec((2, d, LANES), lambda i: (0, 0, 0)))
        out_specs.append(pl.BlockSpec((tm, LANES), row))
        out_shape.append(jax.ShapeDtypeStruct((t, LANES), F32))
        args.append(w_router)
    return pl.pallas_call(
        functools.partial(_outproj_ln_body, with_router=with_router),
        grid=(t // tm,), in_specs=in_specs, out_specs=out_specs, out_shape=out_shape,
        compiler_params=_cparams(("parallel",)), name="outproj_ln1",
    )(*args)


def _ple_ln_body(hb_ref, h_ref, f_ref, p_ref, wg_ref, wp_ref, g_ref, b_ref, o_ref, ob_ref):
    ple = _sigmoid(_dot(hb_ref[...], wg_ref[...])) * _dot(p_ref[...], wp_ref[...])
    out = _layer_norm(ALPHA * h_ref[...] + f_ref[...] + ple, g_ref[...], b_ref[...])
    o_ref[...] = out
    ob_ref[...] = out.astype(BF16)


def _ple_ln(h_bf, h, f, p_bf, w_gate, w_proj, g, b, *, tm):
    t, d = h.shape
    row = lambda i: (i, 0)
    fix = lambda i: (0, 0)
    return pl.pallas_call(
        _ple_ln_body,
        grid=(t // tm,),
        in_specs=[pl.BlockSpec((tm, d), row), pl.BlockSpec((tm, d), row), pl.BlockSpec((tm, d), row),
                  pl.BlockSpec((tm, PLE_DIM), row), pl.BlockSpec((d, d), fix),
                  pl.BlockSpec((PLE_DIM, d), fix), pl.BlockSpec((1, d), fix), pl.BlockSpec((1, d), fix)],
        out_specs=[pl.BlockSpec((tm, d), row), pl.BlockSpec((tm, d), row)],
        out_shape=[jax.ShapeDtypeStruct((t, d), F32), jax.ShapeDtypeStruct((t, d), BF16)],
        compiler_params=_cparams(("parallel",)), name="ple_ln2",
    )(h_bf, h, f, p_bf, w_gate, w_proj, g, b)


def _swiglu_tile(x, wg, wu, wd):
    g = _dot(x, wg)
    u = _dot(x, wu)
    return _dot((g * _sigmoid(g) * u).astype(BF16), wd)


def _ffn_body(x_ref, wg_ref, wu_ref, wd_ref, o_ref):
    j = pl.program_id(1)
    y = _swiglu_tile(x_ref[...], wg_ref[...], wu_ref[...], wd_ref[...])

    @pl.when(j == 0)
    def _():
        o_ref[...] = y

    @pl.when(j > 0)
    def _():
        o_ref[...] += y


def _ffn(x_bf, wg, wu, wd, *, tm, tf):
    t, d = x_bf.shape
    f = wg.shape[1]
    return pl.pallas_call(
        _ffn_body,
        grid=(t // tm, f // tf),
        in_specs=[pl.BlockSpec((tm, d), lambda i, j: (i, 0)),
                  pl.BlockSpec((d, tf), lambda i, j: (0, j)),
                  pl.BlockSpec((d, tf), lambda i, j: (0, j)),
                  pl.BlockSpec((tf, d), lambda i, j: (j, 0))],
        out_specs=pl.BlockSpec((tm, d), lambda i, j: (i, 0)),
        out_shape=jax.ShapeDtypeStruct((t, d), F32),
        compiler_params=_cparams(("parallel", "arbitrary")), name="ffn_swiglu",
    )(x_bf, wg, wu, wd)


def _moe_ffn_body(te_ref, nt_ref, x_ref, wg_ref, wu_ref, wd_ref, rw_ref, o_ref):
    i = pl.program_id(0)
    j = pl.program_id(1)
    active = i < nt_ref[0]

    @pl.when(active)
    def _():
        y = _swiglu_tile(x_ref[...], wg_ref[0], wu_ref[0], wd_ref[0])

        @pl.when(j == 0)
        def _():
            o_ref[...] = y

        @pl.when(j > 0)
        def _():
            o_ref[...] += y

        @pl.when(j == pl.num_programs(1) - 1)
        def _():
            o_ref[...] = o_ref[...] * rw_ref[:, 0:1]

    @pl.when(jnp.logical_not(active) & (j == 0))
    def _():
        o_ref[...] = jnp.zeros(o_ref.shape, F32)


def _moe_ffn(tile_expert, n_tiles, x_sorted, wg, wu, wd, row_w, *, tm, tf):
    r, d = x_sorted.shape
    f = wg.shape[2]
    nj = f // tf

    def jj(i, j, nt):
        return jnp.where(i < nt[0], j, nj - 1)

    return pl.pallas_call(
        _moe_ffn_body,
        grid_spec=pltpu.PrefetchScalarGridSpec(
            num_scalar_prefetch=2, grid=(r // tm, nj),
            in_specs=[pl.BlockSpec((tm, d), lambda i, j, te, nt: (i, 0)),
                      pl.BlockSpec((1, d, tf), lambda i, j, te, nt: (te[i], 0, jj(i, j, nt))),
                      pl.BlockSpec((1, d, tf), lambda i, j, te, nt: (te[i], 0, jj(i, j, nt))),
                      pl.BlockSpec((1, tf, d), lambda i, j, te, nt: (te[i], jj(i, j, nt), 0)),
                      pl.BlockSpec((tm, LANES), lambda i, j, te, nt: (i, 0))],
            out_specs=pl.BlockSpec((tm, d), lambda i, j, te, nt: (i, 0))),
        out_shape=jax.ShapeDtypeStruct((r, d), F32),
        compiler_params=_cparams(("arbitrary", "arbitrary")), name="moe_grouped_ffn",
    )(tile_expert, n_tiles, x_sorted, wg, wu, wd, row_w)


def _row_copy(src_ref, src_row, dst_ref, dst_row, sem):
    return pltpu.make_async_copy(src_ref.at[pl.ds(src_row, 1)], dst_ref.at[pl.ds(dst_row, 1)], sem)


def _gather_rows_body(idx_ref, src_ref, o_ref, buf_ref, sem, *, tm):
    base = pl.program_id(0) * tm

    def start(r, c):
        _row_copy(src_ref, idx_ref[base + r], buf_ref, r, sem).start()
        return c

    def wait(r, c):
        _row_copy(src_ref, 0, buf_ref, r, sem).wait()
        return c

    lax.fori_loop(0, tm, start, 0)
    lax.fori_loop(0, tm, wait, 0)
    o_ref[...] = buf_ref[...].astype(o_ref.dtype)


def _gather_rows(idx, src, n_rows, out_dtype, *, tm):
    d = src.shape[1]
    return pl.pallas_call(
        functools.partial(_gather_rows_body, tm=tm),
        grid_spec=pltpu.PrefetchScalarGridSpec(
            num_scalar_prefetch=1, grid=(n_rows // tm,),
            in_specs=[pl.BlockSpec(memory_space=pl.ANY)],
            out_specs=pl.BlockSpec((tm, d), lambda i, idx: (i, 0)),
            scratch_shapes=[pltpu.VMEM((tm, d), src.dtype), pltpu.SemaphoreType.DMA(())]),
        out_shape=jax.ShapeDtypeStruct((n_rows, d), out_dtype),
        compiler_params=_cparams(("arbitrary",)), name="moe_gather_rows",
    )(idx, src)


def _combine_body(idx_ref, src_ref, o_ref, a_ref, b_ref, sem, *, tm):
    base = pl.program_id(0) * tm

    def start(r, c):
        _row_copy(src_ref, idx_ref[2 * (base + r)], a_ref, r, sem.at[0]).start()
        _row_copy(src_ref, idx_ref[2 * (base + r) + 1], b_ref, r, sem.at[1]).start()
        return c

    def wait(r, c):
        _row_copy(src_ref, 0, a_ref, r, sem.at[0]).wait()
        _row_copy(src_ref, 0, b_ref, r, sem.at[1]).wait()
        return c

    lax.fori_loop(0, tm, start, 0)
    lax.fori_loop(0, tm, wait, 0)
    o_ref[...] = a_ref[...] + b_ref[...]


def _combine_pairs(pos, y_sorted, n_tokens, *, tm):
    d = y_sorted.shape[1]
    return pl.pallas_call(
        functools.partial(_combine_body, tm=tm),
        grid_spec=pltpu.PrefetchScalarGridSpec(
            num_scalar_prefetch=1, grid=(n_tokens // tm,),
            in_specs=[pl.BlockSpec(memory_space=pl.ANY)],
            out_specs=pl.BlockSpec((tm, d), lambda i, idx: (i, 0)),
            scratch_shapes=[pltpu.VMEM((tm, d), F32), pltpu.VMEM((tm, d), F32),
                            pltpu.SemaphoreType.DMA((2,))]),
        out_shape=jax.ShapeDtypeStruct((n_tokens, d), F32),
        compiler_params=_cparams(("arbitrary",)), name="moe_combine",
    )(pos, y_sorted)


def _col_slices():
    out, off = {}, 0
    for name, width in IN_SPLITS:
        out[name] = (off, width)
        off += width
    return out


def _gather_cols(w, names, pad_to=None):
    cs = _col_slices()
    parts = [w[:, cs[n][0]:cs[n][0] + cs[n][1]] for n in names]
    width = sum(cs[n][1] for n in names)
    if pad_to is not None and pad_to > width:
        parts.append(jnp.zeros((w.shape[0], pad_to - width), w.dtype))
    return jnp.concatenate(parts, axis=1)


def _selection_map_t(n_cmp_pad, n_sel):
    ci = np.arange(n_cmp_pad)[:, None] * NSA_CMP_D
    sj = np.arange(n_sel)[None, :] * NSA_SEL_L
    ov = np.clip(np.minimum(ci + NSA_CMP_L, sj + NSA_SEL_L) - np.maximum(ci, sj), 0, None)
    return np.ascontiguousarray((ov / NSA_CMP_D).astype(np.float32).T)


def _gate_expand_matrices():
    e = np.zeros((2, 3, LANES, 2 * LANES), np.float32)
    for g in range(2):
        for a in range(4):
            for c in range(3):
                e[g, c, (4 * g + a) * 3 + c, a * HEAD_DIM:(a + 1) * HEAD_DIM] = 1.0
    return e


def _tile(n, pref):
    return pref if n % pref == 0 else n


def _token_mixer(h, h_bf, layer, b, s, tabs, w_in, cmp_pe, cmp_w1, cmp_b1, cmp_w2, sinks, fox_bf,
                 diff_lambda, diff_gain, w_branch, w_out, ln_g, ln_b, w_router):
    t = b * s
    (cos64, sin64), (cos32, sin32) = tabs
    tm = _tile(t, 1024)
    w_bf = w_in.astype(BF16)
    r64 = _proj(h_bf, _gather_cols(w_bf, SEG_ROPE64), BF16, tm, 512, rope=(cos64, sin64, HEAD_DIM // 2))
    r32 = _proj(h_bf, _gather_cols(w_bf, SEG_ROPE32), BF16, tm, 512, rope=(cos32, sin32, DIFF_SUB // 2))
    plain = _proj(h_bf, _gather_cols(w_bf, SEG_PLAIN), BF16, tm, 512)
    small = _proj(h_bf, _gather_cols(w_bf, SEG_SMALL, pad_to=LANES), F32, tm, LANES)
    r64_3, r32_3, plain_3, small_3 = (a.reshape(b, s, -1) for a in (r64, r32, plain, small))
    scale = HEAD_DIM ** -0.5

    ncp = s // NSA_CMP_D
    n_sel = s // NSA_SEL_L
    topn = min(NSA_TOPN, n_sel)

    def cmp_blocks(x2d):
        c = x2d.reshape(b, s, 2, HEAD_DIM).transpose(0, 2, 1, 3).reshape(b * 2, ncp, NSA_CMP_D * HEAD_DIM)
        nxt = jnp.concatenate([c[:, 1:], jnp.zeros_like(c[:, :1])], axis=1)
        return jnp.concatenate([c, nxt], axis=-1)

    xk = cmp_blocks(r64[:, 512:640])
    xv = cmp_blocks(plain[:, 0:128])
    cmp_kv = _nsa_compress(jnp.stack([xk, xv]), cmp_pe.reshape(2, 1, -1), cmp_w1.astype(BF16),
                           cmp_b1.reshape(2, 1, -1), cmp_w2.astype(BF16))
    kc = cmp_kv[0].astype(BF16).reshape(b, 2, ncp, HEAD_DIM)
    vc = cmp_kv[1].astype(BF16).reshape(b, 2, ncp, HEAD_DIM)
    kc4 = jnp.tile(kc, (1, 1, 1, 4))
    vct = vc.transpose(0, 1, 3, 2)
    selt = jnp.asarray(_selection_map_t(ncp, n_sel), BF16)
    tq_c = _tile(s, 256)
    cmp_out = _nsa_cmp_topk(r64_3, 0, kc4, vct, selt, tq=tq_c, topn=topn, scale=scale)
    mneg = cmp_out[..., 2 * LANES:2 * LANES + n_sel]
    if n_sel < HEAD_DIM:
        mneg = jnp.concatenate([mneg, jnp.zeros((b, 2, s, HEAD_DIM - n_sel), F32)], -1)
    mneg = jnp.broadcast_to(mneg.transpose(0, 2, 1, 3)[:, :, :, None, :], (b, s, 2, 4, HEAD_DIM))
    q_aug = jnp.concatenate([r64_3[..., 0:512].reshape(b, s, 2, 4, HEAD_DIM), mneg.astype(BF16)],
                            axis=-1).reshape(b, s, 8 * LANES)
    onehot = jax.nn.one_hot(jnp.arange(s) // NSA_SEL_L, HEAD_DIM, dtype=BF16)
    k_aug = jnp.concatenate([r64_3[..., 640:768].reshape(b, s, 2, HEAD_DIM),
                             jnp.broadcast_to(onehot[None, :, None, :], (b, s, 2, HEAD_DIM))],
                            axis=-1).reshape(b, s, 2 * LANES)
    tq_w = _tile(s, 512)
    o_win = _band_attention(r64_3, 0, r64_3, 6, plain_3, 2, tq=tq_w, window=NSA_WIN, scale=scale)
    e_mat = jnp.asarray(_gate_expand_matrices(), BF16)
    o_a = _nsa_sel_attention(q_aug, k_aug, plain_3, 1, cmp_out, o_win, small_3, e_mat,
                             tq=_tile(s, 256), scale=scale)

    o_b = _band_attention(r64_3, 7, r64_3, 11, plain_3, 3, tq=_tile(s, 256), window=SWA_WIN,
                          scale=scale, sinks=sinks.astype(F32))

    f_logit = (small[:, 24:32] + fox_bf[None, :]).reshape(b, s, 8).transpose(0, 2, 1)
    cum = _cum_log_forget(f_logit.reshape(b * 8, s // LANES, LANES)).reshape(b, 8, 1, s)
    tq_f = _tile(s, 512)
    o_c = _causal_pair_attention(plain_3, 4, plain_3, 8, plain_3, 12, tq=tq_f, kind="fox",
                                 scale=scale, ck=cum)

    lam_init = 0.8 - 0.6 * math.exp(-0.3 * layer)
    lf = diff_lambda.astype(F32)
    lam = jnp.exp(jnp.sum(lf[0] * lf[1])) - jnp.exp(jnp.sum(lf[2] * lf[3])) + lam_init
    lam_arr = jnp.stack([lam, jnp.asarray(1.0 - lam_init, F32)]).astype(F32)
    gain2 = jnp.tile(diff_gain.astype(F32), 2).reshape(1, LANES)
    o_d = _causal_pair_attention(r32_3, 0, r32_3, 4, plain_3, 16, tq=tq_f, kind="diff",
                                 scale=DIFF_SUB ** -0.5, lam=lam_arr, gain=gain2)

    o_list = [o.reshape(t, BRANCH_W) for o in (o_a, o_b, o_c, o_d)]
    merged = _merge(o_list, plain, 2560, w_branch.astype(BF16), tm=tm, tn=512)
    return _outproj_ln(merged, w_out.astype(BF16), h, ln_g.reshape(1, -1), ln_b.reshape(1, -1),
                       w_router, tm=_tile(t, 256))


def _moe_layer(h1, logits_pad, b_router, wg, wu, wd, *, tm, tf):
    t, d = h1.shape
    logits = logits_pad[:, :N_EXPERTS] + b_router.astype(F32)[None, :]
    top_v, top_i = lax.top_k(logits, TOP_K)
    top_w = jax.nn.softmax(top_v, axis=-1)
    flat_e = top_i.reshape(-1)
    onehot = jax.nn.one_hot(flat_e, N_EXPERTS, dtype=jnp.int32)
    rank = jnp.sum((jnp.cumsum(onehot, axis=0) - onehot) * onehot, axis=1)
    cnt = jnp.sum(onehot, axis=0)
    padded = ((cnt + tm - 1) // tm) * tm
    ends = jnp.cumsum(padded)
    starts = ends - padded
    pos = (starts[flat_e] + rank).astype(jnp.int32)
    n_rows = TOP_K * t + N_EXPERTS * tm
    row_token = jnp.zeros((n_rows,), jnp.int32).at[pos].set(jnp.arange(TOP_K * t, dtype=jnp.int32) // TOP_K)
    row_w = jnp.zeros((n_rows,), F32).at[pos].set(top_w.reshape(-1))
    tile_start = jnp.arange(n_rows // tm, dtype=jnp.int32) * tm
    tile_expert = jnp.minimum(jnp.sum(tile_start[:, None] >= ends[None, :], axis=1), N_EXPERTS - 1)
    n_tiles = (ends[-1] // tm).astype(jnp.int32).reshape(1)
    x_sorted = _gather_rows(row_token, h1, n_rows, BF16, tm=256)
    y_sorted = _moe_ffn(tile_expert.astype(jnp.int32), n_tiles, x_sorted, wg, wu, wd,
                        jnp.broadcast_to(row_w[:, None], (n_rows, LANES)), tm=tm, tf=tf)
    return _combine_pairs(pos, y_sorted, t, tm=_tile(t, 256))


def kernel(x, p, positions, w_in, nsa_cmp_pe, nsa_cmp_w1, nsa_cmp_b1, nsa_cmp_w2, swa_sinks, fox_bf,
           diff_lambda, diff_gain, w_branch, w_out, ln1_g, ln1_b, ffn_wg, ffn_wu, ffn_wd, moe_router,
           moe_router_b, moe_wg, moe_wu, moe_wd, ple_proj, ple_gate, ln2_g, ln2_b):
    b, s, d = x.shape
    t = b * s
    tabs = (_rope_tabs(positions, HEAD_DIM), _rope_tabs(positions, DIFF_SUB))
    h = x.reshape(t, d).astype(F32)
    h_bf = h.astype(BF16)
    for i in range(DEPTH):
        is_moe = i % 2 == 1
        w_router = None
        if is_moe:
            wr = jnp.zeros((d, LANES), F32).at[:, :N_EXPERTS].set(moe_router[i // 2].astype(F32))
            wr_hi = wr.astype(BF16)
            w_router = jnp.stack([wr_hi, (wr - wr_hi.astype(F32)).astype(BF16)])
        res = _token_mixer(h, h_bf, i, b, s, tabs, w_in[i], nsa_cmp_pe[i], nsa_cmp_w1[i], nsa_cmp_b1[i],
                           nsa_cmp_w2[i], swa_sinks[i], fox_bf[i], diff_lambda[i], diff_gain[i],
                           w_branch[i], w_out[i], ln1_g[i], ln1_b[i], w_router)
        h1, h1_bf = res[0], res[1]
        if not is_moe:
            fpad = (-D_FF) % 512
            wg = jnp.pad(ffn_wg[i // 2].astype(BF16), ((0, 0), (0, fpad)))
            wu = jnp.pad(ffn_wu[i // 2].astype(BF16), ((0, 0), (0, fpad)))
            wd = jnp.pad(ffn_wd[i // 2].astype(BF16), ((0, fpad), (0, 0)))
            f = _ffn(h1_bf, wg, wu, wd, tm=_tile(t, 1024), tf=512)
        else:
            f = _moe_layer(h1, res[2], moe_router_b[i // 2], moe_wg[i // 2].astype(BF16),
                           moe_wu[i // 2].astype(BF16), moe_wd[i // 2].astype(BF16),
                           tm=_tile(t, 512), tf=512)
        h, h_bf = _ple_ln(h1_bf, h1, f, p[i].reshape(t, PLE_DIM).astype(BF16), ple_gate[i].astype(BF16),
                          ple_proj[i].astype(BF16), ln2_g[i].reshape(1, -1), ln2_b[i].reshape(1, -1),
                          tm=_tile(t, 256))
    return h.reshape(b, s, d).astype(x.dtype)
```

```python
import functools
import math

import numpy as np
import jax
import jax.numpy as jnp
from jax import lax
from jax.experimental import pallas as pl
from jax.experimental.pallas import tpu as pltpu

F32 = jnp.float32
BF16 = jnp.bfloat16

D_MODEL = 2048
DEPTH = 2
HEAD_DIM = 64
ROPE_THETA = 10000.0
PLE_DIM = 256
LN_EPS = 1e-5
NSA_CMP_L = 32
NSA_CMP_D = 16
NSA_SEL_L = 64
NSA_TOPN = 16
NSA_WIN = 512
NSA_CMP_HIDDEN = 256
NSA_FORCE = 1e9
SWA_WIN = 128
DIFF_SUB = HEAD_DIM // 2
N_BRANCH = 4
BRANCH_W = 8 * HEAD_DIM
D_FF = 5504
N_EXPERTS = 8
TOP_K = 2
D_FF_EXPERT = 7168
ALPHA = (2.0 * DEPTH) ** 0.25

IN_SPLITS = (
    ("a_q", 512), ("a_kc", 128), ("a_vc", 128), ("a_ks", 128), ("a_vs", 128),
    ("a_kw", 128), ("a_vw", 128), ("a_g", 24),
    ("b_q", 512), ("b_k", 128), ("b_v", 128),
    ("c_q", 512), ("c_k", 512), ("c_v", 512), ("c_f", 8),
    ("d_q", 512), ("d_k", 512), ("d_v", 512),
    ("merge_gate", N_BRANCH * D_MODEL),
)
SEG_ROPE64 = ("a_q", "a_kc", "a_ks", "a_kw", "b_q", "b_k")
SEG_ROPE32 = ("d_q", "d_k")
SEG_PLAIN = ("a_vc", "a_vs", "a_vw", "b_v", "c_q", "c_k", "c_v", "d_v", "merge_gate")
SEG_SMALL = ("a_g", "c_f")

LANES = 128
NEG = -1e30
LOG2E = math.log2(math.e)
VMEM_LIMIT = 56 * 1024 * 1024
ATT_TQ = 512
ATT_TKC = ATT_TQ

Q_FOLD = {"a_q": HEAD_DIM ** -0.5 * LOG2E, "b_q": HEAD_DIM ** -0.5 * LOG2E,
          "c_q": HEAD_DIM ** -0.5 * LOG2E, "d_q": DIFF_SUB ** -0.5 * LOG2E}


def _cparams(sem):
    return pltpu.CompilerParams(dimension_semantics=sem, vmem_limit_bytes=VMEM_LIMIT)


def _sigmoid(x):
    return 1.0 / (1.0 + jnp.exp(-x))


def _dot(a, b):
    return jnp.dot(a, b, preferred_element_type=F32)


def _dot_nt(a, b):
    return lax.dot_general(a, b, (((1,), (1,)), ((), ())), preferred_element_type=F32)


def _split2(x):
    hi = x.astype(BF16)
    lo = (x - hi.astype(F32)).astype(BF16)
    return hi, lo


def _split3(x):
    hi = x.astype(BF16)
    r = x - hi.astype(F32)
    mid = r.astype(BF16)
    lo = (r - mid.astype(F32)).astype(BF16)
    return hi, mid, lo


def _proj_body(x_ref, w_ref, *rest, rope_half):
    if rope_half:
        cos_ref, sin_ref, o_ref = rest
    else:
        (o_ref,) = rest
    acc = _dot(x_ref[...], w_ref[...])
    if not rope_half:
        o_ref[...] = acc.astype(o_ref.dtype)
        return
    cos = cos_ref[...]
    sin = sin_ref[...]
    lane = lax.broadcasted_iota(jnp.int32, cos.shape, 1)
    first = (lane % (2 * rope_half)) < rope_half
    for c in range(acc.shape[1] // LANES):
        a = acc[:, c * LANES:(c + 1) * LANES]
        rot = jnp.where(first, pltpu.roll(a, LANES - rope_half, 1), pltpu.roll(a, rope_half, 1))
        o_ref[:, c * LANES:(c + 1) * LANES] = (a * cos + rot * sin).astype(o_ref.dtype)


def _proj(x, w, out_dtype, tm, tn, rope=None):
    m, k = x.shape
    n = w.shape[1]
    in_specs = [pl.BlockSpec((tm, k), lambda i, j: (i, 0)),
                pl.BlockSpec((k, tn), lambda i, j: (0, j))]
    args = [x, w]
    rope_half = 0
    if rope is not None:
        cos_tab, sin_tab, rope_half = rope
        in_specs += [pl.BlockSpec((tm, LANES), lambda i, j: (i, 0)),
                     pl.BlockSpec((tm, LANES), lambda i, j: (i, 0))]
        args += [cos_tab, sin_tab]
    return pl.pallas_call(
        functools.partial(_proj_body, rope_half=rope_half),
        grid=(m // tm, n // tn),
        in_specs=in_specs,
        out_specs=pl.BlockSpec((tm, tn), lambda i, j: (i, j)),
        out_shape=jax.ShapeDtypeStruct((m, n), out_dtype),
        compiler_params=_cparams(("parallel", "parallel")),
        name="proj_rope" if rope_half else "proj",
    )(*args)


def _rope_tabs(positions, dim):
    inv = 1.0 / (ROPE_THETA ** (jnp.arange(0, dim, 2, dtype=F32) / dim))
    ang = positions.astype(F32).reshape(-1)[:, None] * inv
    c, s = jnp.cos(ang), jnp.sin(ang)
    reps = LANES // dim
    return (jnp.tile(jnp.concatenate([c, c], -1), (1, reps)),
            jnp.tile(jnp.concatenate([-s, s], -1), (1, reps)))


def _tattn_body(*refs, nmaps, kmap, tq, tkc, mode, window, fin):
    refs = list(refs)
    sc_ref = refs.pop(0) if fin in ("diff", "swa") else None
    qt_ref, k_ref, vt_ref = refs[:3]
    extras, o_ref = refs[3:-4], refs[-4]
    m_ref, l_ref, acc_ref = refs[-3:]
    u = pl.program_id(1)
    q0 = pl.program_id(2) * tq
    t_pos = q0 + lax.broadcasted_iota(jnp.int32, (tkc, tq), 1)
    key_iota = lax.broadcasted_iota(jnp.int32, (tkc, tq), 0)

    qi = pl.program_id(2)

    def scores(qt, kidx, c, masked):
        st = _dot(k_ref[0, 0, kidx, c], qt)
        if masked:
            key = c * tkc + key_iota
            keep = key <= t_pos
            if mode == "band":
                keep = keep & (t_pos - key < window)
            st = jnp.where(keep, st, NEG)
        return st

    def group8(x, op):
        return op(x.reshape(tkc // 8, 8, tq), axis=0)

    m_ref[...] = jnp.full(m_ref.shape, NEG, F32)
    l_ref[...] = jnp.zeros(l_ref.shape, F32)
    acc_ref[...] = jnp.zeros(acc_ref.shape, F32)

    def chunk_step(c, carry, masked):
        for mp in range(nmaps):
            st = scores(qt_ref[0, 0, mp], kmap[mp], c, masked)
            m_old = m_ref[mp]
            m_new = jnp.maximum(m_old, jnp.max(group8(st, jnp.max), axis=0, keepdims=True))
            alpha = jnp.exp2(m_old - m_new)
            p = jnp.exp2(st - m_new)
            l_ref[mp] = alpha * l_ref[mp] + group8(p, jnp.sum)
            acc_ref[mp] = alpha * acc_ref[mp] + _dot(vt_ref[0, 0, c], p.astype(BF16))
            m_ref[mp] = m_new
        return carry

    lo = 0 if mode == "causal" else jnp.maximum(q0 - (window - 1), 0) // tkc
    lax.fori_loop(lo, qi, functools.partial(chunk_step, masked=mode == "band"), 0)
    chunk_step(qi, 0, True)
    results = [(m_ref[mp], jnp.sum(l_ref[mp], axis=0, keepdims=True), acc_ref[mp]) for mp in range(nmaps)]

    def normed(mp):
        m_i, l_i, acc = results[mp]
        if fin == "swa":
            sk = sc_ref[2 * u + mp]
            m_f = jnp.maximum(m_i, sk)
            corr = jnp.exp2(m_i - m_f)
            return acc * (corr / (l_i * corr + jnp.exp2(sk - m_f)))
        return acc * (1.0 / l_i)

    if fin == "diff":
        lam = sc_ref[0]
        halves = []
        for hh in range(2):
            o = (normed(2 * hh) - lam * normed(2 * hh + 1))[hh * HEAD_DIM:(hh + 1) * HEAD_DIM]
            ms = jnp.mean(o * o, axis=0, keepdims=True)
            halves.append(o * lax.rsqrt(ms + LN_EPS))
        ot = jnp.concatenate(halves, axis=0) * extras[0][...] * sc_ref[1]
    else:
        ot = jnp.concatenate([normed(0)[:HEAD_DIM], normed(1)[HEAD_DIM:]], axis=0)
    o = ot.T
    if fin == "nsa":
        ocmp_ref, owin_ref, sm_ref, e_ref = extras
        hi, lo = _split2(sm_ref[0])
        gates = [_sigmoid(_dot(hi, e_ref[0, c]) + _dot(lo, e_ref[0, c])) for c in range(3)]
        o = gates[0] * ocmp_ref[0, 0] + gates[1] * o + gates[2] * owin_ref[0].astype(F32)
    o_ref[0] = o.astype(o_ref.dtype)


def _tattn(qt, k, vt, *, kmap, mode, fin, name, window=0, scalars=None, extras=(), extra_specs=()):
    b, nu, nmaps, _, s = qt.shape
    tq = min(ATT_TQ, s)
    nc, tkc = k.shape[3], k.shape[4]
    kdiv = nu // k.shape[1]
    vdiv = nu // vt.shape[1]
    in_specs = [pl.BlockSpec((1, 1, nmaps, LANES, tq), lambda bi, u, qi, *_: (bi, u, 0, 0, qi)),
                pl.BlockSpec((1, 1, k.shape[2], nc, tkc, LANES), lambda bi, u, qi, *_: (bi, u // kdiv, 0, 0, 0, 0)),
                pl.BlockSpec((1, 1, nc, LANES, tkc), lambda bi, u, qi, *_: (bi, u // vdiv, 0, 0, 0))]
    in_specs += list(extra_specs)
    body = functools.partial(_tattn_body, nmaps=nmaps, kmap=kmap, tq=tq, tkc=tkc, mode=mode,
                             window=window, fin=fin)
    args = ([] if scalars is None else [scalars]) + [qt, k, vt] + list(extras)
    return pl.pallas_call(
        body,
        grid_spec=pltpu.PrefetchScalarGridSpec(
            num_scalar_prefetch=0 if scalars is None else 1, grid=(b, nu, s // tq),
            in_specs=in_specs,
            out_specs=pl.BlockSpec((1, tq, LANES), lambda bi, u, qi, *_: (bi, qi, u)),
            scratch_shapes=[pltpu.VMEM((nmaps, 1, tq), F32), pltpu.VMEM((nmaps, 8, tq), F32),
                            pltpu.VMEM((nmaps, LANES, tq), F32)]),
        out_shape=jax.ShapeDtypeStruct((b, s, nu * LANES), BF16),
        compiler_params=_cparams(("parallel", "parallel", "parallel")),
        name=name,
    )(*args)


def _gelu_tanh(x):
    return 0.5 * x * (1.0 + jnp.tanh(math.sqrt(2.0 / math.pi) * (x + 0.044715 * (x * x * x))))


def _compress_body(x_ref, pe_ref, w1_ref, b1_ref, w2_ref, o_ref):
    x = (x_ref[0, 0].astype(F32) + pe_ref[0]).astype(BF16)
    hid = _gelu_tanh(_dot(x, w1_ref[0]) + b1_ref[0])
    o_ref[0, 0] = _dot(hid.astype(BF16), w2_ref[0])


def _nsa_compress(x, pe, w1, b1, w2):
    _, nb, ncp, ld = x.shape
    hid = w1.shape[-1]
    return pl.pallas_call(
        _compress_body,
        grid=(2, nb),
        in_specs=[pl.BlockSpec((1, 1, ncp, ld), lambda t, i: (t, i, 0, 0)),
                  pl.BlockSpec((1, 1, ld), lambda t, i: (t, 0, 0)),
                  pl.BlockSpec((1, ld, hid), lambda t, i: (t, 0, 0)),
                  pl.BlockSpec((1, 1, hid), lambda t, i: (t, 0, 0)),
                  pl.BlockSpec((1, hid, HEAD_DIM), lambda t, i: (t, 0, 0))],
        out_specs=pl.BlockSpec((1, 1, ncp, HEAD_DIM), lambda t, i: (t, i, 0, 0)),
        out_shape=jax.ShapeDtypeStruct((2, nb, ncp, HEAD_DIM), F32),
        compiler_params=_cparams(("parallel", "parallel")),
        name="nsa_compress",
    )(x, pe, w1, b1, w2)


def _cmp_topk_body(q_ref, kc_ref, vct_ref, selt_ref, o_ref, mt_ref, *, tq, ncp, nsel, topn):
    qi = pl.program_id(2)
    q = q_ref[0]
    kc4 = kc_ref[0, 0]
    vct = vct_ref[0, 0]
    lane = lax.broadcasted_iota(jnp.int32, kc4.shape, 1)
    ci = lax.broadcasted_iota(jnp.int32, (ncp, tq), 0)
    tpos = qi * tq + lax.broadcasted_iota(jnp.int32, (ncp, tq), 1)
    cmask = ci * NSA_CMP_D + (NSA_CMP_L - 1) <= tpos
    psum = jnp.zeros((ncp, tq), F32)
    rows = []
    for a in range(4):
        kcm = jnp.where((lane >= a * HEAD_DIM) & (lane < (a + 1) * HEAD_DIM), kc4, jnp.zeros_like(kc4))
        st = jnp.where(cmask, _dot_nt(kcm, q), NEG)
        m = jnp.max(st, axis=0, keepdims=True)
        e = jnp.where(cmask, jnp.exp2(st - m), 0.0)
        l = jnp.sum(e, axis=0, keepdims=True)
        p = e * jnp.where(l > 0.0, 1.0 / l, 0.0)
        psum = psum + p
        rows.append(_dot(vct, p.astype(BF16)))
    hi, lo = _split2(psum)
    selt = selt_ref[...]
    imp = _dot(selt, hi) + _dot(selt, lo)
    blk = lax.broadcasted_iota(jnp.int32, (nsel, tq), 0)
    cur = (qi * tq + lax.broadcasted_iota(jnp.int32, (nsel, tq), 1)) // NSA_SEL_L
    forced = (blk == 0) | (blk == cur) | (blk == cur - 1)
    imp = jnp.where(forced, NSA_FORCE, jnp.where(blk > cur, -NSA_FORCE, imp))
    cnt = jnp.zeros((nsel, tq), jnp.int32)
    for jp in range(nsel):
        v = imp[jp:jp + 1, :]
        tie = jnp.where(blk > jp, 1, 0)
        cnt = cnt + jnp.where(v > imp, 1, jnp.where(v == imp, tie, 0))
    mneg = jnp.where(cnt < topn, 0.0, NEG)
    if nsel < HEAD_DIM:
        mneg = jnp.concatenate([mneg, jnp.zeros((HEAD_DIM - nsel, tq), F32)], axis=0)
    mt_ref[0, 0] = mneg.astype(mt_ref.dtype)
    o_ref[0, 0] = jnp.concatenate(rows, axis=0).T


def _nsa_cmp_topk(q_arr, q_off256, kc4, vct, selt, *, tq, topn):
    b, s, _ = q_arr.shape
    ncp = kc4.shape[2]
    nsel = selt.shape[0]
    body = functools.partial(_cmp_topk_body, tq=tq, ncp=ncp, nsel=nsel, topn=topn)
    return pl.pallas_call(
        body,
        grid=(b, 2, s // tq),
        in_specs=[pl.BlockSpec((1, tq, 2 * LANES), lambda bi, g, qi: (bi, qi, q_off256 + g)),
                  pl.BlockSpec((1, 1, ncp, 2 * LANES), lambda bi, g, qi: (bi, g, 0, 0)),
                  pl.BlockSpec((1, 1, HEAD_DIM, ncp), lambda bi, g, qi: (bi, g, 0, 0)),
                  pl.BlockSpec((nsel, ncp), lambda bi, g, qi: (0, 0))],
        out_specs=[pl.BlockSpec((1, 1, tq, 2 * LANES), lambda bi, g, qi: (bi, g, qi, 0)),
                   pl.BlockSpec((1, 1, HEAD_DIM, tq), lambda bi, g, qi: (bi, g, 0, qi))],
        out_shape=[jax.ShapeDtypeStruct((b, 2, s, 2 * LANES), F32),
                   jax.ShapeDtypeStruct((b, 2, HEAD_DIM, s), BF16)],
        compiler_params=_cparams(("parallel", "parallel", "parallel")),
        name="nsa_cmp_topk",
    )(q_arr, kc4, vct, selt)


def _cumgate_body(x_ref, o_ref):
    x = x_ref[0]
    r = x.shape[0]
    ls = jnp.minimum(x, 0.0) - jnp.log1p(jnp.exp(-jnp.abs(x)))
    i0 = lax.broadcasted_iota(jnp.int32, (LANES, LANES), 0)
    i1 = lax.broadcasted_iota(jnp.int32, (LANES, LANES), 1)
    upper = jnp.where(i0 <= i1, 1.0, 0.0).astype(BF16)
    ones = jnp.ones((LANES, LANES), BF16)
    r0 = lax.broadcasted_iota(jnp.int32, (r, r), 0)
    r1 = lax.broadcasted_iota(jnp.int32, (r, r), 1)
    strict = jnp.where(r1 < r0, 1.0, 0.0).astype(BF16)
    parts = _split3(ls)
    intra = sum(_dot(pp, upper) for pp in parts)
    rowtot = sum(_dot(pp, ones) for pp in parts)
    off = sum(_dot(strict, pp) for pp in _split3(rowtot))
    o_ref[0] = intra + off


def _cum_log_forget(x):
    n, r, _ = x.shape
    return pl.pallas_call(
        _cumgate_body,
        grid=(n,),
        in_specs=[pl.BlockSpec((1, r, LANES), lambda i: (i, 0, 0))],
        out_specs=pl.BlockSpec((1, r, LANES), lambda i: (i, 0, 0)),
        out_shape=jax.ShapeDtypeStruct((n, r, LANES), F32),
        compiler_params=_cparams(("parallel",)),
        name="cum_log_forget",
    )(x)


def _merge_body(oa_ref, ob_ref, oc_ref, od_ref, g0_ref, g1_ref, g2_ref, g3_ref, wb_ref, o_ref):
    acc = None
    for n, (o_r, g_r) in enumerate(((oa_ref, g0_ref), (ob_ref, g1_ref), (oc_ref, g2_ref), (od_ref, g3_ref))):
        term = _sigmoid(g_r[...].astype(F32)) * _dot(o_r[...], wb_ref[n])
        acc = term if acc is None else acc + term
    o_ref[...] = acc.astype(o_ref.dtype)


def _merge(o_list, plain, gate_off, wb, *, tm, tn):
    t = plain.shape[0]
    d = wb.shape[-1]
    nj = d // tn
    ospec = pl.BlockSpec((tm, BRANCH_W), lambda i, j: (i, 0))
    gspecs = [pl.BlockSpec((tm, tn), functools.partial(lambda i, j, n: (i, gate_off // tn + n * nj + j), n=n))
              for n in range(N_BRANCH)]
    return pl.pallas_call(
        _merge_body,
        grid=(t // tm, nj),
        in_specs=[ospec] * 4 + gspecs + [pl.BlockSpec((N_BRANCH, BRANCH_W, tn), lambda i, j: (0, 0, j))],
        out_specs=pl.BlockSpec((tm, tn), lambda i, j: (i, j)),
        out_shape=jax.ShapeDtypeStruct((t, d), BF16),
        compiler_params=_cparams(("parallel", "parallel")),
        name="gated_merge",
    )(*o_list, plain, plain, plain, plain, wb)


def _layer_norm(y, g, b):
    mu = jnp.mean(y, axis=-1, keepdims=True)
    yc = y - mu
    var = jnp.mean(yc * yc, axis=-1, keepdims=True)
    return yc * lax.rsqrt(var + LN_EPS) * g + b


def _outproj_ln_body(mg_ref, wo_ref, h_ref, g_ref, b_ref, *rest, with_router):
    if with_router:
        wr_ref, o_ref, ob_ref, lg_ref = rest
    else:
        o_ref, ob_ref = rest
    y = ALPHA * h_ref[...] + _dot(mg_ref[...], wo_ref[...])
    out = _layer_norm(y, g_ref[...], b_ref[...])
    o_ref[...] = out
    ob_ref[...] = out.astype(BF16)
    if with_router:
        hi, lo = _split2(out)
        lg_ref[...] = _dot(hi, wr_ref[0]) + _dot(lo, wr_ref[0]) + _dot(hi, wr_ref[1])


def _outproj_ln(merged, w_out, h, g, b, w_router=None, *, tm):
    t, d = h.shape
    with_router = w_router is not None
    row = lambda i: (i, 0)
    fix = lambda i: (0, 0)
    in_specs = [pl.BlockSpec((tm, d), row), pl.BlockSpec((d, d), fix), pl.BlockSpec((tm, d), row),
                pl.BlockSpec((1, d), fix), pl.BlockSpec((1, d), fix)]
    out_specs = [pl.BlockSpec((tm, d), row), pl.BlockSpec((tm, d), row)]
    out_shape = [jax.ShapeDtypeStruct((t, d), F32), jax.ShapeDtypeStruct((t, d), BF16)]
    args = [merged, w_out, h, g, b]
    if with_router:
        in_specs.append(pl.BlockSpec((2, d, LANES), lambda i: (0, 0, 0)))
        out_specs.append(pl.BlockSpec((tm, LANES), row))
        out_shape.append(jax.ShapeDtypeStruct((t, LANES), F32))
        args.append(w_router)
    return pl.pallas_call(
        functools.partial(_outproj_ln_body, with_router=with_router),
        grid=(t // tm,), in_specs=in_specs, out_specs=out_specs, out_shape=out_shape,
        compiler_params=_cparams(("parallel",)), name="outproj_ln1",
    )(*args)


def _ple_ln_body(hb_ref, h_ref, f_ref, p_ref, wg_ref, wp_ref, g_ref, b_ref, o_ref, ob_ref):
    ple = _sigmoid(_dot(hb_ref[...], wg_ref[...])) * _dot(p_ref[...], wp_ref[...])
    out = _layer_norm(ALPHA * h_ref[...] + f_ref[...] + ple, g_ref[...], b_ref[...])
    o_ref[...] = out
    ob_ref[...] = out.astype(BF16)


def _ple_ln(h_bf, h, f, p_bf, w_gate, w_proj, g, b, *, tm):
    t, d = h.shape
    row = lambda i: (i, 0)
    fix = lambda i: (0, 0)
    return pl.pallas_call(
        _ple_ln_body,
        grid=(t // tm,),
        in_specs=[pl.BlockSpec((tm, d), row), pl.BlockSpec((tm, d), row), pl.BlockSpec((tm, d), row),
                  pl.BlockSpec((tm, PLE_DIM), row), pl.BlockSpec((d, d), fix),
                  pl.BlockSpec((PLE_DIM, d), fix), pl.BlockSpec((1, d), fix), pl.BlockSpec((1, d), fix)],
        out_specs=[pl.BlockSpec((tm, d), row), pl.BlockSpec((tm, d), row)],
        out_shape=[jax.ShapeDtypeStruct((t, d), F32), jax.ShapeDtypeStruct((t, d), BF16)],
        compiler_params=_cparams(("parallel",)), name="ple_ln2",
    )(h_bf, h, f, p_bf, w_gate, w_proj, g, b)


def _swiglu_tile(x, wg, wu, wd):
    g = _dot(x, wg)
    u = _dot(x, wu)
    return _dot((g * _sigmoid(g) * u).astype(BF16), wd)


def _ffn_body(x_ref, wg_ref, wu_ref, wd_ref, o_ref):
    j = pl.program_id(1)
    y = _swiglu_tile(x_ref[...], wg_ref[...], wu_ref[...], wd_ref[...])

    @pl.when(j == 0)
    def _():
        o_ref[...] = y

    @pl.when(j > 0)
    def _():
        o_ref[...] += y


def _ffn(x_bf, wg, wu, wd, *, tm, tf):
    t, d = x_bf.shape
    f = wg.shape[1]
    return pl.pallas_call(
        _ffn_body,
        grid=(t // tm, f // tf),
        in_specs=[pl.BlockSpec((tm, d), lambda i, j: (i, 0)),
                  pl.BlockSpec((d, tf), lambda i, j: (0, j)),
                  pl.BlockSpec((d, tf), lambda i, j: (0, j)),
                  pl.BlockSpec((tf, d), lambda i, j: (j, 0))],
        out_specs=pl.BlockSpec((tm, d), lambda i, j: (i, 0)),
        out_shape=jax.ShapeDtypeStruct((t, d), F32),
        compiler_params=_cparams(("parallel", "arbitrary")), name="ffn_swiglu",
    )(x_bf, wg, wu, wd)


def _moe_ffn_body(te_ref, nt_ref, x_ref, wg_ref, wu_ref, wd_ref, rw_ref, o_ref):
    i = pl.program_id(0)
    j = pl.program_id(1)
    active = i < nt_ref[0]

    @pl.when(active)
    def _():
        y = _swiglu_tile(x_ref[...], wg_ref[0], wu_ref[0], wd_ref[0])

        @pl.when(j == 0)
        def _():
            o_ref[...] = y

        @pl.when(j > 0)
        def _():
            o_ref[...] += y

        @pl.when(j == pl.num_programs(1) - 1)
        def _():
            o_ref[...] = o_ref[...] * rw_ref[:, 0:1]

    @pl.when(jnp.logical_not(active) & (j == 0))
    def _():
        o_ref[...] = jnp.zeros(o_ref.shape, F32)


def _moe_ffn(tile_expert, n_tiles, x_sorted, wg, wu, wd, row_w, *, tm, tf):
    r, d = x_sorted.shape
    f = wg.shape[2]
    nj = f // tf

    def jj(i, j, nt):
        return jnp.where(i < nt[0], j, nj - 1)

    return pl.pallas_call(
        _moe_ffn_body,
        grid_spec=pltpu.PrefetchScalarGridSpec(
            num_scalar_prefetch=2, grid=(r // tm, nj),
            in_specs=[pl.BlockSpec((tm, d), lambda i, j, te, nt: (i, 0)),
                      pl.BlockSpec((1, d, tf), lambda i, j, te, nt: (te[i], 0, jj(i, j, nt))),
                      pl.BlockSpec((1, d, tf), lambda i, j, te, nt: (te[i], 0, jj(i, j, nt))),
                      pl.BlockSpec((1, tf, d), lambda i, j, te, nt: (te[i], jj(i, j, nt), 0)),
                      pl.BlockSpec((tm, LANES), lambda i, j, te, nt: (i, 0))],
            out_specs=pl.BlockSpec((tm, d), lambda i, j, te, nt: (i, 0))),
        out_shape=jax.ShapeDtypeStruct((r, d), F32),
        compiler_params=_cparams(("arbitrary", "arbitrary")), name="moe_grouped_ffn",
    )(tile_expert, n_tiles, x_sorted, wg, wu, wd, row_w)


def _row_copy(src_ref, src_row, dst_ref, dst_row, sem):
    return pltpu.make_async_copy(src_ref.at[pl.ds(src_row, 1)], dst_ref.at[pl.ds(dst_row, 1)], sem)


def _gather_rows_body(idx_ref, src_ref, o_ref, buf_ref, sem, *, tm):
    base = pl.program_id(0) * tm

    def start(r, c):
        _row_copy(src_ref, idx_ref[base + r], buf_ref, r, sem).start()
        return c

    def wait(r, c):
        _row_copy(src_ref, 0, buf_ref, r, sem).wait()
        return c

    lax.fori_loop(0, tm, start, 0)
    lax.fori_loop(0, tm, wait, 0)
    o_ref[...] = buf_ref[...].astype(o_ref.dtype)


def _gather_rows(idx, src, n_rows, out_dtype, *, tm):
    d = src.shape[1]
    return pl.pallas_call(
        functools.partial(_gather_rows_body, tm=tm),
        grid_spec=pltpu.PrefetchScalarGridSpec(
            num_scalar_prefetch=1, grid=(n_rows // tm,),
            in_specs=[pl.BlockSpec(memory_space=pl.ANY)],
            out_specs=pl.BlockSpec((tm, d), lambda i, idx: (i, 0)),
            scratch_shapes=[pltpu.VMEM((tm, d), src.dtype), pltpu.SemaphoreType.DMA(())]),
        out_shape=jax.ShapeDtypeStruct((n_rows, d), out_dtype),
        compiler_params=_cparams(("arbitrary",)), name="moe_gather_rows",
    )(idx, src)


def _combine_body(idx_ref, src_ref, o_ref, a_ref, b_ref, sem, *, tm):
    base = pl.program_id(0) * tm

    def start(r, c):
        _row_copy(src_ref, idx_ref[2 * (base + r)], a_ref, r, sem.at[0]).start()
        _row_copy(src_ref, idx_ref[2 * (base + r) + 1], b_ref, r, sem.at[1]).start()
        return c

    def wait(r, c):
        _row_copy(src_ref, 0, a_ref, r, sem.at[0]).wait()
        _row_copy(src_ref, 0, b_ref, r, sem.at[1]).wait()
        return c

    lax.fori_loop(0, tm, start, 0)
    lax.fori_loop(0, tm, wait, 0)
    o_ref[...] = a_ref[...] + b_ref[...]


def _combine_pairs(pos, y_sorted, n_tokens, *, tm):
    d = y_sorted.shape[1]
    return pl.pallas_call(
        functools.partial(_combine_body, tm=tm),
        grid_spec=pltpu.PrefetchScalarGridSpec(
            num_scalar_prefetch=1, grid=(n_tokens // tm,),
            in_specs=[pl.BlockSpec(memory_space=pl.ANY)],
            out_specs=pl.BlockSpec((tm, d), lambda i, idx: (i, 0)),
            scratch_shapes=[pltpu.VMEM((tm, d), F32), pltpu.VMEM((tm, d), F32),
                            pltpu.SemaphoreType.DMA((2,))]),
        out_shape=jax.ShapeDtypeStruct((n_tokens, d), F32),
        compiler_params=_cparams(("arbitrary",)), name="moe_combine",
    )(pos, y_sorted)


def _col_slices():
    out, off = {}, 0
    for name, width in IN_SPLITS:
        out[name] = (off, width)
        off += width
    return out


def _gather_cols(w, names, pad_to=None):
    cs = _col_slices()
    parts = []
    for n in names:
        col = w[:, cs[n][0]:cs[n][0] + cs[n][1]]
        parts.append(col * Q_FOLD[n] if n in Q_FOLD else col)
    width = sum(cs[n][1] for n in names)
    if pad_to is not None and pad_to > width:
        parts.append(jnp.zeros((w.shape[0], pad_to - width), w.dtype))
    return jnp.concatenate(parts, axis=1).astype(BF16)


def _selection_map_t(n_cmp_pad, n_sel):
    ci = np.arange(n_cmp_pad)[:, None] * NSA_CMP_D
    sj = np.arange(n_sel)[None, :] * NSA_SEL_L
    ov = np.clip(np.minimum(ci + NSA_CMP_L, sj + NSA_SEL_L) - np.maximum(ci, sj), 0, None)
    return np.ascontiguousarray((ov / NSA_CMP_D).astype(np.float32).T)


def _gate_expand_matrices():
    e = np.zeros((4, 3, LANES, LANES), np.float32)
    for j in range(4):
        for hh in range(2):
            for c in range(3):
                e[j, c, (2 * j + hh) * 3 + c, hh * HEAD_DIM:(hh + 1) * HEAD_DIM] = 1.0
    return e


def _split_bits(x, n):
    parts = []
    r = x
    for _ in range(n):
        hi = lax.bitcast_convert_type(
            lax.bitcast_convert_type(r, jnp.uint32) & jnp.uint32(0xFFFF0000), F32)
        parts.append(hi.astype(BF16))
        r = r - hi
    return parts


def _tile(n, pref):
    return pref if n % pref == 0 else n


def _heads_t(x3, lane_off, nheads):
    b, s, _ = x3.shape
    return x3[..., lane_off:lane_off + nheads * HEAD_DIM].reshape(b, s, nheads, HEAD_DIM).transpose(0, 2, 3, 1)


def _qt_units(xt, aug=None):
    b, nh, _, s = xt.shape
    aug = jnp.zeros_like(xt) if aug is None else aug
    return jnp.concatenate([xt, aug], axis=2).reshape(b, nh // 2, 2, LANES, s)


def _k_units(x3, lane_off, nheads, per_unit, tkc, aug=None):
    b, s, _ = x3.shape
    k = x3[..., lane_off:lane_off + nheads * HEAD_DIM].reshape(b, s, nheads, HEAD_DIM)
    aug = jnp.zeros_like(k) if aug is None else aug
    k = jnp.concatenate([k, aug], axis=-1).transpose(0, 2, 1, 3)
    return k.reshape(b, nheads // per_unit, per_unit, s // tkc, tkc, LANES)


def _vt_units(x3, lane_off, nheads, tkc, dup):
    b, s, _ = x3.shape
    v = x3[..., lane_off:lane_off + nheads * HEAD_DIM].reshape(b, s // tkc, tkc, nheads, HEAD_DIM)
    v = v.transpose(0, 3, 1, 4, 2)
    if dup:
        return jnp.concatenate([v, v], axis=3)
    return v.reshape(b, nheads // 2, 2, s // tkc, HEAD_DIM, tkc).transpose(0, 1, 3, 2, 4, 5).reshape(
        b, nheads // 2, s // tkc, LANES, tkc)


def _token_mixer(h, h_bf, layer, b, s, tabs, w_in, cmp_pe, cmp_w1, cmp_b1, cmp_w2, sinks, fox_bf,
                 diff_lambda, diff_gain, w_branch, w_out, ln_g, ln_b, w_router):
    t = b * s
    (cos64, sin64), (cos32, sin32) = tabs
    tm = _tile(t, 1024)
    r64 = _proj(h_bf, _gather_cols(w_in, SEG_ROPE64), BF16, tm, 512, rope=(cos64, sin64, HEAD_DIM // 2))
    r32 = _proj(h_bf, _gather_cols(w_in, SEG_ROPE32), BF16, tm, 512, rope=(cos32, sin32, DIFF_SUB // 2))
    plain = _proj(h_bf, _gather_cols(w_in, SEG_PLAIN), BF16, tm, 512)
    small = _proj(h_bf, _gather_cols(w_in, SEG_SMALL, pad_to=LANES), F32, tm, LANES)
    r64_3, r32_3, plain_3, small_3 = (a.reshape(b, s, -1) for a in (r64, r32, plain, small))
    tq = min(ATT_TQ, s)
    tkc = min(ATT_TKC, s)

    ncp = s // NSA_CMP_D
    n_sel = s // NSA_SEL_L
    topn = min(NSA_TOPN, n_sel)

    def cmp_blocks(x2d):
        c = x2d.reshape(b, s, 2, HEAD_DIM).transpose(0, 2, 1, 3).reshape(b * 2, ncp, NSA_CMP_D * HEAD_DIM)
        nxt = jnp.concatenate([c[:, 1:], jnp.zeros_like(c[:, :1])], axis=1)
        return jnp.concatenate([c, nxt], axis=-1)

    xk = cmp_blocks(r64[:, 512:640])
    xv = cmp_blocks(plain[:, 0:128])
    cmp_kv = _nsa_compress(jnp.stack([xk, xv]), cmp_pe.reshape(2, 1, -1), cmp_w1.astype(BF16),
                           cmp_b1.reshape(2, 1, -1), cmp_w2.astype(BF16))
    kc = cmp_kv[0].astype(BF16).reshape(b, 2, ncp, HEAD_DIM)
    vc = cmp_kv[1].astype(BF16).reshape(b, 2, ncp, HEAD_DIM)
    kc4 = jnp.tile(kc, (1, 1, 1, 4))
    vct = vc.transpose(0, 1, 3, 2)
    selt = jnp.asarray(_selection_map_t(ncp, n_sel), BF16)
    o_cmp, mneg_t = _nsa_cmp_topk(r64_3, 0, kc4, vct, selt, tq=_tile(s, 256), topn=topn)
    qa_t = _heads_t(r64_3, 0, 8)
    sel_aug = jnp.repeat(mneg_t, 4, axis=1)
    onehot = jax.nn.one_hot(jnp.arange(s) // NSA_SEL_L, HEAD_DIM, dtype=BF16)
    ks_aug = jnp.broadcast_to(onehot[None, :, None, :], (b, s, 2, HEAD_DIM))
    o_win = _tattn(_qt_units(qa_t), _k_units(r64_3, 768, 2, 1, tkc), _vt_units(plain_3, 256, 2, tkc, True),
                   kmap=(0, 0), mode="band", window=NSA_WIN, fin="win", name="nsa_window_attention")
    e_mat = jnp.asarray(_gate_expand_matrices(), BF16)
    nsa_specs = [pl.BlockSpec((1, 1, tq, LANES), lambda bi, u, qi: (bi, u // 2, qi, u % 2)),
                 pl.BlockSpec((1, tq, LANES), lambda bi, u, qi: (bi, qi, u)),
                 pl.BlockSpec((1, tq, LANES), lambda bi, u, qi: (bi, qi, 0)),
                 pl.BlockSpec((1, 3, LANES, LANES), lambda bi, u, qi: (u, 0, 0, 0))]
    o_a = _tattn(_qt_units(qa_t, sel_aug), _k_units(r64_3, 640, 2, 1, tkc, ks_aug),
                 _vt_units(plain_3, 128, 2, tkc, True), kmap=(0, 0), mode="causal", fin="nsa",
                 name="nsa_selected_attention", extras=(o_cmp, o_win, small_3, e_mat), extra_specs=nsa_specs)

    o_b = _tattn(_qt_units(_heads_t(r64_3, 896, 8)), _k_units(r64_3, 1408, 2, 1, tkc),
                 _vt_units(plain_3, 384, 2, tkc, True), kmap=(0, 0), mode="band", window=SWA_WIN,
                 fin="swa", name="swa_attention", scalars=sinks.astype(F32) * LOG2E)

    f_logit = (small[:, 24:32] + fox_bf[None, :]).reshape(b, s, 8).transpose(0, 2, 1)
    cum = _cum_log_forget(f_logit.reshape(b * 8, s // LANES, LANES)).reshape(b, 8, s)
    cum3 = jnp.stack(_split_bits(cum * LOG2E, 3), axis=-1).transpose(0, 2, 1, 3)
    ck_aug = jnp.concatenate([cum3, jnp.zeros((b, s, 8, HEAD_DIM - 3), BF16)], axis=-1)
    q_aug = jnp.zeros((HEAD_DIM, s), BF16).at[0:3].set(-1.0)
    o_c = _tattn(_qt_units(_heads_t(plain_3, 512, 8), jnp.broadcast_to(q_aug, (b, 8, HEAD_DIM, s))),
                 _k_units(plain_3, 1024, 8, 2, tkc, ck_aug), _vt_units(plain_3, 1536, 8, tkc, False),
                 kmap=(0, 1), mode="causal", fin="fox", name="fox_attention")

    lam_init = 0.8 - 0.6 * math.exp(-0.3 * layer)
    lf = diff_lambda.astype(F32)
    lam = jnp.exp(jnp.sum(lf[0] * lf[1])) - jnp.exp(jnp.sum(lf[2] * lf[3])) + lam_init
    lam_arr = jnp.stack([lam, jnp.asarray(1.0 - lam_init, F32)]).astype(F32)
    qd_t = _heads_t(r32_3, 0, 8)
    first = (jnp.arange(HEAD_DIM) < DIFF_SUB)[None, None, :, None]
    zero = jnp.zeros_like(qd_t)
    qd_maps = jnp.stack([jnp.where(first, qd_t, zero), jnp.where(first, zero, qd_t)], axis=2)
    qd_units = jnp.concatenate([qd_maps, jnp.zeros_like(qd_maps)], axis=3).reshape(b, 4, 4, LANES, s)
    gain_t = jnp.broadcast_to(jnp.tile(diff_gain.astype(F32), 2)[:, None], (LANES, tq))
    o_d = _tattn(qd_units, _k_units(r32_3, 512, 8, 2, tkc), _vt_units(plain_3, 2048, 8, tkc, False),
                 kmap=(0, 0, 1, 1), mode="causal", fin="diff", name="diff_attention", scalars=lam_arr,
                 extras=(gain_t,), extra_specs=[pl.BlockSpec((LANES, tq), lambda bi, u, qi, sc: (0, 0))])

    o_list = [o.reshape(t, BRANCH_W) for o in (o_a, o_b, o_c, o_d)]
    merged = _merge(o_list, plain, 2560, w_branch.astype(BF16), tm=tm, tn=512)
    return _outproj_ln(merged, w_out.astype(BF16), h, ln_g.reshape(1, -1), ln_b.reshape(1, -1),
                       w_router, tm=_tile(t, 256))


def _moe_layer(h1, logits_pad, b_router, wg, wu, wd, *, tm, tf):
    t, d = h1.shape
    logits = logits_pad[:, :N_EXPERTS] + b_router.astype(F32)[None, :]
    top_v, top_i = lax.top_k(logits, TOP_K)
    top_w = jax.nn.softmax(top_v, axis=-1)
    flat_e = top_i.reshape(-1)
    onehot = jax.nn.one_hot(flat_e, N_EXPERTS, dtype=jnp.int32)
    rank = jnp.sum((jnp.cumsum(onehot, axis=0) - onehot) * onehot, axis=1)
    cnt = jnp.sum(onehot, axis=0)
    padded = ((cnt + tm - 1) // tm) * tm
    ends = jnp.cumsum(padded)
    starts = ends - padded
    pos = (starts[flat_e] + rank).astype(jnp.int32)
    n_rows = TOP_K * t + N_EXPERTS * tm
    row_token = jnp.zeros((n_rows,), jnp.int32).at[pos].set(jnp.arange(TOP_K * t, dtype=jnp.int32) // TOP_K)
    row_w = jnp.zeros((n_rows,), F32).at[pos].set(top_w.reshape(-1))
    tile_start = jnp.arange(n_rows // tm, dtype=jnp.int32) * tm
    tile_expert = jnp.minimum(jnp.sum(tile_start[:, None] >= ends[None, :], axis=1), N_EXPERTS - 1)
    n_tiles = (ends[-1] // tm).astype(jnp.int32).reshape(1)
    x_sorted = _gather_rows(row_token, h1, n_rows, BF16, tm=256)
    y_sorted = _moe_ffn(tile_expert.astype(jnp.int32), n_tiles, x_sorted, wg, wu, wd,
                        jnp.broadcast_to(row_w[:, None], (n_rows, LANES)), tm=tm, tf=tf)
    return _combine_pairs(pos, y_sorted, t, tm=_tile(t, 256))


def kernel(x, p, positions, w_in, nsa_cmp_pe, nsa_cmp_w1, nsa_cmp_b1, nsa_cmp_w2, swa_sinks, fox_bf,
           diff_lambda, diff_gain, w_branch, w_out, ln1_g, ln1_b, ffn_wg, ffn_wu, ffn_wd, moe_router,
           moe_router_b, moe_wg, moe_wu, moe_wd, ple_proj, ple_gate, ln2_g, ln2_b):
    b, s, d = x.shape
    t = b * s
    tabs = (_rope_tabs(positions, HEAD_DIM), _rope_tabs(positions, DIFF_SUB))
    h = x.reshape(t, d).astype(F32)
    h_bf = h.astype(BF16)
    for i in range(DEPTH):
        is_moe = i % 2 == 1
        w_router = None
        if is_moe:
            wr = jnp.zeros((d, LANES), F32).at[:, :N_EXPERTS].set(moe_router[i // 2].astype(F32))
            w_router = jnp.stack(_split_bits(wr, 2))
        res = _token_mixer(h, h_bf, i, b, s, tabs, w_in[i], nsa_cmp_pe[i], nsa_cmp_w1[i], nsa_cmp_b1[i],
                           nsa_cmp_w2[i], swa_sinks[i], fox_bf[i], diff_lambda[i], diff_gain[i],
                           w_branch[i], w_out[i], ln1_g[i], ln1_b[i], w_router)
        h1, h1_bf = res[0], res[1]
        if not is_moe:
            fpad = (-D_FF) % 512
            wg = jnp.pad(ffn_wg[i // 2].astype(BF16), ((0, 0), (0, fpad)))
            wu = jnp.pad(ffn_wu[i // 2].astype(BF16), ((0, 0), (0, fpad)))
            wd = jnp.pad(ffn_wd[i // 2].astype(BF16), ((0, fpad), (0, 0)))
            f = _ffn(h1_bf, wg, wu, wd, tm=_tile(t, 1024), tf=512)
        else:
            f = _moe_layer(h1, res[2], moe_router_b[i // 2], moe_wg[i // 2].astype(BF16),
                           moe_wu[i // 2].astype(BF16), moe_wd[i // 2].astype(BF16),
                           tm=_tile(t, 512), tf=512)
        h, h_bf = _ple_ln(h1_bf, h1, f, p[i].reshape(t, PLE_DIM).astype(BF16), ple_gate[i].astype(BF16),
                          ple_proj[i].astype(BF16), ln2_g[i].reshape(1, -1), ln2_b[i].reshape(1, -1),
                          tm=_tile(t, 256))
    return h.reshape(b, s, d).astype(x.dtype)
```

```python
import functools
import math

import numpy as np
import jax
import jax.numpy as jnp
from jax import lax
from jax.experimental import pallas as pl
from jax.experimental.pallas import tpu as pltpu

F32 = jnp.float32
BF16 = jnp.bfloat16

D_MODEL = 2048
DEPTH = 2
HEAD_DIM = 64
ROPE_THETA = 10000.0
PLE_DIM = 256
LN_EPS = 1e-5
NSA_CMP_L = 32
NSA_CMP_D = 16
NSA_SEL_L = 64
NSA_TOPN = 16
NSA_WIN = 512
NSA_CMP_HIDDEN = 256
NSA_FORCE = 1e9
SWA_WIN = 128
DIFF_SUB = HEAD_DIM // 2
N_BRANCH = 4
BRANCH_W = 8 * HEAD_DIM
D_FF = 5504
N_EXPERTS = 8
TOP_K = 2
D_FF_EXPERT = 7168
ALPHA = (2.0 * DEPTH) ** 0.25

IN_SPLITS = (
    ("a_q", 512), ("a_kc", 128), ("a_vc", 128), ("a_ks", 128), ("a_vs", 128),
    ("a_kw", 128), ("a_vw", 128), ("a_g", 24),
    ("b_q", 512), ("b_k", 128), ("b_v", 128),
    ("c_q", 512), ("c_k", 512), ("c_v", 512), ("c_f", 8),
    ("d_q", 512), ("d_k", 512), ("d_v", 512),
    ("merge_gate", N_BRANCH * D_MODEL),
)
SEG_ROPE64 = ("a_q", "a_kc", "a_ks", "a_kw", "b_q", "b_k")
SEG_ROPE32 = ("d_q", "d_k")
SEG_PLAIN = ("a_vc", "a_vs", "a_vw", "b_v", "c_q", "c_k", "c_v", "d_v", "merge_gate")
SEG_SMALL = ("a_g", "c_f")

LANES = 128
NEG = -1e30
LOG2E = math.log2(math.e)
VMEM_LIMIT = 56 * 1024 * 1024
ATT_TKC = 256
ATT_TQ = 2 * ATT_TKC
ATT_VROWS = 80

Q_FOLD = {"a_q": HEAD_DIM ** -0.5 * LOG2E, "b_q": HEAD_DIM ** -0.5 * LOG2E,
          "c_q": HEAD_DIM ** -0.5 * LOG2E, "d_q": DIFF_SUB ** -0.5 * LOG2E}


def _cparams(sem):
    return pltpu.CompilerParams(dimension_semantics=sem, vmem_limit_bytes=VMEM_LIMIT)


def _sigmoid(x):
    return 1.0 / (1.0 + jnp.exp(-x))


def _dot(a, b):
    return jnp.dot(a, b, preferred_element_type=F32)


def _dot_nt(a, b):
    return lax.dot_general(a, b, (((1,), (1,)), ((), ())), preferred_element_type=F32)


def _split2(x):
    hi = x.astype(BF16)
    lo = (x - hi.astype(F32)).astype(BF16)
    return hi, lo


def _split3(x):
    hi = x.astype(BF16)
    r = x - hi.astype(F32)
    mid = r.astype(BF16)
    lo = (r - mid.astype(F32)).astype(BF16)
    return hi, mid, lo


def _proj_body(x_ref, w_ref, *rest, rope_half):
    if rope_half:
        cos_ref, sin_ref, o_ref = rest
    else:
        (o_ref,) = rest
    acc = _dot(x_ref[...], w_ref[...])
    if not rope_half:
        o_ref[...] = acc.astype(o_ref.dtype)
        return
    cos = cos_ref[...]
    sin = sin_ref[...]
    lane = lax.broadcasted_iota(jnp.int32, cos.shape, 1)
    first = (lane % (2 * rope_half)) < rope_half
    for c in range(acc.shape[1] // LANES):
        a = acc[:, c * LANES:(c + 1) * LANES]
        rot = jnp.where(first, pltpu.roll(a, LANES - rope_half, 1), pltpu.roll(a, rope_half, 1))
        o_ref[:, c * LANES:(c + 1) * LANES] = (a * cos + rot * sin).astype(o_ref.dtype)


def _proj(x, w, out_dtype, tm, tn, rope=None):
    m, k = x.shape
    n = w.shape[1]
    in_specs = [pl.BlockSpec((tm, k), lambda i, j: (i, 0)),
                pl.BlockSpec((k, tn), lambda i, j: (0, j))]
    args = [x, w]
    rope_half = 0
    if rope is not None:
        cos_tab, sin_tab, rope_half = rope
        in_specs += [pl.BlockSpec((tm, LANES), lambda i, j: (i, 0)),
                     pl.BlockSpec((tm, LANES), lambda i, j: (i, 0))]
        args += [cos_tab, sin_tab]
    return pl.pallas_call(
        functools.partial(_proj_body, rope_half=rope_half),
        grid=(m // tm, n // tn),
        in_specs=in_specs,
        out_specs=pl.BlockSpec((tm, tn), lambda i, j: (i, j)),
        out_shape=jax.ShapeDtypeStruct((m, n), out_dtype),
        compiler_params=_cparams(("parallel", "parallel")),
        name="proj_rope" if rope_half else "proj",
    )(*args)


def _rope_tabs(positions, dim):
    inv = 1.0 / (ROPE_THETA ** (jnp.arange(0, dim, 2, dtype=F32) / dim))
    ang = positions.astype(F32).reshape(-1)[:, None] * inv
    c, s = jnp.cos(ang), jnp.sin(ang)
    reps = LANES // dim
    return (jnp.tile(jnp.concatenate([c, c], -1), (1, reps)),
            jnp.tile(jnp.concatenate([-s, s], -1), (1, reps)))


def _tattn_body(*refs, nmaps, kmap, vmap, tq, tkc, mode, window, fin):
    refs = list(refs)
    sc_ref = refs.pop(0) if fin in ("diff", "swa") else None
    qt_ref, k_ref, vt_ref = refs[:3]
    extras, o_ref = refs[3:-4], refs[-4]
    st_ref, m_ref, acc_ref = refs[-3:]
    u = pl.program_id(1)
    qi = pl.program_id(2)
    q0 = qi * tq
    last_chunk = k_ref.shape[3] - 1
    t_pos = q0 + lax.broadcasted_iota(jnp.int32, (tkc, tq), 1)
    key_iota = lax.broadcasted_iota(jnp.int32, (tkc, tq), 0)

    m_ref[...] = jnp.full(m_ref.shape, NEG, F32)
    acc_ref[...] = jnp.zeros(acc_ref.shape, F32)

    def qk(c, buf):
        cc = jnp.minimum(c, last_chunk)
        for mp in range(nmaps):
            st_ref[buf, mp] = _dot(k_ref[0, 0, kmap[mp], cc], qt_ref[0, 0, mp])

    def soft(c, buf, masked):
        for mp in range(nmaps):
            st = st_ref[buf, mp]
            if masked:
                key = c * tkc + key_iota
                keep = key <= t_pos
                if mode == "band":
                    keep = keep & (t_pos - key < window)
                st = jnp.where(keep, st, NEG)
            m_old = m_ref[mp]
            m8 = jnp.max(st.reshape(tkc // 8, 8, tq), axis=0)
            m_new = jnp.maximum(m_old, jnp.max(m8, axis=0, keepdims=True))
            p = jnp.exp2(st - m_new).astype(BF16)
            acc_ref[mp] = jnp.exp2(m_old - m_new) * acc_ref[mp] + _dot(vt_ref[0, 0, vmap[mp], c], p)
            m_ref[mp] = m_new

    def pair(pidx, carry, masked, lookahead=True):
        c0 = 2 * pidx
        qk(c0 + 1, 1)
        soft(c0, 0, masked)
        if lookahead:
            qk(c0 + 2, 0)
        soft(c0 + 1, 1, masked)
        return carry

    lo = 0 if mode == "causal" else jnp.maximum(q0 - (window - 1), 0) // tq
    qk(2 * lo, 0)
    lax.fori_loop(lo, qi, functools.partial(pair, masked=mode == "band"), 0)
    pair(qi, 0, True, lookahead=False)

    def normed(mp):
        acc = acc_ref[mp]
        num, l_i = acc[:HEAD_DIM], acc[HEAD_DIM:HEAD_DIM + 1]
        if fin == "swa":
            m_i = m_ref[mp]
            sk = sc_ref[2 * u + mp]
            m_f = jnp.maximum(m_i, sk)
            corr = jnp.exp2(m_i - m_f)
            return num * (corr / (l_i * corr + jnp.exp2(sk - m_f)))
        return num * (1.0 / l_i)

    if fin == "diff":
        lam = sc_ref[0]
        halves = []
        for hh in range(2):
            o = normed(2 * hh) - lam * normed(2 * hh + 1)
            ms = jnp.mean(o * o, axis=0, keepdims=True)
            halves.append(o * lax.rsqrt(ms + LN_EPS))
        ot = jnp.concatenate(halves, axis=0) * extras[0][...] * sc_ref[1]
    else:
        ot = jnp.concatenate([normed(0), normed(1)], axis=0)
    o = ot.T
    if fin == "nsa":
        ocmp_ref, owin_ref, sm_ref, e_ref = extras
        hi, lo_part = _split2(sm_ref[0])
        gates = [_sigmoid(_dot(hi, e_ref[0, c]) + _dot(lo_part, e_ref[0, c])) for c in range(3)]
        o = gates[0] * ocmp_ref[0, 0] + gates[1] * o + gates[2] * owin_ref[0].astype(F32)
    o_ref[0] = o.astype(o_ref.dtype)


def _tattn(qt, k, vt, *, kmap, vmap, mode, fin, name, window=0, scalars=None, extras=(), extra_specs=()):
    b, nu, nmaps, _, s = qt.shape
    nc, tkc = k.shape[3], k.shape[4]
    tq = 2 * tkc
    vrows = vt.shape[4]
    kdiv = nu // k.shape[1]
    vdiv = nu // vt.shape[1]
    in_specs = [pl.BlockSpec((1, 1, nmaps, LANES, tq), lambda bi, u, qi, *_: (bi, u, 0, 0, qi)),
                pl.BlockSpec((1, 1, k.shape[2], nc, tkc, LANES), lambda bi, u, qi, *_: (bi, u // kdiv, 0, 0, 0, 0)),
                pl.BlockSpec((1, 1, vt.shape[2], nc, vrows, tkc), lambda bi, u, qi, *_: (bi, u // vdiv, 0, 0, 0, 0))]
    in_specs += list(extra_specs)
    body = functools.partial(_tattn_body, nmaps=nmaps, kmap=kmap, vmap=vmap, tq=tq, tkc=tkc, mode=mode,
                             window=window, fin=fin)
    args = ([] if scalars is None else [scalars]) + [qt, k, vt] + list(extras)
    return pl.pallas_call(
        body,
        grid_spec=pltpu.PrefetchScalarGridSpec(
            num_scalar_prefetch=0 if scalars is None else 1, grid=(b, nu, s // tq),
            in_specs=in_specs,
            out_specs=pl.BlockSpec((1, tq, LANES), lambda bi, u, qi, *_: (bi, qi, u)),
            scratch_shapes=[pltpu.VMEM((2, nmaps, tkc, tq), F32), pltpu.VMEM((nmaps, 1, tq), F32),
                            pltpu.VMEM((nmaps, vrows, tq), F32)]),
        out_shape=jax.ShapeDtypeStruct((b, s, nu * LANES), BF16),
        compiler_params=_cparams(("parallel", "parallel", "parallel")),
        name=name,
    )(*args)


def _gelu_tanh(x):
    return 0.5 * x * (1.0 + jnp.tanh(math.sqrt(2.0 / math.pi) * (x + 0.044715 * (x * x * x))))


def _compress_body(x_ref, pe_ref, w1_ref, b1_ref, w2_ref, o_ref):
    x = (x_ref[0, 0].astype(F32) + pe_ref[0]).astype(BF16)
    hid = _gelu_tanh(_dot(x, w1_ref[0]) + b1_ref[0])
    o_ref[0, 0] = _dot(hid.astype(BF16), w2_ref[0])


def _nsa_compress(x, pe, w1, b1, w2):
    _, nb, ncp, ld = x.shape
    hid = w1.shape[-1]
    return pl.pallas_call(
        _compress_body,
        grid=(2, nb),
        in_specs=[pl.BlockSpec((1, 1, ncp, ld), lambda t, i: (t, i, 0, 0)),
                  pl.BlockSpec((1, 1, ld), lambda t, i: (t, 0, 0)),
                  pl.BlockSpec((1, ld, hid), lambda t, i: (t, 0, 0)),
                  pl.BlockSpec((1, 1, hid), lambda t, i: (t, 0, 0)),
                  pl.BlockSpec((1, hid, HEAD_DIM), lambda t, i: (t, 0, 0))],
        out_specs=pl.BlockSpec((1, 1, ncp, HEAD_DIM), lambda t, i: (t, i, 0, 0)),
        out_shape=jax.ShapeDtypeStruct((2, nb, ncp, HEAD_DIM), F32),
        compiler_params=_cparams(("parallel", "parallel")),
        name="nsa_compress",
    )(x, pe, w1, b1, w2)


def _cmp_topk_body(q_ref, kc_ref, vct_ref, selt_ref, o_ref, mt_ref, *, tq, ncp, nsel, topn):
    qi = pl.program_id(2)
    q = q_ref[0]
    kc4 = kc_ref[0, 0]
    vct = vct_ref[0, 0]
    lane = lax.broadcasted_iota(jnp.int32, kc4.shape, 1)
    ci = lax.broadcasted_iota(jnp.int32, (ncp, tq), 0)
    tpos = qi * tq + lax.broadcasted_iota(jnp.int32, (ncp, tq), 1)
    cmask = ci * NSA_CMP_D + (NSA_CMP_L - 1) <= tpos
    psum = jnp.zeros((ncp, tq), F32)
    rows = []
    for a in range(4):
        kcm = jnp.where((lane >= a * HEAD_DIM) & (lane < (a + 1) * HEAD_DIM), kc4, jnp.zeros_like(kc4))
        st = jnp.where(cmask, _dot_nt(kcm, q), NEG)
        m = jnp.max(st, axis=0, keepdims=True)
        e = jnp.where(cmask, jnp.exp2(st - m), 0.0)
        l = jnp.sum(e, axis=0, keepdims=True)
        p = e * jnp.where(l > 0.0, 1.0 / l, 0.0)
        psum = psum + p
        rows.append(_dot(vct, p.astype(BF16)))
    hi, lo = _split2(psum)
    selt = selt_ref[...]
    imp = _dot(selt, hi) + _dot(selt, lo)
    blk = lax.broadcasted_iota(jnp.int32, (nsel, tq), 0)
    cur = (qi * tq + lax.broadcasted_iota(jnp.int32, (nsel, tq), 1)) // NSA_SEL_L
    forced = (blk == 0) | (blk == cur) | (blk == cur - 1)
    imp = jnp.where(forced, NSA_FORCE, jnp.where(blk > cur, -NSA_FORCE, imp))
    cnt = jnp.zeros((nsel, tq), jnp.int32)
    for jp in range(nsel):
        v = imp[jp:jp + 1, :]
        tie = jnp.where(blk > jp, 1, 0)
        cnt = cnt + jnp.where(v > imp, 1, jnp.where(v == imp, tie, 0))
    mneg = jnp.where(cnt < topn, 0.0, NEG)
    if nsel < HEAD_DIM:
        mneg = jnp.concatenate([mneg, jnp.zeros((HEAD_DIM - nsel, tq), F32)], axis=0)
    mt_ref[0, 0] = mneg.astype(mt_ref.dtype)
    o_ref[0, 0] = jnp.concatenate(rows, axis=0).T


def _nsa_cmp_topk(q_arr, q_off256, kc4, vct, selt, *, tq, topn):
    b, s, _ = q_arr.shape
    ncp = kc4.shape[2]
    nsel = selt.shape[0]
    body = functools.partial(_cmp_topk_body, tq=tq, ncp=ncp, nsel=nsel, topn=topn)
    return pl.pallas_call(
        body,
        grid=(b, 2, s // tq),
        in_specs=[pl.BlockSpec((1, tq, 2 * LANES), lambda bi, g, qi: (bi, qi, q_off256 + g)),
                  pl.BlockSpec((1, 1, ncp, 2 * LANES), lambda bi, g, qi: (bi, g, 0, 0)),
                  pl.BlockSpec((1, 1, HEAD_DIM, ncp), lambda bi, g, qi: (bi, g, 0, 0)),
                  pl.BlockSpec((nsel, ncp), lambda bi, g, qi: (0, 0))],
        out_specs=[pl.BlockSpec((1, 1, tq, 2 * LANES), lambda bi, g, qi: (bi, g, qi, 0)),
                   pl.BlockSpec((1, 1, HEAD_DIM, tq), lambda bi, g, qi: (bi, g, 0, qi))],
        out_shape=[jax.ShapeDtypeStruct((b, 2, s, 2 * LANES), F32),
                   jax.ShapeDtypeStruct((b, 2, HEAD_DIM, s), BF16)],
        compiler_params=_cparams(("parallel", "parallel", "parallel")),
        name="nsa_cmp_topk",
    )(q_arr, kc4, vct, selt)


def _cumgate_body(x_ref, o_ref):
    x = x_ref[0]
    r = x.shape[0]
    ls = jnp.minimum(x, 0.0) - jnp.log1p(jnp.exp(-jnp.abs(x)))
    i0 = lax.broadcasted_iota(jnp.int32, (LANES, LANES), 0)
    i1 = lax.broadcasted_iota(jnp.int32, (LANES, LANES), 1)
    upper = jnp.where(i0 <= i1, 1.0, 0.0).astype(BF16)
    ones = jnp.ones((LANES, LANES), BF16)
    r0 = lax.broadcasted_iota(jnp.int32, (r, r), 0)
    r1 = lax.broadcasted_iota(jnp.int32, (r, r), 1)
    strict = jnp.where(r1 < r0, 1.0, 0.0).astype(BF16)
    parts = _split3(ls)
    intra = sum(_dot(pp, upper) for pp in parts)
    rowtot = sum(_dot(pp, ones) for pp in parts)
    off = sum(_dot(strict, pp) for pp in _split3(rowtot))
    o_ref[0] = intra + off


def _cum_log_forget(x):
    n, r, _ = x.shape
    return pl.pallas_call(
        _cumgate_body,
        grid=(n,),
        in_specs=[pl.BlockSpec((1, r, LANES), lambda i: (i, 0, 0))],
        out_specs=pl.BlockSpec((1, r, LANES), lambda i: (i, 0, 0)),
        out_shape=jax.ShapeDtypeStruct((n, r, LANES), F32),
        compiler_params=_cparams(("parallel",)),
        name="cum_log_forget",
    )(x)


def _merge_body(oa_ref, ob_ref, oc_ref, od_ref, g0_ref, g1_ref, g2_ref, g3_ref, wb_ref, o_ref):
    acc = None
    for n, (o_r, g_r) in enumerate(((oa_ref, g0_ref), (ob_ref, g1_ref), (oc_ref, g2_ref), (od_ref, g3_ref))):
        term = _sigmoid(g_r[...].astype(F32)) * _dot(o_r[...], wb_ref[n])
        acc = term if acc is None else acc + term
    o_ref[...] = acc.astype(o_ref.dtype)


def _merge(o_list, plain, gate_off, wb, *, tm, tn):
    t = plain.shape[0]
    d = wb.shape[-1]
    nj = d // tn
    ospec = pl.BlockSpec((tm, BRANCH_W), lambda i, j: (i, 0))
    gspecs = [pl.BlockSpec((tm, tn), functools.partial(lambda i, j, n: (i, gate_off // tn + n * nj + j), n=n))
              for n in range(N_BRANCH)]
    return pl.pallas_call(
        _merge_body,
        grid=(t // tm, nj),
        in_specs=[ospec] * 4 + gspecs + [pl.BlockSpec((N_BRANCH, BRANCH_W, tn), lambda i, j: (0, 0, j))],
        out_specs=pl.BlockSpec((tm, tn), lambda i, j: (i, j)),
        out_shape=jax.ShapeDtypeStruct((t, d), BF16),
        compiler_params=_cparams(("parallel", "parallel")),
        name="gated_merge",
    )(*o_list, plain, plain, plain, plain, wb)


def _layer_norm(y, g, b):
    mu = jnp.mean(y, axis=-1, keepdims=True)
    yc = y - mu
    var = jnp.mean(yc * yc, axis=-1, keepdims=True)
    return yc * lax.rsqrt(var + LN_EPS) * g + b


def _outproj_ln_body(mg_ref, wo_ref, h_ref, g_ref, b_ref, *rest, with_router):
    if with_router:
        wr_ref, o_ref, ob_ref, lg_ref = rest
    else:
        o_ref, ob_ref = rest
    y = ALPHA * h_ref[...] + _dot(mg_ref[...], wo_ref[...])
    out = _layer_norm(y, g_ref[...], b_ref[...])
    o_ref[...] = out
    ob_ref[...] = out.astype(BF16)
    if with_router:
        hi, lo = _split2(out)
        lg_ref[...] = _dot(hi, wr_ref[0]) + _dot(lo, wr_ref[0]) + _dot(hi, wr_ref[1])


def _outproj_ln(merged, w_out, h, g, b, w_router=None, *, tm):
    t, d = h.shape
    with_router = w_router is not None
    row = lambda i: (i, 0)
    fix = lambda i: (0, 0)
    in_specs = [pl.BlockSpec((tm, d), row), pl.BlockSpec((d, d), fix), pl.BlockSpec((tm, d), row),
                pl.BlockSpec((1, d), fix), pl.BlockSpec((1, d), fix)]
    out_specs = [pl.BlockSpec((tm, d), row), pl.BlockSpec((tm, d), row)]
    out_shape = [jax.ShapeDtypeStruct((t, d), F32), jax.ShapeDtypeStruct((t, d), BF16)]
    args = [merged, w_out, h, g, b]
    if with_router:
        in_specs.append(pl.BlockSpec((2, d, LANES), lambda i: (0, 0, 0)))
        out_specs.append(pl.BlockSpec((tm, LANES), row))
        out_shape.append(jax.ShapeDtypeStruct((t, LANES), F32))
        args.append(w_router)
    return pl.pallas_call(
        functools.partial(_outproj_ln_body, with_router=with_router),
        grid=(t // tm,), in_specs=in_specs, out_specs=out_specs, out_shape=out_shape,
        compiler_params=_cparams(("parallel",)), name="outproj_ln1",
    )(*args)


def _ple_ln_body(hb_ref, h_ref, f_ref, p_ref, wg_ref, wp_ref, g_ref, b_ref, o_ref, ob_ref):
    ple = _sigmoid(_dot(hb_ref[...], wg_ref[...])) * _dot(p_ref[...], wp_ref[...])
    out = _layer_norm(ALPHA * h_ref[...] + f_ref[...] + ple, g_ref[...], b_ref[...])
    o_ref[...] = out
    ob_ref[...] = out.astype(BF16)


def _ple_ln(h_bf, h, f, p_bf, w_gate, w_proj, g, b, *, tm):
    t, d = h.shape
    row = lambda i: (i, 0)
    fix = lambda i: (0, 0)
    return pl.pallas_call(
        _ple_ln_body,
        grid=(t // tm,),
        in_specs=[pl.BlockSpec((tm, d), row), pl.BlockSpec((tm, d), row), pl.BlockSpec((tm, d), row),
                  pl.BlockSpec((tm, PLE_DIM), row), pl.BlockSpec((d, d), fix),
                  pl.BlockSpec((PLE_DIM, d), fix), pl.BlockSpec((1, d), fix), pl.BlockSpec((1, d), fix)],
        out_specs=[pl.BlockSpec((tm, d), row), pl.BlockSpec((tm, d), row)],
        out_shape=[jax.ShapeDtypeStruct((t, d), F32), jax.ShapeDtypeStruct((t, d), BF16)],
        compiler_params=_cparams(("parallel",)), name="ple_ln2",
    )(h_bf, h, f, p_bf, w_gate, w_proj, g, b)


def _swiglu_tile(x, wg, wu, wd):
    g = _dot(x, wg)
    u = _dot(x, wu)
    return _dot((g * _sigmoid(g) * u).astype(BF16), wd)


def _ffn_body(x_ref, wg_ref, wu_ref, wd_ref, o_ref):
    j = pl.program_id(1)
    y = _swiglu_tile(x_ref[...], wg_ref[...], wu_ref[...], wd_ref[...])

    @pl.when(j == 0)
    def _():
        o_ref[...] = y

    @pl.when(j > 0)
    def _():
        o_ref[...] += y


def _ffn(x_bf, wg, wu, wd, *, tm, tf):
    t, d = x_bf.shape
    f = wg.shape[1]
    return pl.pallas_call(
        _ffn_body,
        grid=(t // tm, f // tf),
        in_specs=[pl.BlockSpec((tm, d), lambda i, j: (i, 0)),
                  pl.BlockSpec((d, tf), lambda i, j: (0, j)),
                  pl.BlockSpec((d, tf), lambda i, j: (0, j)),
                  pl.BlockSpec((tf, d), lambda i, j: (j, 0))],
        out_specs=pl.BlockSpec((tm, d), lambda i, j: (i, 0)),
        out_shape=jax.ShapeDtypeStruct((t, d), F32),
        compiler_params=_cparams(("parallel", "arbitrary")), name="ffn_swiglu",
    )(x_bf, wg, wu, wd)


def _moe_ffn_body(te_ref, nt_ref, x_ref, wg_ref, wu_ref, wd_ref, rw_ref, o_ref):
    i = pl.program_id(0)
    j = pl.program_id(1)
    active = i < nt_ref[0]

    @pl.when(active)
    def _():
        y = _swiglu_tile(x_ref[...], wg_ref[0], wu_ref[0], wd_ref[0])

        @pl.when(j == 0)
        def _():
            o_ref[...] = y

        @pl.when(j > 0)
        def _():
            o_ref[...] += y

        @pl.when(j == pl.num_programs(1) - 1)
        def _():
            o_ref[...] = o_ref[...] * rw_ref[:, 0:1]

    @pl.when(jnp.logical_not(active) & (j == 0))
    def _():
        o_ref[...] = jnp.zeros(o_ref.shape, F32)


def _moe_ffn(tile_expert, n_tiles, x_sorted, wg, wu, wd, row_w, *, tm, tf):
    r, d = x_sorted.shape
    f = wg.shape[2]
    nj = f // tf

    def jj(i, j, nt):
        return jnp.where(i < nt[0], j, nj - 1)

    return pl.pallas_call(
        _moe_ffn_body,
        grid_spec=pltpu.PrefetchScalarGridSpec(
            num_scalar_prefetch=2, grid=(r // tm, nj),
            in_specs=[pl.BlockSpec((tm, d), lambda i, j, te, nt: (i, 0)),
                      pl.BlockSpec((1, d, tf), lambda i, j, te, nt: (te[i], 0, jj(i, j, nt))),
                      pl.BlockSpec((1, d, tf), lambda i, j, te, nt: (te[i], 0, jj(i, j, nt))),
                      pl.BlockSpec((1, tf, d), lambda i, j, te, nt: (te[i], jj(i, j, nt), 0)),
                      pl.BlockSpec((tm, LANES), lambda i, j, te, nt: (i, 0))],
            out_specs=pl.BlockSpec((tm, d), lambda i, j, te, nt: (i, 0))),
        out_shape=jax.ShapeDtypeStruct((r, d), F32),
        compiler_params=_cparams(("arbitrary", "arbitrary")), name="moe_grouped_ffn",
    )(tile_expert, n_tiles, x_sorted, wg, wu, wd, row_w)


def _row_copy(src_ref, src_row, dst_ref, dst_row, sem):
    return pltpu.make_async_copy(src_ref.at[pl.ds(src_row, 1)], dst_ref.at[pl.ds(dst_row, 1)], sem)


def _gather_rows_body(idx_ref, src_ref, o_ref, buf_ref, sem, *, tm):
    base = pl.program_id(0) * tm

    def start(r, c):
        _row_copy(src_ref, idx_ref[base + r], buf_ref, r, sem).start()
        return c

    def wait(r, c):
        _row_copy(src_ref, 0, buf_ref, r, sem).wait()
        return c

    lax.fori_loop(0, tm, start, 0)
    lax.fori_loop(0, tm, wait, 0)
    o_ref[...] = buf_ref[...].astype(o_ref.dtype)


def _gather_rows(idx, src, n_rows, out_dtype, *, tm):
    d = src.shape[1]
    return pl.pallas_call(
        functools.partial(_gather_rows_body, tm=tm),
        grid_spec=pltpu.PrefetchScalarGridSpec(
            num_scalar_prefetch=1, grid=(n_rows // tm,),
            in_specs=[pl.BlockSpec(memory_space=pl.ANY)],
            out_specs=pl.BlockSpec((tm, d), lambda i, idx: (i, 0)),
            scratch_shapes=[pltpu.VMEM((tm, d), src.dtype), pltpu.SemaphoreType.DMA(())]),
        out_shape=jax.ShapeDtypeStruct((n_rows, d), out_dtype),
        compiler_params=_cparams(("arbitrary",)), name="moe_gather_rows",
    )(idx, src)


def _combine_body(idx_ref, src_ref, o_ref, a_ref, b_ref, sem, *, tm):
    base = pl.program_id(0) * tm

    def start(r, c):
        _row_copy(src_ref, idx_ref[2 * (base + r)], a_ref, r, sem.at[0]).start()
        _row_copy(src_ref, idx_ref[2 * (base + r) + 1], b_ref, r, sem.at[1]).start()
        return c

    def wait(r, c):
        _row_copy(src_ref, 0, a_ref, r, sem.at[0]).wait()
        _row_copy(src_ref, 0, b_ref, r, sem.at[1]).wait()
        return c

    lax.fori_loop(0, tm, start, 0)
    lax.fori_loop(0, tm, wait, 0)
    o_ref[...] = a_ref[...] + b_ref[...]


def _combine_pairs(pos, y_sorted, n_tokens, *, tm):
    d = y_sorted.shape[1]
    return pl.pallas_call(
        functools.partial(_combine_body, tm=tm),
        grid_spec=pltpu.PrefetchScalarGridSpec(
            num_scalar_prefetch=1, grid=(n_tokens // tm,),
            in_specs=[pl.BlockSpec(memory_space=pl.ANY)],
            out_specs=pl.BlockSpec((tm, d), lambda i, idx: (i, 0)),
            scratch_shapes=[pltpu.VMEM((tm, d), F32), pltpu.VMEM((tm, d), F32),
                            pltpu.SemaphoreType.DMA((2,))]),
        out_shape=jax.ShapeDtypeStruct((n_tokens, d), F32),
        compiler_params=_cparams(("arbitrary",)), name="moe_combine",
    )(pos, y_sorted)


def _col_slices():
    out, off = {}, 0
    for name, width in IN_SPLITS:
        out[name] = (off, width)
        off += width
    return out


def _gather_cols(w, names, pad_to=None):
    cs = _col_slices()
    parts = []
    for n in names:
        col = w[:, cs[n][0]:cs[n][0] + cs[n][1]]
        parts.append(col * Q_FOLD[n] if n in Q_FOLD else col)
    width = sum(cs[n][1] for n in names)
    if pad_to is not None and pad_to > width:
        parts.append(jnp.zeros((w.shape[0], pad_to - width), w.dtype))
    return jnp.concatenate(parts, axis=1).astype(BF16)


def _selection_map_t(n_cmp_pad, n_sel):
    ci = np.arange(n_cmp_pad)[:, None] * NSA_CMP_D
    sj = np.arange(n_sel)[None, :] * NSA_SEL_L
    ov = np.clip(np.minimum(ci + NSA_CMP_L, sj + NSA_SEL_L) - np.maximum(ci, sj), 0, None)
    return np.ascontiguousarray((ov / NSA_CMP_D).astype(np.float32).T)


def _gate_expand_matrices():
    e = np.zeros((4, 3, LANES, LANES), np.float32)
    for j in range(4):
        for hh in range(2):
            for c in range(3):
                e[j, c, (2 * j + hh) * 3 + c, hh * HEAD_DIM:(hh + 1) * HEAD_DIM] = 1.0
    return e


def _split_bits(x, n):
    parts = []
    r = x
    for _ in range(n):
        hi = lax.bitcast_convert_type(
            lax.bitcast_convert_type(r, jnp.uint32) & jnp.uint32(0xFFFF0000), F32)
        parts.append(hi.astype(BF16))
        r = r - hi
    return parts


def _tile(n, pref):
    return pref if n % pref == 0 else n


def _heads_t(x3, lane_off, nheads):
    b, s, _ = x3.shape
    return x3[..., lane_off:lane_off + nheads * HEAD_DIM].reshape(b, s, nheads, HEAD_DIM).transpose(0, 2, 3, 1)


def _qt_units(xt, aug=None):
    b, nh, _, s = xt.shape
    aug = jnp.zeros_like(xt) if aug is None else aug
    return jnp.concatenate([xt, aug], axis=2).reshape(b, nh // 2, 2, LANES, s)


def _k_units(x3, lane_off, nheads, per_unit, tkc, aug=None):
    b, s, _ = x3.shape
    k = x3[..., lane_off:lane_off + nheads * HEAD_DIM].reshape(b, s, nheads, HEAD_DIM)
    aug = jnp.zeros_like(k) if aug is None else aug
    k = jnp.concatenate([k, aug], axis=-1).transpose(0, 2, 1, 3)
    return k.reshape(b, nheads // per_unit, per_unit, s // tkc, tkc, LANES)


def _vt_units(x3, lane_off, nheads, per_unit, tkc):
    b, s, _ = x3.shape
    nc = s // tkc
    v = x3[..., lane_off:lane_off + nheads * HEAD_DIM].reshape(b, nc, tkc, nheads, HEAD_DIM)
    v = v.transpose(0, 3, 1, 4, 2)
    ones = jnp.ones((b, nheads, nc, 1, tkc), v.dtype)
    zeros = jnp.zeros((b, nheads, nc, ATT_VROWS - HEAD_DIM - 1, tkc), v.dtype)
    v = jnp.concatenate([v, ones, zeros], axis=3)
    return v.reshape(b, nheads // per_unit, per_unit, nc, ATT_VROWS, tkc)


def _token_mixer(h, h_bf, layer, b, s, tabs, w_in, cmp_pe, cmp_w1, cmp_b1, cmp_w2, sinks, fox_bf,
                 diff_lambda, diff_gain, w_branch, w_out, ln_g, ln_b, w_router):
    t = b * s
    (cos64, sin64), (cos32, sin32) = tabs
    tm = _tile(t, 1024)
    r64 = _proj(h_bf, _gather_cols(w_in, SEG_ROPE64), BF16, tm, 512, rope=(cos64, sin64, HEAD_DIM // 2))
    r32 = _proj(h_bf, _gather_cols(w_in, SEG_ROPE32), BF16, tm, 512, rope=(cos32, sin32, DIFF_SUB // 2))
    plain = _proj(h_bf, _gather_cols(w_in, SEG_PLAIN), BF16, tm, 512)
    small = _proj(h_bf, _gather_cols(w_in, SEG_SMALL, pad_to=LANES), F32, tm, LANES)
    r64_3, r32_3, plain_3, small_3 = (a.reshape(b, s, -1) for a in (r64, r32, plain, small))
    tkc = min(ATT_TKC, s // 2)
    tq = 2 * tkc

    ncp = s // NSA_CMP_D
    n_sel = s // NSA_SEL_L
    topn = min(NSA_TOPN, n_sel)

    def cmp_blocks(x2d):
        c = x2d.reshape(b, s, 2, HEAD_DIM).transpose(0, 2, 1, 3).reshape(b * 2, ncp, NSA_CMP_D * HEAD_DIM)
        nxt = jnp.concatenate([c[:, 1:], jnp.zeros_like(c[:, :1])], axis=1)
        return jnp.concatenate([c, nxt], axis=-1)

    xk = cmp_blocks(r64[:, 512:640])
    xv = cmp_blocks(plain[:, 0:128])
    cmp_kv = _nsa_compress(jnp.stack([xk, xv]), cmp_pe.reshape(2, 1, -1), cmp_w1.astype(BF16),
                           cmp_b1.reshape(2, 1, -1), cmp_w2.astype(BF16))
    kc = cmp_kv[0].astype(BF16).reshape(b, 2, ncp, HEAD_DIM)
    vc = cmp_kv[1].astype(BF16).reshape(b, 2, ncp, HEAD_DIM)
    kc4 = jnp.tile(kc, (1, 1, 1, 4))
    vct = vc.transpose(0, 1, 3, 2)
    selt = jnp.asarray(_selection_map_t(ncp, n_sel), BF16)
    o_cmp, mneg_t = _nsa_cmp_topk(r64_3, 0, kc4, vct, selt, tq=_tile(s, 256), topn=topn)
    qa_t = _heads_t(r64_3, 0, 8)
    sel_aug = jnp.repeat(mneg_t, 4, axis=1)
    onehot = jax.nn.one_hot(jnp.arange(s) // NSA_SEL_L, HEAD_DIM, dtype=BF16)
    ks_aug = jnp.broadcast_to(onehot[None, :, None, :], (b, s, 2, HEAD_DIM))
    o_win = _tattn(_qt_units(qa_t), _k_units(r64_3, 768, 2, 1, tkc), _vt_units(plain_3, 256, 2, 1, tkc),
                   kmap=(0, 0), vmap=(0, 0), mode="band", window=NSA_WIN, fin="win", name="nsa_window_attention")
    e_mat = jnp.asarray(_gate_expand_matrices(), BF16)
    nsa_specs = [pl.BlockSpec((1, 1, tq, LANES), lambda bi, u, qi: (bi, u // 2, qi, u % 2)),
                 pl.BlockSpec((1, tq, LANES), lambda bi, u, qi: (bi, qi, u)),
                 pl.BlockSpec((1, tq, LANES), lambda bi, u, qi: (bi, qi, 0)),
                 pl.BlockSpec((1, 3, LANES, LANES), lambda bi, u, qi: (u, 0, 0, 0))]
    o_a = _tattn(_qt_units(qa_t, sel_aug), _k_units(r64_3, 640, 2, 1, tkc, ks_aug),
                 _vt_units(plain_3, 128, 2, 1, tkc), kmap=(0, 0), vmap=(0, 0), mode="causal", fin="nsa",
                 name="nsa_selected_attention", extras=(o_cmp, o_win, small_3, e_mat), extra_specs=nsa_specs)

    o_b = _tattn(_qt_units(_heads_t(r64_3, 896, 8)), _k_units(r64_3, 1408, 2, 1, tkc),
                 _vt_units(plain_3, 384, 2, 1, tkc), kmap=(0, 0), vmap=(0, 0), mode="band", window=SWA_WIN,
                 fin="swa", name="swa_attention", scalars=sinks.astype(F32) * LOG2E)

    f_logit = (small[:, 24:32] + fox_bf[None, :]).reshape(b, s, 8).transpose(0, 2, 1)
    cum = _cum_log_forget(f_logit.reshape(b * 8, s // LANES, LANES)).reshape(b, 8, s)
    cum3 = jnp.stack(_split_bits(cum * LOG2E, 3), axis=-1).transpose(0, 2, 1, 3)
    ck_aug = jnp.concatenate([cum3, jnp.zeros((b, s, 8, HEAD_DIM - 3), BF16)], axis=-1)
    q_aug = jnp.zeros((HEAD_DIM, s), BF16).at[0:3].set(-1.0)
    o_c = _tattn(_qt_units(_heads_t(plain_3, 512, 8), jnp.broadcast_to(q_aug, (b, 8, HEAD_DIM, s))),
                 _k_units(plain_3, 1024, 8, 2, tkc, ck_aug), _vt_units(plain_3, 1536, 8, 2, tkc),
                 kmap=(0, 1), vmap=(0, 1), mode="causal", fin="fox", name="fox_attention")

    lam_init = 0.8 - 0.6 * math.exp(-0.3 * layer)
    lf = diff_lambda.astype(F32)
    lam = jnp.exp(jnp.sum(lf[0] * lf[1])) - jnp.exp(jnp.sum(lf[2] * lf[3])) + lam_init
    lam_arr = jnp.stack([lam, jnp.asarray(1.0 - lam_init, F32)]).astype(F32)
    qd_t = _heads_t(r32_3, 0, 8)
    first = (jnp.arange(HEAD_DIM) < DIFF_SUB)[None, None, :, None]
    zero = jnp.zeros_like(qd_t)
    qd_maps = jnp.stack([jnp.where(first, qd_t, zero), jnp.where(first, zero, qd_t)], axis=2)
    qd_units = jnp.concatenate([qd_maps, jnp.zeros_like(qd_maps)], axis=3).reshape(b, 4, 4, LANES, s)
    gain_t = jnp.broadcast_to(jnp.tile(diff_gain.astype(F32), 2)[:, None], (LANES, tq))
    o_d = _tattn(qd_units, _k_units(r32_3, 512, 8, 2, tkc), _vt_units(plain_3, 2048, 8, 2, tkc),
                 kmap=(0, 0, 1, 1), vmap=(0, 0, 1, 1), mode="causal", fin="diff", name="diff_attention", scalars=lam_arr,
                 extras=(gain_t,), extra_specs=[pl.BlockSpec((LANES, tq), lambda bi, u, qi, sc: (0, 0))])

    o_list = [o.reshape(t, BRANCH_W) for o in (o_a, o_b, o_c, o_d)]
    merged = _merge(o_list, plain, 2560, w_branch.astype(BF16), tm=tm, tn=512)
    return _outproj_ln(merged, w_out.astype(BF16), h, ln_g.reshape(1, -1), ln_b.reshape(1, -1),
                       w_router, tm=_tile(t, 256))


def _moe_layer(h1, logits_pad, b_router, wg, wu, wd, *, tm, tf):
    t, d = h1.shape
    logits = logits_pad[:, :N_EXPERTS] + b_router.astype(F32)[None, :]
    top_v, top_i = lax.top_k(logits, TOP_K)
    top_w = jax.nn.softmax(top_v, axis=-1)
    flat_e = top_i.reshape(-1)
    onehot = jax.nn.one_hot(flat_e, N_EXPERTS, dtype=jnp.int32)
    rank = jnp.sum((jnp.cumsum(onehot, axis=0) - onehot) * onehot, axis=1)
    cnt = jnp.sum(onehot, axis=0)
    padded = ((cnt + tm - 1) // tm) * tm
    ends = jnp.cumsum(padded)
    starts = ends - padded
    pos = (starts[flat_e] + rank).astype(jnp.int32)
    n_rows = TOP_K * t + N_EXPERTS * tm
    row_token = jnp.zeros((n_rows,), jnp.int32).at[pos].set(jnp.arange(TOP_K * t, dtype=jnp.int32) // TOP_K)
    row_w = jnp.zeros((n_rows,), F32).at[pos].set(top_w.reshape(-1))
    tile_start = jnp.arange(n_rows // tm, dtype=jnp.int32) * tm
    tile_expert = jnp.minimum(jnp.sum(tile_start[:, None] >= ends[None, :], axis=1), N_EXPERTS - 1)
    n_tiles = (ends[-1] // tm).astype(jnp.int32).reshape(1)
    x_sorted = _gather_rows(row_token, h1, n_rows, BF16, tm=256)
    y_sorted = _moe_ffn(tile_expert.astype(jnp.int32), n_tiles, x_sorted, wg, wu, wd,
                        jnp.broadcast_to(row_w[:, None], (n_rows, LANES)), tm=tm, tf=tf)
    return _combine_pairs(pos, y_sorted, t, tm=_tile(t, 256))


def kernel(x, p, positions, w_in, nsa_cmp_pe, nsa_cmp_w1, nsa_cmp_b1, nsa_cmp_w2, swa_sinks, fox_bf,
           diff_lambda, diff_gain, w_branch, w_out, ln1_g, ln1_b, ffn_wg, ffn_wu, ffn_wd, moe_router,
           moe_router_b, moe_wg, moe_wu, moe_wd, ple_proj, ple_gate, ln2_g, ln2_b):
    b, s, d = x.shape
    t = b * s
    tabs = (_rope_tabs(positions, HEAD_DIM), _rope_tabs(positions, DIFF_SUB))
    h = x.reshape(t, d).astype(F32)
    h_bf = h.astype(BF16)
    for i in range(DEPTH):
        is_moe = i % 2 == 1
        w_router = None
        if is_moe:
            wr = jnp.zeros((d, LANES), F32).at[:, :N_EXPERTS].set(moe_router[i // 2].astype(F32))
            w_router = jnp.stack(_split_bits(wr, 2))
        res = _token_mixer(h, h_bf, i, b, s, tabs, w_in[i], nsa_cmp_pe[i], nsa_cmp_w1[i], nsa_cmp_b1[i],
                           nsa_cmp_w2[i], swa_sinks[i], fox_bf[i], diff_lambda[i], diff_gain[i],
                           w_branch[i], w_out[i], ln1_g[i], ln1_b[i], w_router)
        h1, h1_bf = res[0], res[1]
        if not is_moe:
            fpad = (-D_FF) % 512
            wg = jnp.pad(ffn_wg[i // 2].astype(BF16), ((0, 0), (0, fpad)))
            wu = jnp.pad(ffn_wu[i // 2].astype(BF16), ((0, 0), (0, fpad)))
            wd = jnp.pad(ffn_wd[i // 2].astype(BF16), ((0, fpad), (0, 0)))
            f = _ffn(h1_bf, wg, wu, wd, tm=_tile(t, 1024), tf=512)
        else:
            f = _moe_layer(h1, res[2], moe_router_b[i // 2], moe_wg[i // 2].astype(BF16),
                           moe_wu[i // 2].astype(BF16), moe_wd[i // 2].astype(BF16),
                           tm=_tile(t, 512), tf=1024)
        h, h_bf = _ple_ln(h1_bf, h1, f, p[i].reshape(t, PLE_DIM).astype(BF16), ple_gate[i].astype(BF16),
                          ple_proj[i].astype(BF16), ln2_g[i].reshape(1, -1), ln2_b[i].reshape(1, -1),
                          tm=_tile(t, 256))
    return h.reshape(b, s, d).astype(x.dtype)
```

```python
import functools
import math

import numpy as np
import jax
import jax.numpy as jnp
from jax import lax
from jax.experimental import pallas as pl
from jax.experimental.pallas import tpu as pltpu

F32 = jnp.float32
BF16 = jnp.bfloat16

D_MODEL = 2048
DEPTH = 2
HEAD_DIM = 64
ROPE_THETA = 10000.0
PLE_DIM = 256
LN_EPS = 1e-5
NSA_CMP_L = 32
NSA_CMP_D = 16
NSA_SEL_L = 64
NSA_TOPN = 16
NSA_WIN = 512
NSA_CMP_HIDDEN = 256
NSA_FORCE = 1e9
SWA_WIN = 128
DIFF_SUB = HEAD_DIM // 2
N_BRANCH = 4
BRANCH_W = 8 * HEAD_DIM
D_FF = 5504
N_EXPERTS = 8
TOP_K = 2
D_FF_EXPERT = 7168
ALPHA = (2.0 * DEPTH) ** 0.25

IN_SPLITS = (
    ("a_q", 512), ("a_kc", 128), ("a_vc", 128), ("a_ks", 128), ("a_vs", 128),
    ("a_kw", 128), ("a_vw", 128), ("a_g", 24),
    ("b_q", 512), ("b_k", 128), ("b_v", 128),
    ("c_q", 512), ("c_k", 512), ("c_v", 512), ("c_f", 8),
    ("d_q", 512), ("d_k", 512), ("d_v", 512),
    ("merge_gate", N_BRANCH * D_MODEL),
)
SEG_ROPE64 = ("a_q", "a_kc", "a_ks", "a_kw", "b_q", "b_k")
SEG_ROPE32 = ("d_q", "d_k")
SEG_PLAIN = ("a_vc", "a_vs", "a_vw", "b_v", "c_q", "c_k", "c_v", "d_v", "merge_gate")
SEG_SMALL = ("a_g", "c_f")

LANES = 128
NEG = -1e30
LOG2E = math.log2(math.e)
VMEM_LIMIT = 56 * 1024 * 1024
ATT_TKC = 256
ATT_TQ = 2 * ATT_TKC
ATT_VROWS = 80

Q_FOLD = {"a_q": HEAD_DIM ** -0.5 * LOG2E, "b_q": HEAD_DIM ** -0.5 * LOG2E,
          "c_q": HEAD_DIM ** -0.5 * LOG2E, "d_q": DIFF_SUB ** -0.5 * LOG2E}


def _cparams(sem):
    return pltpu.CompilerParams(dimension_semantics=sem, vmem_limit_bytes=VMEM_LIMIT)


def _sigmoid(x):
    return 1.0 / (1.0 + jnp.exp(-x))


def _dot(a, b):
    return jnp.dot(a, b, preferred_element_type=F32)


def _dot_nt(a, b):
    return lax.dot_general(a, b, (((1,), (1,)), ((), ())), preferred_element_type=F32)


def _split2(x):
    hi = x.astype(BF16)
    lo = (x - hi.astype(F32)).astype(BF16)
    return hi, lo


def _split3(x):
    hi = x.astype(BF16)
    r = x - hi.astype(F32)
    mid = r.astype(BF16)
    lo = (r - mid.astype(F32)).astype(BF16)
    return hi, mid, lo


def _proj_body(x_ref, w_ref, *rest, rope_half):
    if rope_half:
        cos_ref, sin_ref, o_ref = rest
    else:
        (o_ref,) = rest
    acc = _dot(x_ref[...], w_ref[...])
    if not rope_half:
        o_ref[...] = acc.astype(o_ref.dtype)
        return
    cos = cos_ref[...]
    sin = sin_ref[...]
    lane = lax.broadcasted_iota(jnp.int32, cos.shape, 1)
    first = (lane % (2 * rope_half)) < rope_half
    for c in range(acc.shape[1] // LANES):
        a = acc[:, c * LANES:(c + 1) * LANES]
        rot = jnp.where(first, pltpu.roll(a, LANES - rope_half, 1), pltpu.roll(a, rope_half, 1))
        o_ref[:, c * LANES:(c + 1) * LANES] = (a * cos + rot * sin).astype(o_ref.dtype)


def _proj(x, w, out_dtype, tm, tn, rope=None):
    m, k = x.shape
    n = w.shape[1]
    in_specs = [pl.BlockSpec((tm, k), lambda i, j: (i, 0)),
                pl.BlockSpec((k, tn), lambda i, j: (0, j))]
    args = [x, w]
    rope_half = 0
    if rope is not None:
        cos_tab, sin_tab, rope_half = rope
        in_specs += [pl.BlockSpec((tm, LANES), lambda i, j: (i, 0)),
                     pl.BlockSpec((tm, LANES), lambda i, j: (i, 0))]
        args += [cos_tab, sin_tab]
    return pl.pallas_call(
        functools.partial(_proj_body, rope_half=rope_half),
        grid=(m // tm, n // tn),
        in_specs=in_specs,
        out_specs=pl.BlockSpec((tm, tn), lambda i, j: (i, j)),
        out_shape=jax.ShapeDtypeStruct((m, n), out_dtype),
        compiler_params=_cparams(("parallel", "parallel")),
        name="proj_rope" if rope_half else "proj",
    )(*args)


def _rope_tabs(positions, dim):
    inv = 1.0 / (ROPE_THETA ** (jnp.arange(0, dim, 2, dtype=F32) / dim))
    ang = positions.astype(F32).reshape(-1)[:, None] * inv
    c, s = jnp.cos(ang), jnp.sin(ang)
    reps = LANES // dim
    return (jnp.tile(jnp.concatenate([c, c], -1), (1, reps)),
            jnp.tile(jnp.concatenate([-s, s], -1), (1, reps)))


def _tattn_body(*refs, nmaps, qsel, vmap, tq, tkc, mode, window, bias, fin):
    refs = list(refs)
    sc_ref = refs.pop(0) if fin in ("diff", "swa") else None
    q_ref, k_ref, vt_ref = refs[:3]
    extras, o_ref = refs[3:-5], refs[-5]
    qm_ref, st_ref, m_ref, acc_ref = refs[-4:]
    bias_ref = None
    if bias is not None:
        bias_ref, extras = extras[0], extras[1:]
    u = pl.program_id(1)
    qi = pl.program_id(2)
    q0 = qi * tq
    last_chunk = k_ref.shape[1] // tkc - 1
    t_pos = q0 + lax.broadcasted_iota(jnp.int32, (tkc, tq), 1)
    key_iota = lax.broadcasted_iota(jnp.int32, (tkc, tq), 0)

    lane = lax.broadcasted_iota(jnp.int32, (tq, LANES), 1)
    if qsel == "gqa":
        lo_lane = (u // 2) * HEAD_DIM
        q = q_ref[0].astype(F32)
        q_rolled = pltpu.roll(q, HEAD_DIM, 1)
        in_group = (lane >= lo_lane) & (lane < lo_lane + HEAD_DIM)
        for e in range(2):
            q_e = jnp.where(lo_lane == e * HEAD_DIM, q, q_rolled)
            qm_ref[e] = jnp.where(in_group, q_e, 0.0).astype(BF16)
    else:
        q = q_ref[0]
        width = LANES // nmaps
        for mp in range(nmaps):
            qm_ref[mp] = jnp.where((lane >= mp * width) & (lane < (mp + 1) * width), q, jnp.zeros_like(q))

    m_ref[...] = jnp.full(m_ref.shape, NEG, F32)
    acc_ref[...] = jnp.zeros(acc_ref.shape, F32)

    def qk(c, buf):
        cc = jnp.minimum(c, last_chunk)
        kc = k_ref[0, pl.ds(pl.multiple_of(cc * tkc, tkc), tkc), :]
        for mp in range(nmaps):
            st_ref[buf, mp] = _dot_nt(kc, qm_ref[mp])

    def soft(c, buf, masked):
        off = pl.multiple_of(c * tkc, tkc)
        if bias == "nsa":
            nb = tkc // NSA_SEL_L
            sel = jnp.concatenate(
                [jnp.broadcast_to(bias_ref[0, 0, pl.ds(c * nb + i, 1), :], (NSA_SEL_L, tq)) for i in range(nb)],
                axis=0)
        for mp in range(nmaps):
            st = st_ref[buf, mp]
            if bias == "fox":
                ck = bias_ref[0, mp, pl.ds(off, tkc), :]
                st = st - jnp.concatenate([ck] * (tq // LANES), axis=1)
            elif bias == "nsa":
                st = st + sel
            if masked:
                key = c * tkc + key_iota
                keep = key <= t_pos
                if mode == "band":
                    keep = keep & (t_pos - key < window)
                st = jnp.where(keep, st, NEG)
            m_old = m_ref[mp]
            m8 = jnp.max(st.reshape(tkc // 8, 8, tq), axis=0)
            m_new = jnp.maximum(m_old, jnp.max(m8, axis=0, keepdims=True))
            p = jnp.exp2(st - m_new).astype(BF16)
            acc_ref[mp] = jnp.exp2(m_old - m_new) * acc_ref[mp] + _dot(vt_ref[0, vmap[mp], c], p)
            m_ref[mp] = m_new

    def pair(pidx, carry, masked, lookahead=True):
        c0 = 2 * pidx
        qk(c0 + 1, 1)
        soft(c0, 0, masked)
        if lookahead:
            qk(c0 + 2, 0)
        soft(c0 + 1, 1, masked)
        return carry

    lo = 0 if mode == "causal" else jnp.maximum(q0 - (window - 1), 0) // tq
    qk(2 * lo, 0)
    lax.fori_loop(lo, qi, functools.partial(pair, masked=mode == "band"), 0)
    pair(qi, 0, True, lookahead=False)

    def normed(mp):
        acc = acc_ref[mp]
        num, l_i = acc[:HEAD_DIM], acc[HEAD_DIM:HEAD_DIM + 1]
        if fin == "swa":
            m_i = m_ref[mp]
            sk = sc_ref[2 * u + mp]
            m_f = jnp.maximum(m_i, sk)
            corr = jnp.exp2(m_i - m_f)
            return num * (corr / (l_i * corr + jnp.exp2(sk - m_f)))
        return num * (1.0 / l_i)

    if fin == "diff":
        lam = sc_ref[0]
        halves = []
        for hh in range(2):
            o = normed(2 * hh) - lam * normed(2 * hh + 1)
            ms = jnp.mean(o * o, axis=0, keepdims=True)
            halves.append(o * lax.rsqrt(ms + LN_EPS))
        ot = jnp.concatenate(halves, axis=0) * extras[0][...] * sc_ref[1]
    else:
        ot = jnp.concatenate([normed(0), normed(1)], axis=0)
    o = ot.T
    if fin == "nsa":
        ocmp_ref, owin_ref, sm_ref, e_ref = extras
        hi, lo_part = _split2(sm_ref[0])
        gates = [_sigmoid(_dot(hi, e_ref[0, c]) + _dot(lo_part, e_ref[0, c])) for c in range(3)]
        o = gates[0] * ocmp_ref[0, 0] + gates[1] * o + gates[2] * owin_ref[0].astype(F32)
    o_ref[0] = o.astype(o_ref.dtype)


def _tattn(q_arr, q_blk, k_arr, k_blk, vt_all, v_head, *, nmaps, qsel, vmap, mode, fin, name, window=0,
           bias=None, scalars=None, extras=(), extra_specs=()):
    b, s, _ = q_arr.shape
    nc, vrows, tkc = vt_all.shape[2], vt_all.shape[3], vt_all.shape[4]
    tq = 2 * tkc
    nu = 4
    if qsel == "gqa":
        kspec = pl.BlockSpec((1, s, LANES), lambda bi, u, qi, *_: (bi, 0, k_blk))
        vspec = pl.BlockSpec((1, 1, nc, vrows, tkc), lambda bi, u, qi, *_: (bi, v_head + u // 2, 0, 0, 0))
    else:
        kspec = pl.BlockSpec((1, s, LANES), lambda bi, u, qi, *_: (bi, 0, k_blk + u))
        vspec = pl.BlockSpec((1, 2, nc, vrows, tkc), lambda bi, u, qi, *_: (bi, v_head // 2 + u, 0, 0, 0))
    in_specs = [pl.BlockSpec((1, tq, LANES), lambda bi, u, qi, *_: (bi, qi, q_blk + u)), kspec, vspec]
    in_specs += list(extra_specs)
    body = functools.partial(_tattn_body, nmaps=nmaps, qsel=qsel, vmap=vmap, tq=tq, tkc=tkc, mode=mode,
                             window=window, bias=bias, fin=fin)
    args = ([] if scalars is None else [scalars]) + [q_arr, k_arr, vt_all] + list(extras)
    return pl.pallas_call(
        body,
        grid_spec=pltpu.PrefetchScalarGridSpec(
            num_scalar_prefetch=0 if scalars is None else 1, grid=(b, nu, s // tq),
            in_specs=in_specs,
            out_specs=pl.BlockSpec((1, tq, LANES), lambda bi, u, qi, *_: (bi, qi, u)),
            scratch_shapes=[pltpu.VMEM((nmaps, tq, LANES), BF16), pltpu.VMEM((2, nmaps, tkc, tq), F32),
                            pltpu.VMEM((nmaps, 1, tq), F32), pltpu.VMEM((nmaps, vrows, tq), F32)]),
        out_shape=jax.ShapeDtypeStruct((b, s, nu * LANES), BF16),
        compiler_params=_cparams(("parallel", "parallel", "parallel")),
        name=name,
    )(*args)


def _gelu_tanh(x):
    return 0.5 * x * (1.0 + jnp.tanh(math.sqrt(2.0 / math.pi) * (x + 0.044715 * (x * x * x))))


def _compress_body(x_ref, pe_ref, w1_ref, b1_ref, w2_ref, o_ref):
    x = (x_ref[0, 0].astype(F32) + pe_ref[0]).astype(BF16)
    hid = _gelu_tanh(_dot(x, w1_ref[0]) + b1_ref[0])
    o_ref[0, 0] = _dot(hid.astype(BF16), w2_ref[0])


def _nsa_compress(x, pe, w1, b1, w2):
    _, nb, ncp, ld = x.shape
    hid = w1.shape[-1]
    return pl.pallas_call(
        _compress_body,
        grid=(2, nb),
        in_specs=[pl.BlockSpec((1, 1, ncp, ld), lambda t, i: (t, i, 0, 0)),
                  pl.BlockSpec((1, 1, ld), lambda t, i: (t, 0, 0)),
                  pl.BlockSpec((1, ld, hid), lambda t, i: (t, 0, 0)),
                  pl.BlockSpec((1, 1, hid), lambda t, i: (t, 0, 0)),
                  pl.BlockSpec((1, hid, HEAD_DIM), lambda t, i: (t, 0, 0))],
        out_specs=pl.BlockSpec((1, 1, ncp, HEAD_DIM), lambda t, i: (t, i, 0, 0)),
        out_shape=jax.ShapeDtypeStruct((2, nb, ncp, HEAD_DIM), F32),
        compiler_params=_cparams(("parallel", "parallel")),
        name="nsa_compress",
    )(x, pe, w1, b1, w2)


def _cmp_topk_body(q_ref, kc_ref, vct_ref, selt_ref, o_ref, mt_ref, *, tq, ncp, nsel, topn):
    qi = pl.program_id(2)
    q = q_ref[0]
    kc4 = kc_ref[0, 0]
    vct = vct_ref[0, 0]
    lane = lax.broadcasted_iota(jnp.int32, kc4.shape, 1)
    ci = lax.broadcasted_iota(jnp.int32, (ncp, tq), 0)
    tpos = qi * tq + lax.broadcasted_iota(jnp.int32, (ncp, tq), 1)
    cmask = ci * NSA_CMP_D + (NSA_CMP_L - 1) <= tpos
    psum = jnp.zeros((ncp, tq), F32)
    rows = []
    for a in range(4):
        kcm = jnp.where((lane >= a * HEAD_DIM) & (lane < (a + 1) * HEAD_DIM), kc4, jnp.zeros_like(kc4))
        st = jnp.where(cmask, _dot_nt(kcm, q), NEG)
        m = jnp.max(st, axis=0, keepdims=True)
        e = jnp.where(cmask, jnp.exp2(st - m), 0.0)
        l = jnp.sum(e, axis=0, keepdims=True)
        p = e * jnp.where(l > 0.0, 1.0 / l, 0.0)
        psum = psum + p
        rows.append(_dot(vct, p.astype(BF16)))
    hi, lo = _split2(psum)
    selt = selt_ref[...]
    imp = _dot(selt, hi) + _dot(selt, lo)
    blk = lax.broadcasted_iota(jnp.int32, (nsel, tq), 0)
    cur = (qi * tq + lax.broadcasted_iota(jnp.int32, (nsel, tq), 1)) // NSA_SEL_L
    forced = (blk == 0) | (blk == cur) | (blk == cur - 1)
    imp = jnp.where(forced, NSA_FORCE, jnp.where(blk > cur, -NSA_FORCE, imp))
    cnt = jnp.zeros((nsel, tq), jnp.int32)
    for jp in range(nsel):
        v = imp[jp:jp + 1, :]
        tie = jnp.where(blk > jp, 1, 0)
        cnt = cnt + jnp.where(v > imp, 1, jnp.where(v == imp, tie, 0))
    mneg = jnp.where(cnt < topn, 0.0, NEG)
    if nsel < HEAD_DIM:
        mneg = jnp.concatenate([mneg, jnp.zeros((HEAD_DIM - nsel, tq), F32)], axis=0)
    mt_ref[0, 0] = mneg.astype(mt_ref.dtype)
    o_ref[0, 0] = jnp.concatenate(rows, axis=0).T


def _nsa_cmp_topk(q_arr, q_off256, kc4, vct, selt, *, tq, topn):
    b, s, _ = q_arr.shape
    ncp = kc4.shape[2]
    nsel = selt.shape[0]
    body = functools.partial(_cmp_topk_body, tq=tq, ncp=ncp, nsel=nsel, topn=topn)
    return pl.pallas_call(
        body,
        grid=(b, 2, s // tq),
        in_specs=[pl.BlockSpec((1, tq, 2 * LANES), lambda bi, g, qi: (bi, qi, q_off256 + g)),
                  pl.BlockSpec((1, 1, ncp, 2 * LANES), lambda bi, g, qi: (bi, g, 0, 0)),
                  pl.BlockSpec((1, 1, HEAD_DIM, ncp), lambda bi, g, qi: (bi, g, 0, 0)),
                  pl.BlockSpec((nsel, ncp), lambda bi, g, qi: (0, 0))],
        out_specs=[pl.BlockSpec((1, 1, tq, 2 * LANES), lambda bi, g, qi: (bi, g, qi, 0)),
                   pl.BlockSpec((1, 1, HEAD_DIM, tq), lambda bi, g, qi: (bi, g, 0, qi))],
        out_shape=[jax.ShapeDtypeStruct((b, 2, s, 2 * LANES), F32),
                   jax.ShapeDtypeStruct((b, 2, HEAD_DIM, s), F32)],
        compiler_params=_cparams(("parallel", "parallel", "parallel")),
        name="nsa_cmp_topk",
    )(q_arr, kc4, vct, selt)


def _cumgate_body(x_ref, o_ref):
    x = x_ref[0]
    r = x.shape[0]
    ls = jnp.minimum(x, 0.0) - jnp.log1p(jnp.exp(-jnp.abs(x)))
    i0 = lax.broadcasted_iota(jnp.int32, (LANES, LANES), 0)
    i1 = lax.broadcasted_iota(jnp.int32, (LANES, LANES), 1)
    upper = jnp.where(i0 <= i1, 1.0, 0.0).astype(BF16)
    ones = jnp.ones((LANES, LANES), BF16)
    r0 = lax.broadcasted_iota(jnp.int32, (r, r), 0)
    r1 = lax.broadcasted_iota(jnp.int32, (r, r), 1)
    strict = jnp.where(r1 < r0, 1.0, 0.0).astype(BF16)
    parts = _split3(ls)
    intra = sum(_dot(pp, upper) for pp in parts)
    rowtot = sum(_dot(pp, ones) for pp in parts)
    off = sum(_dot(strict, pp) for pp in _split3(rowtot))
    o_ref[0] = intra + off


def _cum_log_forget(x):
    n, r, _ = x.shape
    return pl.pallas_call(
        _cumgate_body,
        grid=(n,),
        in_specs=[pl.BlockSpec((1, r, LANES), lambda i: (i, 0, 0))],
        out_specs=pl.BlockSpec((1, r, LANES), lambda i: (i, 0, 0)),
        out_shape=jax.ShapeDtypeStruct((n, r, LANES), F32),
        compiler_params=_cparams(("parallel",)),
        name="cum_log_forget",
    )(x)


def _merge_body(oa_ref, ob_ref, oc_ref, od_ref, g0_ref, g1_ref, g2_ref, g3_ref, wb_ref, o_ref):
    acc = None
    for n, (o_r, g_r) in enumerate(((oa_ref, g0_ref), (ob_ref, g1_ref), (oc_ref, g2_ref), (od_ref, g3_ref))):
        term = _sigmoid(g_r[...].astype(F32)) * _dot(o_r[...], wb_ref[n])
        acc = term if acc is None else acc + term
    o_ref[...] = acc.astype(o_ref.dtype)


def _merge(o_list, plain, gate_off, wb, *, tm, tn):
    t = plain.shape[0]
    d = wb.shape[-1]
    nj = d // tn
    ospec = pl.BlockSpec((tm, BRANCH_W), lambda i, j: (i, 0))
    gspecs = [pl.BlockSpec((tm, tn), functools.partial(lambda i, j, n: (i, gate_off // tn + n * nj + j), n=n))
              for n in range(N_BRANCH)]
    return pl.pallas_call(
        _merge_body,
        grid=(t // tm, nj),
        in_specs=[ospec] * 4 + gspecs + [pl.BlockSpec((N_BRANCH, BRANCH_W, tn), lambda i, j: (0, 0, j))],
        out_specs=pl.BlockSpec((tm, tn), lambda i, j: (i, j)),
        out_shape=jax.ShapeDtypeStruct((t, d), BF16),
        compiler_params=_cparams(("parallel", "parallel")),
        name="gated_merge",
    )(*o_list, plain, plain, plain, plain, wb)


def _layer_norm(y, g, b):
    mu = jnp.mean(y, axis=-1, keepdims=True)
    yc = y - mu
    var = jnp.mean(yc * yc, axis=-1, keepdims=True)
    return yc * lax.rsqrt(var + LN_EPS) * g + b


def _outproj_ln_body(mg_ref, wo_ref, h_ref, g_ref, b_ref, *rest, with_router):
    if with_router:
        wr_ref, o_ref, ob_ref, lg_ref = rest
    else:
        o_ref, ob_ref = rest
    y = ALPHA * h_ref[...] + _dot(mg_ref[...], wo_ref[...])
    out = _layer_norm(y, g_ref[...], b_ref[...])
    o_ref[...] = out
    ob_ref[...] = out.astype(BF16)
    if with_router:
        hi, lo = _split2(out)
        lg_ref[...] = _dot(hi, wr_ref[0]) + _dot(lo, wr_ref[0]) + _dot(hi, wr_ref[1])


def _outproj_ln(merged, w_out, h, g, b, w_router=None, *, tm):
    t, d = h.shape
    with_router = w_router is not None
    row = lambda i: (i, 0)
    fix = lambda i: (0, 0)
    in_specs = [pl.BlockSpec((tm, d), row), pl.BlockSpec((d, d), fix), pl.BlockSpec((tm, d), row),
                pl.BlockSpec((1, d), fix), pl.BlockSpec((1, d), fix)]
    out_specs = [pl.BlockSpec((tm, d), row), pl.BlockSpec((tm, d), row)]
    out_shape = [jax.ShapeDtypeStruct((t, d), F32), jax.ShapeDtypeStruct((t, d), BF16)]
    args = [merged, w_out, h, g, b]
    if with_router:
        in_specs.append(pl.BlockSpec((2, d, LANES), lambda i: (0, 0, 0)))
        out_specs.append(pl.BlockSpec((tm, LANES), row))
        out_shape.append(jax.ShapeDtypeStruct((t, LANES), F32))
        args.append(w_router)
    return pl.pallas_call(
        functools.partial(_outproj_ln_body, with_router=with_router),
        grid=(t // tm,), in_specs=in_specs, out_specs=out_specs, out_shape=out_shape,
        compiler_params=_cparams(("parallel",)), name="outproj_ln1",
    )(*args)


def _ple_ln_body(hb_ref, h_ref, f_ref, p_ref, wg_ref, wp_ref, g_ref, b_ref, o_ref, ob_ref):
    ple = _sigmoid(_dot(hb_ref[...], wg_ref[...])) * _dot(p_ref[...], wp_ref[...])
    out = _layer_norm(ALPHA * h_ref[...] + f_ref[...] + ple, g_ref[...], b_ref[...])
    o_ref[...] = out
    ob_ref[...] = out.astype(BF16)


def _ple_ln(h_bf, h, f, p_bf, w_gate, w_proj, g, b, *, tm):
    t, d = h.shape
    row = lambda i: (i, 0)
    fix = lambda i: (0, 0)
    return pl.pallas_call(
        _ple_ln_body,
        grid=(t // tm,),
        in_specs=[pl.BlockSpec((tm, d), row), pl.BlockSpec((tm, d), row), pl.BlockSpec((tm, d), row),
                  pl.BlockSpec((tm, PLE_DIM), row), pl.BlockSpec((d, d), fix),
                  pl.BlockSpec((PLE_DIM, d), fix), pl.BlockSpec((1, d), fix), pl.BlockSpec((1, d), fix)],
        out_specs=[pl.BlockSpec((tm, d), row), pl.BlockSpec((tm, d), row)],
        out_shape=[jax.ShapeDtypeStruct((t, d), F32), jax.ShapeDtypeStruct((t, d), BF16)],
        compiler_params=_cparams(("parallel",)), name="ple_ln2",
    )(h_bf, h, f, p_bf, w_gate, w_proj, g, b)


def _swiglu_tile(x, wg, wu, wd):
    g = _dot(x, wg)
    u = _dot(x, wu)
    return _dot((g * _sigmoid(g) * u).astype(BF16), wd)


def _ffn_body(x_ref, wg_ref, wu_ref, wd_ref, o_ref):
    j = pl.program_id(1)
    y = _swiglu_tile(x_ref[...], wg_ref[...], wu_ref[...], wd_ref[...])

    @pl.when(j == 0)
    def _():
        o_ref[...] = y

    @pl.when(j > 0)
    def _():
        o_ref[...] += y


def _ffn(x_bf, wg, wu, wd, *, tm, tf):
    t, d = x_bf.shape
    f = wg.shape[1]
    return pl.pallas_call(
        _ffn_body,
        grid=(t // tm, f // tf),
        in_specs=[pl.BlockSpec((tm, d), lambda i, j: (i, 0)),
                  pl.BlockSpec((d, tf), lambda i, j: (0, j)),
                  pl.BlockSpec((d, tf), lambda i, j: (0, j)),
                  pl.BlockSpec((tf, d), lambda i, j: (j, 0))],
        out_specs=pl.BlockSpec((tm, d), lambda i, j: (i, 0)),
        out_shape=jax.ShapeDtypeStruct((t, d), F32),
        compiler_params=_cparams(("parallel", "arbitrary")), name="ffn_swiglu",
    )(x_bf, wg, wu, wd)


def _moe_ffn_body(te_ref, nt_ref, x_ref, wg_ref, wu_ref, wd_ref, rw_ref, o_ref):
    i = pl.program_id(0)
    j = pl.program_id(1)
    active = i < nt_ref[0]

    @pl.when(active)
    def _():
        y = _swiglu_tile(x_ref[...], wg_ref[0], wu_ref[0], wd_ref[0])

        @pl.when(j == 0)
        def _():
            o_ref[...] = y

        @pl.when(j > 0)
        def _():
            o_ref[...] += y

        @pl.when(j == pl.num_programs(1) - 1)
        def _():
            o_ref[...] = o_ref[...] * rw_ref[:, 0:1]

    @pl.when(jnp.logical_not(active) & (j == 0))
    def _():
        o_ref[...] = jnp.zeros(o_ref.shape, F32)


def _moe_ffn(tile_expert, n_tiles, x_sorted, wg, wu, wd, row_w, *, tm, tf):
    r, d = x_sorted.shape
    f = wg.shape[2]
    nj = f // tf

    def jj(i, j, nt):
        return jnp.where(i < nt[0], j, nj - 1)

    return pl.pallas_call(
        _moe_ffn_body,
        grid_spec=pltpu.PrefetchScalarGridSpec(
            num_scalar_prefetch=2, grid=(r // tm, nj),
            in_specs=[pl.BlockSpec((tm, d), lambda i, j, te, nt: (i, 0)),
                      pl.BlockSpec((1, d, tf), lambda i, j, te, nt: (te[i], 0, jj(i, j, nt))),
                      pl.BlockSpec((1, d, tf), lambda i, j, te, nt: (te[i], 0, jj(i, j, nt))),
                      pl.BlockSpec((1, tf, d), lambda i, j, te, nt: (te[i], jj(i, j, nt), 0)),
                      pl.BlockSpec((tm, LANES), lambda i, j, te, nt: (i, 0))],
            out_specs=pl.BlockSpec((tm, d), lambda i, j, te, nt: (i, 0))),
        out_shape=jax.ShapeDtypeStruct((r, d), F32),
        compiler_params=_cparams(("arbitrary", "arbitrary")), name="moe_grouped_ffn",
    )(tile_expert, n_tiles, x_sorted, wg, wu, wd, row_w)


def _row_copy(src_ref, src_row, dst_ref, dst_row, sem):
    return pltpu.make_async_copy(src_ref.at[pl.ds(src_row, 1)], dst_ref.at[pl.ds(dst_row, 1)], sem)


def _gather_rows_body(idx_ref, src_ref, o_ref, buf_ref, sem, *, tm):
    base = pl.program_id(0) * tm

    def start(r, c):
        _row_copy(src_ref, idx_ref[base + r], buf_ref, r, sem).start()
        return c

    def wait(r, c):
        _row_copy(src_ref, 0, buf_ref, r, sem).wait()
        return c

    lax.fori_loop(0, tm, start, 0)
    lax.fori_loop(0, tm, wait, 0)
    o_ref[...] = buf_ref[...].astype(o_ref.dtype)


def _gather_rows(idx, src, n_rows, out_dtype, *, tm):
    d = src.shape[1]
    return pl.pallas_call(
        functools.partial(_gather_rows_body, tm=tm),
        grid_spec=pltpu.PrefetchScalarGridSpec(
            num_scalar_prefetch=1, grid=(n_rows // tm,),
            in_specs=[pl.BlockSpec(memory_space=pl.ANY)],
            out_specs=pl.BlockSpec((tm, d), lambda i, idx: (i, 0)),
            scratch_shapes=[pltpu.VMEM((tm, d), src.dtype), pltpu.SemaphoreType.DMA(())]),
        out_shape=jax.ShapeDtypeStruct((n_rows, d), out_dtype),
        compiler_params=_cparams(("arbitrary",)), name="moe_gather_rows",
    )(idx, src)


def _combine_body(idx_ref, src_ref, o_ref, a_ref, b_ref, sem, *, tm):
    base = pl.program_id(0) * tm

    def start(r, c):
        _row_copy(src_ref, idx_ref[2 * (base + r)], a_ref, r, sem.at[0]).start()
        _row_copy(src_ref, idx_ref[2 * (base + r) + 1], b_ref, r, sem.at[1]).start()
        return c

    def wait(r, c):
        _row_copy(src_ref, 0, a_ref, r, sem.at[0]).wait()
        _row_copy(src_ref, 0, b_ref, r, sem.at[1]).wait()
        return c

    lax.fori_loop(0, tm, start, 0)
    lax.fori_loop(0, tm, wait, 0)
    o_ref[...] = a_ref[...] + b_ref[...]


def _combine_pairs(pos, y_sorted, n_tokens, *, tm):
    d = y_sorted.shape[1]
    return pl.pallas_call(
        functools.partial(_combine_body, tm=tm),
        grid_spec=pltpu.PrefetchScalarGridSpec(
            num_scalar_prefetch=1, grid=(n_tokens // tm,),
            in_specs=[pl.BlockSpec(memory_space=pl.ANY)],
            out_specs=pl.BlockSpec((tm, d), lambda i, idx: (i, 0)),
            scratch_shapes=[pltpu.VMEM((tm, d), F32), pltpu.VMEM((tm, d), F32),
                            pltpu.SemaphoreType.DMA((2,))]),
        out_shape=jax.ShapeDtypeStruct((n_tokens, d), F32),
        compiler_params=_cparams(("arbitrary",)), name="moe_combine",
    )(pos, y_sorted)


def _col_slices():
    out, off = {}, 0
    for name, width in IN_SPLITS:
        out[name] = (off, width)
        off += width
    return out


def _gather_cols(w, names, pad_to=None):
    cs = _col_slices()
    parts = []
    for n in names:
        col = w[:, cs[n][0]:cs[n][0] + cs[n][1]]
        parts.append(col * Q_FOLD[n] if n in Q_FOLD else col)
    width = sum(cs[n][1] for n in names)
    if pad_to is not None and pad_to > width:
        parts.append(jnp.zeros((w.shape[0], pad_to - width), w.dtype))
    return jnp.concatenate(parts, axis=1).astype(BF16)


def _selection_map_t(n_cmp_pad, n_sel):
    ci = np.arange(n_cmp_pad)[:, None] * NSA_CMP_D
    sj = np.arange(n_sel)[None, :] * NSA_SEL_L
    ov = np.clip(np.minimum(ci + NSA_CMP_L, sj + NSA_SEL_L) - np.maximum(ci, sj), 0, None)
    return np.ascontiguousarray((ov / NSA_CMP_D).astype(np.float32).T)


def _gate_expand_matrices():
    e = np.zeros((4, 3, LANES, LANES), np.float32)
    for j in range(4):
        for hh in range(2):
            for c in range(3):
                e[j, c, (2 * j + hh) * 3 + c, hh * HEAD_DIM:(hh + 1) * HEAD_DIM] = 1.0
    return e


def _split_bits(x, n):
    parts = []
    r = x
    for _ in range(n):
        hi = lax.bitcast_convert_type(
            lax.bitcast_convert_type(r, jnp.uint32) & jnp.uint32(0xFFFF0000), F32)
        parts.append(hi.astype(BF16))
        r = r - hi
    return parts


def _tile(n, pref):
    return pref if n % pref == 0 else n


def _vt_heads(x3, tkc):
    b, s, c = x3.shape
    nh, nc = c // HEAD_DIM, s // tkc
    v = x3.reshape(b, nc, tkc, nh, HEAD_DIM).transpose(0, 3, 1, 4, 2)
    ones = jnp.ones((b, nh, nc, 1, tkc), v.dtype)
    zeros = jnp.zeros((b, nh, nc, ATT_VROWS - HEAD_DIM - 1, tkc), v.dtype)
    return jnp.concatenate([v, ones, zeros], axis=3)


def _token_mixer(h, h_bf, layer, b, s, tabs, w_in, cmp_pe, cmp_w1, cmp_b1, cmp_w2, sinks, fox_bf,
                 diff_lambda, diff_gain, w_branch, w_out, ln_g, ln_b, w_router):
    t = b * s
    (cos64, sin64), (cos32, sin32) = tabs
    tm = _tile(t, 1024)
    r64 = _proj(h_bf, _gather_cols(w_in, SEG_ROPE64), BF16, tm, 512, rope=(cos64, sin64, HEAD_DIM // 2))
    r32 = _proj(h_bf, _gather_cols(w_in, SEG_ROPE32), BF16, tm, 512, rope=(cos32, sin32, DIFF_SUB // 2))
    plain = _proj(h_bf, _gather_cols(w_in, SEG_PLAIN), BF16, tm, 512)
    small = _proj(h_bf, _gather_cols(w_in, SEG_SMALL, pad_to=LANES), F32, tm, LANES)
    r64_3, r32_3, plain_3, small_3 = (a.reshape(b, s, -1) for a in (r64, r32, plain, small))
    tkc = min(ATT_TKC, s // 2)
    tq = 2 * tkc

    ncp = s // NSA_CMP_D
    n_sel = s // NSA_SEL_L
    topn = min(NSA_TOPN, n_sel)

    def cmp_blocks(x2d):
        c = x2d.reshape(b, s, 2, HEAD_DIM).transpose(0, 2, 1, 3).reshape(b * 2, ncp, NSA_CMP_D * HEAD_DIM)
        nxt = jnp.concatenate([c[:, 1:], jnp.zeros_like(c[:, :1])], axis=1)
        return jnp.concatenate([c, nxt], axis=-1)

    xk = cmp_blocks(r64[:, 512:640])
    xv = cmp_blocks(plain[:, 0:128])
    cmp_kv = _nsa_compress(jnp.stack([xk, xv]), cmp_pe.reshape(2, 1, -1), cmp_w1.astype(BF16),
                           cmp_b1.reshape(2, 1, -1), cmp_w2.astype(BF16))
    kc = cmp_kv[0].astype(BF16).reshape(b, 2, ncp, HEAD_DIM)
    vc = cmp_kv[1].astype(BF16).reshape(b, 2, ncp, HEAD_DIM)
    kc4 = jnp.tile(kc, (1, 1, 1, 4))
    vct = vc.transpose(0, 1, 3, 2)
    selt = jnp.asarray(_selection_map_t(ncp, n_sel), BF16)
    o_cmp, mneg_t = _nsa_cmp_topk(r64_3, 0, kc4, vct, selt, tq=_tile(s, 256), topn=topn)
    vt_all = _vt_heads(jnp.concatenate([plain_3[..., 128:512], plain_3[..., 1536:2560]], axis=-1), tkc)
    o_win = _tattn(r64_3, 0, r64_3, 6, vt_all, 2, nmaps=2, qsel="gqa", vmap=(0, 0), mode="band",
                   window=NSA_WIN, fin="win", name="nsa_window_attention")
    e_mat = jnp.asarray(_gate_expand_matrices(), BF16)
    nsa_specs = [pl.BlockSpec((1, 1, HEAD_DIM, tq), lambda bi, u, qi, *_: (bi, u // 2, 0, qi)),
                 pl.BlockSpec((1, 1, tq, LANES), lambda bi, u, qi, *_: (bi, u // 2, qi, u % 2)),
                 pl.BlockSpec((1, tq, LANES), lambda bi, u, qi, *_: (bi, qi, u)),
                 pl.BlockSpec((1, tq, LANES), lambda bi, u, qi, *_: (bi, qi, 0)),
                 pl.BlockSpec((1, 3, LANES, LANES), lambda bi, u, qi, *_: (u, 0, 0, 0))]
    o_a = _tattn(r64_3, 0, r64_3, 5, vt_all, 0, nmaps=2, qsel="gqa", vmap=(0, 0), mode="causal", fin="nsa",
                 bias="nsa", name="nsa_selected_attention", extras=(mneg_t, o_cmp, o_win, small_3, e_mat),
                 extra_specs=nsa_specs)

    o_b = _tattn(r64_3, 7, r64_3, 11, vt_all, 4, nmaps=2, qsel="gqa", vmap=(0, 0), mode="band",
                 window=SWA_WIN, fin="swa", name="swa_attention", scalars=sinks.astype(F32) * LOG2E)

    f_logit = (small[:, 24:32] + fox_bf[None, :]).reshape(b, s, 8).transpose(0, 2, 1)
    cum = _cum_log_forget(f_logit.reshape(b * 8, s // LANES, LANES)).reshape(b, 8, s)
    ck_rep = jnp.broadcast_to((cum * LOG2E)[..., None], (b, 8, s, LANES))
    o_c = _tattn(plain_3, 4, plain_3, 8, vt_all, 6, nmaps=2, qsel="pair", vmap=(0, 1), mode="causal",
                 fin="fox", bias="fox", name="fox_attention", extras=(ck_rep,),
                 extra_specs=[pl.BlockSpec((1, 2, s, LANES), lambda bi, u, qi, *_: (bi, u, 0, 0))])

    lam_init = 0.8 - 0.6 * math.exp(-0.3 * layer)
    lf = diff_lambda.astype(F32)
    lam = jnp.exp(jnp.sum(lf[0] * lf[1])) - jnp.exp(jnp.sum(lf[2] * lf[3])) + lam_init
    lam_arr = jnp.stack([lam, jnp.asarray(1.0 - lam_init, F32)]).astype(F32)
    gain_t = jnp.broadcast_to(jnp.tile(diff_gain.astype(F32), 2)[:, None], (LANES, tq))
    o_d = _tattn(r32_3, 0, r32_3, 4, vt_all, 14, nmaps=4, qsel="pair", vmap=(0, 0, 1, 1), mode="causal",
                 fin="diff", name="diff_attention", scalars=lam_arr, extras=(gain_t,),
                 extra_specs=[pl.BlockSpec((LANES, tq), lambda bi, u, qi, *_: (0, 0))])

    o_list = [o.reshape(t, BRANCH_W) for o in (o_a, o_b, o_c, o_d)]
    merged = _merge(o_list, plain, 2560, w_branch.astype(BF16), tm=tm, tn=512)
    return _outproj_ln(merged, w_out.astype(BF16), h, ln_g.reshape(1, -1), ln_b.reshape(1, -1),
                       w_router, tm=_tile(t, 256))


def _moe_layer(h1, logits_pad, b_router, wg, wu, wd, *, tm, tf):
    t, d = h1.shape
    logits = logits_pad[:, :N_EXPERTS] + b_router.astype(F32)[None, :]
    top_v, top_i = lax.top_k(logits, TOP_K)
    top_w = jax.nn.softmax(top_v, axis=-1)
    flat_e = top_i.reshape(-1)
    onehot = jax.nn.one_hot(flat_e, N_EXPERTS, dtype=jnp.int32)
    rank = jnp.sum((jnp.cumsum(onehot, axis=0) - onehot) * onehot, axis=1)
    cnt = jnp.sum(onehot, axis=0)
    padded = ((cnt + tm - 1) // tm) * tm
    ends = jnp.cumsum(padded)
    starts = ends - padded
    pos = (starts[flat_e] + rank).astype(jnp.int32)
    n_rows = TOP_K * t + N_EXPERTS * tm
    row_token = jnp.zeros((n_rows,), jnp.int32).at[pos].set(jnp.arange(TOP_K * t, dtype=jnp.int32) // TOP_K)
    row_w = jnp.zeros((n_rows,), F32).at[pos].set(top_w.reshape(-1))
    tile_start = jnp.arange(n_rows // tm, dtype=jnp.int32) * tm
    tile_expert = jnp.minimum(jnp.sum(tile_start[:, None] >= ends[None, :], axis=1), N_EXPERTS - 1)
    n_tiles = (ends[-1] // tm).astype(jnp.int32).reshape(1)
    x_sorted = _gather_rows(row_token, h1, n_rows, BF16, tm=256)
    y_sorted = _moe_ffn(tile_expert.astype(jnp.int32), n_tiles, x_sorted, wg, wu, wd,
                        jnp.broadcast_to(row_w[:, None], (n_rows, LANES)), tm=tm, tf=tf)
    return _combine_pairs(pos, y_sorted, t, tm=_tile(t, 256))


def kernel(x, p, positions, w_in, nsa_cmp_pe, nsa_cmp_w1, nsa_cmp_b1, nsa_cmp_w2, swa_sinks, fox_bf,
           diff_lambda, diff_gain, w_branch, w_out, ln1_g, ln1_b, ffn_wg, ffn_wu, ffn_wd, moe_router,
           moe_router_b, moe_wg, moe_wu, moe_wd, ple_proj, ple_gate, ln2_g, ln2_b):
    b, s, d = x.shape
    t = b * s
    tabs = (_rope_tabs(positions, HEAD_DIM), _rope_tabs(positions, DIFF_SUB))
    h = x.reshape(t, d).astype(F32)
    h_bf = h.astype(BF16)
    for i in range(DEPTH):
        is_moe = i % 2 == 1
        w_router = None
        if is_moe:
            wr = jnp.zeros((d, LANES), F32).at[:, :N_EXPERTS].set(moe_router[i // 2].astype(F32))
            w_router = jnp.stack(_split_bits(wr, 2))
        res = _token_mixer(h, h_bf, i, b, s, tabs, w_in[i], nsa_cmp_pe[i], nsa_cmp_w1[i], nsa_cmp_b1[i],
                           nsa_cmp_w2[i], swa_sinks[i], fox_bf[i], diff_lambda[i], diff_gain[i],
                           w_branch[i], w_out[i], ln1_g[i], ln1_b[i], w_router)
        h1, h1_bf = res[0], res[1]
        if not is_moe:
            fpad = (-D_FF) % 512
            wg = jnp.pad(ffn_wg[i // 2].astype(BF16), ((0, 0), (0, fpad)))
            wu = jnp.pad(ffn_wu[i // 2].astype(BF16), ((0, 0), (0, fpad)))
            wd = jnp.pad(ffn_wd[i // 2].astype(BF16), ((0, fpad), (0, 0)))
            f = _ffn(h1_bf, wg, wu, wd, tm=_tile(t, 1024), tf=512)
        else:
            f = _moe_layer(h1, res[2], moe_router_b[i // 2], moe_wg[i // 2].astype(BF16),
                           moe_wu[i // 2].astype(BF16), moe_wd[i // 2].astype(BF16),
                           tm=_tile(t, 512), tf=1024)
        h, h_bf = _ple_ln(h1_bf, h1, f, p[i].reshape(t, PLE_DIM).astype(BF16), ple_gate[i].astype(BF16),
                          ple_proj[i].astype(BF16), ln2_g[i].reshape(1, -1), ln2_b[i].reshape(1, -1),
                          tm=_tile(t, 256))
    return h.reshape(b, s, d).astype(x.dtype)
```

```python
import functools
import math

import numpy as np
import jax
import jax.numpy as jnp
from jax import lax
from jax.experimental import pallas as pl
from jax.experimental.pallas import tpu as pltpu

F32 = jnp.float32
BF16 = jnp.bfloat16

D_MODEL = 2048
DEPTH = 2
HEAD_DIM = 64
ROPE_THETA = 10000.0
PLE_DIM = 256
LN_EPS = 1e-5
NSA_CMP_L = 32
NSA_CMP_D = 16
NSA_SEL_L = 64
NSA_TOPN = 16
NSA_WIN = 512
NSA_CMP_HIDDEN = 256
NSA_FORCE = 1e9
SWA_WIN = 128
DIFF_SUB = HEAD_DIM // 2
N_BRANCH = 4
BRANCH_W = 8 * HEAD_DIM
D_FF = 5504
N_EXPERTS = 8
TOP_K = 2
D_FF_EXPERT = 7168
ALPHA = (2.0 * DEPTH) ** 0.25

IN_SPLITS = (
    ("a_q", 512), ("a_kc", 128), ("a_vc", 128), ("a_ks", 128), ("a_vs", 128),
    ("a_kw", 128), ("a_vw", 128), ("a_g", 24),
    ("b_q", 512), ("b_k", 128), ("b_v", 128),
    ("c_q", 512), ("c_k", 512), ("c_v", 512), ("c_f", 8),
    ("d_q", 512), ("d_k", 512), ("d_v", 512),
    ("merge_gate", N_BRANCH * D_MODEL),
)
SEG_ROPE64 = ("a_q", "a_kc", "a_ks", "a_kw", "b_q", "b_k")
SEG_ROPE32 = ("d_q", "d_k")
SEG_PLAIN = ("a_vc", "a_vs", "a_vw", "b_v", "c_q", "c_k", "c_v", "d_v", "merge_gate")
SEG_SMALL = ("a_g", "c_f")

LANES = 128
NEG = -1e30
LOG2E = math.log2(math.e)
VMEM_LIMIT = 56 * 1024 * 1024
ATT_TKC = 256
ATT_TQ = 2 * ATT_TKC
ATT_VROWS = 80

Q_FOLD = {"a_q": HEAD_DIM ** -0.5 * LOG2E, "b_q": HEAD_DIM ** -0.5 * LOG2E,
          "c_q": HEAD_DIM ** -0.5 * LOG2E, "d_q": DIFF_SUB ** -0.5 * LOG2E}


def _cparams(sem):
    return pltpu.CompilerParams(dimension_semantics=sem, vmem_limit_bytes=VMEM_LIMIT)


def _sigmoid(x):
    return 1.0 / (1.0 + jnp.exp(-x))


def _dot(a, b):
    return jnp.dot(a, b, preferred_element_type=F32)


def _dot_nt(a, b):
    return lax.dot_general(a, b, (((1,), (1,)), ((), ())), preferred_element_type=F32)


def _split2(x):
    hi = x.astype(BF16)
    lo = (x - hi.astype(F32)).astype(BF16)
    return hi, lo


def _split3(x):
    hi = x.astype(BF16)
    r = x - hi.astype(F32)
    mid = r.astype(BF16)
    lo = (r - mid.astype(F32)).astype(BF16)
    return hi, mid, lo


def _proj_body(x_ref, w_ref, *rest, rope_half):
    if rope_half:
        cos_ref, sin_ref, o_ref = rest
    else:
        (o_ref,) = rest
    acc = _dot(x_ref[...], w_ref[...])
    if not rope_half:
        o_ref[...] = acc.astype(o_ref.dtype)
        return
    cos = cos_ref[...]
    sin = sin_ref[...]
    lane = lax.broadcasted_iota(jnp.int32, cos.shape, 1)
    first = (lane % (2 * rope_half)) < rope_half
    for c in range(acc.shape[1] // LANES):
        a = acc[:, c * LANES:(c + 1) * LANES]
        rot = jnp.where(first, pltpu.roll(a, LANES - rope_half, 1), pltpu.roll(a, rope_half, 1))
        o_ref[:, c * LANES:(c + 1) * LANES] = (a * cos + rot * sin).astype(o_ref.dtype)


def _proj(x, w, out_dtype, tm, tn, rope=None):
    m, k = x.shape
    n = w.shape[1]
    in_specs = [pl.BlockSpec((tm, k), lambda i, j: (i, 0)),
                pl.BlockSpec((k, tn), lambda i, j: (0, j))]
    args = [x, w]
    rope_half = 0
    if rope is not None:
        cos_tab, sin_tab, rope_half = rope
        in_specs += [pl.BlockSpec((tm, LANES), lambda i, j: (i, 0)),
                     pl.BlockSpec((tm, LANES), lambda i, j: (i, 0))]
        args += [cos_tab, sin_tab]
    return pl.pallas_call(
        functools.partial(_proj_body, rope_half=rope_half),
        grid=(m // tm, n // tn),
        in_specs=in_specs,
        out_specs=pl.BlockSpec((tm, tn), lambda i, j: (i, j)),
        out_shape=jax.ShapeDtypeStruct((m, n), out_dtype),
        compiler_params=_cparams(("parallel", "parallel")),
        name="proj_rope" if rope_half else "proj",
    )(*args)


def _rope_tabs(positions, dim):
    inv = 1.0 / (ROPE_THETA ** (jnp.arange(0, dim, 2, dtype=F32) / dim))
    ang = positions.astype(F32).reshape(-1)[:, None] * inv
    c, s = jnp.cos(ang), jnp.sin(ang)
    reps = LANES // dim
    return (jnp.tile(jnp.concatenate([c, c], -1), (1, reps)),
            jnp.tile(jnp.concatenate([-s, s], -1), (1, reps)))


def _tattn_body(*refs, nmaps, qsel, vmap, tq, tkc, mode, window, bias, fin):
    refs = list(refs)
    sc_ref = refs.pop(0) if fin in ("diff", "swa") else None
    q_ref, k_ref, vt_ref = refs[:3]
    extras, o_ref = refs[3:-5], refs[-5]
    qm_ref, st_ref, m_ref, acc_ref = refs[-4:]
    bias_ref = None
    if bias is not None:
        bias_ref, extras = extras[0], extras[1:]
    u = pl.program_id(1)
    qi = pl.program_id(2)
    q0 = qi * tq
    last_chunk = k_ref.shape[1] // tkc - 1
    t_pos = q0 + lax.broadcasted_iota(jnp.int32, (tkc, tq), 1)
    key_iota = lax.broadcasted_iota(jnp.int32, (tkc, tq), 0)

    lane = lax.broadcasted_iota(jnp.int32, (tq, LANES), 1)
    klane = lax.broadcasted_iota(jnp.int32, (tkc, LANES), 1)
    q = q_ref[0].astype(F32)
    if qsel == "gqa":
        lo_lane = (u // 2) * HEAD_DIM
        q_rolled = pltpu.roll(q, HEAD_DIM, 1)
        in_group = (lane >= lo_lane) & (lane < lo_lane + HEAD_DIM)
        k_in_group = (klane >= lo_lane) & (klane < lo_lane + HEAD_DIM)
        fill = bias_ref[0, 0].astype(F32) if bias == "nsa" else 0.0
        for e in range(2):
            q_e = jnp.where(lo_lane == e * HEAD_DIM, q, q_rolled)
            qm_ref[e] = jnp.where(in_group, q_e, fill).T.astype(BF16)
    else:
        width = LANES // nmaps
        fill = jnp.where(lane % HEAD_DIM < 3, -1.0, 0.0) if bias == "fox" else 0.0
        for mp in range(nmaps):
            qm_ref[mp] = jnp.where((lane >= mp * width) & (lane < (mp + 1) * width), q, fill).T.astype(BF16)

    m_ref[...] = jnp.full(m_ref.shape, NEG, F32)
    acc_ref[...] = jnp.zeros(acc_ref.shape, F32)

    def qk(c, buf):
        cc = jnp.minimum(c, last_chunk)
        off = pl.multiple_of(cc * tkc, tkc)
        kc = k_ref[0, pl.ds(off, tkc), :]
        if bias == "nsa":
            kc = jnp.where(k_in_group, kc, extras[0][pl.ds(off, tkc), :])
        for mp in range(nmaps):
            kc_mp = kc
            if bias == "fox":
                kc_mp = jnp.where((klane >= mp * HEAD_DIM) & (klane < (mp + 1) * HEAD_DIM), kc,
                                  bias_ref[0, mp, pl.ds(off, tkc), :])
            st_ref[buf, mp] = _dot(kc_mp, qm_ref[mp])

    def soft(c, buf, masked):
        for mp in range(nmaps):
            st = st_ref[buf, mp]
            if masked:
                key = c * tkc + key_iota
                keep = key <= t_pos
                if mode == "band":
                    keep = keep & (t_pos - key < window)
                st = jnp.where(keep, st, NEG)
            m_old = m_ref[mp]
            m8 = jnp.max(st.reshape(tkc // 8, 8, tq), axis=0)
            m_new = jnp.maximum(m_old, jnp.max(m8, axis=0, keepdims=True))
            p = jnp.exp2(st - m_new).astype(BF16)
            acc_ref[mp] = jnp.exp2(m_old - m_new) * acc_ref[mp] + _dot(vt_ref[0, vmap[mp], c], p)
            m_ref[mp] = m_new

    def pair(pidx, carry, masked, lookahead=True):
        c0 = 2 * pidx
        qk(c0 + 1, 1)
        soft(c0, 0, masked)
        if lookahead:
            qk(c0 + 2, 0)
        soft(c0 + 1, 1, masked)
        return carry

    lo = 0 if mode == "causal" else jnp.maximum(q0 - (window - 1), 0) // tq
    qk(2 * lo, 0)
    lax.fori_loop(lo, qi, functools.partial(pair, masked=mode == "band"), 0)
    pair(qi, 0, True, lookahead=False)

    def normed(mp):
        acc = acc_ref[mp]
        num, l_i = acc[:HEAD_DIM], acc[HEAD_DIM:HEAD_DIM + 1]
        if fin == "swa":
            m_i = m_ref[mp]
            sk = sc_ref[2 * u + mp]
            m_f = jnp.maximum(m_i, sk)
            corr = jnp.exp2(m_i - m_f)
            return num * (corr / (l_i * corr + jnp.exp2(sk - m_f)))
        return num * (1.0 / l_i)

    if fin == "diff":
        lam = sc_ref[0]
        halves = []
        for hh in range(2):
            o = normed(2 * hh) - lam * normed(2 * hh + 1)
            ms = jnp.mean(o * o, axis=0, keepdims=True)
            halves.append(o * lax.rsqrt(ms + LN_EPS))
        ot = jnp.concatenate(halves, axis=0) * extras[0][...] * sc_ref[1]
    else:
        ot = jnp.concatenate([normed(0), normed(1)], axis=0)
    o = ot.T
    if fin == "nsa":
        ocmp_ref, owin_ref, sm_ref, e_ref = extras[1:]
        hi, lo_part = _split2(sm_ref[0])
        gates = [_sigmoid(_dot(hi, e_ref[0, c]) + _dot(lo_part, e_ref[0, c])) for c in range(3)]
        o = gates[0] * ocmp_ref[0, 0] + gates[1] * o + gates[2] * owin_ref[0].astype(F32)
    o_ref[0] = o.astype(o_ref.dtype)


def _tattn(q_arr, q_blk, k_arr, k_blk, vt_all, v_head, *, nmaps, qsel, vmap, mode, fin, name, window=0,
           bias=None, scalars=None, extras=(), extra_specs=()):
    b, s, _ = q_arr.shape
    nc, vrows, tkc = vt_all.shape[2], vt_all.shape[3], vt_all.shape[4]
    tq = 2 * tkc
    nu = 4
    if qsel == "gqa":
        kspec = pl.BlockSpec((1, s, LANES), lambda bi, u, qi, *_: (bi, 0, k_blk))
        vspec = pl.BlockSpec((1, 1, nc, vrows, tkc), lambda bi, u, qi, *_: (bi, v_head + u // 2, 0, 0, 0))
    else:
        kspec = pl.BlockSpec((1, s, LANES), lambda bi, u, qi, *_: (bi, 0, k_blk + u))
        vspec = pl.BlockSpec((1, 2, nc, vrows, tkc), lambda bi, u, qi, *_: (bi, v_head // 2 + u, 0, 0, 0))
    in_specs = [pl.BlockSpec((1, tq, LANES), lambda bi, u, qi, *_: (bi, qi, q_blk + u)), kspec, vspec]
    in_specs += list(extra_specs)
    body = functools.partial(_tattn_body, nmaps=nmaps, qsel=qsel, vmap=vmap, tq=tq, tkc=tkc, mode=mode,
                             window=window, bias=bias, fin=fin)
    args = ([] if scalars is None else [scalars]) + [q_arr, k_arr, vt_all] + list(extras)
    return pl.pallas_call(
        body,
        grid_spec=pltpu.PrefetchScalarGridSpec(
            num_scalar_prefetch=0 if scalars is None else 1, grid=(b, nu, s // tq),
            in_specs=in_specs,
            out_specs=pl.BlockSpec((1, tq, LANES), lambda bi, u, qi, *_: (bi, qi, u)),
            scratch_shapes=[pltpu.VMEM((nmaps, LANES, tq), BF16), pltpu.VMEM((2, nmaps, tkc, tq), F32),
                            pltpu.VMEM((nmaps, 1, tq), F32), pltpu.VMEM((nmaps, vrows, tq), F32)]),
        out_shape=jax.ShapeDtypeStruct((b, s, nu * LANES), BF16),
        compiler_params=_cparams(("parallel", "parallel", "parallel")),
        name=name,
    )(*args)


def _gelu_tanh(x):
    return 0.5 * x * (1.0 + jnp.tanh(math.sqrt(2.0 / math.pi) * (x + 0.044715 * (x * x * x))))


def _compress_body(x_ref, pe_ref, w1_ref, b1_ref, w2_ref, o_ref):
    x = (x_ref[0, 0].astype(F32) + pe_ref[0]).astype(BF16)
    hid = _gelu_tanh(_dot(x, w1_ref[0]) + b1_ref[0])
    o_ref[0, 0] = _dot(hid.astype(BF16), w2_ref[0])


def _nsa_compress(x, pe, w1, b1, w2):
    _, nb, ncp, ld = x.shape
    hid = w1.shape[-1]
    return pl.pallas_call(
        _compress_body,
        grid=(2, nb),
        in_specs=[pl.BlockSpec((1, 1, ncp, ld), lambda t, i: (t, i, 0, 0)),
                  pl.BlockSpec((1, 1, ld), lambda t, i: (t, 0, 0)),
                  pl.BlockSpec((1, ld, hid), lambda t, i: (t, 0, 0)),
                  pl.BlockSpec((1, 1, hid), lambda t, i: (t, 0, 0)),
                  pl.BlockSpec((1, hid, HEAD_DIM), lambda t, i: (t, 0, 0))],
        out_specs=pl.BlockSpec((1, 1, ncp, HEAD_DIM), lambda t, i: (t, i, 0, 0)),
        out_shape=jax.ShapeDtypeStruct((2, nb, ncp, HEAD_DIM), F32),
        compiler_params=_cparams(("parallel", "parallel")),
        name="nsa_compress",
    )(x, pe, w1, b1, w2)


def _cmp_topk_body(q_ref, kc_ref, vct_ref, selt_ref, o_ref, mt_ref, *, tq, ncp, nsel, topn):
    qi = pl.program_id(2)
    q = q_ref[0]
    kc4 = kc_ref[0, 0]
    vct = vct_ref[0, 0]
    lane = lax.broadcasted_iota(jnp.int32, kc4.shape, 1)
    ci = lax.broadcasted_iota(jnp.int32, (ncp, tq), 0)
    tpos = qi * tq + lax.broadcasted_iota(jnp.int32, (ncp, tq), 1)
    cmask = ci * NSA_CMP_D + (NSA_CMP_L - 1) <= tpos
    psum = jnp.zeros((ncp, tq), F32)
    rows = []
    for a in range(4):
        kcm = jnp.where((lane >= a * HEAD_DIM) & (lane < (a + 1) * HEAD_DIM), kc4, jnp.zeros_like(kc4))
        st = jnp.where(cmask, _dot_nt(kcm, q), NEG)
        m = jnp.max(st, axis=0, keepdims=True)
        e = jnp.where(cmask, jnp.exp2(st - m), 0.0)
        l = jnp.sum(e, axis=0, keepdims=True)
        p = e * jnp.where(l > 0.0, 1.0 / l, 0.0)
        psum = psum + p
        rows.append(_dot(vct, p.astype(BF16)))
    hi, lo = _split2(psum)
    selt = selt_ref[...]
    imp = _dot(selt, hi) + _dot(selt, lo)
    blk = lax.broadcasted_iota(jnp.int32, (nsel, tq), 0)
    cur = (qi * tq + lax.broadcasted_iota(jnp.int32, (nsel, tq), 1)) // NSA_SEL_L
    forced = (blk == 0) | (blk == cur) | (blk == cur - 1)
    imp = jnp.where(forced, NSA_FORCE, jnp.where(blk > cur, -NSA_FORCE, imp))
    cnt = jnp.zeros((nsel, tq), jnp.int32)
    for jp in range(nsel):
        v = imp[jp:jp + 1, :]
        tie = jnp.where(blk > jp, 1, 0)
        cnt = cnt + jnp.where(v > imp, 1, jnp.where(v == imp, tie, 0))
    mneg = jnp.where(cnt < topn, 0.0, NEG)
    if nsel < HEAD_DIM:
        mneg = jnp.concatenate([mneg, jnp.zeros((HEAD_DIM - nsel, tq), F32)], axis=0)
    mt_ref[0, 0] = jnp.concatenate([mneg, mneg], axis=0).T.astype(mt_ref.dtype)
    o_ref[0, 0] = jnp.concatenate(rows, axis=0).T


def _nsa_cmp_topk(q_arr, q_off256, kc4, vct, selt, *, tq, topn):
    b, s, _ = q_arr.shape
    ncp = kc4.shape[2]
    nsel = selt.shape[0]
    body = functools.partial(_cmp_topk_body, tq=tq, ncp=ncp, nsel=nsel, topn=topn)
    return pl.pallas_call(
        body,
        grid=(b, 2, s // tq),
        in_specs=[pl.BlockSpec((1, tq, 2 * LANES), lambda bi, g, qi: (bi, qi, q_off256 + g)),
                  pl.BlockSpec((1, 1, ncp, 2 * LANES), lambda bi, g, qi: (bi, g, 0, 0)),
                  pl.BlockSpec((1, 1, HEAD_DIM, ncp), lambda bi, g, qi: (bi, g, 0, 0)),
                  pl.BlockSpec((nsel, ncp), lambda bi, g, qi: (0, 0))],
        out_specs=[pl.BlockSpec((1, 1, tq, 2 * LANES), lambda bi, g, qi: (bi, g, qi, 0)),
                   pl.BlockSpec((1, 1, tq, LANES), lambda bi, g, qi: (bi, g, qi, 0))],
        out_shape=[jax.ShapeDtypeStruct((b, 2, s, 2 * LANES), F32),
                   jax.ShapeDtypeStruct((b, 2, s, LANES), BF16)],
        compiler_params=_cparams(("parallel", "parallel", "parallel")),
        name="nsa_cmp_topk",
    )(q_arr, kc4, vct, selt)


def _cumgate_body(x_ref, o_ref):
    x = x_ref[0]
    r = x.shape[0]
    ls = jnp.minimum(x, 0.0) - jnp.log1p(jnp.exp(-jnp.abs(x)))
    i0 = lax.broadcasted_iota(jnp.int32, (LANES, LANES), 0)
    i1 = lax.broadcasted_iota(jnp.int32, (LANES, LANES), 1)
    upper = jnp.where(i0 <= i1, 1.0, 0.0).astype(BF16)
    ones = jnp.ones((LANES, LANES), BF16)
    r0 = lax.broadcasted_iota(jnp.int32, (r, r), 0)
    r1 = lax.broadcasted_iota(jnp.int32, (r, r), 1)
    strict = jnp.where(r1 < r0, 1.0, 0.0).astype(BF16)
    parts = _split3(ls)
    intra = sum(_dot(pp, upper) for pp in parts)
    rowtot = sum(_dot(pp, ones) for pp in parts)
    off = sum(_dot(strict, pp) for pp in _split3(rowtot))
    o_ref[0] = intra + off


def _cum_log_forget(x):
    n, r, _ = x.shape
    return pl.pallas_call(
        _cumgate_body,
        grid=(n,),
        in_specs=[pl.BlockSpec((1, r, LANES), lambda i: (i, 0, 0))],
        out_specs=pl.BlockSpec((1, r, LANES), lambda i: (i, 0, 0)),
        out_shape=jax.ShapeDtypeStruct((n, r, LANES), F32),
        compiler_params=_cparams(("parallel",)),
        name="cum_log_forget",
    )(x)


def _merge_body(oa_ref, ob_ref, oc_ref, od_ref, g0_ref, g1_ref, g2_ref, g3_ref, wb_ref, o_ref):
    acc = None
    for n, (o_r, g_r) in enumerate(((oa_ref, g0_ref), (ob_ref, g1_ref), (oc_ref, g2_ref), (od_ref, g3_ref))):
        term = _sigmoid(g_r[...].astype(F32)) * _dot(o_r[...], wb_ref[n])
        acc = term if acc is None else acc + term
    o_ref[...] = acc.astype(o_ref.dtype)


def _merge(o_list, plain, gate_off, wb, *, tm, tn):
    t = plain.shape[0]
    d = wb.shape[-1]
    nj = d // tn
    ospec = pl.BlockSpec((tm, BRANCH_W), lambda i, j: (i, 0))
    gspecs = [pl.BlockSpec((tm, tn), functools.partial(lambda i, j, n: (i, gate_off // tn + n * nj + j), n=n))
              for n in range(N_BRANCH)]
    return pl.pallas_call(
        _merge_body,
        grid=(t // tm, nj),
        in_specs=[ospec] * 4 + gspecs + [pl.BlockSpec((N_BRANCH, BRANCH_W, tn), lambda i, j: (0, 0, j))],
        out_specs=pl.BlockSpec((tm, tn), lambda i, j: (i, j)),
        out_shape=jax.ShapeDtypeStruct((t, d), BF16),
        compiler_params=_cparams(("parallel", "parallel")),
        name="gated_merge",
    )(*o_list, plain, plain, plain, plain, wb)


def _layer_norm(y, g, b):
    mu = jnp.mean(y, axis=-1, keepdims=True)
    yc = y - mu
    var = jnp.mean(yc * yc, axis=-1, keepdims=True)
    return yc * lax.rsqrt(var + LN_EPS) * g + b


def _outproj_ln_body(mg_ref, wo_ref, h_ref, g_ref, b_ref, *rest, with_router):
    if with_router:
        wr_ref, o_ref, ob_ref, lg_ref = rest
    else:
        o_ref, ob_ref = rest
    y = ALPHA * h_ref[...] + _dot(mg_ref[...], wo_ref[...])
    out = _layer_norm(y, g_ref[...], b_ref[...])
    o_ref[...] = out
    ob_ref[...] = out.astype(BF16)
    if with_router:
        hi, lo = _split2(out)
        lg_ref[...] = _dot(hi, wr_ref[0]) + _dot(lo, wr_ref[0]) + _dot(hi, wr_ref[1])


def _outproj_ln(merged, w_out, h, g, b, w_router=None, *, tm):
    t, d = h.shape
    with_router = w_router is not None
    row = lambda i: (i, 0)
    fix = lambda i: (0, 0)
    in_specs = [pl.BlockSpec((tm, d), row), pl.BlockSpec((d, d), fix, pipeline_mode=pl.Buffered(1)),
                pl.BlockSpec((tm, d), row),
                pl.BlockSpec((1, d), fix), pl.BlockSpec((1, d), fix)]
    out_specs = [pl.BlockSpec((tm, d), row), pl.BlockSpec((tm, d), row)]
    out_shape = [jax.ShapeDtypeStruct((t, d), F32), jax.ShapeDtypeStruct((t, d), BF16)]
    args = [merged, w_out, h, g, b]
    if with_router:
        in_specs.append(pl.BlockSpec((2, d, LANES), lambda i: (0, 0, 0)))
        out_specs.append(pl.BlockSpec((tm, LANES), row))
        out_shape.append(jax.ShapeDtypeStruct((t, LANES), F32))
        args.append(w_router)
    return pl.pallas_call(
        functools.partial(_outproj_ln_body, with_router=with_router),
        grid=(t // tm,), in_specs=in_specs, out_specs=out_specs, out_shape=out_shape,
        compiler_params=_cparams(("parallel",)), name="outproj_ln1",
    )(*args)


def _ple_ln_body(hb_ref, h_ref, f_ref, p_ref, wg_ref, wp_ref, g_ref, b_ref, o_ref, ob_ref):
    ple = _sigmoid(_dot(hb_ref[...], wg_ref[...])) * _dot(p_ref[...], wp_ref[...])
    out = _layer_norm(ALPHA * h_ref[...] + f_ref[...] + ple, g_ref[...], b_ref[...])
    o_ref[...] = out
    ob_ref[...] = out.astype(BF16)


def _ple_ln(h_bf, h, f, p_bf, w_gate, w_proj, g, b, *, tm):
    t, d = h.shape
    row = lambda i: (i, 0)
    fix = lambda i: (0, 0)
    return pl.pallas_call(
        _ple_ln_body,
        grid=(t // tm,),
        in_specs=[pl.BlockSpec((tm, d), row), pl.BlockSpec((tm, d), row), pl.BlockSpec((tm, d), row),
                  pl.BlockSpec((tm, PLE_DIM), row), pl.BlockSpec((d, d), fix, pipeline_mode=pl.Buffered(1)),
                  pl.BlockSpec((PLE_DIM, d), fix), pl.BlockSpec((1, d), fix), pl.BlockSpec((1, d), fix)],
        out_specs=[pl.BlockSpec((tm, d), row), pl.BlockSpec((tm, d), row)],
        out_shape=[jax.ShapeDtypeStruct((t, d), F32), jax.ShapeDtypeStruct((t, d), BF16)],
        compiler_params=_cparams(("parallel",)), name="ple_ln2",
    )(h_bf, h, f, p_bf, w_gate, w_proj, g, b)


def _swiglu_tile(x, wg, wu, wd):
    g = _dot(x, wg)
    u = _dot(x, wu)
    return _dot((g * _sigmoid(g) * u).astype(BF16), wd)


def _ffn_body(x_ref, wg_ref, wu_ref, wd_ref, o_ref):
    j = pl.program_id(1)
    y = _swiglu_tile(x_ref[...], wg_ref[...], wu_ref[...], wd_ref[...])

    @pl.when(j == 0)
    def _():
        o_ref[...] = y

    @pl.when(j > 0)
    def _():
        o_ref[...] += y


def _ffn(x_bf, wg, wu, wd, *, tm, tf):
    t, d = x_bf.shape
    f = wg.shape[1]
    return pl.pallas_call(
        _ffn_body,
        grid=(t // tm, f // tf),
        in_specs=[pl.BlockSpec((tm, d), lambda i, j: (i, 0)),
                  pl.BlockSpec((d, tf), lambda i, j: (0, j)),
                  pl.BlockSpec((d, tf), lambda i, j: (0, j)),
                  pl.BlockSpec((tf, d), lambda i, j: (j, 0))],
        out_specs=pl.BlockSpec((tm, d), lambda i, j: (i, 0)),
        out_shape=jax.ShapeDtypeStruct((t, d), F32),
        compiler_params=_cparams(("parallel", "arbitrary")), name="ffn_swiglu",
    )(x_bf, wg, wu, wd)


def _moe_ffn_body(te_ref, nt_ref, x_ref, wg_ref, wu_ref, wd_ref, rw_ref, o_ref):
    i = pl.program_id(0)
    j = pl.program_id(1)
    active = i < nt_ref[0]

    @pl.when(active)
    def _():
        y = _swiglu_tile(x_ref[...], wg_ref[0], wu_ref[0], wd_ref[0])

        @pl.when(j == 0)
        def _():
            o_ref[...] = y

        @pl.when(j > 0)
        def _():
            o_ref[...] += y

        @pl.when(j == pl.num_programs(1) - 1)
        def _():
            o_ref[...] = o_ref[...] * rw_ref[:, 0:1]

    @pl.when(jnp.logical_not(active) & (j == 0))
    def _():
        o_ref[...] = jnp.zeros(o_ref.shape, F32)


def _moe_ffn(tile_expert, n_tiles, x_sorted, wg, wu, wd, row_w, *, tm, tf):
    r, d = x_sorted.shape
    f = wg.shape[2]
    nj = f // tf

    def jj(i, j, nt):
        return jnp.where(i < nt[0], j, nj - 1)

    return pl.pallas_call(
        _moe_ffn_body,
        grid_spec=pltpu.PrefetchScalarGridSpec(
            num_scalar_prefetch=2, grid=(r // tm, nj),
            in_specs=[pl.BlockSpec((tm, d), lambda i, j, te, nt: (i, 0)),
                      pl.BlockSpec((1, d, tf), lambda i, j, te, nt: (te[i], 0, jj(i, j, nt))),
                      pl.BlockSpec((1, d, tf), lambda i, j, te, nt: (te[i], 0, jj(i, j, nt))),
                      pl.BlockSpec((1, tf, d), lambda i, j, te, nt: (te[i], jj(i, j, nt), 0)),
                      pl.BlockSpec((tm, LANES), lambda i, j, te, nt: (i, 0))],
            out_specs=pl.BlockSpec((tm, d), lambda i, j, te, nt: (i, 0))),
        out_shape=jax.ShapeDtypeStruct((r, d), F32),
        compiler_params=_cparams(("arbitrary", "arbitrary")), name="moe_grouped_ffn",
    )(tile_expert, n_tiles, x_sorted, wg, wu, wd, row_w)


def _row_copy(src_ref, src_row, dst_ref, dst_row, sem):
    return pltpu.make_async_copy(src_ref.at[pl.ds(src_row, 1)], dst_ref.at[pl.ds(dst_row, 1)], sem)


def _gather_rows_body(idx_ref, src_ref, o_ref, buf_ref, sem, *, tm):
    base = pl.program_id(0) * tm

    def start(r, c):
        _row_copy(src_ref, idx_ref[base + r], buf_ref, r, sem).start()
        return c

    def wait(r, c):
        _row_copy(src_ref, 0, buf_ref, r, sem).wait()
        return c

    lax.fori_loop(0, tm, start, 0)
    lax.fori_loop(0, tm, wait, 0)
    o_ref[...] = buf_ref[...].astype(o_ref.dtype)


def _gather_rows(idx, src, n_rows, out_dtype, *, tm):
    d = src.shape[1]
    return pl.pallas_call(
        functools.partial(_gather_rows_body, tm=tm),
        grid_spec=pltpu.PrefetchScalarGridSpec(
            num_scalar_prefetch=1, grid=(n_rows // tm,),
            in_specs=[pl.BlockSpec(memory_space=pl.ANY)],
            out_specs=pl.BlockSpec((tm, d), lambda i, idx: (i, 0)),
            scratch_shapes=[pltpu.VMEM((tm, d), src.dtype), pltpu.SemaphoreType.DMA(())]),
        out_shape=jax.ShapeDtypeStruct((n_rows, d), out_dtype),
        compiler_params=_cparams(("arbitrary",)), name="moe_gather_rows",
    )(idx, src)


def _combine_body(idx_ref, src_ref, o_ref, a_ref, b_ref, sem, *, tm):
    base = pl.program_id(0) * tm

    def start(r, c):
        _row_copy(src_ref, idx_ref[2 * (base + r)], a_ref, r, sem.at[0]).start()
        _row_copy(src_ref, idx_ref[2 * (base + r) + 1], b_ref, r, sem.at[1]).start()
        return c

    def wait(r, c):
        _row_copy(src_ref, 0, a_ref, r, sem.at[0]).wait()
        _row_copy(src_ref, 0, b_ref, r, sem.at[1]).wait()
        return c

    lax.fori_loop(0, tm, start, 0)
    lax.fori_loop(0, tm, wait, 0)
    o_ref[...] = a_ref[...] + b_ref[...]


def _combine_pairs(pos, y_sorted, n_tokens, *, tm):
    d = y_sorted.shape[1]
    return pl.pallas_call(
        functools.partial(_combine_body, tm=tm),
        grid_spec=pltpu.PrefetchScalarGridSpec(
            num_scalar_prefetch=1, grid=(n_tokens // tm,),
            in_specs=[pl.BlockSpec(memory_space=pl.ANY)],
            out_specs=pl.BlockSpec((tm, d), lambda i, idx: (i, 0)),
            scratch_shapes=[pltpu.VMEM((tm, d), F32), pltpu.VMEM((tm, d), F32),
                            pltpu.SemaphoreType.DMA((2,))]),
        out_shape=jax.ShapeDtypeStruct((n_tokens, d), F32),
        compiler_params=_cparams(("arbitrary",)), name="moe_combine",
    )(pos, y_sorted)


def _col_slices():
    out, off = {}, 0
    for name, width in IN_SPLITS:
        out[name] = (off, width)
        off += width
    return out


def _gather_cols(w, names, pad_to=None):
    cs = _col_slices()
    parts = []
    for n in names:
        col = w[:, cs[n][0]:cs[n][0] + cs[n][1]]
        parts.append(col * Q_FOLD[n] if n in Q_FOLD else col)
    width = sum(cs[n][1] for n in names)
    if pad_to is not None and pad_to > width:
        parts.append(jnp.zeros((w.shape[0], pad_to - width), w.dtype))
    return jnp.concatenate(parts, axis=1).astype(BF16)


def _selection_map_t(n_cmp_pad, n_sel):
    ci = np.arange(n_cmp_pad)[:, None] * NSA_CMP_D
    sj = np.arange(n_sel)[None, :] * NSA_SEL_L
    ov = np.clip(np.minimum(ci + NSA_CMP_L, sj + NSA_SEL_L) - np.maximum(ci, sj), 0, None)
    return np.ascontiguousarray((ov / NSA_CMP_D).astype(np.float32).T)


def _gate_expand_matrices():
    e = np.zeros((4, 3, LANES, LANES), np.float32)
    for j in range(4):
        for hh in range(2):
            for c in range(3):
                e[j, c, (2 * j + hh) * 3 + c, hh * HEAD_DIM:(hh + 1) * HEAD_DIM] = 1.0
    return e


def _split_bits(x, n):
    parts = []
    r = x
    for _ in range(n):
        hi = lax.bitcast_convert_type(
            lax.bitcast_convert_type(r, jnp.uint32) & jnp.uint32(0xFFFF0000), F32)
        parts.append(hi.astype(BF16))
        r = r - hi
    return parts


def _tile(n, pref):
    return pref if n % pref == 0 else n


def _vt_heads(x3, tkc):
    b, s, c = x3.shape
    nh, nc = c // HEAD_DIM, s // tkc
    v = x3.reshape(b, nc, tkc, nh, HEAD_DIM).transpose(0, 3, 1, 4, 2)
    ones = jnp.ones((b, nh, nc, 1, tkc), v.dtype)
    zeros = jnp.zeros((b, nh, nc, ATT_VROWS - HEAD_DIM - 1, tkc), v.dtype)
    return jnp.concatenate([v, ones, zeros], axis=3)


def _token_mixer(h, h_bf, layer, b, s, tabs, w_in, cmp_pe, cmp_w1, cmp_b1, cmp_w2, sinks, fox_bf,
                 diff_lambda, diff_gain, w_branch, w_out, ln_g, ln_b, w_router):
    t = b * s
    (cos64, sin64), (cos32, sin32) = tabs
    tm = _tile(t, 1024)
    r64 = _proj(h_bf, _gather_cols(w_in, SEG_ROPE64), BF16, tm, 512, rope=(cos64, sin64, HEAD_DIM // 2))
    r32 = _proj(h_bf, _gather_cols(w_in, SEG_ROPE32), BF16, tm, 512, rope=(cos32, sin32, DIFF_SUB // 2))
    plain = _proj(h_bf, _gather_cols(w_in, SEG_PLAIN), BF16, tm, 512)
    small = _proj(h_bf, _gather_cols(w_in, SEG_SMALL, pad_to=LANES), F32, tm, LANES)
    r64_3, r32_3, plain_3, small_3 = (a.reshape(b, s, -1) for a in (r64, r32, plain, small))
    tkc = min(ATT_TKC, s // 2)
    tq = 2 * tkc

    ncp = s // NSA_CMP_D
    n_sel = s // NSA_SEL_L
    topn = min(NSA_TOPN, n_sel)

    def cmp_blocks(x2d):
        c = x2d.reshape(b, s, 2, HEAD_DIM).transpose(0, 2, 1, 3).reshape(b * 2, ncp, NSA_CMP_D * HEAD_DIM)
        nxt = jnp.concatenate([c[:, 1:], jnp.zeros_like(c[:, :1])], axis=1)
        return jnp.concatenate([c, nxt], axis=-1)

    xk = cmp_blocks(r64[:, 512:640])
    xv = cmp_blocks(plain[:, 0:128])
    cmp_kv = _nsa_compress(jnp.stack([xk, xv]), cmp_pe.reshape(2, 1, -1), cmp_w1.astype(BF16),
                           cmp_b1.reshape(2, 1, -1), cmp_w2.astype(BF16))
    kc = cmp_kv[0].astype(BF16).reshape(b, 2, ncp, HEAD_DIM)
    vc = cmp_kv[1].astype(BF16).reshape(b, 2, ncp, HEAD_DIM)
    kc4 = jnp.tile(kc, (1, 1, 1, 4))
    vct = vc.transpose(0, 1, 3, 2)
    selt = jnp.asarray(_selection_map_t(ncp, n_sel), BF16)
    o_cmp, mneg_t = _nsa_cmp_topk(r64_3, 0, kc4, vct, selt, tq=_tile(s, 256), topn=topn)
    vt_all = _vt_heads(jnp.concatenate([plain_3[..., 128:512], plain_3[..., 1536:2560]], axis=-1), tkc)
    o_win = _tattn(r64_3, 0, r64_3, 6, vt_all, 2, nmaps=2, qsel="gqa", vmap=(0, 0), mode="band",
                   window=NSA_WIN, fin="win", name="nsa_window_attention")
    e_mat = jnp.asarray(_gate_expand_matrices(), BF16)
    block_id = jax.nn.one_hot(jnp.arange(s) // NSA_SEL_L, HEAD_DIM, dtype=BF16)
    block_id = jnp.concatenate([block_id, block_id], axis=1)
    nsa_specs = [pl.BlockSpec((1, 1, tq, LANES), lambda bi, u, qi, *_: (bi, u // 2, qi, 0)),
                 pl.BlockSpec((s, LANES), lambda bi, u, qi, *_: (0, 0)),
                 pl.BlockSpec((1, 1, tq, LANES), lambda bi, u, qi, *_: (bi, u // 2, qi, u % 2)),
                 pl.BlockSpec((1, tq, LANES), lambda bi, u, qi, *_: (bi, qi, u)),
                 pl.BlockSpec((1, tq, LANES), lambda bi, u, qi, *_: (bi, qi, 0)),
                 pl.BlockSpec((1, 3, LANES, LANES), lambda bi, u, qi, *_: (u, 0, 0, 0))]
    o_a = _tattn(r64_3, 0, r64_3, 5, vt_all, 0, nmaps=2, qsel="gqa", vmap=(0, 0), mode="causal", fin="nsa",
                 bias="nsa", name="nsa_selected_attention", extras=(mneg_t, block_id, o_cmp, o_win, small_3, e_mat),
                 extra_specs=nsa_specs)

    o_b = _tattn(r64_3, 7, r64_3, 11, vt_all, 4, nmaps=2, qsel="gqa", vmap=(0, 0), mode="band",
                 window=SWA_WIN, fin="swa", name="swa_attention", scalars=sinks.astype(F32) * LOG2E)

    f_logit = (small[:, 24:32] + fox_bf[None, :]).reshape(b, s, 8).transpose(0, 2, 1)
    cum = _cum_log_forget(f_logit.reshape(b * 8, s // LANES, LANES)).reshape(b, 8, s)
    ck3 = jnp.stack(_split_bits(cum * LOG2E, 3), axis=-1)
    ck3 = jnp.concatenate([ck3, jnp.zeros((b, 8, s, HEAD_DIM - 3), BF16)], axis=-1)
    ck3 = jnp.concatenate([ck3, ck3], axis=-1)
    o_c = _tattn(plain_3, 4, plain_3, 8, vt_all, 6, nmaps=2, qsel="pair", vmap=(0, 1), mode="causal",
                 fin="fox", bias="fox", name="fox_attention", extras=(ck3,),
                 extra_specs=[pl.BlockSpec((1, 2, s, LANES), lambda bi, u, qi, *_: (bi, u, 0, 0))])

    lam_init = 0.8 - 0.6 * math.exp(-0.3 * layer)
    lf = diff_lambda.astype(F32)
    lam = jnp.exp(jnp.sum(lf[0] * lf[1])) - jnp.exp(jnp.sum(lf[2] * lf[3])) + lam_init
    lam_arr = jnp.stack([lam, jnp.asarray(1.0 - lam_init, F32)]).astype(F32)
    gain_t = jnp.broadcast_to(jnp.tile(diff_gain.astype(F32), 2)[:, None], (LANES, tq))
    o_d = _tattn(r32_3, 0, r32_3, 4, vt_all, 14, nmaps=4, qsel="pair", vmap=(0, 0, 1, 1), mode="causal",
                 fin="diff", name="diff_attention", scalars=lam_arr, extras=(gain_t,),
                 extra_specs=[pl.BlockSpec((LANES, tq), lambda bi, u, qi, *_: (0, 0))])

    o_list = [o.reshape(t, BRANCH_W) for o in (o_a, o_b, o_c, o_d)]
    merged = _merge(o_list, plain, 2560, w_branch.astype(BF16), tm=tm, tn=512)
    return _outproj_ln(merged, w_out.astype(BF16), h, ln_g.reshape(1, -1), ln_b.reshape(1, -1),
                       w_router, tm=_tile(t, 512))


def _moe_layer(h1, logits_pad, b_router, wg, wu, wd, *, tm, tf):
    t, d = h1.shape
    logits = logits_pad[:, :N_EXPERTS] + b_router.astype(F32)[None, :]
    top_v, top_i = lax.top_k(logits, TOP_K)
    top_w = jax.nn.softmax(top_v, axis=-1)
    flat_e = top_i.reshape(-1)
    onehot = jax.nn.one_hot(flat_e, N_EXPERTS, dtype=jnp.int32)
    rank = jnp.sum((jnp.cumsum(onehot, axis=0) - onehot) * onehot, axis=1)
    cnt = jnp.sum(onehot, axis=0)
    padded = ((cnt + tm - 1) // tm) * tm
    ends = jnp.cumsum(padded)
    starts = ends - padded
    pos = (starts[flat_e] + rank).astype(jnp.int32)
    n_rows = TOP_K * t + N_EXPERTS * tm
    row_token = jnp.zeros((n_rows,), jnp.int32).at[pos].set(jnp.arange(TOP_K * t, dtype=jnp.int32) // TOP_K)
    row_w = jnp.zeros((n_rows,), F32).at[pos].set(top_w.reshape(-1))
    tile_start = jnp.arange(n_rows // tm, dtype=jnp.int32) * tm
    tile_expert = jnp.minimum(jnp.sum(tile_start[:, None] >= ends[None, :], axis=1), N_EXPERTS - 1)
    n_tiles = (ends[-1] // tm).astype(jnp.int32).reshape(1)
    x_sorted = _gather_rows(row_token, h1, n_rows, BF16, tm=256)
    y_sorted = _moe_ffn(tile_expert.astype(jnp.int32), n_tiles, x_sorted, wg, wu, wd,
                        jnp.broadcast_to(row_w[:, None], (n_rows, LANES)), tm=tm, tf=tf)
    return _combine_pairs(pos, y_sorted, t, tm=_tile(t, 256))


def kernel(x, p, positions, w_in, nsa_cmp_pe, nsa_cmp_w1, nsa_cmp_b1, nsa_cmp_w2, swa_sinks, fox_bf,
           diff_lambda, diff_gain, w_branch, w_out, ln1_g, ln1_b, ffn_wg, ffn_wu, ffn_wd, moe_router,
           moe_router_b, moe_wg, moe_wu, moe_wd, ple_proj, ple_gate, ln2_g, ln2_b):
    b, s, d = x.shape
    t = b * s
    tabs = (_rope_tabs(positions, HEAD_DIM), _rope_tabs(positions, DIFF_SUB))
    h = x.reshape(t, d).astype(F32)
    h_bf = h.astype(BF16)
    for i in range(DEPTH):
        is_moe = i % 2 == 1
        w_router = None
        if is_moe:
            wr = jnp.zeros((d, LANES), F32).at[:, :N_EXPERTS].set(moe_router[i // 2].astype(F32))
            w_router = jnp.stack(_split_bits(wr, 2))
        res = _token_mixer(h, h_bf, i, b, s, tabs, w_in[i], nsa_cmp_pe[i], nsa_cmp_w1[i], nsa_cmp_b1[i],
                           nsa_cmp_w2[i], swa_sinks[i], fox_bf[i], diff_lambda[i], diff_gain[i],
                           w_branch[i], w_out[i], ln1_g[i], ln1_b[i], w_router)
        h1, h1_bf = res[0], res[1]
        if not is_moe:
            fpad = (-D_FF) % 512
            wg = jnp.pad(ffn_wg[i // 2].astype(BF16), ((0, 0), (0, fpad)))
            wu = jnp.pad(ffn_wu[i // 2].astype(BF16), ((0, 0), (0, fpad)))
            wd = jnp.pad(ffn_wd[i // 2].astype(BF16), ((0, fpad), (0, 0)))
            f = _ffn(h1_bf, wg, wu, wd, tm=_tile(t, 1024), tf=512)
        else:
            f = _moe_layer(h1, res[2], moe_router_b[i // 2], moe_wg[i // 2].astype(BF16),
                           moe_wu[i // 2].astype(BF16), moe_wd[i // 2].astype(BF16),
                           tm=_tile(t, 512), tf=1024)
        h, h_bf = _ple_ln(h1_bf, h1, f, p[i].reshape(t, PLE_DIM).astype(BF16), ple_gate[i].astype(BF16),
                          ple_proj[i].astype(BF16), ln2_g[i].reshape(1, -1), ln2_b[i].reshape(1, -1),
                          tm=_tile(t, 512))
    return h.reshape(b, s, d).astype(x.dtype)
```

```python
import functools
import math

import numpy as np
import jax
import jax.numpy as jnp
from jax import lax
from jax.experimental import pallas as pl
from jax.experimental.pallas import tpu as pltpu

F32 = jnp.float32
BF16 = jnp.bfloat16

D_MODEL = 2048
DEPTH = 2
HEAD_DIM = 64
ROPE_THETA = 10000.0
PLE_DIM = 256
LN_EPS = 1e-5
NSA_CMP_L = 32
NSA_CMP_D = 16
NSA_SEL_L = 64
NSA_TOPN = 16
NSA_WIN = 512
NSA_CMP_HIDDEN = 256
NSA_FORCE = 1e9
SWA_WIN = 128
DIFF_SUB = HEAD_DIM // 2
N_BRANCH = 4
BRANCH_W = 8 * HEAD_DIM
D_FF = 5504
N_EXPERTS = 8
TOP_K = 2
D_FF_EXPERT = 7168
ALPHA = (2.0 * DEPTH) ** 0.25

IN_SPLITS = (
    ("a_q", 512), ("a_kc", 128), ("a_vc", 128), ("a_ks", 128), ("a_vs", 128),
    ("a_kw", 128), ("a_vw", 128), ("a_g", 24),
    ("b_q", 512), ("b_k", 128), ("b_v", 128),
    ("c_q", 512), ("c_k", 512), ("c_v", 512), ("c_f", 8),
    ("d_q", 512), ("d_k", 512), ("d_v", 512),
    ("merge_gate", N_BRANCH * D_MODEL),
)
SEG_ROPE64 = ("a_q", "a_kc", "a_ks", "a_kw", "b_q", "b_k")
SEG_ROPE32 = ("d_q", "d_k")
SEG_PLAIN = ("c_q", "c_k", "a_vc", "a_vs", "a_vw", "b_v", "c_v", "d_v")
SEG_GATES = ("merge_gate",)
SEG_SMALL = ("a_g", "c_f")

LANES = 128
NEG = -1e30
LOG2E = math.log2(math.e)
VMEM_LIMIT = 56 * 1024 * 1024
ATT_TKC = 256
ATT_TQ = 2 * ATT_TKC
ATT_VROWS = 80

Q_FOLD = {"a_q": HEAD_DIM ** -0.5 * LOG2E, "b_q": HEAD_DIM ** -0.5 * LOG2E,
          "c_q": HEAD_DIM ** -0.5 * LOG2E, "d_q": DIFF_SUB ** -0.5 * LOG2E}


def _cparams(sem):
    return pltpu.CompilerParams(dimension_semantics=sem, vmem_limit_bytes=VMEM_LIMIT)


def _sigmoid(x):
    return 1.0 / (1.0 + jnp.exp(-x))


def _dot(a, b):
    return jnp.dot(a, b, preferred_element_type=F32)


def _dot_nt(a, b):
    return lax.dot_general(a, b, (((1,), (1,)), ((), ())), preferred_element_type=F32)


def _split2(x):
    hi = x.astype(BF16)
    lo = (x - hi.astype(F32)).astype(BF16)
    return hi, lo


def _split3(x):
    hi = x.astype(BF16)
    r = x - hi.astype(F32)
    mid = r.astype(BF16)
    lo = (r - mid.astype(F32)).astype(BF16)
    return hi, mid, lo


def _proj_body(x_ref, w_ref, *rest, rope_half):
    if rope_half:
        cos_ref, sin_ref, o_ref = rest
    else:
        (o_ref,) = rest
    acc = _dot(x_ref[...], w_ref[...])
    if not rope_half:
        o_ref[...] = acc.astype(o_ref.dtype)
        return
    cos = cos_ref[...]
    sin = sin_ref[...]
    lane = lax.broadcasted_iota(jnp.int32, cos.shape, 1)
    first = (lane % (2 * rope_half)) < rope_half
    for c in range(acc.shape[1] // LANES):
        a = acc[:, c * LANES:(c + 1) * LANES]
        rot = jnp.where(first, pltpu.roll(a, LANES - rope_half, 1), pltpu.roll(a, rope_half, 1))
        o_ref[:, c * LANES:(c + 1) * LANES] = (a * cos + rot * sin).astype(o_ref.dtype)


def _proj(x, w, out_dtype, tm, tn, rope=None):
    m, k = x.shape
    n = w.shape[1]
    in_specs = [pl.BlockSpec((tm, k), lambda i, j: (i, 0)),
                pl.BlockSpec((k, tn), lambda i, j: (0, j))]
    args = [x, w]
    rope_half = 0
    if rope is not None:
        cos_tab, sin_tab, rope_half = rope
        in_specs += [pl.BlockSpec((tm, LANES), lambda i, j: (i, 0)),
                     pl.BlockSpec((tm, LANES), lambda i, j: (i, 0))]
        args += [cos_tab, sin_tab]
    return pl.pallas_call(
        functools.partial(_proj_body, rope_half=rope_half),
        grid=(m // tm, n // tn),
        in_specs=in_specs,
        out_specs=pl.BlockSpec((tm, tn), lambda i, j: (i, j)),
        out_shape=jax.ShapeDtypeStruct((m, n), out_dtype),
        compiler_params=_cparams(("parallel", "parallel")),
        name="proj_rope" if rope_half else "proj",
    )(*args)


def _rope_tabs(positions, dim):
    inv = 1.0 / (ROPE_THETA ** (jnp.arange(0, dim, 2, dtype=F32) / dim))
    ang = positions.astype(F32).reshape(-1)[:, None] * inv
    c, s = jnp.cos(ang), jnp.sin(ang)
    reps = LANES // dim
    return (jnp.tile(jnp.concatenate([c, c], -1), (1, reps)),
            jnp.tile(jnp.concatenate([-s, s], -1), (1, reps)))


def _tattn_body(*refs, nmaps, qsel, vmap, tq, tkc, mode, window, bias, fin):
    refs = list(refs)
    sc_ref = refs.pop(0) if fin in ("diff", "swa") else None
    q_ref, k_ref, vt_ref = refs[:3]
    extras, o_ref = refs[3:-5], refs[-5]
    qm_ref, st_ref, m_ref, acc_ref = refs[-4:]
    bias_ref = None
    if bias is not None:
        bias_ref, extras = extras[0], extras[1:]
    u = pl.program_id(1)
    qi = pl.program_id(2)
    q0 = qi * tq
    last_chunk = k_ref.shape[1] // tkc - 1

    lane = lax.broadcasted_iota(jnp.int32, (tq, LANES), 1)
    klane = lax.broadcasted_iota(jnp.int32, (tkc, LANES), 1)
    q = q_ref[0].astype(F32)
    if qsel == "gqa":
        lo_lane = (u // 2) * HEAD_DIM
        q_rolled = pltpu.roll(q, HEAD_DIM, 1)
        in_group = (lane >= lo_lane) & (lane < lo_lane + HEAD_DIM)
        k_in_group = (klane >= lo_lane) & (klane < lo_lane + HEAD_DIM)
        fill = bias_ref[0, 0].astype(F32) if bias == "nsa" else 0.0
        for e in range(2):
            q_e = jnp.where(lo_lane == e * HEAD_DIM, q, q_rolled)
            qm_ref[e] = jnp.where(in_group, q_e, fill).T.astype(BF16)
    else:
        width = LANES // nmaps
        fill = jnp.where(lane % HEAD_DIM < 3, -1.0, 0.0) if bias == "fox" else 0.0
        for mp in range(nmaps):
            qm_ref[mp] = jnp.where((lane >= mp * width) & (lane < (mp + 1) * width), q, fill).T.astype(BF16)

    m_ref[...] = jnp.full(m_ref.shape, NEG, F32)
    acc_ref[...] = jnp.zeros(acc_ref.shape, F32)

    def qk(c, buf, q_lo=0):
        cc = jnp.minimum(c, last_chunk)
        off = pl.multiple_of(cc * tkc, tkc)
        kc = k_ref[0, pl.ds(off, tkc), :]
        if bias == "nsa":
            kc = jnp.where(k_in_group, kc, extras[0][pl.ds(off, tkc), :])
        for mp in range(nmaps):
            kc_mp = kc
            if bias == "fox":
                kc_mp = jnp.where((klane >= mp * HEAD_DIM) & (klane < (mp + 1) * HEAD_DIM), kc,
                                  bias_ref[0, mp, pl.ds(off, tkc), :])
            st_ref[buf, mp, :, q_lo:] = _dot(kc_mp, qm_ref[mp, :, q_lo:])

    def soft(c, buf, masked, q_lo=0, q_hi=tq):
        for mp in range(nmaps):
            st = st_ref[buf, mp, :, q_lo:q_hi]
            if masked:
                key = c * tkc + lax.broadcasted_iota(jnp.int32, st.shape, 0)
                t_pos = q0 + q_lo + lax.broadcasted_iota(jnp.int32, st.shape, 1)
                keep = key <= t_pos
                if mode == "band":
                    keep = keep & (t_pos - key < window)
                st = jnp.where(keep, st, NEG)
            m_old = m_ref[mp, :, q_lo:q_hi]
            m8 = jnp.max(st.reshape(tkc // 8, 8, q_hi - q_lo), axis=0)
            m_new = jnp.maximum(m_old, jnp.max(m8, axis=0, keepdims=True))
            p = jnp.exp2(st - m_new).astype(BF16)
            acc_ref[mp, :, q_lo:q_hi] = (jnp.exp2(m_old - m_new) * acc_ref[mp, :, q_lo:q_hi]
                                         + _dot(vt_ref[0, vmap[mp], c], p))
            m_ref[mp, :, q_lo:q_hi] = m_new

    def pair(pidx, carry, masked):
        c0 = 2 * pidx
        qk(c0 + 1, 1)
        soft(c0, 0, masked)
        qk(c0 + 2, 0)
        soft(c0 + 1, 1, masked)
        return carry

    lo = 0 if mode == "causal" else jnp.maximum(q0 - (window - 1), 0) // tq
    qk(2 * lo, 0)
    lax.fori_loop(lo, qi, functools.partial(pair, masked=mode == "band"), 0)
    qk(2 * qi + 1, 1, q_lo=tkc)
    soft(2 * qi, 0, True, q_hi=tkc)
    soft(2 * qi, 0, mode == "band", q_lo=tkc)
    soft(2 * qi + 1, 1, True, q_lo=tkc)

    def normed(mp):
        acc = acc_ref[mp]
        num, l_i = acc[:HEAD_DIM], acc[HEAD_DIM:HEAD_DIM + 1]
        if fin == "swa":
            m_i = m_ref[mp]
            sk = sc_ref[2 * u + mp]
            m_f = jnp.maximum(m_i, sk)
            corr = jnp.exp2(m_i - m_f)
            return num * (corr / (l_i * corr + jnp.exp2(sk - m_f)))
        return num * (1.0 / l_i)

    if fin == "diff":
        lam = sc_ref[0]
        halves = []
        for hh in range(2):
            o = normed(2 * hh) - lam * normed(2 * hh + 1)
            ms = jnp.mean(o * o, axis=0, keepdims=True)
            halves.append(o * lax.rsqrt(ms + LN_EPS))
        ot = jnp.concatenate(halves, axis=0) * extras[0][...] * sc_ref[1]
    else:
        ot = jnp.concatenate([normed(0), normed(1)], axis=0)
    o = ot.T
    if fin == "nsa":
        ocmp_ref, owin_ref, sm_ref, e_ref = extras[1:]
        hi, lo_part = _split2(sm_ref[0])
        gates = [_sigmoid(_dot(hi, e_ref[0, c]) + _dot(lo_part, e_ref[0, c])) for c in range(3)]
        o = gates[0] * ocmp_ref[0, 0] + gates[1] * o + gates[2] * owin_ref[0].astype(F32)
    o_ref[0] = o.astype(o_ref.dtype)


def _tattn(q_arr, q_blk, k_arr, k_blk, vt_all, v_head, *, nmaps, qsel, vmap, mode, fin, name, window=0,
           bias=None, scalars=None, extras=(), extra_specs=()):
    b, s, _ = q_arr.shape
    nc, vrows, tkc = vt_all.shape[2], vt_all.shape[3], vt_all.shape[4]
    tq = 2 * tkc
    nu = 4
    if qsel == "gqa":
        kspec = pl.BlockSpec((1, s, LANES), lambda bi, u, qi, *_: (bi, 0, k_blk))
        vspec = pl.BlockSpec((1, 1, nc, vrows, tkc), lambda bi, u, qi, *_: (bi, v_head + u // 2, 0, 0, 0))
    else:
        kspec = pl.BlockSpec((1, s, LANES), lambda bi, u, qi, *_: (bi, 0, k_blk + u))
        vspec = pl.BlockSpec((1, 2, nc, vrows, tkc), lambda bi, u, qi, *_: (bi, v_head // 2 + u, 0, 0, 0))
    in_specs = [pl.BlockSpec((1, tq, LANES), lambda bi, u, qi, *_: (bi, qi, q_blk + u)), kspec, vspec]
    in_specs += list(extra_specs)
    body = functools.partial(_tattn_body, nmaps=nmaps, qsel=qsel, vmap=vmap, tq=tq, tkc=tkc, mode=mode,
                             window=window, bias=bias, fin=fin)
    args = ([] if scalars is None else [scalars]) + [q_arr, k_arr, vt_all] + list(extras)
    return pl.pallas_call(
        body,
        grid_spec=pltpu.PrefetchScalarGridSpec(
            num_scalar_prefetch=0 if scalars is None else 1, grid=(b, nu, s // tq),
            in_specs=in_specs,
            out_specs=pl.BlockSpec((1, tq, LANES), lambda bi, u, qi, *_: (bi, qi, u)),
            scratch_shapes=[pltpu.VMEM((nmaps, LANES, tq), BF16), pltpu.VMEM((2, nmaps, tkc, tq), F32),
                            pltpu.VMEM((nmaps, 1, tq), F32), pltpu.VMEM((nmaps, vrows, tq), F32)]),
        out_shape=jax.ShapeDtypeStruct((b, s, nu * LANES), BF16),
        compiler_params=_cparams(("parallel", "parallel", "parallel")),
        name=name,
    )(*args)


def _gelu_tanh(x):
    return 0.5 * x * (1.0 + jnp.tanh(math.sqrt(2.0 / math.pi) * (x + 0.044715 * (x * x * x))))


def _compress_body(x_ref, pe_ref, w1_ref, b1_ref, w2_ref, o_ref):
    x = (x_ref[0, 0].astype(F32) + pe_ref[0]).astype(BF16)
    hid = _gelu_tanh(_dot(x, w1_ref[0]) + b1_ref[0])
    o_ref[0, 0] = _dot(hid.astype(BF16), w2_ref[0])


def _nsa_compress(x, pe, w1, b1, w2):
    _, nb, ncp, ld = x.shape
    hid = w1.shape[-1]
    return pl.pallas_call(
        _compress_body,
        grid=(2, nb),
        in_specs=[pl.BlockSpec((1, 1, ncp, ld), lambda t, i: (t, i, 0, 0)),
                  pl.BlockSpec((1, 1, ld), lambda t, i: (t, 0, 0)),
                  pl.BlockSpec((1, ld, hid), lambda t, i: (t, 0, 0)),
                  pl.BlockSpec((1, 1, hid), lambda t, i: (t, 0, 0)),
                  pl.BlockSpec((1, hid, HEAD_DIM), lambda t, i: (t, 0, 0))],
        out_specs=pl.BlockSpec((1, 1, ncp, HEAD_DIM), lambda t, i: (t, i, 0, 0)),
        out_shape=jax.ShapeDtypeStruct((2, nb, ncp, HEAD_DIM), F32),
        compiler_params=_cparams(("parallel", "parallel")),
        name="nsa_compress",
    )(x, pe, w1, b1, w2)


def _cmp_topk_body(q_ref, kc_ref, vct_ref, selt_ref, o_ref, mt_ref, *, tq, ncp, nsel, topn):
    qi = pl.program_id(2)
    q = q_ref[0]
    kc4 = kc_ref[0, 0]
    vct = vct_ref[0, 0]
    lane = lax.broadcasted_iota(jnp.int32, kc4.shape, 1)
    ci = lax.broadcasted_iota(jnp.int32, (ncp, tq), 0)
    tpos = qi * tq + lax.broadcasted_iota(jnp.int32, (ncp, tq), 1)
    cmask = ci * NSA_CMP_D + (NSA_CMP_L - 1) <= tpos
    psum = jnp.zeros((ncp, tq), F32)
    rows = []
    for a in range(4):
        kcm = jnp.where((lane >= a * HEAD_DIM) & (lane < (a + 1) * HEAD_DIM), kc4, jnp.zeros_like(kc4))
        st = jnp.where(cmask, _dot_nt(kcm, q), NEG)
        m = jnp.max(st, axis=0, keepdims=True)
        e = jnp.where(cmask, jnp.exp2(st - m), 0.0)
        l = jnp.sum(e, axis=0, keepdims=True)
        p = e * jnp.where(l > 0.0, 1.0 / l, 0.0)
        psum = psum + p
        rows.append(_dot(vct, p.astype(BF16)))
    hi, lo = _split2(psum)
    selt = selt_ref[...]
    imp = _dot(selt, hi) + _dot(selt, lo)
    blk = lax.broadcasted_iota(jnp.int32, (nsel, tq), 0)
    cur = (qi * tq + lax.broadcasted_iota(jnp.int32, (nsel, tq), 1)) // NSA_SEL_L
    forced = (blk == 0) | (blk == cur) | (blk == cur - 1)
    imp = jnp.where(forced, NSA_FORCE, jnp.where(blk > cur, -NSA_FORCE, imp))
    cnt = jnp.zeros((nsel, tq), jnp.int32)
    for jp in range(nsel):
        v = imp[jp:jp + 1, :]
        tie = jnp.where(blk > jp, 1, 0)
        cnt = cnt + jnp.where(v > imp, 1, jnp.where(v == imp, tie, 0))
    mneg = jnp.where(cnt < topn, 0.0, NEG)
    if nsel < HEAD_DIM:
        mneg = jnp.concatenate([mneg, jnp.zeros((HEAD_DIM - nsel, tq), F32)], axis=0)
    mt_ref[0, 0] = jnp.concatenate([mneg, mneg], axis=0).T.astype(mt_ref.dtype)
    o_ref[0, 0] = jnp.concatenate(rows, axis=0).T


def _nsa_cmp_topk(q_arr, q_off256, kc4, vct, selt, *, tq, topn):
    b, s, _ = q_arr.shape
    ncp = kc4.shape[2]
    nsel = selt.shape[0]
    body = functools.partial(_cmp_topk_body, tq=tq, ncp=ncp, nsel=nsel, topn=topn)
    return pl.pallas_call(
        body,
        grid=(b, 2, s // tq),
        in_specs=[pl.BlockSpec((1, tq, 2 * LANES), lambda bi, g, qi: (bi, qi, q_off256 + g)),
                  pl.BlockSpec((1, 1, ncp, 2 * LANES), lambda bi, g, qi: (bi, g, 0, 0)),
                  pl.BlockSpec((1, 1, HEAD_DIM, ncp), lambda bi, g, qi: (bi, g, 0, 0)),
                  pl.BlockSpec((nsel, ncp), lambda bi, g, qi: (0, 0))],
        out_specs=[pl.BlockSpec((1, 1, tq, 2 * LANES), lambda bi, g, qi: (bi, g, qi, 0)),
                   pl.BlockSpec((1, 1, tq, LANES), lambda bi, g, qi: (bi, g, qi, 0))],
        out_shape=[jax.ShapeDtypeStruct((b, 2, s, 2 * LANES), F32),
                   jax.ShapeDtypeStruct((b, 2, s, LANES), BF16)],
        compiler_params=_cparams(("parallel", "parallel", "parallel")),
        name="nsa_cmp_topk",
    )(q_arr, kc4, vct, selt)


def _cumgate_body(x_ref, o_ref):
    x = x_ref[0]
    r = x.shape[0]
    ls = jnp.minimum(x, 0.0) - jnp.log1p(jnp.exp(-jnp.abs(x)))
    i0 = lax.broadcasted_iota(jnp.int32, (LANES, LANES), 0)
    i1 = lax.broadcasted_iota(jnp.int32, (LANES, LANES), 1)
    upper = jnp.where(i0 <= i1, 1.0, 0.0).astype(BF16)
    ones = jnp.ones((LANES, LANES), BF16)
    r0 = lax.broadcasted_iota(jnp.int32, (r, r), 0)
    r1 = lax.broadcasted_iota(jnp.int32, (r, r), 1)
    strict = jnp.where(r1 < r0, 1.0, 0.0).astype(BF16)
    parts = _split3(ls)
    intra = sum(_dot(pp, upper) for pp in parts)
    rowtot = sum(_dot(pp, ones) for pp in parts)
    off = sum(_dot(strict, pp) for pp in _split3(rowtot))
    o_ref[0] = intra + off


def _cum_log_forget(x):
    n, r, _ = x.shape
    return pl.pallas_call(
        _cumgate_body,
        grid=(n,),
        in_specs=[pl.BlockSpec((1, r, LANES), lambda i: (i, 0, 0))],
        out_specs=pl.BlockSpec((1, r, LANES), lambda i: (i, 0, 0)),
        out_shape=jax.ShapeDtypeStruct((n, r, LANES), F32),
        compiler_params=_cparams(("parallel",)),
        name="cum_log_forget",
    )(x)


def _merge_body(oa_ref, ob_ref, oc_ref, od_ref, g0_ref, g1_ref, g2_ref, g3_ref, wb_ref, o_ref):
    acc = None
    for n, (o_r, g_r) in enumerate(((oa_ref, g0_ref), (ob_ref, g1_ref), (oc_ref, g2_ref), (od_ref, g3_ref))):
        term = _sigmoid(g_r[...].astype(F32)) * _dot(o_r[...], wb_ref[n])
        acc = term if acc is None else acc + term
    o_ref[...] = acc.astype(o_ref.dtype)


def _merge(o_list, plain, gate_off, wb, *, tm, tn):
    t = plain.shape[0]
    d = wb.shape[-1]
    nj = d // tn
    ospec = pl.BlockSpec((tm, BRANCH_W), lambda i, j: (i, 0))
    gspecs = [pl.BlockSpec((tm, tn), functools.partial(lambda i, j, n: (i, gate_off // tn + n * nj + j), n=n))
              for n in range(N_BRANCH)]
    return pl.pallas_call(
        _merge_body,
        grid=(t // tm, nj),
        in_specs=[ospec] * 4 + gspecs + [pl.BlockSpec((N_BRANCH, BRANCH_W, tn), lambda i, j: (0, 0, j))],
        out_specs=pl.BlockSpec((tm, tn), lambda i, j: (i, j)),
        out_shape=jax.ShapeDtypeStruct((t, d), BF16),
        compiler_params=_cparams(("parallel", "parallel")),
        name="gated_merge",
    )(*o_list, plain, plain, plain, plain, wb)


def _layer_norm(y, g, b):
    mu = jnp.mean(y, axis=-1, keepdims=True)
    yc = y - mu
    var = jnp.mean(yc * yc, axis=-1, keepdims=True)
    return yc * lax.rsqrt(var + LN_EPS) * g + b


def _outproj_ln_body(mg_ref, wo_ref, h_ref, g_ref, b_ref, *rest, with_router):
    if with_router:
        wr_ref, o_ref, ob_ref, lg_ref = rest
    else:
        o_ref, ob_ref = rest
    y = ALPHA * h_ref[...] + _dot(mg_ref[...], wo_ref[...])
    out = _layer_norm(y, g_ref[...], b_ref[...])
    o_ref[...] = out
    ob_ref[...] = out.astype(BF16)
    if with_router:
        hi, lo = _split2(out)
        lg_ref[...] = _dot(hi, wr_ref[0]) + _dot(lo, wr_ref[0]) + _dot(hi, wr_ref[1])


def _outproj_ln(merged, w_out, h, g, b, w_router=None, *, tm):
    t, d = h.shape
    with_router = w_router is not None
    row = lambda i: (i, 0)
    fix = lambda i: (0, 0)
    in_specs = [pl.BlockSpec((tm, d), row), pl.BlockSpec((d, d), fix, pipeline_mode=pl.Buffered(1)),
                pl.BlockSpec((tm, d), row),
                pl.BlockSpec((1, d), fix), pl.BlockSpec((1, d), fix)]
    out_specs = [pl.BlockSpec((tm, d), row), pl.BlockSpec((tm, d), row)]
    out_shape = [jax.ShapeDtypeStruct((t, d), F32), jax.ShapeDtypeStruct((t, d), BF16)]
    args = [merged, w_out, h, g, b]
    if with_router:
        in_specs.append(pl.BlockSpec((2, d, LANES), lambda i: (0, 0, 0)))
        out_specs.append(pl.BlockSpec((tm, LANES), row))
        out_shape.append(jax.ShapeDtypeStruct((t, LANES), F32))
        args.append(w_router)
    return pl.pallas_call(
        functools.partial(_outproj_ln_body, with_router=with_router),
        grid=(t // tm,), in_specs=in_specs, out_specs=out_specs, out_shape=out_shape,
        compiler_params=_cparams(("parallel",)), name="outproj_ln1",
    )(*args)


def _ple_ln_body(hb_ref, h_ref, f_ref, p_ref, wg_ref, wp_ref, g_ref, b_ref, o_ref, ob_ref):
    ple = _sigmoid(_dot(hb_ref[...], wg_ref[...])) * _dot(p_ref[...], wp_ref[...])
    out = _layer_norm(ALPHA * h_ref[...] + f_ref[...] + ple, g_ref[...], b_ref[...])
    o_ref[...] = out
    ob_ref[...] = out.astype(BF16)


def _ple_ln(h_bf, h, f, p_bf, w_gate, w_proj, g, b, *, tm):
    t, d = h.shape
    row = lambda i: (i, 0)
    fix = lambda i: (0, 0)
    return pl.pallas_call(
        _ple_ln_body,
        grid=(t // tm,),
        in_specs=[pl.BlockSpec((tm, d), row), pl.BlockSpec((tm, d), row), pl.BlockSpec((tm, d), row),
                  pl.BlockSpec((tm, PLE_DIM), row), pl.BlockSpec((d, d), fix, pipeline_mode=pl.Buffered(1)),
                  pl.BlockSpec((PLE_DIM, d), fix), pl.BlockSpec((1, d), fix), pl.BlockSpec((1, d), fix)],
        out_specs=[pl.BlockSpec((tm, d), row), pl.BlockSpec((tm, d), row)],
        out_shape=[jax.ShapeDtypeStruct((t, d), F32), jax.ShapeDtypeStruct((t, d), BF16)],
        compiler_params=_cparams(("parallel",)), name="ple_ln2",
    )(h_bf, h, f, p_bf, w_gate, w_proj, g, b)


def _swiglu_tile(x, wg, wu, wd):
    g = _dot(x, wg)
    u = _dot(x, wu)
    return _dot((g * _sigmoid(g) * u).astype(BF16), wd)


def _ffn_body(x_ref, wg_ref, wu_ref, wd_ref, o_ref):
    j = pl.program_id(1)
    y = _swiglu_tile(x_ref[...], wg_ref[...], wu_ref[...], wd_ref[...])

    @pl.when(j == 0)
    def _():
        o_ref[...] = y

    @pl.when(j > 0)
    def _():
        o_ref[...] += y


def _ffn(x_bf, wg, wu, wd, *, tm, tf):
    t, d = x_bf.shape
    f = wg.shape[1]
    return pl.pallas_call(
        _ffn_body,
        grid=(t // tm, f // tf),
        in_specs=[pl.BlockSpec((tm, d), lambda i, j: (i, 0)),
                  pl.BlockSpec((d, tf), lambda i, j: (0, j)),
                  pl.BlockSpec((d, tf), lambda i, j: (0, j)),
                  pl.BlockSpec((tf, d), lambda i, j: (j, 0))],
        out_specs=pl.BlockSpec((tm, d), lambda i, j: (i, 0)),
        out_shape=jax.ShapeDtypeStruct((t, d), F32),
        compiler_params=_cparams(("parallel", "arbitrary")), name="ffn_swiglu",
    )(x_bf, wg, wu, wd)


def _moe_ffn_body(te_ref, nt_ref, x_ref, wg_ref, wu_ref, wd_ref, rw_ref, o_ref):
    i = pl.program_id(0)
    j = pl.program_id(1)
    active = i < nt_ref[0]

    @pl.when(active)
    def _():
        y = _swiglu_tile(x_ref[...], wg_ref[0], wu_ref[0], wd_ref[0])

        @pl.when(j == 0)
        def _():
            o_ref[...] = y

        @pl.when(j > 0)
        def _():
            o_ref[...] += y

        @pl.when(j == pl.num_programs(1) - 1)
        def _():
            o_ref[...] = o_ref[...] * rw_ref[:, 0:1]

    @pl.when(jnp.logical_not(active) & (j == 0))
    def _():
        o_ref[...] = jnp.zeros(o_ref.shape, F32)


def _moe_ffn(tile_expert, n_tiles, x_sorted, wg, wu, wd, row_w, *, tm, tf):
    r, d = x_sorted.shape
    f = wg.shape[2]
    nj = f // tf

    def jj(i, j, nt):
        return jnp.where(i < nt[0], j, nj - 1)

    return pl.pallas_call(
        _moe_ffn_body,
        grid_spec=pltpu.PrefetchScalarGridSpec(
            num_scalar_prefetch=2, grid=(r // tm, nj),
            in_specs=[pl.BlockSpec((tm, d), lambda i, j, te, nt: (i, 0)),
                      pl.BlockSpec((1, d, tf), lambda i, j, te, nt: (te[i], 0, jj(i, j, nt))),
                      pl.BlockSpec((1, d, tf), lambda i, j, te, nt: (te[i], 0, jj(i, j, nt))),
                      pl.BlockSpec((1, tf, d), lambda i, j, te, nt: (te[i], jj(i, j, nt), 0)),
                      pl.BlockSpec((tm, LANES), lambda i, j, te, nt: (i, 0))],
            out_specs=pl.BlockSpec((tm, d), lambda i, j, te, nt: (i, 0))),
        out_shape=jax.ShapeDtypeStruct((r, d), F32),
        compiler_params=_cparams(("arbitrary", "arbitrary")), name="moe_grouped_ffn",
    )(tile_expert, n_tiles, x_sorted, wg, wu, wd, row_w)


def _row_copy(src_ref, src_row, dst_ref, dst_row, sem):
    return pltpu.make_async_copy(src_ref.at[pl.ds(src_row, 1)], dst_ref.at[pl.ds(dst_row, 1)], sem)


def _gather_rows_body(idx_ref, src_ref, o_ref, buf_ref, sem, *, tm):
    base = pl.program_id(0) * tm

    def start(r, c):
        _row_copy(src_ref, idx_ref[base + r], buf_ref, r, sem).start()
        return c

    def wait(r, c):
        _row_copy(src_ref, 0, buf_ref, r, sem).wait()
        return c

    lax.fori_loop(0, tm, start, 0)
    lax.fori_loop(0, tm, wait, 0)
    o_ref[...] = buf_ref[...].astype(o_ref.dtype)


def _gather_rows(idx, src, n_rows, out_dtype, *, tm):
    d = src.shape[1]
    return pl.pallas_call(
        functools.partial(_gather_rows_body, tm=tm),
        grid_spec=pltpu.PrefetchScalarGridSpec(
            num_scalar_prefetch=1, grid=(n_rows // tm,),
            in_specs=[pl.BlockSpec(memory_space=pl.ANY)],
            out_specs=pl.BlockSpec((tm, d), lambda i, idx: (i, 0)),
            scratch_shapes=[pltpu.VMEM((tm, d), src.dtype), pltpu.SemaphoreType.DMA(())]),
        out_shape=jax.ShapeDtypeStruct((n_rows, d), out_dtype),
        compiler_params=_cparams(("arbitrary",)), name="moe_gather_rows",
    )(idx, src)


def _combine_body(idx_ref, src_ref, o_ref, a_ref, b_ref, sem, *, tm):
    base = pl.program_id(0) * tm

    def start(r, c):
        _row_copy(src_ref, idx_ref[2 * (base + r)], a_ref, r, sem.at[0]).start()
        _row_copy(src_ref, idx_ref[2 * (base + r) + 1], b_ref, r, sem.at[1]).start()
        return c

    def wait(r, c):
        _row_copy(src_ref, 0, a_ref, r, sem.at[0]).wait()
        _row_copy(src_ref, 0, b_ref, r, sem.at[1]).wait()
        return c

    lax.fori_loop(0, tm, start, 0)
    lax.fori_loop(0, tm, wait, 0)
    o_ref[...] = a_ref[...] + b_ref[...]


def _combine_pairs(pos, y_sorted, n_tokens, *, tm):
    d = y_sorted.shape[1]
    return pl.pallas_call(
        functools.partial(_combine_body, tm=tm),
        grid_spec=pltpu.PrefetchScalarGridSpec(
            num_scalar_prefetch=1, grid=(n_tokens // tm,),
            in_specs=[pl.BlockSpec(memory_space=pl.ANY)],
            out_specs=pl.BlockSpec((tm, d), lambda i, idx: (i, 0)),
            scratch_shapes=[pltpu.VMEM((tm, d), F32), pltpu.VMEM((tm, d), F32),
                            pltpu.SemaphoreType.DMA((2,))]),
        out_shape=jax.ShapeDtypeStruct((n_tokens, d), F32),
        compiler_params=_cparams(("arbitrary",)), name="moe_combine",
    )(pos, y_sorted)


def _col_slices():
    out, off = {}, 0
    for name, width in IN_SPLITS:
        out[name] = (off, width)
        off += width
    return out


def _gather_cols(w, names, pad_to=None):
    cs = _col_slices()
    parts = []
    for n in names:
        col = w[:, cs[n][0]:cs[n][0] + cs[n][1]]
        parts.append(col * Q_FOLD[n] if n in Q_FOLD else col)
    width = sum(cs[n][1] for n in names)
    if pad_to is not None and pad_to > width:
        parts.append(jnp.zeros((w.shape[0], pad_to - width), w.dtype))
    return jnp.concatenate(parts, axis=1).astype(BF16)


def _selection_map_t(n_cmp_pad, n_sel):
    ci = np.arange(n_cmp_pad)[:, None] * NSA_CMP_D
    sj = np.arange(n_sel)[None, :] * NSA_SEL_L
    ov = np.clip(np.minimum(ci + NSA_CMP_L, sj + NSA_SEL_L) - np.maximum(ci, sj), 0, None)
    return np.ascontiguousarray((ov / NSA_CMP_D).astype(np.float32).T)


def _gate_expand_matrices():
    e = np.zeros((4, 3, LANES, LANES), np.float32)
    for j in range(4):
        for hh in range(2):
            for c in range(3):
                e[j, c, (2 * j + hh) * 3 + c, hh * HEAD_DIM:(hh + 1) * HEAD_DIM] = 1.0
    return e


def _split_bits(x, n):
    parts = []
    r = x
    for _ in range(n):
        hi = lax.bitcast_convert_type(
            lax.bitcast_convert_type(r, jnp.uint32) & jnp.uint32(0xFFFF0000), F32)
        parts.append(hi.astype(BF16))
        r = r - hi
    return parts


def _tile(n, pref):
    return pref if n % pref == 0 else n


def _vt_heads(x3, tkc):
    b, s, c = x3.shape
    nh, nc = c // HEAD_DIM, s // tkc
    v = x3.reshape(b, nc, tkc, nh, HEAD_DIM).transpose(0, 3, 1, 4, 2)
    ones = jnp.ones((b, nh, nc, 1, tkc), v.dtype)
    zeros = jnp.zeros((b, nh, nc, ATT_VROWS - HEAD_DIM - 1, tkc), v.dtype)
    return jnp.concatenate([v, ones, zeros], axis=3)


def _token_mixer(h, h_bf, layer, b, s, tabs, w_in, cmp_pe, cmp_w1, cmp_b1, cmp_w2, sinks, fox_bf,
                 diff_lambda, diff_gain, w_branch, w_out, ln_g, ln_b, w_router):
    t = b * s
    (cos64, sin64), (cos32, sin32) = tabs
    tm = _tile(t, 1024)
    r64 = _proj(h_bf, _gather_cols(w_in, SEG_ROPE64), BF16, tm, 512, rope=(cos64, sin64, HEAD_DIM // 2))
    r32 = _proj(h_bf, _gather_cols(w_in, SEG_ROPE32), BF16, tm, 512, rope=(cos32, sin32, DIFF_SUB // 2))
    plain = _proj(h_bf, _gather_cols(w_in, SEG_PLAIN), BF16, tm, 512)
    gates = _proj(h_bf, _gather_cols(w_in, SEG_GATES), BF16, tm, 512)
    small =_proj(h_bf, _gather_cols(w_in, SEG_SMALL, pad_to=LANES), F32, tm, LANES)
    r64_3, r32_3, plain_3, small_3 = (a.reshape(b, s, -1) for a in (r64, r32, plain, small))
    tkc = min(ATT_TKC, s // 2)
    tq = 2 * tkc

    ncp = s // NSA_CMP_D
    n_sel = s // NSA_SEL_L
    topn = min(NSA_TOPN, n_sel)

    def cmp_blocks(x2d):
        c = x2d.reshape(b, s, 2, HEAD_DIM).transpose(0, 2, 1, 3).reshape(b * 2, ncp, NSA_CMP_D * HEAD_DIM)
        nxt = jnp.concatenate([c[:, 1:], jnp.zeros_like(c[:, :1])], axis=1)
        return jnp.concatenate([c, nxt], axis=-1)

    xk = cmp_blocks(r64[:, 512:640])
    xv = cmp_blocks(plain[:, 1024:1152])
    cmp_kv = _nsa_compress(jnp.stack([xk, xv]), cmp_pe.reshape(2, 1, -1), cmp_w1.astype(BF16),
                           cmp_b1.reshape(2, 1, -1), cmp_w2.astype(BF16))
    kc = cmp_kv[0].astype(BF16).reshape(b, 2, ncp, HEAD_DIM)
    vc = cmp_kv[1].astype(BF16).reshape(b, 2, ncp, HEAD_DIM)
    kc4 = jnp.tile(kc, (1, 1, 1, 4))
    vct = vc.transpose(0, 1, 3, 2)
    selt = jnp.asarray(_selection_map_t(ncp, n_sel), BF16)
    o_cmp, mneg_t = _nsa_cmp_topk(r64_3, 0, kc4, vct, selt, tq=_tile(s, 256), topn=topn)
    vt_all = _vt_heads(plain_3[..., 1152:2560], tkc)
    o_win = _tattn(r64_3, 0, r64_3, 6, vt_all, 2, nmaps=2, qsel="gqa", vmap=(0, 0), mode="band",
                   window=NSA_WIN, fin="win", name="nsa_window_attention")
    e_mat = jnp.asarray(_gate_expand_matrices(), BF16)
    block_id = jax.nn.one_hot(jnp.arange(s) // NSA_SEL_L, HEAD_DIM, dtype=BF16)
    block_id = jnp.concatenate([block_id, block_id], axis=1)
    nsa_specs = [pl.BlockSpec((1, 1, tq, LANES), lambda bi, u, qi, *_: (bi, u // 2, qi, 0)),
                 pl.BlockSpec((s, LANES), lambda bi, u, qi, *_: (0, 0)),
                 pl.BlockSpec((1, 1, tq, LANES), lambda bi, u, qi, *_: (bi, u // 2, qi, u % 2)),
                 pl.BlockSpec((1, tq, LANES), lambda bi, u, qi, *_: (bi, qi, u)),
                 pl.BlockSpec((1, tq, LANES), lambda bi, u, qi, *_: (bi, qi, 0)),
                 pl.BlockSpec((1, 3, LANES, LANES), lambda bi, u, qi, *_: (u, 0, 0, 0))]
    o_a = _tattn(r64_3, 0, r64_3, 5, vt_all, 0, nmaps=2, qsel="gqa", vmap=(0, 0), mode="causal", fin="nsa",
                 bias="nsa", name="nsa_selected_attention", extras=(mneg_t, block_id, o_cmp, o_win, small_3, e_mat),
                 extra_specs=nsa_specs)

    o_b = _tattn(r64_3, 7, r64_3, 11, vt_all, 4, nmaps=2, qsel="gqa", vmap=(0, 0), mode="band",
                 window=SWA_WIN, fin="swa", name="swa_attention", scalars=sinks.astype(F32) * LOG2E)

    f_logit = (small[:, 24:32] + fox_bf[None, :]).reshape(b, s, 8).transpose(0, 2, 1)
    cum = _cum_log_forget(f_logit.reshape(b * 8, s // LANES, LANES)).reshape(b, 8, s)
    hi, mid, lo_piece = (piece[..., None] for piece in _split_bits(cum * LOG2E, 3))
    slot = (jnp.arange(LANES) % HEAD_DIM)[None, None, None, :]
    ck3 = jnp.where(slot == 0, hi, jnp.where(slot == 1, mid, jnp.where(slot == 2, lo_piece, jnp.zeros((), BF16))))
    o_c = _tattn(plain_3, 0, plain_3, 4, vt_all, 6, nmaps=2, qsel="pair", vmap=(0, 1), mode="causal",
                 fin="fox", bias="fox", name="fox_attention", extras=(ck3,),
                 extra_specs=[pl.BlockSpec((1, 2, s, LANES), lambda bi, u, qi, *_: (bi, u, 0, 0))])

    lam_init = 0.8 - 0.6 * math.exp(-0.3 * layer)
    lf = diff_lambda.astype(F32)
    lam = jnp.exp(jnp.sum(lf[0] * lf[1])) - jnp.exp(jnp.sum(lf[2] * lf[3])) + lam_init
    lam_arr = jnp.stack([lam, jnp.asarray(1.0 - lam_init, F32)]).astype(F32)
    gain_t = jnp.broadcast_to(jnp.tile(diff_gain.astype(F32), 2)[:, None], (LANES, tq))
    o_d = _tattn(r32_3, 0, r32_3, 4, vt_all, 14, nmaps=4, qsel="pair", vmap=(0, 0, 1, 1), mode="causal",
                 fin="diff", name="diff_attention", scalars=lam_arr, extras=(gain_t,),
                 extra_specs=[pl.BlockSpec((LANES, tq), lambda bi, u, qi, *_: (0, 0))])

    o_list = [o.reshape(t, BRANCH_W) for o in (o_a, o_b, o_c, o_d)]
    merged = _merge(o_list, gates, 0, w_branch.astype(BF16), tm=tm, tn=512)
    return _outproj_ln(merged, w_out.astype(BF16), h, ln_g.reshape(1, -1), ln_b.reshape(1, -1),
                       w_router, tm=_tile(t, 256))


def _moe_layer(h1, logits_pad, b_router, wg, wu, wd, *, tm, tf):
    t, d = h1.shape
    logits = logits_pad[:, :N_EXPERTS] + b_router.astype(F32)[None, :]
    top_v, top_i = lax.top_k(logits, TOP_K)
    top_w = jax.nn.softmax(top_v, axis=-1)
    flat_e = top_i.reshape(-1)
    onehot = jax.nn.one_hot(flat_e, N_EXPERTS, dtype=jnp.int32)
    rank = jnp.sum((jnp.cumsum(onehot, axis=0) - onehot) * onehot, axis=1)
    cnt = jnp.sum(onehot, axis=0)
    padded = ((cnt + tm - 1) // tm) * tm
    ends = jnp.cumsum(padded)
    starts = ends - padded
    pos = (starts[flat_e] + rank).astype(jnp.int32)
    n_rows = TOP_K * t + N_EXPERTS * tm
    row_token = jnp.zeros((n_rows,), jnp.int32).at[pos].set(jnp.arange(TOP_K * t, dtype=jnp.int32) // TOP_K)
    row_w = jnp.zeros((n_rows,), F32).at[pos].set(top_w.reshape(-1))
    tile_start = jnp.arange(n_rows // tm, dtype=jnp.int32) * tm
    tile_expert = jnp.minimum(jnp.sum(tile_start[:, None] >= ends[None, :], axis=1), N_EXPERTS - 1)
    n_tiles = (ends[-1] // tm).astype(jnp.int32).reshape(1)
    x_sorted = _gather_rows(row_token, h1, n_rows, BF16, tm=256)
    y_sorted = _moe_ffn(tile_expert.astype(jnp.int32), n_tiles, x_sorted, wg, wu, wd,
                        jnp.broadcast_to(row_w[:, None], (n_rows, LANES)), tm=tm, tf=tf)
    return _combine_pairs(pos, y_sorted, t, tm=_tile(t, 256))


def kernel(x, p, positions, w_in, nsa_cmp_pe, nsa_cmp_w1, nsa_cmp_b1, nsa_cmp_w2, swa_sinks, fox_bf,
           diff_lambda, diff_gain, w_branch, w_out, ln1_g, ln1_b, ffn_wg, ffn_wu, ffn_wd, moe_router,
           moe_router_b, moe_wg, moe_wu, moe_wd, ple_proj, ple_gate, ln2_g, ln2_b):
    b, s, d = x.shape
    t = b * s
    tabs = (_rope_tabs(positions, HEAD_DIM), _rope_tabs(positions, DIFF_SUB))
    h = x.reshape(t, d).astype(F32)
    h_bf = h.astype(BF16)
    for i in range(DEPTH):
        is_moe = i % 2 == 1
        w_router = None
        if is_moe:
            wr = jnp.zeros((d, LANES), F32).at[:, :N_EXPERTS].set(moe_router[i // 2].astype(F32))
            w_router = jnp.stack(_split_bits(wr, 2))
        res = _token_mixer(h, h_bf, i, b, s, tabs, w_in[i], nsa_cmp_pe[i], nsa_cmp_w1[i], nsa_cmp_b1[i],
                           nsa_cmp_w2[i], swa_sinks[i], fox_bf[i], diff_lambda[i], diff_gain[i],
                           w_branch[i], w_out[i], ln1_g[i], ln1_b[i], w_router)
        h1, h1_bf = res[0], res[1]
        if not is_moe:
            fpad = (-D_FF) % 512
            wg = jnp.pad(ffn_wg[i // 2].astype(BF16), ((0, 0), (0, fpad)))
            wu = jnp.pad(ffn_wu[i // 2].astype(BF16), ((0, 0), (0, fpad)))
            wd = jnp.pad(ffn_wd[i // 2].astype(BF16), ((0, fpad), (0, 0)))
            f = _ffn(h1_bf, wg, wu, wd, tm=_tile(t, 1024), tf=512)
        else:
            f = _moe_layer(h1, res[2], moe_router_b[i // 2], moe_wg[i // 2].astype(BF16),
                           moe_wu[i // 2].astype(BF16), moe_wd[i // 2].astype(BF16),
                           tm=_tile(t, 512), tf=1024)
        h, h_bf = _ple_ln(h1_bf, h1, f, p[i].reshape(t, PLE_DIM).astype(BF16), ple_gate[i].astype(BF16),
                          ple_proj[i].astype(BF16), ln2_g[i].reshape(1, -1), ln2_b[i].reshape(1, -1),
                          tm=_tile(t, 512))
    return h.reshape(b, s, d).astype(x.dtype)
```

```python
import functools
import math

import numpy as np
import jax
import jax.numpy as jnp
from jax import lax
from jax.experimental import pallas as pl
from jax.experimental.pallas import tpu as pltpu

F32 = jnp.float32
BF16 = jnp.bfloat16

D_MODEL = 2048
DEPTH = 2
HEAD_DIM = 64
ROPE_THETA = 10000.0
PLE_DIM = 256
LN_EPS = 1e-5
NSA_CMP_L = 32
NSA_CMP_D = 16
NSA_SEL_L = 64
NSA_TOPN = 16
NSA_WIN = 512
NSA_CMP_HIDDEN = 256
NSA_FORCE = 1e9
SWA_WIN = 128
DIFF_SUB = HEAD_DIM // 2
N_BRANCH = 4
BRANCH_W = 8 * HEAD_DIM
D_FF = 5504
N_EXPERTS = 8
TOP_K = 2
D_FF_EXPERT = 7168
ALPHA = (2.0 * DEPTH) ** 0.25

IN_SPLITS = (
    ("a_q", 512), ("a_kc", 128), ("a_vc", 128), ("a_ks", 128), ("a_vs", 128),
    ("a_kw", 128), ("a_vw", 128), ("a_g", 24),
    ("b_q", 512), ("b_k", 128), ("b_v", 128),
    ("c_q", 512), ("c_k", 512), ("c_v", 512), ("c_f", 8),
    ("d_q", 512), ("d_k", 512), ("d_v", 512),
    ("merge_gate", N_BRANCH * D_MODEL),
)
SEG_ROPE64 = ("a_q", "a_kc", "a_ks", "a_kw", "b_q", "b_k")
SEG_ROPE32 = ("d_q", "d_k")
SEG_PLAIN = ("c_q", "c_k", "a_vc", "a_vs", "a_vw", "b_v", "c_v", "d_v")
SEG_GATES = ("merge_gate",)
SEG_SMALL = ("a_g", "c_f")

LANES = 128
NEG = -1e30
LOG2E = math.log2(math.e)
VMEM_LIMIT = 56 * 1024 * 1024
ATT_TKC = 256
ATT_TQ = 2 * ATT_TKC
ATT_VROWS = 80

Q_FOLD = {"a_q": HEAD_DIM ** -0.5 * LOG2E, "b_q": HEAD_DIM ** -0.5 * LOG2E,
          "c_q": HEAD_DIM ** -0.5 * LOG2E, "d_q": DIFF_SUB ** -0.5 * LOG2E}


def _cparams(sem):
    return pltpu.CompilerParams(dimension_semantics=sem, vmem_limit_bytes=VMEM_LIMIT)


def _sigmoid(x):
    return 1.0 / (1.0 + jnp.exp(-x))


def _dot(a, b):
    return jnp.dot(a, b, preferred_element_type=F32)


def _dot_nt(a, b):
    return lax.dot_general(a, b, (((1,), (1,)), ((), ())), preferred_element_type=F32)


def _split2(x):
    hi = x.astype(BF16)
    lo = (x - hi.astype(F32)).astype(BF16)
    return hi, lo


def _split3(x):
    hi = x.astype(BF16)
    r = x - hi.astype(F32)
    mid = r.astype(BF16)
    lo = (r - mid.astype(F32)).astype(BF16)
    return hi, mid, lo


def _proj_body(x_ref, w_ref, *rest, rope_half):
    if rope_half:
        cos_ref, sin_ref, o_ref = rest
    else:
        (o_ref,) = rest
    acc = _dot(x_ref[...], w_ref[...])
    if not rope_half:
        o_ref[...] = acc.astype(o_ref.dtype)
        return
    cos = cos_ref[...]
    sin = sin_ref[...]
    lane = lax.broadcasted_iota(jnp.int32, cos.shape, 1)
    first = (lane % (2 * rope_half)) < rope_half
    for c in range(acc.shape[1] // LANES):
        a = acc[:, c * LANES:(c + 1) * LANES]
        rot = jnp.where(first, pltpu.roll(a, LANES - rope_half, 1), pltpu.roll(a, rope_half, 1))
        o_ref[:, c * LANES:(c + 1) * LANES] = (a * cos + rot * sin).astype(o_ref.dtype)


def _proj(x, w, out_dtype, tm, tn, rope=None):
    m, k = x.shape
    n = w.shape[1]
    in_specs = [pl.BlockSpec((tm, k), lambda i, j: (i, 0)),
                pl.BlockSpec((k, tn), lambda i, j: (0, j))]
    args = [x, w]
    rope_half = 0
    if rope is not None:
        cos_tab, sin_tab, rope_half = rope
        in_specs += [pl.BlockSpec((tm, LANES), lambda i, j: (i, 0)),
                     pl.BlockSpec((tm, LANES), lambda i, j: (i, 0))]
        args += [cos_tab, sin_tab]
    return pl.pallas_call(
        functools.partial(_proj_body, rope_half=rope_half),
        grid=(m // tm, n // tn),
        in_specs=in_specs,
        out_specs=pl.BlockSpec((tm, tn), lambda i, j: (i, j)),
        out_shape=jax.ShapeDtypeStruct((m, n), out_dtype),
        compiler_params=_cparams(("parallel", "parallel")),
        name="proj_rope" if rope_half else "proj",
    )(*args)


def _rope_tabs(positions, dim):
    inv = 1.0 / (ROPE_THETA ** (jnp.arange(0, dim, 2, dtype=F32) / dim))
    ang = positions.astype(F32).reshape(-1)[:, None] * inv
    c, s = jnp.cos(ang), jnp.sin(ang)
    reps = LANES // dim
    return (jnp.tile(jnp.concatenate([c, c], -1), (1, reps)),
            jnp.tile(jnp.concatenate([-s, s], -1), (1, reps)))


def _tattn_body(*refs, nmaps, qsel, vmap, tq, tkc, mode, window, bias, fin):
    refs = list(refs)
    sc_ref = refs.pop(0) if fin in ("diff", "swa") else None
    q_ref, k_ref, vt_ref = refs[:3]
    extras, o_ref = refs[3:-5], refs[-5]
    qm_ref, st_ref, m_ref, acc_ref = refs[-4:]
    bias_ref = None
    if bias is not None:
        bias_ref, extras = extras[0], extras[1:]
    u = pl.program_id(1)
    qi = pl.program_id(2)
    q0 = qi * tq
    last_chunk = k_ref.shape[1] // tkc - 1

    lane = lax.broadcasted_iota(jnp.int32, (tq, LANES), 1)
    klane = lax.broadcasted_iota(jnp.int32, (tkc, LANES), 1)
    q = q_ref[0].astype(F32)
    if qsel == "gqa":
        lo_lane = (u // 2) * HEAD_DIM
        q_rolled = pltpu.roll(q, HEAD_DIM, 1)
        in_group = (lane >= lo_lane) & (lane < lo_lane + HEAD_DIM)
        k_in_group = (klane >= lo_lane) & (klane < lo_lane + HEAD_DIM)
        fill = bias_ref[0, 0].astype(F32) if bias == "nsa" else 0.0
        for e in range(2):
            q_e = jnp.where(lo_lane == e * HEAD_DIM, q, q_rolled)
            qm_ref[e] = jnp.where(in_group, q_e, fill).T.astype(BF16)
    else:
        width = LANES // nmaps
        fill = jnp.where(lane % HEAD_DIM < 3, -1.0, 0.0) if bias == "fox" else 0.0
        for mp in range(nmaps):
            qm_ref[mp] = jnp.where((lane >= mp * width) & (lane < (mp + 1) * width), q, fill).T.astype(BF16)

    m_ref[...] = jnp.full(m_ref.shape, NEG, F32)
    acc_ref[...] = jnp.zeros(acc_ref.shape, F32)

    def qk(c, buf, q_lo=0, q_hi=tq):
        cc = jnp.clip(c, 0, last_chunk)
        off = pl.multiple_of(cc * tkc, tkc)
        kc = k_ref[0, pl.ds(off, tkc), :]
        if bias == "nsa":
            kc = jnp.where(k_in_group, kc, extras[0][pl.ds(off, tkc), :])
        for mp in range(nmaps):
            kc_mp = kc
            if bias == "fox":
                kc_mp = jnp.where((klane >= mp * HEAD_DIM) & (klane < (mp + 1) * HEAD_DIM), kc,
                                  bias_ref[0, mp, pl.ds(off, tkc), :])
            st_ref[buf, mp, :, q_lo:q_hi] = _dot(kc_mp, qm_ref[mp, :, q_lo:q_hi])

    def soft(c, buf, masked, q_lo=0, q_hi=tq):
        for mp in range(nmaps):
            st = st_ref[buf, mp, :, q_lo:q_hi]
            if masked:
                key = c * tkc + lax.broadcasted_iota(jnp.int32, st.shape, 0)
                t_pos = q0 + q_lo + lax.broadcasted_iota(jnp.int32, st.shape, 1)
                keep = key <= t_pos
                if mode == "band":
                    keep = keep & (t_pos - key < window) & (key >= 0)
                st = jnp.where(keep, st, NEG)
            m_old = m_ref[mp, :, q_lo:q_hi]
            m8 = jnp.max(st.reshape(tkc // 8, 8, q_hi - q_lo), axis=0)
            m_new = jnp.maximum(m_old, jnp.max(m8, axis=0, keepdims=True))
            p = jnp.exp2(st - m_new).astype(BF16)
            acc_ref[mp, :, q_lo:q_hi] = (jnp.exp2(m_old - m_new) * acc_ref[mp, :, q_lo:q_hi]
                                         + _dot(vt_ref[0, vmap[mp], jnp.maximum(c, 0)], p))
            m_ref[mp, :, q_lo:q_hi] = m_new

    def pair(pidx, carry, masked):
        c0 = 2 * pidx
        qk(c0 + 1, 1)
        soft(c0, 0, masked)
        qk(c0 + 2, 0)
        soft(c0 + 1, 1, masked)
        return carry

    if mode == "causal":
        qk(0, 0)
        lax.fori_loop(0, qi, functools.partial(pair, masked=False), 0)
        qk(2 * qi + 1, 1, q_lo=tkc)
        soft(2 * qi, 0, True, q_hi=tkc)
        soft(2 * qi, 0, False, q_lo=tkc)
        soft(2 * qi + 1, 1, True, q_lo=tkc)
    else:
        sched = []
        for d in range(-((window - 1 + tkc - 1) // tkc), 2):
            hi_lane = min(tq, d * tkc + tkc - 1 + window)
            sched.append((2 * qi + d, max(0, d * tkc), -(-hi_lane // LANES) * LANES))
        qk(sched[0][0], 0, sched[0][1], sched[0][2])
        for i, (c, lo_lane_q, hi_lane_q) in enumerate(sched):
            if i + 1 < len(sched):
                qk(sched[i + 1][0], (i + 1) % 2, sched[i + 1][1], sched[i + 1][2])
            soft(c, i % 2, True, lo_lane_q, hi_lane_q)

    def normed(mp):
        acc = acc_ref[mp]
        num, l_i = acc[:HEAD_DIM], acc[HEAD_DIM:HEAD_DIM + 1]
        if fin == "swa":
            m_i = m_ref[mp]
            sk = sc_ref[2 * u + mp]
            m_f = jnp.maximum(m_i, sk)
            corr = jnp.exp2(m_i - m_f)
            return num * (corr / (l_i * corr + jnp.exp2(sk - m_f)))
        return num * (1.0 / l_i)

    if fin == "diff":
        lam = sc_ref[0]
        halves = []
        for hh in range(2):
            o = normed(2 * hh) - lam * normed(2 * hh + 1)
            ms = jnp.mean(o * o, axis=0, keepdims=True)
            halves.append(o * lax.rsqrt(ms + LN_EPS))
        ot = jnp.concatenate(halves, axis=0) * extras[0][...] * sc_ref[1]
    else:
        ot = jnp.concatenate([normed(0), normed(1)], axis=0)
    o = ot.T
    if fin == "nsa":
        ocmp_ref, owin_ref, sm_ref, e_ref = extras[1:]
        hi, lo_part = _split2(sm_ref[0])
        gates = [_sigmoid(_dot(hi, e_ref[0, c]) + _dot(lo_part, e_ref[0, c])) for c in range(3)]
        o = gates[0] * ocmp_ref[0, 0] + gates[1] * o + gates[2] * owin_ref[0].astype(F32)
    o_ref[0] = o.astype(o_ref.dtype)


def _tattn(q_arr, q_blk, k_arr, k_blk, vt_all, v_head, *, nmaps, qsel, vmap, mode, fin, name, window=0,
           bias=None, scalars=None, extras=(), extra_specs=()):
    b, s, _ = q_arr.shape
    nc, vrows, tkc = vt_all.shape[2], vt_all.shape[3], vt_all.shape[4]
    tq = 2 * tkc
    nu = 4
    if qsel == "gqa":
        kspec = pl.BlockSpec((1, s, LANES), lambda bi, u, qi, *_: (bi, 0, k_blk))
        vspec = pl.BlockSpec((1, 1, nc, vrows, tkc), lambda bi, u, qi, *_: (bi, v_head + u // 2, 0, 0, 0))
    else:
        kspec = pl.BlockSpec((1, s, LANES), lambda bi, u, qi, *_: (bi, 0, k_blk + u))
        vspec = pl.BlockSpec((1, 2, nc, vrows, tkc), lambda bi, u, qi, *_: (bi, v_head // 2 + u, 0, 0, 0))
    in_specs = [pl.BlockSpec((1, tq, LANES), lambda bi, u, qi, *_: (bi, qi, q_blk + u)), kspec, vspec]
    in_specs += list(extra_specs)
    body = functools.partial(_tattn_body, nmaps=nmaps, qsel=qsel, vmap=vmap, tq=tq, tkc=tkc, mode=mode,
                             window=window, bias=bias, fin=fin)
    args = ([] if scalars is None else [scalars]) + [q_arr, k_arr, vt_all] + list(extras)
    return pl.pallas_call(
        body,
        grid_spec=pltpu.PrefetchScalarGridSpec(
            num_scalar_prefetch=0 if scalars is None else 1, grid=(b, nu, s // tq),
            in_specs=in_specs,
            out_specs=pl.BlockSpec((1, tq, LANES), lambda bi, u, qi, *_: (bi, qi, u)),
            scratch_shapes=[pltpu.VMEM((nmaps, LANES, tq), BF16), pltpu.VMEM((2, nmaps, tkc, tq), F32),
                            pltpu.VMEM((nmaps, 1, tq), F32), pltpu.VMEM((nmaps, vrows, tq), F32)]),
        out_shape=jax.ShapeDtypeStruct((b, s, nu * LANES), BF16),
        compiler_params=_cparams(("parallel", "parallel", "parallel")),
        name=name,
    )(*args)


def _gelu_tanh(x):
    return 0.5 * x * (1.0 + jnp.tanh(math.sqrt(2.0 / math.pi) * (x + 0.044715 * (x * x * x))))


def _compress_body(x_ref, pe_ref, w1_ref, b1_ref, w2_ref, o_ref):
    x = (x_ref[0, 0].astype(F32) + pe_ref[0]).astype(BF16)
    hid = _gelu_tanh(_dot(x, w1_ref[0]) + b1_ref[0])
    o_ref[0, 0] = _dot(hid.astype(BF16), w2_ref[0])


def _nsa_compress(x, pe, w1, b1, w2):
    _, nb, ncp, ld = x.shape
    hid = w1.shape[-1]
    return pl.pallas_call(
        _compress_body,
        grid=(2, nb),
        in_specs=[pl.BlockSpec((1, 1, ncp, ld), lambda t, i: (t, i, 0, 0)),
                  pl.BlockSpec((1, 1, ld), lambda t, i: (t, 0, 0)),
                  pl.BlockSpec((1, ld, hid), lambda t, i: (t, 0, 0)),
                  pl.BlockSpec((1, 1, hid), lambda t, i: (t, 0, 0)),
                  pl.BlockSpec((1, hid, HEAD_DIM), lambda t, i: (t, 0, 0))],
        out_specs=pl.BlockSpec((1, 1, ncp, HEAD_DIM), lambda t, i: (t, i, 0, 0)),
        out_shape=jax.ShapeDtypeStruct((2, nb, ncp, HEAD_DIM), F32),
        compiler_params=_cparams(("parallel", "parallel")),
        name="nsa_compress",
    )(x, pe, w1, b1, w2)


def _cmp_topk_body(q_ref, kc_ref, vct_ref, selt_ref, o_ref, mt_ref, *, tq, ncp, nsel, topn):
    qi = pl.program_id(2)
    q = q_ref[0]
    kc4 = kc_ref[0, 0]
    vct = vct_ref[0, 0]
    lane = lax.broadcasted_iota(jnp.int32, kc4.shape, 1)
    ci = lax.broadcasted_iota(jnp.int32, (ncp, tq), 0)
    tpos = qi * tq + lax.broadcasted_iota(jnp.int32, (ncp, tq), 1)
    cmask = ci * NSA_CMP_D + (NSA_CMP_L - 1) <= tpos
    psum = jnp.zeros((ncp, tq), F32)
    rows = []
    for a in range(4):
        kcm = jnp.where((lane >= a * HEAD_DIM) & (lane < (a + 1) * HEAD_DIM), kc4, jnp.zeros_like(kc4))
        st = jnp.where(cmask, _dot_nt(kcm, q), NEG)
        m = jnp.max(st, axis=0, keepdims=True)
        e = jnp.where(cmask, jnp.exp2(st - m), 0.0)
        l = jnp.sum(e, axis=0, keepdims=True)
        p = e * jnp.where(l > 0.0, 1.0 / l, 0.0)
        psum = psum + p
        rows.append(_dot(vct, p.astype(BF16)))
    hi, lo = _split2(psum)
    selt = selt_ref[...]
    imp = _dot(selt, hi) + _dot(selt, lo)
    blk = lax.broadcasted_iota(jnp.int32, (nsel, tq), 0)
    cur = (qi * tq + lax.broadcasted_iota(jnp.int32, (nsel, tq), 1)) // NSA_SEL_L
    forced = (blk == 0) | (blk == cur) | (blk == cur - 1)
    imp = jnp.where(forced, NSA_FORCE, jnp.where(blk > cur, -NSA_FORCE, imp))
    cnt = jnp.zeros((nsel, tq), jnp.int32)
    for jp in range(nsel):
        v = imp[jp:jp + 1, :]
        tie = jnp.where(blk > jp, 1, 0)
        cnt = cnt + jnp.where(v > imp, 1, jnp.where(v == imp, tie, 0))
    mneg = jnp.where(cnt < topn, 0.0, NEG)
    if nsel < HEAD_DIM:
        mneg = jnp.concatenate([mneg, jnp.zeros((HEAD_DIM - nsel, tq), F32)], axis=0)
    mt_ref[0, 0] = jnp.concatenate([mneg, mneg], axis=0).T.astype(mt_ref.dtype)
    o_ref[0, 0] = jnp.concatenate(rows, axis=0).T


def _nsa_cmp_topk(q_arr, q_off256, kc4, vct, selt, *, tq, topn):
    b, s, _ = q_arr.shape
    ncp = kc4.shape[2]
    nsel = selt.shape[0]
    body = functools.partial(_cmp_topk_body, tq=tq, ncp=ncp, nsel=nsel, topn=topn)
    return pl.pallas_call(
        body,
        grid=(b, 2, s // tq),
        in_specs=[pl.BlockSpec((1, tq, 2 * LANES), lambda bi, g, qi: (bi, qi, q_off256 + g)),
                  pl.BlockSpec((1, 1, ncp, 2 * LANES), lambda bi, g, qi: (bi, g, 0, 0)),
                  pl.BlockSpec((1, 1, HEAD_DIM, ncp), lambda bi, g, qi: (bi, g, 0, 0)),
                  pl.BlockSpec((nsel, ncp), lambda bi, g, qi: (0, 0))],
        out_specs=[pl.BlockSpec((1, 1, tq, 2 * LANES), lambda bi, g, qi: (bi, g, qi, 0)),
                   pl.BlockSpec((1, 1, tq, LANES), lambda bi, g, qi: (bi, g, qi, 0))],
        out_shape=[jax.ShapeDtypeStruct((b, 2, s, 2 * LANES), F32),
                   jax.ShapeDtypeStruct((b, 2, s, LANES), BF16)],
        compiler_params=_cparams(("parallel", "parallel", "parallel")),
        name="nsa_cmp_topk",
    )(q_arr, kc4, vct, selt)


def _cumgate_body(x_ref, o_ref):
    x = x_ref[0]
    r = x.shape[0]
    ls = jnp.minimum(x, 0.0) - jnp.log1p(jnp.exp(-jnp.abs(x)))
    i0 = lax.broadcasted_iota(jnp.int32, (LANES, LANES), 0)
    i1 = lax.broadcasted_iota(jnp.int32, (LANES, LANES), 1)
    upper = jnp.where(i0 <= i1, 1.0, 0.0).astype(BF16)
    ones = jnp.ones((LANES, LANES), BF16)
    r0 = lax.broadcasted_iota(jnp.int32, (r, r), 0)
    r1 = lax.broadcasted_iota(jnp.int32, (r, r), 1)
    strict = jnp.where(r1 < r0, 1.0, 0.0).astype(BF16)
    parts = _split3(ls)
    intra = sum(_dot(pp, upper) for pp in parts)
    rowtot = sum(_dot(pp, ones) for pp in parts)
    off = sum(_dot(strict, pp) for pp in _split3(rowtot))
    o_ref[0] = intra + off


def _cum_log_forget(x):
    n, r, _ = x.shape
    return pl.pallas_call(
        _cumgate_body,
        grid=(n,),
        in_specs=[pl.BlockSpec((1, r, LANES), lambda i: (i, 0, 0))],
        out_specs=pl.BlockSpec((1, r, LANES), lambda i: (i, 0, 0)),
        out_shape=jax.ShapeDtypeStruct((n, r, LANES), F32),
        compiler_params=_cparams(("parallel",)),
        name="cum_log_forget",
    )(x)


def _merge_body(oa_ref, ob_ref, oc_ref, od_ref, g0_ref, g1_ref, g2_ref, g3_ref, wb_ref, o_ref):
    acc = None
    for n, (o_r, g_r) in enumerate(((oa_ref, g0_ref), (ob_ref, g1_ref), (oc_ref, g2_ref), (od_ref, g3_ref))):
        term = _sigmoid(g_r[...].astype(F32)) * _dot(o_r[...], wb_ref[n])
        acc = term if acc is None else acc + term
    o_ref[...] = acc.astype(o_ref.dtype)


def _merge(o_list, plain, gate_off, wb, *, tm, tn):
    t = plain.shape[0]
    d = wb.shape[-1]
    nj = d // tn
    ospec = pl.BlockSpec((tm, BRANCH_W), lambda i, j: (i, 0))
    gspecs = [pl.BlockSpec((tm, tn), functools.partial(lambda i, j, n: (i, gate_off // tn + n * nj + j), n=n))
              for n in range(N_BRANCH)]
    return pl.pallas_call(
        _merge_body,
        grid=(t // tm, nj),
        in_specs=[ospec] * 4 + gspecs + [pl.BlockSpec((N_BRANCH, BRANCH_W, tn), lambda i, j: (0, 0, j))],
        out_specs=pl.BlockSpec((tm, tn), lambda i, j: (i, j)),
        out_shape=jax.ShapeDtypeStruct((t, d), BF16),
        compiler_params=_cparams(("parallel", "parallel")),
        name="gated_merge",
    )(*o_list, plain, plain, plain, plain, wb)


def _layer_norm(y, g, b):
    mu = jnp.mean(y, axis=-1, keepdims=True)
    yc = y - mu
    var = jnp.mean(yc * yc, axis=-1, keepdims=True)
    return yc * lax.rsqrt(var + LN_EPS) * g + b


def _outproj_ln_body(mg_ref, wo_ref, h_ref, g_ref, b_ref, *rest, with_router):
    if with_router:
        wr_ref, o_ref, ob_ref, lg_ref = rest
    else:
        o_ref, ob_ref = rest
    y = ALPHA * h_ref[...] + _dot(mg_ref[...], wo_ref[...])
    out = _layer_norm(y, g_ref[...], b_ref[...])
    o_ref[...] = out
    ob_ref[...] = out.astype(BF16)
    if with_router:
        hi, lo = _split2(out)
        lg_ref[...] = _dot(hi, wr_ref[0]) + _dot(lo, wr_ref[0]) + _dot(hi, wr_ref[1])


def _outproj_ln(merged, w_out, h, g, b, w_router=None, *, tm):
    t, d = h.shape
    with_router = w_router is not None
    row = lambda i: (i, 0)
    fix = lambda i: (0, 0)
    in_specs = [pl.BlockSpec((tm, d), row), pl.BlockSpec((d, d), fix, pipeline_mode=pl.Buffered(1)),
                pl.BlockSpec((tm, d), row),
                pl.BlockSpec((1, d), fix), pl.BlockSpec((1, d), fix)]
    out_specs = [pl.BlockSpec((tm, d), row), pl.BlockSpec((tm, d), row)]
    out_shape = [jax.ShapeDtypeStruct((t, d), F32), jax.ShapeDtypeStruct((t, d), BF16)]
    args = [merged, w_out, h, g, b]
    if with_router:
        in_specs.append(pl.BlockSpec((2, d, LANES), lambda i: (0, 0, 0)))
        out_specs.append(pl.BlockSpec((tm, LANES), row))
        out_shape.append(jax.ShapeDtypeStruct((t, LANES), F32))
        args.append(w_router)
    return pl.pallas_call(
        functools.partial(_outproj_ln_body, with_router=with_router),
        grid=(t // tm,), in_specs=in_specs, out_specs=out_specs, out_shape=out_shape,
        compiler_params=_cparams(("parallel",)), name="outproj_ln1",
    )(*args)


def _ple_ln_body(hb_ref, h_ref, f_ref, p_ref, wg_ref, wp_ref, g_ref, b_ref, o_ref, ob_ref):
    ple = _sigmoid(_dot(hb_ref[...], wg_ref[...])) * _dot(p_ref[...], wp_ref[...])
    out = _layer_norm(ALPHA * h_ref[...] + f_ref[...] + ple, g_ref[...], b_ref[...])
    o_ref[...] = out
    ob_ref[...] = out.astype(BF16)


def _ple_ln(h_bf, h, f, p_bf, w_gate, w_proj, g, b, *, tm):
    t, d = h.shape
    row = lambda i: (i, 0)
    fix = lambda i: (0, 0)
    return pl.pallas_call(
        _ple_ln_body,
        grid=(t // tm,),
        in_specs=[pl.BlockSpec((tm, d), row), pl.BlockSpec((tm, d), row), pl.BlockSpec((tm, d), row),
                  pl.BlockSpec((tm, PLE_DIM), row), pl.BlockSpec((d, d), fix, pipeline_mode=pl.Buffered(1)),
                  pl.BlockSpec((PLE_DIM, d), fix), pl.BlockSpec((1, d), fix), pl.BlockSpec((1, d), fix)],
        out_specs=[pl.BlockSpec((tm, d), row), pl.BlockSpec((tm, d), row)],
        out_shape=[jax.ShapeDtypeStruct((t, d), F32), jax.ShapeDtypeStruct((t, d), BF16)],
        compiler_params=_cparams(("parallel",)), name="ple_ln2",
    )(h_bf, h, f, p_bf, w_gate, w_proj, g, b)


def _swiglu_tile(x, wg, wu, wd):
    g = _dot(x, wg)
    u = _dot(x, wu)
    return _dot((g * _sigmoid(g) * u).astype(BF16), wd)


def _ffn_body(x_ref, wg_ref, wu_ref, wd_ref, o_ref):
    j = pl.program_id(1)
    y = _swiglu_tile(x_ref[...], wg_ref[...], wu_ref[...], wd_ref[...])

    @pl.when(j == 0)
    def _():
        o_ref[...] = y

    @pl.when(j > 0)
    def _():
        o_ref[...] += y


def _ffn(x_bf, wg, wu, wd, *, tm, tf):
    t, d = x_bf.shape
    f = wg.shape[1]
    return pl.pallas_call(
        _ffn_body,
        grid=(t // tm, f // tf),
        in_specs=[pl.BlockSpec((tm, d), lambda i, j: (i, 0)),
                  pl.BlockSpec((d, tf), lambda i, j: (0, j)),
                  pl.BlockSpec((d, tf), lambda i, j: (0, j)),
                  pl.BlockSpec((tf, d), lambda i, j: (j, 0))],
        out_specs=pl.BlockSpec((tm, d), lambda i, j: (i, 0)),
        out_shape=jax.ShapeDtypeStruct((t, d), F32),
        compiler_params=_cparams(("parallel", "arbitrary")), name="ffn_swiglu",
    )(x_bf, wg, wu, wd)


def _row_copy(src_ref, src_row, dst_ref, dst_row, sem):
    return pltpu.make_async_copy(src_ref.at[pl.ds(src_row, 1)], dst_ref.at[pl.ds(dst_row, 1)], sem)


def _moe_ffn_body(te_ref, nt_ref, tok_ref, h_ref, wg_ref, wu_ref, wd_ref, rw_ref, o_ref,
                  xf_ref, xb_ref, sem, *, tm, rows_per_step):
    i = pl.program_id(0)
    j = pl.program_id(1)
    nt = nt_ref[0]
    active = i < nt

    def start_rows(tile, r0, r1):
        slot = tile % 2

        def body(r, c):
            _row_copy(h_ref, tok_ref[tile * tm + r], xf_ref.at[slot], r, sem.at[slot]).start()
            return c

        lax.fori_loop(r0, r1, body, 0)

    def wait_rows(slot):
        def body(r, c):
            _row_copy(h_ref, 0, xf_ref.at[slot], r, sem.at[slot]).wait()
            return c

        lax.fori_loop(0, tm, body, 0)

    @pl.when((i == 0) & (j == 0))
    def _():
        start_rows(0, 0, tm)

    @pl.when(active & (j == 0))
    def _():
        wait_rows(i % 2)
        xb_ref[...] = xf_ref[i % 2].astype(BF16)

    @pl.when(active & (i + 1 < nt))
    def _():
        start_rows(i + 1, j * rows_per_step, jnp.minimum((j + 1) * rows_per_step, tm))

    @pl.when(active)
    def _():
        y = _swiglu_tile(xb_ref[...], wg_ref[0], wu_ref[0], wd_ref[0])

        @pl.when(j == 0)
        def _():
            o_ref[...] = y

        @pl.when(j > 0)
        def _():
            o_ref[...] += y

        @pl.when(j == pl.num_programs(1) - 1)
        def _():
            o_ref[...] = o_ref[...] * rw_ref[:, 0:1]

    @pl.when(jnp.logical_not(active) & (j == 0))
    def _():
        o_ref[...] = jnp.zeros(o_ref.shape, F32)


def _moe_ffn(tile_expert, n_tiles, row_token, h, wg, wu, wd, row_w, *, tm, tf):
    r = row_token.shape[0]
    d = h.shape[1]
    f = wg.shape[2]
    nj = f // tf

    def jj(i, j, nt):
        return jnp.where(i < nt[0], j, nj - 1)

    return pl.pallas_call(
        functools.partial(_moe_ffn_body, tm=tm, rows_per_step=-(-tm // nj)),
        grid_spec=pltpu.PrefetchScalarGridSpec(
            num_scalar_prefetch=3, grid=(r // tm, nj),
            in_specs=[pl.BlockSpec(memory_space=pl.ANY),
                      pl.BlockSpec((1, d, tf), lambda i, j, te, nt, tok: (te[i], 0, jj(i, j, nt))),
                      pl.BlockSpec((1, d, tf), lambda i, j, te, nt, tok: (te[i], 0, jj(i, j, nt))),
                      pl.BlockSpec((1, tf, d), lambda i, j, te, nt, tok: (te[i], jj(i, j, nt), 0)),
                      pl.BlockSpec((tm, LANES), lambda i, j, te, nt, tok: (i, 0))],
            out_specs=pl.BlockSpec((tm, d), lambda i, j, te, nt, tok: (i, 0)),
            scratch_shapes=[pltpu.VMEM((2, tm, d), F32), pltpu.VMEM((tm, d), BF16),
                            pltpu.SemaphoreType.DMA((2,))]),
        out_shape=jax.ShapeDtypeStruct((r, d), F32),
        compiler_params=_cparams(("arbitrary", "arbitrary")), name="moe_grouped_ffn",
    )(tile_expert, n_tiles, row_token, h, wg, wu, wd, row_w)


def _combine_body(idx_ref, src_ref, o_ref, a_ref, b_ref, sem, *, tm):
    base = pl.program_id(0) * tm

    def start(r, c):
        _row_copy(src_ref, idx_ref[2 * (base + r)], a_ref, r, sem.at[0]).start()
        _row_copy(src_ref, idx_ref[2 * (base + r) + 1], b_ref, r, sem.at[1]).start()
        return c

    def wait(r, c):
        _row_copy(src_ref, 0, a_ref, r, sem.at[0]).wait()
        _row_copy(src_ref, 0, b_ref, r, sem.at[1]).wait()
        return c

    lax.fori_loop(0, tm, start, 0)
    lax.fori_loop(0, tm, wait, 0)
    o_ref[...] = a_ref[...] + b_ref[...]


def _combine_pairs(pos, y_sorted, n_tokens, *, tm):
    d = y_sorted.shape[1]
    return pl.pallas_call(
        functools.partial(_combine_body, tm=tm),
        grid_spec=pltpu.PrefetchScalarGridSpec(
            num_scalar_prefetch=1, grid=(n_tokens // tm,),
            in_specs=[pl.BlockSpec(memory_space=pl.ANY)],
            out_specs=pl.BlockSpec((tm, d), lambda i, idx: (i, 0)),
            scratch_shapes=[pltpu.VMEM((tm, d), F32), pltpu.VMEM((tm, d), F32),
                            pltpu.SemaphoreType.DMA((2,))]),
        out_shape=jax.ShapeDtypeStruct((n_tokens, d), F32),
        compiler_params=_cparams(("arbitrary",)), name="moe_combine",
    )(pos, y_sorted)


def _col_slices():
    out, off = {}, 0
    for name, width in IN_SPLITS:
        out[name] = (off, width)
        off += width
    return out


def _gather_cols(w, names, pad_to=None):
    cs = _col_slices()
    parts = []
    for n in names:
        col = w[:, cs[n][0]:cs[n][0] + cs[n][1]]
        parts.append(col * Q_FOLD[n] if n in Q_FOLD else col)
    width = sum(cs[n][1] for n in names)
    if pad_to is not None and pad_to > width:
        parts.append(jnp.zeros((w.shape[0], pad_to - width), w.dtype))
    return jnp.concatenate(parts, axis=1).astype(BF16)


def _selection_map_t(n_cmp_pad, n_sel):
    ci = np.arange(n_cmp_pad)[:, None] * NSA_CMP_D
    sj = np.arange(n_sel)[None, :] * NSA_SEL_L
    ov = np.clip(np.minimum(ci + NSA_CMP_L, sj + NSA_SEL_L) - np.maximum(ci, sj), 0, None)
    return np.ascontiguousarray((ov / NSA_CMP_D).astype(np.float32).T)


def _gate_expand_matrices():
    e = np.zeros((4, 3, LANES, LANES), np.float32)
    for j in range(4):
        for hh in range(2):
            for c in range(3):
                e[j, c, (2 * j + hh) * 3 + c, hh * HEAD_DIM:(hh + 1) * HEAD_DIM] = 1.0
    return e


def _split_bits(x, n):
    parts = []
    r = x
    for _ in range(n):
        hi = lax.bitcast_convert_type(
            lax.bitcast_convert_type(r, jnp.uint32) & jnp.uint32(0xFFFF0000), F32)
        parts.append(hi.astype(BF16))
        r = r - hi
    return parts


def _tile(n, pref):
    return pref if n % pref == 0 else n


def _vt_heads(x3, tkc):
    b, s, c = x3.shape
    nh, nc = c // HEAD_DIM, s // tkc
    v = x3.reshape(b, nc, tkc, nh, HEAD_DIM).transpose(0, 3, 1, 4, 2)
    ones = jnp.ones((b, nh, nc, 1, tkc), v.dtype)
    zeros = jnp.zeros((b, nh, nc, ATT_VROWS - HEAD_DIM - 1, tkc), v.dtype)
    return jnp.concatenate([v, ones, zeros], axis=3)


def _token_mixer(h, h_bf, layer, b, s, tabs, w_in, cmp_pe, cmp_w1, cmp_b1, cmp_w2, sinks, fox_bf,
                 diff_lambda, diff_gain, w_branch, w_out, ln_g, ln_b, w_router):
    t = b * s
    (cos64, sin64), (cos32, sin32) = tabs
    tm = _tile(t, 1024)
    r64 = _proj(h_bf, _gather_cols(w_in, SEG_ROPE64), BF16, tm, 512, rope=(cos64, sin64, HEAD_DIM // 2))
    r32 = _proj(h_bf, _gather_cols(w_in, SEG_ROPE32), BF16, tm, 512, rope=(cos32, sin32, DIFF_SUB // 2))
    plain = _proj(h_bf, _gather_cols(w_in, SEG_PLAIN), BF16, tm, 512)
    gates = _proj(h_bf, _gather_cols(w_in, SEG_GATES), BF16, tm, 512)
    small =_proj(h_bf, _gather_cols(w_in, SEG_SMALL, pad_to=LANES), F32, tm, LANES)
    r64_3, r32_3, plain_3, small_3 = (a.reshape(b, s, -1) for a in (r64, r32, plain, small))
    tkc = min(ATT_TKC, s // 2)
    tq = 2 * tkc

    ncp = s // NSA_CMP_D
    n_sel = s // NSA_SEL_L
    topn = min(NSA_TOPN, n_sel)

    def cmp_blocks(x2d):
        c = x2d.reshape(b, s, 2, HEAD_DIM).transpose(0, 2, 1, 3).reshape(b * 2, ncp, NSA_CMP_D * HEAD_DIM)
        nxt = jnp.concatenate([c[:, 1:], jnp.zeros_like(c[:, :1])], axis=1)
        return jnp.concatenate([c, nxt], axis=-1)

    xk = cmp_blocks(r64[:, 512:640])
    xv = cmp_blocks(plain[:, 1024:1152])
    cmp_kv = _nsa_compress(jnp.stack([xk, xv]), cmp_pe.reshape(2, 1, -1), cmp_w1.astype(BF16),
                           cmp_b1.reshape(2, 1, -1), cmp_w2.astype(BF16))
    kc = cmp_kv[0].astype(BF16).reshape(b, 2, ncp, HEAD_DIM)
    vc = cmp_kv[1].astype(BF16).reshape(b, 2, ncp, HEAD_DIM)
    kc4 = jnp.tile(kc, (1, 1, 1, 4))
    vct = vc.transpose(0, 1, 3, 2)
    selt = jnp.asarray(_selection_map_t(ncp, n_sel), BF16)
    o_cmp, mneg_t = _nsa_cmp_topk(r64_3, 0, kc4, vct, selt, tq=_tile(s, 256), topn=topn)
    vt_all = _vt_heads(plain_3[..., 1152:2560], tkc)
    o_win = _tattn(r64_3, 0, r64_3, 6, vt_all, 2, nmaps=2, qsel="gqa", vmap=(0, 0), mode="band",
                   window=NSA_WIN, fin="win", name="nsa_window_attention")
    e_mat = jnp.asarray(_gate_expand_matrices(), BF16)
    block_id = jax.nn.one_hot(jnp.arange(s) // NSA_SEL_L, HEAD_DIM, dtype=BF16)
    block_id = jnp.concatenate([block_id, block_id], axis=1)
    nsa_specs = [pl.BlockSpec((1, 1, tq, LANES), lambda bi, u, qi, *_: (bi, u // 2, qi, 0)),
                 pl.BlockSpec((s, LANES), lambda bi, u, qi, *_: (0, 0)),
                 pl.BlockSpec((1, 1, tq, LANES), lambda bi, u, qi, *_: (bi, u // 2, qi, u % 2)),
                 pl.BlockSpec((1, tq, LANES), lambda bi, u, qi, *_: (bi, qi, u)),
                 pl.BlockSpec((1, tq, LANES), lambda bi, u, qi, *_: (bi, qi, 0)),
                 pl.BlockSpec((1, 3, LANES, LANES), lambda bi, u, qi, *_: (u, 0, 0, 0))]
    o_a = _tattn(r64_3, 0, r64_3, 5, vt_all, 0, nmaps=2, qsel="gqa", vmap=(0, 0), mode="causal", fin="nsa",
                 bias="nsa", name="nsa_selected_attention", extras=(mneg_t, block_id, o_cmp, o_win, small_3, e_mat),
                 extra_specs=nsa_specs)

    o_b = _tattn(r64_3, 7, r64_3, 11, vt_all, 4, nmaps=2, qsel="gqa", vmap=(0, 0), mode="band",
                 window=SWA_WIN, fin="swa", name="swa_attention", scalars=sinks.astype(F32) * LOG2E)

    f_logit = (small[:, 24:32] + fox_bf[None, :]).reshape(b, s, 8).transpose(0, 2, 1)
    cum = _cum_log_forget(f_logit.reshape(b * 8, s // LANES, LANES)).reshape(b, 8, s)
    hi, mid, lo_piece = (piece[..., None] for piece in _split_bits(cum * LOG2E, 3))
    slot = (jnp.arange(LANES) % HEAD_DIM)[None, None, None, :]
    ck3 = jnp.where(slot == 0, hi, jnp.where(slot == 1, mid, jnp.where(slot == 2, lo_piece, jnp.zeros((), BF16))))
    o_c = _tattn(plain_3, 0, plain_3, 4, vt_all, 6, nmaps=2, qsel="pair", vmap=(0, 1), mode="causal",
                 fin="fox", bias="fox", name="fox_attention", extras=(ck3,),
                 extra_specs=[pl.BlockSpec((1, 2, s, LANES), lambda bi, u, qi, *_: (bi, u, 0, 0))])

    lam_init = 0.8 - 0.6 * math.exp(-0.3 * layer)
    lf = diff_lambda.astype(F32)
    lam = jnp.exp(jnp.sum(lf[0] * lf[1])) - jnp.exp(jnp.sum(lf[2] * lf[3])) + lam_init
    lam_arr = jnp.stack([lam, jnp.asarray(1.0 - lam_init, F32)]).astype(F32)
    gain_t = jnp.broadcast_to(jnp.tile(diff_gain.astype(F32), 2)[:, None], (LANES, tq))
    o_d = _tattn(r32_3, 0, r32_3, 4, vt_all, 14, nmaps=4, qsel="pair", vmap=(0, 0, 1, 1), mode="causal",
                 fin="diff", name="diff_attention", scalars=lam_arr, extras=(gain_t,),
                 extra_specs=[pl.BlockSpec((LANES, tq), lambda bi, u, qi, *_: (0, 0))])

    o_list = [o.reshape(t, BRANCH_W) for o in (o_a, o_b, o_c, o_d)]
    merged = _merge(o_list, gates, 0, w_branch.astype(BF16), tm=tm, tn=512)
    return _outproj_ln(merged, w_out.astype(BF16), h, ln_g.reshape(1, -1), ln_b.reshape(1, -1),
                       w_router, tm=_tile(t, 256))


def _moe_layer(h1, logits_pad, b_router, wg, wu, wd, *, tm, tf):
    t, d = h1.shape
    logits = logits_pad[:, :N_EXPERTS] + b_router.astype(F32)[None, :]
    top_v, top_i = lax.top_k(logits, TOP_K)
    top_w = jax.nn.softmax(top_v, axis=-1)
    flat_e = top_i.reshape(-1)
    onehot = jax.nn.one_hot(flat_e, N_EXPERTS, dtype=jnp.int32)
    rank = jnp.sum((jnp.cumsum(onehot, axis=0) - onehot) * onehot, axis=1)
    cnt = jnp.sum(onehot, axis=0)
    padded = ((cnt + tm - 1) // tm) * tm
    ends = jnp.cumsum(padded)
    starts = ends - padded
    pos = (starts[flat_e] + rank).astype(jnp.int32)
    n_rows = TOP_K * t + N_EXPERTS * tm
    row_token = jnp.zeros((n_rows,), jnp.int32).at[pos].set(jnp.arange(TOP_K * t, dtype=jnp.int32) // TOP_K)
    row_w = jnp.zeros((n_rows,), F32).at[pos].set(top_w.reshape(-1))
    tile_start = jnp.arange(n_rows // tm, dtype=jnp.int32) * tm
    tile_expert = jnp.minimum(jnp.sum(tile_start[:, None] >= ends[None, :], axis=1), N_EXPERTS - 1)
    n_tiles = (ends[-1] // tm).astype(jnp.int32).reshape(1)
    y_sorted = _moe_ffn(tile_expert.astype(jnp.int32), n_tiles, row_token, h1, wg, wu, wd,
                        jnp.broadcast_to(row_w[:, None], (n_rows, LANES)), tm=tm, tf=tf)
    return _combine_pairs(pos, y_sorted, t, tm=_tile(t, 256))


def kernel(x, p, positions, w_in, nsa_cmp_pe, nsa_cmp_w1, nsa_cmp_b1, nsa_cmp_w2, swa_sinks, fox_bf,
           diff_lambda, diff_gain, w_branch, w_out, ln1_g, ln1_b, ffn_wg, ffn_wu, ffn_wd, moe_router,
           moe_router_b, moe_wg, moe_wu, moe_wd, ple_proj, ple_gate, ln2_g, ln2_b):
    b, s, d = x.shape
    t = b * s
    tabs = (_rope_tabs(positions, HEAD_DIM), _rope_tabs(positions, DIFF_SUB))
    h = x.reshape(t, d).astype(F32)
    h_bf = h.astype(BF16)
    for i in range(DEPTH):
        is_moe = i % 2 == 1
        w_router = None
        if is_moe:
            wr = jnp.zeros((d, LANES), F32).at[:, :N_EXPERTS].set(moe_router[i // 2].astype(F32))
            w_router = jnp.stack(_split_bits(wr, 2))
        res = _token_mixer(h, h_bf, i, b, s, tabs, w_in[i], nsa_cmp_pe[i], nsa_cmp_w1[i], nsa_cmp_b1[i],
                           nsa_cmp_w2[i], swa_sinks[i], fox_bf[i], diff_lambda[i], diff_gain[i],
                           w_branch[i], w_out[i], ln1_g[i], ln1_b[i], w_router)
        h1, h1_bf = res[0], res[1]
        if not is_moe:
            fpad = (-D_FF) % 512
            wg = jnp.pad(ffn_wg[i // 2].astype(BF16), ((0, 0), (0, fpad)))
            wu = jnp.pad(ffn_wu[i // 2].astype(BF16), ((0, 0), (0, fpad)))
            wd = jnp.pad(ffn_wd[i // 2].astype(BF16), ((0, fpad), (0, 0)))
            f = _ffn(h1_bf, wg, wu, wd, tm=_tile(t, 1024), tf=512)
        else:
            f = _moe_layer(h1, res[2], moe_router_b[i // 2], moe_wg[i // 2].astype(BF16),
                           moe_wu[i // 2].astype(BF16), moe_wd[i // 2].astype(BF16),
                           tm=_tile(t, 512), tf=1024)
        h, h_bf = _ple_ln(h1_bf, h1, f, p[i].reshape(t, PLE_DIM).astype(BF16), ple_gate[i].astype(BF16),
                          ple_proj[i].astype(BF16), ln2_g[i].reshape(1, -1), ln2_b[i].reshape(1, -1),
                          tm=_tile(t, 512))
    return h.reshape(b, s, d).astype(x.dtype)
```

```python
import functools
import math

import numpy as np
import jax
import jax.numpy as jnp
from jax import lax
from jax.experimental import pallas as pl
from jax.experimental.pallas import tpu as pltpu

F32 = jnp.float32
BF16 = jnp.bfloat16

D_MODEL = 2048
DEPTH = 2
HEAD_DIM = 64
ROPE_THETA = 10000.0
PLE_DIM = 256
LN_EPS = 1e-5
NSA_CMP_L = 32
NSA_CMP_D = 16
NSA_SEL_L = 64
NSA_TOPN = 16
NSA_WIN = 512
NSA_CMP_HIDDEN = 256
NSA_FORCE = 1e9
SWA_WIN = 128
DIFF_SUB = HEAD_DIM // 2
N_BRANCH = 4
BRANCH_W = 8 * HEAD_DIM
D_FF = 5504
N_EXPERTS = 8
TOP_K = 2
D_FF_EXPERT = 7168
ALPHA = (2.0 * DEPTH) ** 0.25

IN_SPLITS = (
    ("a_q", 512), ("a_kc", 128), ("a_vc", 128), ("a_ks", 128), ("a_vs", 128),
    ("a_kw", 128), ("a_vw", 128), ("a_g", 24),
    ("b_q", 512), ("b_k", 128), ("b_v", 128),
    ("c_q", 512), ("c_k", 512), ("c_v", 512), ("c_f", 8),
    ("d_q", 512), ("d_k", 512), ("d_v", 512),
    ("merge_gate", N_BRANCH * D_MODEL),
)
SEG_ROPE64 = ("a_q", "a_kc", "a_ks", "a_kw", "b_q", "b_k")
SEG_ROPE32 = ("d_q", "d_k")
SEG_PLAIN = ("c_q", "c_k", "a_vc", "a_vs", "a_vw", "b_v", "c_v", "d_v")
SEG_GATES = ("merge_gate",)
SEG_SMALL = ("a_g", "c_f")

LANES = 128
NEG = -1e30
LOG2E = math.log2(math.e)
VMEM_LIMIT = 56 * 1024 * 1024
ATT_TKC = 256
ATT_TQ = 2 * ATT_TKC
ATT_VROWS = 80

Q_FOLD = {"a_q": HEAD_DIM ** -0.5 * LOG2E, "b_q": HEAD_DIM ** -0.5 * LOG2E,
          "c_q": HEAD_DIM ** -0.5 * LOG2E, "d_q": DIFF_SUB ** -0.5 * LOG2E}


def _cparams(sem):
    return pltpu.CompilerParams(dimension_semantics=sem, vmem_limit_bytes=VMEM_LIMIT)


def _sigmoid(x):
    return 1.0 / (1.0 + jnp.exp(-x))


def _dot(a, b):
    return jnp.dot(a, b, preferred_element_type=F32)


def _dot_nt(a, b):
    return lax.dot_general(a, b, (((1,), (1,)), ((), ())), preferred_element_type=F32)


def _split2(x):
    hi = x.astype(BF16)
    lo = (x - hi.astype(F32)).astype(BF16)
    return hi, lo


def _split3(x):
    hi = x.astype(BF16)
    r = x - hi.astype(F32)
    mid = r.astype(BF16)
    lo = (r - mid.astype(F32)).astype(BF16)
    return hi, mid, lo


def _proj_body(x_ref, w_ref, *rest, rope_half):
    if rope_half:
        cos_ref, sin_ref, o_ref = rest
    else:
        (o_ref,) = rest
    acc = _dot(x_ref[...], w_ref[...])
    if not rope_half:
        o_ref[...] = acc.astype(o_ref.dtype)
        return
    cos = cos_ref[...]
    sin = sin_ref[...]
    lane = lax.broadcasted_iota(jnp.int32, cos.shape, 1)
    first = (lane % (2 * rope_half)) < rope_half
    for c in range(acc.shape[1] // LANES):
        a = acc[:, c * LANES:(c + 1) * LANES]
        rot = jnp.where(first, pltpu.roll(a, LANES - rope_half, 1), pltpu.roll(a, rope_half, 1))
        o_ref[:, c * LANES:(c + 1) * LANES] = (a * cos + rot * sin).astype(o_ref.dtype)


def _proj(x, w, out_dtype, tm, tn, rope=None):
    m, k = x.shape
    n = w.shape[1]
    in_specs = [pl.BlockSpec((tm, k), lambda i, j: (i, 0)),
                pl.BlockSpec((k, tn), lambda i, j: (0, j))]
    args = [x, w]
    rope_half = 0
    if rope is not None:
        cos_tab, sin_tab, rope_half = rope
        in_specs += [pl.BlockSpec((tm, LANES), lambda i, j: (i, 0)),
                     pl.BlockSpec((tm, LANES), lambda i, j: (i, 0))]
        args += [cos_tab, sin_tab]
    return pl.pallas_call(
        functools.partial(_proj_body, rope_half=rope_half),
        grid=(m // tm, n // tn),
        in_specs=in_specs,
        out_specs=pl.BlockSpec((tm, tn), lambda i, j: (i, j)),
        out_shape=jax.ShapeDtypeStruct((m, n), out_dtype),
        compiler_params=_cparams(("parallel", "parallel")),
        name="proj_rope" if rope_half else "proj",
    )(*args)


def _rope_tabs(positions, dim):
    inv = 1.0 / (ROPE_THETA ** (jnp.arange(0, dim, 2, dtype=F32) / dim))
    ang = positions.astype(F32).reshape(-1)[:, None] * inv
    c, s = jnp.cos(ang), jnp.sin(ang)
    reps = LANES // dim
    return (jnp.tile(jnp.concatenate([c, c], -1), (1, reps)),
            jnp.tile(jnp.concatenate([-s, s], -1), (1, reps)))


def _tattn_body(*refs, nmaps, qsel, vmap, tq, tkc, mode, window, bias, fin):
    refs = list(refs)
    sc_ref = refs.pop(0) if fin in ("diff", "swa") else None
    q_ref, k_ref, vt_ref = refs[:3]
    extras, o_ref = refs[3:-5], refs[-5]
    qm_ref, st_ref, m_ref, acc_ref = refs[-4:]
    bias_ref = None
    if bias is not None:
        bias_ref, extras = extras[0], extras[1:]
    u = pl.program_id(1)
    qi = pl.program_id(2)
    q0 = qi * tq
    last_chunk = k_ref.shape[1] // tkc - 1

    lane = lax.broadcasted_iota(jnp.int32, (tq, LANES), 1)
    klane = lax.broadcasted_iota(jnp.int32, (tkc, LANES), 1)
    q = q_ref[0].astype(F32)
    if qsel == "gqa":
        lo_lane = (u // 2) * HEAD_DIM
        q_rolled = pltpu.roll(q, HEAD_DIM, 1)
        in_group = (lane >= lo_lane) & (lane < lo_lane + HEAD_DIM)
        k_in_group = (klane >= lo_lane) & (klane < lo_lane + HEAD_DIM)
        fill = bias_ref[0, 0].astype(F32) if bias == "nsa" else 0.0
        for e in range(2):
            q_e = jnp.where(lo_lane == e * HEAD_DIM, q, q_rolled)
            qm_ref[e] = jnp.where(in_group, q_e, fill).T.astype(BF16)
    else:
        width = LANES // nmaps
        fill = jnp.where(lane % HEAD_DIM < 3, -1.0, 0.0) if bias == "fox" else 0.0
        for mp in range(nmaps):
            qm_ref[mp] = jnp.where((lane >= mp * width) & (lane < (mp + 1) * width), q, fill).T.astype(BF16)

    m_ref[...] = jnp.full(m_ref.shape, NEG, F32)
    acc_ref[...] = jnp.zeros(acc_ref.shape, F32)

    def qk(c, buf, q_lo=0, q_hi=tq):
        cc = jnp.clip(c, 0, last_chunk)
        off = pl.multiple_of(cc * tkc, tkc)
        kc = k_ref[0, pl.ds(off, tkc), :]
        if bias == "nsa":
            kc = jnp.where(k_in_group, kc, extras[0][pl.ds(off, tkc), :])
        for mp in range(nmaps):
            kc_mp = kc
            if bias == "fox":
                kc_mp = jnp.where((klane >= mp * HEAD_DIM) & (klane < (mp + 1) * HEAD_DIM), kc,
                                  bias_ref[0, mp, pl.ds(off, tkc), :])
            st_ref[buf, mp, :, q_lo:q_hi] = _dot(kc_mp, qm_ref[mp, :, q_lo:q_hi])

    def soft(c, buf, masked, q_lo=0, q_hi=tq):
        for mp in range(nmaps):
            st = st_ref[buf, mp, :, q_lo:q_hi]
            if masked:
                key = c * tkc + lax.broadcasted_iota(jnp.int32, st.shape, 0)
                t_pos = q0 + q_lo + lax.broadcasted_iota(jnp.int32, st.shape, 1)
                keep = key <= t_pos
                if mode == "band":
                    keep = keep & (t_pos - key < window) & (key >= 0)
                st = jnp.where(keep, st, NEG)
            m_old = m_ref[mp, :, q_lo:q_hi]
            m8 = jnp.max(st.reshape(tkc // 8, 8, q_hi - q_lo), axis=0)
            m_new = jnp.maximum(m_old, jnp.max(m8, axis=0, keepdims=True))
            p = jnp.exp2(st - m_new).astype(BF16)
            acc_ref[mp, :, q_lo:q_hi] = (jnp.exp2(m_old - m_new) * acc_ref[mp, :, q_lo:q_hi]
                                         + _dot(vt_ref[0, vmap[mp], jnp.maximum(c, 0)], p))
            m_ref[mp, :, q_lo:q_hi] = m_new

    def pair(pidx, carry, masked):
        c0 = 2 * pidx
        qk(c0 + 1, 1)
        soft(c0, 0, masked)
        qk(c0 + 2, 0)
        soft(c0 + 1, 1, masked)
        return carry

    if mode == "causal":
        qk(0, 0)
        lax.fori_loop(0, qi, functools.partial(pair, masked=False), 0)
        qk(2 * qi + 1, 1, q_lo=tkc)
        soft(2 * qi, 0, True, q_hi=tkc)
        soft(2 * qi, 0, False, q_lo=tkc)
        soft(2 * qi + 1, 1, True, q_lo=tkc)
    else:
        sched = []
        for d in range(-((window - 1 + tkc - 1) // tkc), 2):
            hi_lane = min(tq, d * tkc + tkc - 1 + window)
            sched.append((2 * qi + d, max(0, d * tkc), -(-hi_lane // LANES) * LANES))
        qk(sched[0][0], 0, sched[0][1], sched[0][2])
        for i, (c, lo_lane_q, hi_lane_q) in enumerate(sched):
            if i + 1 < len(sched):
                qk(sched[i + 1][0], (i + 1) % 2, sched[i + 1][1], sched[i + 1][2])
            soft(c, i % 2, True, lo_lane_q, hi_lane_q)

    def normed(mp):
        acc = acc_ref[mp]
        num, l_i = acc[:HEAD_DIM], acc[HEAD_DIM:HEAD_DIM + 1]
        if fin == "swa":
            m_i = m_ref[mp]
            sk = sc_ref[2 * u + mp]
            m_f = jnp.maximum(m_i, sk)
            corr = jnp.exp2(m_i - m_f)
            return num * (corr / (l_i * corr + jnp.exp2(sk - m_f)))
        return num * (1.0 / l_i)

    if fin == "diff":
        lam = sc_ref[0]
        halves = []
        for hh in range(2):
            o = normed(2 * hh) - lam * normed(2 * hh + 1)
            ms = jnp.mean(o * o, axis=0, keepdims=True)
            halves.append(o * lax.rsqrt(ms + LN_EPS))
        ot = jnp.concatenate(halves, axis=0) * extras[0][...] * sc_ref[1]
    else:
        ot = jnp.concatenate([normed(0), normed(1)], axis=0)
    o = ot.T
    if fin == "nsa":
        ocmp_ref, owin_ref, sm_ref, e_ref = extras[1:]
        hi, lo_part = _split2(sm_ref[0])
        gates = [_sigmoid(_dot(hi, e_ref[0, c]) + _dot(lo_part, e_ref[0, c])) for c in range(3)]
        o = gates[0] * ocmp_ref[0, 0] + gates[1] * o + gates[2] * owin_ref[0].astype(F32)
    o_ref[0] = o.astype(o_ref.dtype)


def _tattn(q_arr, q_blk, k_arr, k_blk, vt_all, v_head, *, nmaps, qsel, vmap, mode, fin, name, window=0,
           bias=None, scalars=None, extras=(), extra_specs=()):
    b, s, _ = q_arr.shape
    nc, vrows, tkc = vt_all.shape[2], vt_all.shape[3], vt_all.shape[4]
    tq = 2 * tkc
    nu = 4
    if qsel == "gqa":
        kspec = pl.BlockSpec((1, s, LANES), lambda bi, u, qi, *_: (bi, 0, k_blk))
        vspec = pl.BlockSpec((1, 1, nc, vrows, tkc), lambda bi, u, qi, *_: (bi, v_head + u // 2, 0, 0, 0))
    else:
        kspec = pl.BlockSpec((1, s, LANES), lambda bi, u, qi, *_: (bi, 0, k_blk + u))
        vspec = pl.BlockSpec((1, 2, nc, vrows, tkc), lambda bi, u, qi, *_: (bi, v_head // 2 + u, 0, 0, 0))
    in_specs = [pl.BlockSpec((1, tq, LANES), lambda bi, u, qi, *_: (bi, qi, q_blk + u)), kspec, vspec]
    in_specs += list(extra_specs)
    body = functools.partial(_tattn_body, nmaps=nmaps, qsel=qsel, vmap=vmap, tq=tq, tkc=tkc, mode=mode,
                             window=window, bias=bias, fin=fin)
    args = ([] if scalars is None else [scalars]) + [q_arr, k_arr, vt_all] + list(extras)
    return pl.pallas_call(
        body,
        grid_spec=pltpu.PrefetchScalarGridSpec(
            num_scalar_prefetch=0 if scalars is None else 1, grid=(b, nu, s // tq),
            in_specs=in_specs,
            out_specs=pl.BlockSpec((1, tq, LANES), lambda bi, u, qi, *_: (bi, qi, u)),
            scratch_shapes=[pltpu.VMEM((nmaps, LANES, tq), BF16), pltpu.VMEM((2, nmaps, tkc, tq), F32),
                            pltpu.VMEM((nmaps, 1, tq), F32), pltpu.VMEM((nmaps, vrows, tq), F32)]),
        out_shape=jax.ShapeDtypeStruct((b, s, nu * LANES), BF16),
        compiler_params=_cparams(("parallel", "parallel", "parallel")),
        name=name,
    )(*args)


def _gelu_tanh(x):
    return 0.5 * x * (1.0 + jnp.tanh(math.sqrt(2.0 / math.pi) * (x + 0.044715 * (x * x * x))))


def _compress_body(x_ref, pe_ref, w1_ref, b1_ref, w2_ref, o_ref):
    x = (x_ref[0, 0].astype(F32) + pe_ref[0]).astype(BF16)
    hid = _gelu_tanh(_dot(x, w1_ref[0]) + b1_ref[0])
    o_ref[0, 0] = _dot(hid.astype(BF16), w2_ref[0])


def _nsa_compress(x, pe, w1, b1, w2):
    _, nb, ncp, ld = x.shape
    hid = w1.shape[-1]
    return pl.pallas_call(
        _compress_body,
        grid=(2, nb),
        in_specs=[pl.BlockSpec((1, 1, ncp, ld), lambda t, i: (t, i, 0, 0)),
                  pl.BlockSpec((1, 1, ld), lambda t, i: (t, 0, 0)),
                  pl.BlockSpec((1, ld, hid), lambda t, i: (t, 0, 0)),
                  pl.BlockSpec((1, 1, hid), lambda t, i: (t, 0, 0)),
                  pl.BlockSpec((1, hid, HEAD_DIM), lambda t, i: (t, 0, 0))],
        out_specs=pl.BlockSpec((1, 1, ncp, HEAD_DIM), lambda t, i: (t, i, 0, 0)),
        out_shape=jax.ShapeDtypeStruct((2, nb, ncp, HEAD_DIM), F32),
        compiler_params=_cparams(("parallel", "parallel")),
        name="nsa_compress",
    )(x, pe, w1, b1, w2)


def _cmp_topk_body(q_ref, kc_ref, vct_ref, selt_ref, o_ref, mt_ref, *, tq, ncp, nsel, topn):
    qi = pl.program_id(2)
    q = q_ref[0]
    kc4 = kc_ref[0, 0]
    vct = vct_ref[0, 0]
    lane = lax.broadcasted_iota(jnp.int32, kc4.shape, 1)
    ci = lax.broadcasted_iota(jnp.int32, (ncp, tq), 0)
    tpos = qi * tq + lax.broadcasted_iota(jnp.int32, (ncp, tq), 1)
    cmask = ci * NSA_CMP_D + (NSA_CMP_L - 1) <= tpos
    psum = jnp.zeros((ncp, tq), F32)
    rows = []
    for a in range(4):
        kcm = jnp.where((lane >= a * HEAD_DIM) & (lane < (a + 1) * HEAD_DIM), kc4, jnp.zeros_like(kc4))
        st = jnp.where(cmask, _dot_nt(kcm, q), NEG)
        m = jnp.max(st, axis=0, keepdims=True)
        e = jnp.where(cmask, jnp.exp2(st - m), 0.0)
        l = jnp.sum(e, axis=0, keepdims=True)
        p = e * jnp.where(l > 0.0, 1.0 / l, 0.0)
        psum = psum + p
        rows.append(_dot(vct, p.astype(BF16)))
    hi, lo = _split2(psum)
    selt = selt_ref[...]
    imp = _dot(selt, hi) + _dot(selt, lo)
    blk = lax.broadcasted_iota(jnp.int32, (nsel, tq), 0)
    cur = (qi * tq + lax.broadcasted_iota(jnp.int32, (nsel, tq), 1)) // NSA_SEL_L
    forced = (blk == 0) | (blk == cur) | (blk == cur - 1)
    imp = jnp.where(forced, NSA_FORCE, jnp.where(blk > cur, -NSA_FORCE, imp))
    cnt = jnp.zeros((nsel, tq), jnp.int32)
    for jp in range(nsel):
        v = imp[jp:jp + 1, :]
        tie = jnp.where(blk > jp, 1, 0)
        cnt = cnt + jnp.where(v > imp, 1, jnp.where(v == imp, tie, 0))
    mneg = jnp.where(cnt < topn, 0.0, NEG)
    if nsel < HEAD_DIM:
        mneg = jnp.concatenate([mneg, jnp.zeros((HEAD_DIM - nsel, tq), F32)], axis=0)
    mt_ref[0, 0] = jnp.concatenate([mneg, mneg], axis=0).T.astype(mt_ref.dtype)
    o_ref[0, 0] = jnp.concatenate(rows, axis=0).T


def _nsa_cmp_topk(q_arr, q_off256, kc4, vct, selt, *, tq, topn):
    b, s, _ = q_arr.shape
    ncp = kc4.shape[2]
    nsel = selt.shape[0]
    body = functools.partial(_cmp_topk_body, tq=tq, ncp=ncp, nsel=nsel, topn=topn)
    return pl.pallas_call(
        body,
        grid=(b, 2, s // tq),
        in_specs=[pl.BlockSpec((1, tq, 2 * LANES), lambda bi, g, qi: (bi, qi, q_off256 + g)),
                  pl.BlockSpec((1, 1, ncp, 2 * LANES), lambda bi, g, qi: (bi, g, 0, 0)),
                  pl.BlockSpec((1, 1, HEAD_DIM, ncp), lambda bi, g, qi: (bi, g, 0, 0)),
                  pl.BlockSpec((nsel, ncp), lambda bi, g, qi: (0, 0))],
        out_specs=[pl.BlockSpec((1, 1, tq, 2 * LANES), lambda bi, g, qi: (bi, g, qi, 0)),
                   pl.BlockSpec((1, 1, tq, LANES), lambda bi, g, qi: (bi, g, qi, 0))],
        out_shape=[jax.ShapeDtypeStruct((b, 2, s, 2 * LANES), F32),
                   jax.ShapeDtypeStruct((b, 2, s, LANES), BF16)],
        compiler_params=_cparams(("parallel", "parallel", "parallel")),
        name="nsa_cmp_topk",
    )(q_arr, kc4, vct, selt)


def _cumgate_body(x_ref, o_ref):
    x = x_ref[0]
    r = x.shape[0]
    ls = jnp.minimum(x, 0.0) - jnp.log1p(jnp.exp(-jnp.abs(x)))
    i0 = lax.broadcasted_iota(jnp.int32, (LANES, LANES), 0)
    i1 = lax.broadcasted_iota(jnp.int32, (LANES, LANES), 1)
    upper = jnp.where(i0 <= i1, 1.0, 0.0).astype(BF16)
    ones = jnp.ones((LANES, LANES), BF16)
    r0 = lax.broadcasted_iota(jnp.int32, (r, r), 0)
    r1 = lax.broadcasted_iota(jnp.int32, (r, r), 1)
    strict = jnp.where(r1 < r0, 1.0, 0.0).astype(BF16)
    parts = _split3(ls)
    intra = sum(_dot(pp, upper) for pp in parts)
    rowtot = sum(_dot(pp, ones) for pp in parts)
    off = sum(_dot(strict, pp) for pp in _split3(rowtot))
    o_ref[0] = intra + off


def _cum_log_forget(x):
    n, r, _ = x.shape
    return pl.pallas_call(
        _cumgate_body,
        grid=(n,),
        in_specs=[pl.BlockSpec((1, r, LANES), lambda i: (i, 0, 0))],
        out_specs=pl.BlockSpec((1, r, LANES), lambda i: (i, 0, 0)),
        out_shape=jax.ShapeDtypeStruct((n, r, LANES), F32),
        compiler_params=_cparams(("parallel",)),
        name="cum_log_forget",
    )(x)


def _merge_body(oa_ref, ob_ref, oc_ref, od_ref, g0_ref, g1_ref, g2_ref, g3_ref, wb_ref, o_ref):
    acc = None
    for n, (o_r, g_r) in enumerate(((oa_ref, g0_ref), (ob_ref, g1_ref), (oc_ref, g2_ref), (od_ref, g3_ref))):
        term = _sigmoid(g_r[...].astype(F32)) * _dot(o_r[...], wb_ref[n])
        acc = term if acc is None else acc + term
    o_ref[...] = acc.astype(o_ref.dtype)


def _merge(o_list, plain, gate_off, wb, *, tm, tn):
    t = plain.shape[0]
    d = wb.shape[-1]
    nj = d // tn
    ospec = pl.BlockSpec((tm, BRANCH_W), lambda i, j: (i, 0))
    gspecs = [pl.BlockSpec((tm, tn), functools.partial(lambda i, j, n: (i, gate_off // tn + n * nj + j), n=n))
              for n in range(N_BRANCH)]
    return pl.pallas_call(
        _merge_body,
        grid=(t // tm, nj),
        in_specs=[ospec] * 4 + gspecs + [pl.BlockSpec((N_BRANCH, BRANCH_W, tn), lambda i, j: (0, 0, j))],
        out_specs=pl.BlockSpec((tm, tn), lambda i, j: (i, j)),
        out_shape=jax.ShapeDtypeStruct((t, d), BF16),
        compiler_params=_cparams(("parallel", "parallel")),
        name="gated_merge",
    )(*o_list, plain, plain, plain, plain, wb)


def _layer_norm(y, g, b):
    mu = jnp.mean(y, axis=-1, keepdims=True)
    yc = y - mu
    var = jnp.mean(yc * yc, axis=-1, keepdims=True)
    return yc * lax.rsqrt(var + LN_EPS) * g + b


def _outproj_ln_body(mg_ref, wo_ref, h_ref, g_ref, b_ref, *rest, with_router):
    if with_router:
        wr_ref, o_ref, ob_ref, lg_ref = rest
    else:
        o_ref, ob_ref = rest
    y = ALPHA * h_ref[...] + _dot(mg_ref[...], wo_ref[...])
    out = _layer_norm(y, g_ref[...], b_ref[...])
    o_ref[...] = out
    ob_ref[...] = out.astype(BF16)
    if with_router:
        hi, lo = _split2(out)
        lg_ref[...] = _dot(hi, wr_ref[0]) + _dot(lo, wr_ref[0]) + _dot(hi, wr_ref[1])


def _outproj_ln(merged, w_out, h, g, b, w_router=None, *, tm):
    t, d = h.shape
    with_router = w_router is not None
    row = lambda i: (i, 0)
    fix = lambda i: (0, 0)
    in_specs = [pl.BlockSpec((tm, d), row), pl.BlockSpec((d, d), fix, pipeline_mode=pl.Buffered(1)),
                pl.BlockSpec((tm, d), row),
                pl.BlockSpec((1, d), fix), pl.BlockSpec((1, d), fix)]
    out_specs = [pl.BlockSpec((tm, d), row), pl.BlockSpec((tm, d), row)]
    out_shape = [jax.ShapeDtypeStruct((t, d), F32), jax.ShapeDtypeStruct((t, d), BF16)]
    args = [merged, w_out, h, g, b]
    if with_router:
        in_specs.append(pl.BlockSpec((2, d, LANES), lambda i: (0, 0, 0)))
        out_specs.append(pl.BlockSpec((tm, LANES), row))
        out_shape.append(jax.ShapeDtypeStruct((t, LANES), F32))
        args.append(w_router)
    return pl.pallas_call(
        functools.partial(_outproj_ln_body, with_router=with_router),
        grid=(t // tm,), in_specs=in_specs, out_specs=out_specs, out_shape=out_shape,
        compiler_params=_cparams(("parallel",)), name="outproj_ln1",
    )(*args)


def _ple_ln_body(hb_ref, h_ref, f_ref, p_ref, wg_ref, wp_ref, g_ref, b_ref, o_ref, ob_ref):
    ple = _sigmoid(_dot(hb_ref[...], wg_ref[...])) * _dot(p_ref[...], wp_ref[...])
    out = _layer_norm(ALPHA * h_ref[...] + f_ref[...] + ple, g_ref[...], b_ref[...])
    o_ref[...] = out
    ob_ref[...] = out.astype(BF16)


def _ple_ln(h_bf, h, f, p_bf, w_gate, w_proj, g, b, *, tm):
    t, d = h.shape
    row = lambda i: (i, 0)
    fix = lambda i: (0, 0)
    return pl.pallas_call(
        _ple_ln_body,
        grid=(t // tm,),
        in_specs=[pl.BlockSpec((tm, d), row), pl.BlockSpec((tm, d), row), pl.BlockSpec((tm, d), row),
                  pl.BlockSpec((tm, PLE_DIM), row), pl.BlockSpec((d, d), fix, pipeline_mode=pl.Buffered(1)),
                  pl.BlockSpec((PLE_DIM, d), fix), pl.BlockSpec((1, d), fix), pl.BlockSpec((1, d), fix)],
        out_specs=[pl.BlockSpec((tm, d), row), pl.BlockSpec((tm, d), row)],
        out_shape=[jax.ShapeDtypeStruct((t, d), F32), jax.ShapeDtypeStruct((t, d), BF16)],
        compiler_params=_cparams(("parallel",)), name="ple_ln2",
    )(h_bf, h, f, p_bf, w_gate, w_proj, g, b)


def _swiglu_tile(x, wg, wu, wd):
    g = _dot(x, wg)
    u = _dot(x, wu)
    return _dot((g * _sigmoid(g) * u).astype(BF16), wd)


def _ffn_body(x_ref, wg_ref, wu_ref, wd_ref, o_ref):
    j = pl.program_id(1)
    y = _swiglu_tile(x_ref[...], wg_ref[...], wu_ref[...], wd_ref[...])

    @pl.when(j == 0)
    def _():
        o_ref[...] = y

    @pl.when(j > 0)
    def _():
        o_ref[...] += y


def _ffn(x_bf, wg, wu, wd, *, tm, tf):
    t, d = x_bf.shape
    f = wg.shape[1]
    return pl.pallas_call(
        _ffn_body,
        grid=(t // tm, f // tf),
        in_specs=[pl.BlockSpec((tm, d), lambda i, j: (i, 0)),
                  pl.BlockSpec((d, tf), lambda i, j: (0, j)),
                  pl.BlockSpec((d, tf), lambda i, j: (0, j)),
                  pl.BlockSpec((tf, d), lambda i, j: (j, 0))],
        out_specs=pl.BlockSpec((tm, d), lambda i, j: (i, 0)),
        out_shape=jax.ShapeDtypeStruct((t, d), F32),
        compiler_params=_cparams(("parallel", "arbitrary")), name="ffn_swiglu",
    )(x_bf, wg, wu, wd)


def _row_copy(src_ref, src_row, dst_ref, dst_row, sem):
    return pltpu.make_async_copy(src_ref.at[pl.ds(src_row, 1)], dst_ref.at[pl.ds(dst_row, 1)], sem)


def _moe_ffn_body(te_ref, nt_ref, tok_ref, h_ref, wg_ref, wu_ref, wd_ref, o_ref,
                  xf_ref, xb_ref, sem, *, tm, rows_per_step, n_copies):
    i = pl.program_id(0)
    j = pl.program_id(1)
    nt = nt_ref[0]
    active = i < nt

    def start_rows(tile, r0, count):
        slot = tile % 2

        def body(r, c):
            row = r0 + r
            token = tok_ref[tile * tm + jnp.minimum(row, tm - 1)]
            _row_copy(h_ref, token, xf_ref.at[slot], row, sem.at[slot]).start()
            return c

        lax.fori_loop(0, count, body, 0, unroll=8)

    def wait_rows(slot):
        def body(r, c):
            _row_copy(h_ref, 0, xf_ref.at[slot], r, sem.at[slot]).wait()
            return c

        lax.fori_loop(0, n_copies, body, 0, unroll=8)

    @pl.when((i == 0) & (j == 0))
    def _():
        start_rows(0, 0, n_copies)

    @pl.when(active & (j == 0))
    def _():
        wait_rows(i % 2)
        xb_ref[...] = xf_ref[i % 2, :tm].astype(BF16)

    @pl.when(active)
    def _():
        bursts = (rows_per_step // 8 - 2 * (rows_per_step // 24)) * 8, rows_per_step // 24 * 8, rows_per_step // 24 * 8

        def issue(k):
            @pl.when(i + 1 < nt)
            def _():
                start_rows(i + 1, j * rows_per_step + sum(bursts[:k]), bursts[k])

        x = xb_ref[...]
        issue(0)
        g = _dot(x, wg_ref[0])
        issue(1)
        u = _dot(x, wu_ref[0])
        issue(2)
        y = _dot((g * _sigmoid(g) * u).astype(BF16), wd_ref[0])

        @pl.when(j == 0)
        def _():
            o_ref[...] = y

        @pl.when(j > 0)
        def _():
            o_ref[...] += y

    @pl.when(jnp.logical_not(active) & (j == 0))
    def _():
        o_ref[...] = jnp.zeros(o_ref.shape, F32)


def _moe_ffn(tile_expert, n_tiles, row_token, h, wg, wu, wd, *, tm, tf):
    r = row_token.shape[0]
    d = h.shape[1]
    f = wg.shape[2]
    nj = f // tf

    def jj(i, j, nt):
        return jnp.where(i < nt[0], j, nj - 1)

    rows_per_step = -(-tm // (8 * nj)) * 8
    n_copies = rows_per_step * nj
    return pl.pallas_call(
        functools.partial(_moe_ffn_body, tm=tm, rows_per_step=rows_per_step, n_copies=n_copies),
        grid_spec=pltpu.PrefetchScalarGridSpec(
            num_scalar_prefetch=3, grid=(r // tm, nj),
            in_specs=[pl.BlockSpec(memory_space=pl.ANY),
                      pl.BlockSpec((1, d, tf), lambda i, j, te, nt, tok: (te[i], 0, jj(i, j, nt))),
                      pl.BlockSpec((1, d, tf), lambda i, j, te, nt, tok: (te[i], 0, jj(i, j, nt))),
                      pl.BlockSpec((1, tf, d), lambda i, j, te, nt, tok: (te[i], jj(i, j, nt), 0))],
            out_specs=pl.BlockSpec((tm, d), lambda i, j, te, nt, tok: (i, 0)),
            scratch_shapes=[pltpu.VMEM((2, n_copies, d), F32), pltpu.VMEM((tm, d), BF16),
                            pltpu.SemaphoreType.DMA((2,))]),
        out_shape=jax.ShapeDtypeStruct((r, d), F32),
        compiler_params=_cparams(("arbitrary", "arbitrary")), name="moe_grouped_ffn",
    )(tile_expert, n_tiles, row_token, h, wg, wu, wd)


def _combine_body(idx_ref, src_ref, w_ref, o_ref, a_ref, b_ref, sem, *, tm):
    base = pl.program_id(0) * tm

    def start(r, c):
        _row_copy(src_ref, idx_ref[2 * (base + r)], a_ref, r, sem.at[0]).start()
        _row_copy(src_ref, idx_ref[2 * (base + r) + 1], b_ref, r, sem.at[1]).start()
        return c

    def wait(r, c):
        _row_copy(src_ref, 0, a_ref, r, sem.at[0]).wait()
        _row_copy(src_ref, 0, b_ref, r, sem.at[1]).wait()
        return c

    lax.fori_loop(0, tm, start, 0, unroll=8)
    lax.fori_loop(0, tm, wait, 0, unroll=8)
    w = w_ref[...]
    o_ref[...] = a_ref[...] * w[:, 0:1] + b_ref[...] * w[:, 1:2]


def _combine_pairs(pos, y_sorted, pair_w, n_tokens, *, tm):
    d = y_sorted.shape[1]
    return pl.pallas_call(
        functools.partial(_combine_body, tm=tm),
        grid_spec=pltpu.PrefetchScalarGridSpec(
            num_scalar_prefetch=1, grid=(n_tokens // tm,),
            in_specs=[pl.BlockSpec(memory_space=pl.ANY),
                      pl.BlockSpec((tm, LANES), lambda i, idx: (i, 0))],
            out_specs=pl.BlockSpec((tm, d), lambda i, idx: (i, 0)),
            scratch_shapes=[pltpu.VMEM((tm, d), F32), pltpu.VMEM((tm, d), F32),
                            pltpu.SemaphoreType.DMA((2,))]),
        out_shape=jax.ShapeDtypeStruct((n_tokens, d), F32),
        compiler_params=_cparams(("arbitrary",)), name="moe_combine",
    )(pos, y_sorted, pair_w)


def _col_slices():
    out, off = {}, 0
    for name, width in IN_SPLITS:
        out[name] = (off, width)
        off += width
    return out


def _gather_cols(w, names, pad_to=None):
    cs = _col_slices()
    parts = []
    for n in names:
        col = w[:, cs[n][0]:cs[n][0] + cs[n][1]]
        parts.append(col * Q_FOLD[n] if n in Q_FOLD else col)
    width = sum(cs[n][1] for n in names)
    if pad_to is not None and pad_to > width:
        parts.append(jnp.zeros((w.shape[0], pad_to - width), w.dtype))
    return jnp.concatenate(parts, axis=1).astype(BF16)


def _selection_map_t(n_cmp_pad, n_sel):
    ci = np.arange(n_cmp_pad)[:, None] * NSA_CMP_D
    sj = np.arange(n_sel)[None, :] * NSA_SEL_L
    ov = np.clip(np.minimum(ci + NSA_CMP_L, sj + NSA_SEL_L) - np.maximum(ci, sj), 0, None)
    return np.ascontiguousarray((ov / NSA_CMP_D).astype(np.float32).T)


def _gate_expand_matrices():
    e = np.zeros((4, 3, LANES, LANES), np.float32)
    for j in range(4):
        for hh in range(2):
            for c in range(3):
                e[j, c, (2 * j + hh) * 3 + c, hh * HEAD_DIM:(hh + 1) * HEAD_DIM] = 1.0
    return e


def _split_bits(x, n):
    parts = []
    r = x
    for _ in range(n):
        hi = lax.bitcast_convert_type(
            lax.bitcast_convert_type(r, jnp.uint32) & jnp.uint32(0xFFFF0000), F32)
        parts.append(hi.astype(BF16))
        r = r - hi
    return parts


def _tile(n, pref):
    return pref if n % pref == 0 else n


def _vt_heads(x3, tkc):
    b, s, c = x3.shape
    nh, nc = c // HEAD_DIM, s // tkc
    v = x3.reshape(b, nc, tkc, nh, HEAD_DIM).transpose(0, 3, 1, 4, 2)
    ones = jnp.ones((b, nh, nc, 1, tkc), v.dtype)
    zeros = jnp.zeros((b, nh, nc, ATT_VROWS - HEAD_DIM - 1, tkc), v.dtype)
    return jnp.concatenate([v, ones, zeros], axis=3)


def _token_mixer(h, h_bf, layer, b, s, tabs, w_in, cmp_pe, cmp_w1, cmp_b1, cmp_w2, sinks, fox_bf,
                 diff_lambda, diff_gain, w_branch, w_out, ln_g, ln_b, w_router):
    t = b * s
    (cos64, sin64), (cos32, sin32) = tabs
    tm = _tile(t, 1024)
    r64 = _proj(h_bf, _gather_cols(w_in, SEG_ROPE64), BF16, tm, 512, rope=(cos64, sin64, HEAD_DIM // 2))
    r32 = _proj(h_bf, _gather_cols(w_in, SEG_ROPE32), BF16, tm, 512, rope=(cos32, sin32, DIFF_SUB // 2))
    plain = _proj(h_bf, _gather_cols(w_in, SEG_PLAIN), BF16, tm, 512)
    gates = _proj(h_bf, _gather_cols(w_in, SEG_GATES), BF16, tm, 512)
    small =_proj(h_bf, _gather_cols(w_in, SEG_SMALL, pad_to=LANES), F32, tm, LANES)
    r64_3, r32_3, plain_3, small_3 = (a.reshape(b, s, -1) for a in (r64, r32, plain, small))
    tkc = min(ATT_TKC, s // 2)
    tq = 2 * tkc

    ncp = s // NSA_CMP_D
    n_sel = s // NSA_SEL_L
    topn = min(NSA_TOPN, n_sel)

    def cmp_blocks(x2d):
        c = x2d.reshape(b, s, 2, HEAD_DIM).transpose(0, 2, 1, 3).reshape(b * 2, ncp, NSA_CMP_D * HEAD_DIM)
        nxt = jnp.concatenate([c[:, 1:], jnp.zeros_like(c[:, :1])], axis=1)
        return jnp.concatenate([c, nxt], axis=-1)

    xk = cmp_blocks(r64[:, 512:640])
    xv = cmp_blocks(plain[:, 1024:1152])
    cmp_kv = _nsa_compress(jnp.stack([xk, xv]), cmp_pe.reshape(2, 1, -1), cmp_w1.astype(BF16),
                           cmp_b1.reshape(2, 1, -1), cmp_w2.astype(BF16))
    kc = cmp_kv[0].astype(BF16).reshape(b, 2, ncp, HEAD_DIM)
    vc = cmp_kv[1].astype(BF16).reshape(b, 2, ncp, HEAD_DIM)
    kc4 = jnp.tile(kc, (1, 1, 1, 4))
    vct = vc.transpose(0, 1, 3, 2)
    selt = jnp.asarray(_selection_map_t(ncp, n_sel), BF16)
    o_cmp, mneg_t = _nsa_cmp_topk(r64_3, 0, kc4, vct, selt, tq=_tile(s, 256), topn=topn)
    vt_all = _vt_heads(plain_3[..., 1152:2560], tkc)
    o_win = _tattn(r64_3, 0, r64_3, 6, vt_all, 2, nmaps=2, qsel="gqa", vmap=(0, 0), mode="band",
                   window=NSA_WIN, fin="win", name="nsa_window_attention")
    e_mat = jnp.asarray(_gate_expand_matrices(), BF16)
    block_id = jax.nn.one_hot(jnp.arange(s) // NSA_SEL_L, HEAD_DIM, dtype=BF16)
    block_id = jnp.concatenate([block_id, block_id], axis=1)
    nsa_specs = [pl.BlockSpec((1, 1, tq, LANES), lambda bi, u, qi, *_: (bi, u // 2, qi, 0)),
                 pl.BlockSpec((s, LANES), lambda bi, u, qi, *_: (0, 0)),
                 pl.BlockSpec((1, 1, tq, LANES), lambda bi, u, qi, *_: (bi, u // 2, qi, u % 2)),
                 pl.BlockSpec((1, tq, LANES), lambda bi, u, qi, *_: (bi, qi, u)),
                 pl.BlockSpec((1, tq, LANES), lambda bi, u, qi, *_: (bi, qi, 0)),
                 pl.BlockSpec((1, 3, LANES, LANES), lambda bi, u, qi, *_: (u, 0, 0, 0))]
    o_a = _tattn(r64_3, 0, r64_3, 5, vt_all, 0, nmaps=2, qsel="gqa", vmap=(0, 0), mode="causal", fin="nsa",
                 bias="nsa", name="nsa_selected_attention", extras=(mneg_t, block_id, o_cmp, o_win, small_3, e_mat),
                 extra_specs=nsa_specs)

    o_b = _tattn(r64_3, 7, r64_3, 11, vt_all, 4, nmaps=2, qsel="gqa", vmap=(0, 0), mode="band",
                 window=SWA_WIN, fin="swa", name="swa_attention", scalars=sinks.astype(F32) * LOG2E)

    f_logit = (small[:, 24:32] + fox_bf[None, :]).reshape(b, s, 8).transpose(0, 2, 1)
    cum = _cum_log_forget(f_logit.reshape(b * 8, s // LANES, LANES)).reshape(b, 8, s)
    hi, mid, lo_piece = (piece[..., None] for piece in _split_bits(cum * LOG2E, 3))
    slot = (jnp.arange(LANES) % HEAD_DIM)[None, None, None, :]
    ck3 = jnp.where(slot == 0, hi, jnp.where(slot == 1, mid, jnp.where(slot == 2, lo_piece, jnp.zeros((), BF16))))
    o_c = _tattn(plain_3, 0, plain_3, 4, vt_all, 6, nmaps=2, qsel="pair", vmap=(0, 1), mode="causal",
                 fin="fox", bias="fox", name="fox_attention", extras=(ck3,),
                 extra_specs=[pl.BlockSpec((1, 2, s, LANES), lambda bi, u, qi, *_: (bi, u, 0, 0))])

    lam_init = 0.8 - 0.6 * math.exp(-0.3 * layer)
    lf = diff_lambda.astype(F32)
    lam = jnp.exp(jnp.sum(lf[0] * lf[1])) - jnp.exp(jnp.sum(lf[2] * lf[3])) + lam_init
    lam_arr = jnp.stack([lam, jnp.asarray(1.0 - lam_init, F32)]).astype(F32)
    gain_t = jnp.broadcast_to(jnp.tile(diff_gain.astype(F32), 2)[:, None], (LANES, tq))
    o_d = _tattn(r32_3, 0, r32_3, 4, vt_all, 14, nmaps=4, qsel="pair", vmap=(0, 0, 1, 1), mode="causal",
                 fin="diff", name="diff_attention", scalars=lam_arr, extras=(gain_t,),
                 extra_specs=[pl.BlockSpec((LANES, tq), lambda bi, u, qi, *_: (0, 0))])

    o_list = [o.reshape(t, BRANCH_W) for o in (o_a, o_b, o_c, o_d)]
    merged = _merge(o_list, gates, 0, w_branch.astype(BF16), tm=tm, tn=512)
    return _outproj_ln(merged, w_out.astype(BF16), h, ln_g.reshape(1, -1), ln_b.reshape(1, -1),
                       w_router, tm=_tile(t, 256))


def _moe_layer(h1, logits_pad, b_router, wg, wu, wd, *, tm, tf):
    t, d = h1.shape
    logits = logits_pad[:, :N_EXPERTS] + b_router.astype(F32)[None, :]
    top_v, top_i = lax.top_k(logits, TOP_K)
    top_w = jax.nn.softmax(top_v, axis=-1)
    flat_e = top_i.reshape(-1)
    onehot = jax.nn.one_hot(flat_e, N_EXPERTS, dtype=jnp.int32)
    rank = jnp.sum((jnp.cumsum(onehot, axis=0) - onehot) * onehot, axis=1)
    cnt = jnp.sum(onehot, axis=0)
    padded = ((cnt + tm - 1) // tm) * tm
    ends = jnp.cumsum(padded)
    starts = ends - padded
    pos = (starts[flat_e] + rank).astype(jnp.int32)
    n_rows = TOP_K * t + N_EXPERTS * tm
    row_token = jnp.zeros((n_rows,), jnp.int32).at[pos].set(jnp.arange(TOP_K * t, dtype=jnp.int32) // TOP_K)
    tile_start = jnp.arange(n_rows // tm, dtype=jnp.int32) * tm
    tile_expert = jnp.minimum(jnp.sum(tile_start[:, None] >= ends[None, :], axis=1), N_EXPERTS - 1)
    n_tiles = (ends[-1] // tm).astype(jnp.int32).reshape(1)
    y_sorted = _moe_ffn(tile_expert.astype(jnp.int32), n_tiles, row_token, h1, wg, wu, wd, tm=tm, tf=tf)
    pair_w = jnp.pad(top_w.astype(F32), ((0, 0), (0, LANES - TOP_K)))
    return _combine_pairs(pos, y_sorted, pair_w, t, tm=_tile(t, 256))


def kernel(x, p, positions, w_in, nsa_cmp_pe, nsa_cmp_w1, nsa_cmp_b1, nsa_cmp_w2, swa_sinks, fox_bf,
           diff_lambda, diff_gain, w_branch, w_out, ln1_g, ln1_b, ffn_wg, ffn_wu, ffn_wd, moe_router,
           moe_router_b, moe_wg, moe_wu, moe_wd, ple_proj, ple_gate, ln2_g, ln2_b):
    b, s, d = x.shape
    t = b * s
    tabs = (_rope_tabs(positions, HEAD_DIM), _rope_tabs(positions, DIFF_SUB))
    h = x.reshape(t, d).astype(F32)
    h_bf = h.astype(BF16)
    for i in range(DEPTH):
        is_moe = i % 2 == 1
        w_router = None
        if is_moe:
            wr = jnp.zeros((d, LANES), F32).at[:, :N_EXPERTS].set(moe_router[i // 2].astype(F32))
            w_router = jnp.stack(_split_bits(wr, 2))
        res = _token_mixer(h, h_bf, i, b, s, tabs, w_in[i], nsa_cmp_pe[i], nsa_cmp_w1[i], nsa_cmp_b1[i],
                           nsa_cmp_w2[i], swa_sinks[i], fox_bf[i], diff_lambda[i], diff_gain[i],
                           w_branch[i], w_out[i], ln1_g[i], ln1_b[i], w_router)
        h1, h1_bf = res[0], res[1]
        if not is_moe:
            fpad = (-D_FF) % 512
            wg = jnp.pad(ffn_wg[i // 2].astype(BF16), ((0, 0), (0, fpad)))
            wu = jnp.pad(ffn_wu[i // 2].astype(BF16), ((0, 0), (0, fpad)))
            wd = jnp.pad(ffn_wd[i // 2].astype(BF16), ((0, fpad), (0, 0)))
            f = _ffn(h1_bf, wg, wu, wd, tm=_tile(t, 1024), tf=512)
        else:
            f = _moe_layer(h1, res[2], moe_router_b[i // 2], moe_wg[i // 2].astype(BF16),
                           moe_wu[i // 2].astype(BF16), moe_wd[i // 2].astype(BF16),
                           tm=_tile(t, 512), tf=1024)
        h, h_bf = _ple_ln(h1_bf, h1, f, p[i].reshape(t, PLE_DIM).astype(BF16), ple_gate[i].astype(BF16),
                          ple_proj[i].astype(BF16), ln2_g[i].reshape(1, -1), ln2_b[i].reshape(1, -1),
                          tm=_tile(t, 512))
    return h.reshape(b, s, d).astype(x.dtype)
```

```python
import functools
import math

import numpy as np
import jax
import jax.numpy as jnp
from jax import lax
from jax.experimental import pallas as pl
from jax.experimental.pallas import tpu as pltpu

F32 = jnp.float32
BF16 = jnp.bfloat16

D_MODEL = 2048
DEPTH = 2
HEAD_DIM = 64
ROPE_THETA = 10000.0
PLE_DIM = 256
LN_EPS = 1e-5
NSA_CMP_L = 32
NSA_CMP_D = 16
NSA_SEL_L = 64
NSA_TOPN = 16
NSA_WIN = 512
NSA_CMP_HIDDEN = 256
NSA_FORCE = 1e9
SWA_WIN = 128
DIFF_SUB = HEAD_DIM // 2
N_BRANCH = 4
BRANCH_W = 8 * HEAD_DIM
D_FF = 5504
N_EXPERTS = 8
TOP_K = 2
D_FF_EXPERT = 7168
ALPHA = (2.0 * DEPTH) ** 0.25

IN_SPLITS = (
    ("a_q", 512), ("a_kc", 128), ("a_vc", 128), ("a_ks", 128), ("a_vs", 128),
    ("a_kw", 128), ("a_vw", 128), ("a_g", 24),
    ("b_q", 512), ("b_k", 128), ("b_v", 128),
    ("c_q", 512), ("c_k", 512), ("c_v", 512), ("c_f", 8),
    ("d_q", 512), ("d_k", 512), ("d_v", 512),
    ("merge_gate", N_BRANCH * D_MODEL),
)
SEG_ROPE64 = ("a_q", "a_kc", "a_ks", "a_kw", "b_q", "b_k")
SEG_ROPE32 = ("d_q", "d_k")
SEG_PLAIN = ("c_q", "c_k", "a_vc", "a_vs", "a_vw", "b_v", "c_v", "d_v")
SEG_GATES = ("merge_gate",)
SEG_SMALL = ("a_g", "c_f")

LANES = 128
NEG = -1e30
LOG2E = math.log2(math.e)
VMEM_LIMIT = 56 * 1024 * 1024
ATT_TKC = 256
ATT_TQ = 2 * ATT_TKC
ATT_VROWS = 80

Q_FOLD = {"a_q": HEAD_DIM ** -0.5 * LOG2E, "b_q": HEAD_DIM ** -0.5 * LOG2E,
          "c_q": HEAD_DIM ** -0.5 * LOG2E, "d_q": DIFF_SUB ** -0.5 * LOG2E}


def _cparams(sem):
    return pltpu.CompilerParams(dimension_semantics=sem, vmem_limit_bytes=VMEM_LIMIT)


def _sigmoid(x):
    return 1.0 / (1.0 + jnp.exp(-x))


def _dot(a, b):
    return jnp.dot(a, b, preferred_element_type=F32)


def _dot_nt(a, b):
    return lax.dot_general(a, b, (((1,), (1,)), ((), ())), preferred_element_type=F32)


def _split2(x):
    hi = x.astype(BF16)
    lo = (x - hi.astype(F32)).astype(BF16)
    return hi, lo


def _split3(x):
    hi = x.astype(BF16)
    r = x - hi.astype(F32)
    mid = r.astype(BF16)
    lo = (r - mid.astype(F32)).astype(BF16)
    return hi, mid, lo


def _proj_body(x_ref, w_ref, *rest, rope_half):
    if rope_half:
        cos_ref, sin_ref, o_ref = rest
    else:
        (o_ref,) = rest
    acc = _dot(x_ref[...], w_ref[...])
    if not rope_half:
        o_ref[...] = acc.astype(o_ref.dtype)
        return
    cos = cos_ref[...]
    sin = sin_ref[...]
    lane = lax.broadcasted_iota(jnp.int32, cos.shape, 1)
    first = (lane % (2 * rope_half)) < rope_half
    for c in range(acc.shape[1] // LANES):
        a = acc[:, c * LANES:(c + 1) * LANES]
        rot = jnp.where(first, pltpu.roll(a, LANES - rope_half, 1), pltpu.roll(a, rope_half, 1))
        o_ref[:, c * LANES:(c + 1) * LANES] = (a * cos + rot * sin).astype(o_ref.dtype)


def _proj(x, w, out_dtype, tm, tn, rope=None):
    m, k = x.shape
    n = w.shape[1]
    in_specs = [pl.BlockSpec((tm, k), lambda i, j: (i, 0)),
                pl.BlockSpec((k, tn), lambda i, j: (0, j))]
    args = [x, w]
    rope_half = 0
    if rope is not None:
        cos_tab, sin_tab, rope_half = rope
        in_specs += [pl.BlockSpec((tm, LANES), lambda i, j: (i, 0)),
                     pl.BlockSpec((tm, LANES), lambda i, j: (i, 0))]
        args += [cos_tab, sin_tab]
    return pl.pallas_call(
        functools.partial(_proj_body, rope_half=rope_half),
        grid=(m // tm, n // tn),
        in_specs=in_specs,
        out_specs=pl.BlockSpec((tm, tn), lambda i, j: (i, j)),
        out_shape=jax.ShapeDtypeStruct((m, n), out_dtype),
        compiler_params=_cparams(("parallel", "parallel")),
        name="proj_rope" if rope_half else "proj",
    )(*args)


def _rope_tabs(positions, dim):
    inv = 1.0 / (ROPE_THETA ** (jnp.arange(0, dim, 2, dtype=F32) / dim))
    ang = positions.astype(F32).reshape(-1)[:, None] * inv
    c, s = jnp.cos(ang), jnp.sin(ang)
    reps = LANES // dim
    return (jnp.tile(jnp.concatenate([c, c], -1), (1, reps)),
            jnp.tile(jnp.concatenate([-s, s], -1), (1, reps)))


def _tattn_body(*refs, nmaps, qsel, vmap, tq, tkc, mode, window, bias, fin):
    refs = list(refs)
    sc_ref = refs.pop(0) if fin in ("diff", "swa") else None
    q_ref, k_ref, vt_ref = refs[:3]
    extras, o_ref = refs[3:-5], refs[-5]
    qm_ref, st_ref, m_ref, acc_ref = refs[-4:]
    bias_ref = None
    if bias is not None:
        bias_ref, extras = extras[0], extras[1:]
    u = pl.program_id(1)
    qi = pl.program_id(2)
    q0 = qi * tq
    last_chunk = k_ref.shape[1] // tkc - 1

    lane = lax.broadcasted_iota(jnp.int32, (tq, LANES), 1)
    klane = lax.broadcasted_iota(jnp.int32, (tkc, LANES), 1)
    q = q_ref[0].astype(F32)
    if qsel == "gqa":
        lo_lane = (u // 2) * HEAD_DIM
        q_rolled = pltpu.roll(q, HEAD_DIM, 1)
        in_group = (lane >= lo_lane) & (lane < lo_lane + HEAD_DIM)
        k_in_group = (klane >= lo_lane) & (klane < lo_lane + HEAD_DIM)
        fill = bias_ref[0, 0].astype(F32) if bias == "nsa" else 0.0
        for e in range(2):
            q_e = jnp.where(lo_lane == e * HEAD_DIM, q, q_rolled)
            qm_ref[e] = jnp.where(in_group, q_e, fill).T.astype(BF16)
    else:
        width = LANES // nmaps
        fill = jnp.where(lane % HEAD_DIM < 3, -1.0, 0.0) if bias == "fox" else 0.0
        for mp in range(nmaps):
            qm_ref[mp] = jnp.where((lane >= mp * width) & (lane < (mp + 1) * width), q, fill).T.astype(BF16)

    m_ref[...] = jnp.full(m_ref.shape, NEG, F32)
    acc_ref[...] = jnp.zeros(acc_ref.shape, F32)

    def qk(c, buf, q_lo=0, q_hi=tq):
        cc = jnp.clip(c, 0, last_chunk)
        off = pl.multiple_of(cc * tkc, tkc)
        kc = k_ref[0, pl.ds(off, tkc), :]
        if bias == "nsa":
            kc = jnp.where(k_in_group, kc, extras[0][pl.ds(off, tkc), :])
        for mp in range(nmaps):
            kc_mp = kc
            if bias == "fox":
                kc_mp = jnp.where((klane >= mp * HEAD_DIM) & (klane < (mp + 1) * HEAD_DIM), kc,
                                  bias_ref[0, mp, pl.ds(off, tkc), :])
            st_ref[buf, mp, :, q_lo:q_hi] = _dot(kc_mp, qm_ref[mp, :, q_lo:q_hi])

    def soft(c, buf, masked, q_lo=0, q_hi=tq):
        for mp in range(nmaps):
            st = st_ref[buf, mp, :, q_lo:q_hi]
            if masked:
                key = c * tkc + lax.broadcasted_iota(jnp.int32, st.shape, 0)
                t_pos = q0 + q_lo + lax.broadcasted_iota(jnp.int32, st.shape, 1)
                keep = key <= t_pos
                if mode == "band":
                    keep = keep & (t_pos - key < window) & (key >= 0)
                st = jnp.where(keep, st, NEG)
            m_old = m_ref[mp, :, q_lo:q_hi]
            m8 = jnp.max(st.reshape(tkc // 8, 8, q_hi - q_lo), axis=0)
            m_new = jnp.maximum(m_old, jnp.max(m8, axis=0, keepdims=True))
            p = jnp.exp2(st - m_new).astype(BF16)
            acc_ref[mp, :, q_lo:q_hi] = (jnp.exp2(m_old - m_new) * acc_ref[mp, :, q_lo:q_hi]
                                         + _dot(vt_ref[0, vmap[mp], jnp.maximum(c, 0)], p))
            m_ref[mp, :, q_lo:q_hi] = m_new

    def pair(pidx, carry, masked):
        c0 = 2 * pidx
        qk(c0 + 1, 1)
        soft(c0, 0, masked)
        qk(c0 + 2, 0)
        soft(c0 + 1, 1, masked)
        return carry

    if mode == "causal":
        qk(0, 0)
        lax.fori_loop(0, qi, functools.partial(pair, masked=False), 0)
        qk(2 * qi + 1, 1, q_lo=tkc)
        soft(2 * qi, 0, True, q_hi=tkc)
        soft(2 * qi, 0, False, q_lo=tkc)
        soft(2 * qi + 1, 1, True, q_lo=tkc)
    else:
        sched = []
        for d in range(-((window - 1 + tkc - 1) // tkc), 2):
            hi_lane = min(tq, d * tkc + tkc - 1 + window)
            sched.append((2 * qi + d, max(0, d * tkc), -(-hi_lane // LANES) * LANES))
        qk(sched[0][0], 0, sched[0][1], sched[0][2])
        for i, (c, lo_lane_q, hi_lane_q) in enumerate(sched):
            if i + 1 < len(sched):
                qk(sched[i + 1][0], (i + 1) % 2, sched[i + 1][1], sched[i + 1][2])
            soft(c, i % 2, True, lo_lane_q, hi_lane_q)

    def normed(mp):
        acc = acc_ref[mp]
        num, l_i = acc[:HEAD_DIM], acc[HEAD_DIM:HEAD_DIM + 1]
        if fin == "swa":
            m_i = m_ref[mp]
            sk = sc_ref[2 * u + mp]
            m_f = jnp.maximum(m_i, sk)
            corr = jnp.exp2(m_i - m_f)
            return num * (corr / (l_i * corr + jnp.exp2(sk - m_f)))
        return num * (1.0 / l_i)

    if fin == "diff":
        lam = sc_ref[0]
        halves = []
        for hh in range(2):
            o = normed(2 * hh) - lam * normed(2 * hh + 1)
            ms = jnp.mean(o * o, axis=0, keepdims=True)
            halves.append(o * lax.rsqrt(ms + LN_EPS))
        ot = jnp.concatenate(halves, axis=0) * extras[0][...] * sc_ref[1]
    else:
        ot = jnp.concatenate([normed(0), normed(1)], axis=0)
    o = ot.T
    if fin == "nsa":
        ocmp_ref, owin_ref, sm_ref, e_ref = extras[1:]
        hi, lo_part = _split2(sm_ref[0])
        gates = [_sigmoid(_dot(hi, e_ref[0, c]) + _dot(lo_part, e_ref[0, c])) for c in range(3)]
        o = gates[0] * ocmp_ref[0, 0] + gates[1] * o + gates[2] * owin_ref[0].astype(F32)
    o_ref[0] = o.astype(o_ref.dtype)


def _tattn(q_arr, q_blk, k_arr, k_blk, vt_all, v_head, *, nmaps, qsel, vmap, mode, fin, name, window=0,
           bias=None, scalars=None, extras=(), extra_specs=()):
    b, s, _ = q_arr.shape
    nc, vrows, tkc = vt_all.shape[2], vt_all.shape[3], vt_all.shape[4]
    tq = 2 * tkc
    nu = 4
    if qsel == "gqa":
        kspec = pl.BlockSpec((1, s, LANES), lambda bi, u, qi, *_: (bi, 0, k_blk))
        vspec = pl.BlockSpec((1, 1, nc, vrows, tkc), lambda bi, u, qi, *_: (bi, v_head + u // 2, 0, 0, 0))
    else:
        kspec = pl.BlockSpec((1, s, LANES), lambda bi, u, qi, *_: (bi, 0, k_blk + u))
        vspec = pl.BlockSpec((1, 2, nc, vrows, tkc), lambda bi, u, qi, *_: (bi, v_head // 2 + u, 0, 0, 0))
    in_specs = [pl.BlockSpec((1, tq, LANES), lambda bi, u, qi, *_: (bi, qi, q_blk + u)), kspec, vspec]
    in_specs += list(extra_specs)
    body = functools.partial(_tattn_body, nmaps=nmaps, qsel=qsel, vmap=vmap, tq=tq, tkc=tkc, mode=mode,
                             window=window, bias=bias, fin=fin)
    args = ([] if scalars is None else [scalars]) + [q_arr, k_arr, vt_all] + list(extras)
    return pl.pallas_call(
        body,
        grid_spec=pltpu.PrefetchScalarGridSpec(
            num_scalar_prefetch=0 if scalars is None else 1, grid=(b, nu, s // tq),
            in_specs=in_specs,
            out_specs=pl.BlockSpec((1, tq, LANES), lambda bi, u, qi, *_: (bi, qi, u)),
            scratch_shapes=[pltpu.VMEM((nmaps, LANES, tq), BF16), pltpu.VMEM((2, nmaps, tkc, tq), F32),
                            pltpu.VMEM((nmaps, 1, tq), F32), pltpu.VMEM((nmaps, vrows, tq), F32)]),
        out_shape=jax.ShapeDtypeStruct((b, s, nu * LANES), BF16),
        compiler_params=_cparams(("parallel", "parallel", "parallel")),
        name=name,
    )(*args)


def _gelu_tanh(x):
    return 0.5 * x * (1.0 + jnp.tanh(math.sqrt(2.0 / math.pi) * (x + 0.044715 * (x * x * x))))


def _compress_body(x_ref, pe_ref, w1_ref, b1_ref, w2_ref, o_ref):
    x = (x_ref[0, 0].astype(F32) + pe_ref[0]).astype(BF16)
    hid = _gelu_tanh(_dot(x, w1_ref[0]) + b1_ref[0])
    o_ref[0, 0] = _dot(hid.astype(BF16), w2_ref[0])


def _nsa_compress(x, pe, w1, b1, w2):
    _, nb, ncp, ld = x.shape
    hid = w1.shape[-1]
    return pl.pallas_call(
        _compress_body,
        grid=(2, nb),
        in_specs=[pl.BlockSpec((1, 1, ncp, ld), lambda t, i: (t, i, 0, 0)),
                  pl.BlockSpec((1, 1, ld), lambda t, i: (t, 0, 0)),
                  pl.BlockSpec((1, ld, hid), lambda t, i: (t, 0, 0)),
                  pl.BlockSpec((1, 1, hid), lambda t, i: (t, 0, 0)),
                  pl.BlockSpec((1, hid, HEAD_DIM), lambda t, i: (t, 0, 0))],
        out_specs=pl.BlockSpec((1, 1, ncp, HEAD_DIM), lambda t, i: (t, i, 0, 0)),
        out_shape=jax.ShapeDtypeStruct((2, nb, ncp, HEAD_DIM), F32),
        compiler_params=_cparams(("parallel", "parallel")),
        name="nsa_compress",
    )(x, pe, w1, b1, w2)


def _cmp_topk_body(q_ref, kc_ref, vct_ref, selt_ref, o_ref, mt_ref, *, tq, ncp, nsel, topn):
    qi = pl.program_id(2)
    q = q_ref[0]
    kc4 = kc_ref[0, 0]
    vct = vct_ref[0, 0]
    lane = lax.broadcasted_iota(jnp.int32, kc4.shape, 1)
    ci = lax.broadcasted_iota(jnp.int32, (ncp, tq), 0)
    tpos = qi * tq + lax.broadcasted_iota(jnp.int32, (ncp, tq), 1)
    cmask = ci * NSA_CMP_D + (NSA_CMP_L - 1) <= tpos
    psum = jnp.zeros((ncp, tq), F32)
    rows = []
    for a in range(4):
        kcm = jnp.where((lane >= a * HEAD_DIM) & (lane < (a + 1) * HEAD_DIM), kc4, jnp.zeros_like(kc4))
        st = jnp.where(cmask, _dot_nt(kcm, q), NEG)
        m = jnp.max(st, axis=0, keepdims=True)
        e = jnp.where(cmask, jnp.exp2(st - m), 0.0)
        l = jnp.sum(e, axis=0, keepdims=True)
        p = e * jnp.where(l > 0.0, 1.0 / l, 0.0)
        psum = psum + p
        rows.append(_dot(vct, p.astype(BF16)))
    hi, lo = _split2(psum)
    selt = selt_ref[...]
    imp = _dot(selt, hi) + _dot(selt, lo)
    blk = lax.broadcasted_iota(jnp.int32, (nsel, tq), 0)
    cur = (qi * tq + lax.broadcasted_iota(jnp.int32, (nsel, tq), 1)) // NSA_SEL_L
    forced = (blk == 0) | (blk == cur) | (blk == cur - 1)
    imp = jnp.where(forced, NSA_FORCE, jnp.where(blk > cur, -NSA_FORCE, imp))
    cnt = jnp.zeros((nsel, tq), jnp.int32)
    for jp in range(nsel):
        v = imp[jp:jp + 1, :]
        tie = jnp.where(blk > jp, 1, 0)
        cnt = cnt + jnp.where(v > imp, 1, jnp.where(v == imp, tie, 0))
    mneg = jnp.where(cnt < topn, 0.0, NEG)
    if nsel < HEAD_DIM:
        mneg = jnp.concatenate([mneg, jnp.zeros((HEAD_DIM - nsel, tq), F32)], axis=0)
    mt_ref[0, 0] = jnp.concatenate([mneg, mneg], axis=0).T.astype(mt_ref.dtype)
    o_ref[0, 0] = jnp.concatenate(rows, axis=0).T


def _nsa_cmp_topk(q_arr, q_off256, kc4, vct, selt, *, tq, topn):
    b, s, _ = q_arr.shape
    ncp = kc4.shape[2]
    nsel = selt.shape[0]
    body = functools.partial(_cmp_topk_body, tq=tq, ncp=ncp, nsel=nsel, topn=topn)
    return pl.pallas_call(
        body,
        grid=(b, 2, s // tq),
        in_specs=[pl.BlockSpec((1, tq, 2 * LANES), lambda bi, g, qi: (bi, qi, q_off256 + g)),
                  pl.BlockSpec((1, 1, ncp, 2 * LANES), lambda bi, g, qi: (bi, g, 0, 0)),
                  pl.BlockSpec((1, 1, HEAD_DIM, ncp), lambda bi, g, qi: (bi, g, 0, 0)),
                  pl.BlockSpec((nsel, ncp), lambda bi, g, qi: (0, 0))],
        out_specs=[pl.BlockSpec((1, 1, tq, 2 * LANES), lambda bi, g, qi: (bi, g, qi, 0)),
                   pl.BlockSpec((1, 1, tq, LANES), lambda bi, g, qi: (bi, g, qi, 0))],
        out_shape=[jax.ShapeDtypeStruct((b, 2, s, 2 * LANES), F32),
                   jax.ShapeDtypeStruct((b, 2, s, LANES), BF16)],
        compiler_params=_cparams(("parallel", "parallel", "parallel")),
        name="nsa_cmp_topk",
    )(q_arr, kc4, vct, selt)


def _cumgate_body(x_ref, o_ref):
    x = x_ref[0]
    r = x.shape[0]
    ls = jnp.minimum(x, 0.0) - jnp.log1p(jnp.exp(-jnp.abs(x)))
    i0 = lax.broadcasted_iota(jnp.int32, (LANES, LANES), 0)
    i1 = lax.broadcasted_iota(jnp.int32, (LANES, LANES), 1)
    upper = jnp.where(i0 <= i1, 1.0, 0.0).astype(BF16)
    ones = jnp.ones((LANES, LANES), BF16)
    r0 = lax.broadcasted_iota(jnp.int32, (r, r), 0)
    r1 = lax.broadcasted_iota(jnp.int32, (r, r), 1)
    strict = jnp.where(r1 < r0, 1.0, 0.0).astype(BF16)
    parts = _split3(ls)
    intra = sum(_dot(pp, upper) for pp in parts)
    rowtot = sum(_dot(pp, ones) for pp in parts)
    off = sum(_dot(strict, pp) for pp in _split3(rowtot))
    o_ref[0] = intra + off


def _cum_log_forget(x):
    n, r, _ = x.shape
    return pl.pallas_call(
        _cumgate_body,
        grid=(n,),
        in_specs=[pl.BlockSpec((1, r, LANES), lambda i: (i, 0, 0))],
        out_specs=pl.BlockSpec((1, r, LANES), lambda i: (i, 0, 0)),
        out_shape=jax.ShapeDtypeStruct((n, r, LANES), F32),
        compiler_params=_cparams(("parallel",)),
        name="cum_log_forget",
    )(x)


def _merge_body(oa_ref, ob_ref, oc_ref, od_ref, g0_ref, g1_ref, g2_ref, g3_ref, wb_ref, o_ref):
    acc = None
    for n, (o_r, g_r) in enumerate(((oa_ref, g0_ref), (ob_ref, g1_ref), (oc_ref, g2_ref), (od_ref, g3_ref))):
        term = _sigmoid(g_r[...].astype(F32)) * _dot(o_r[...], wb_ref[n])
        acc = term if acc is None else acc + term
    o_ref[...] = acc.astype(o_ref.dtype)


def _merge(o_list, plain, gate_off, wb, *, tm, tn):
    t = plain.shape[0]
    d = wb.shape[-1]
    nj = d // tn
    ospec = pl.BlockSpec((tm, BRANCH_W), lambda i, j: (i, 0))
    gspecs = [pl.BlockSpec((tm, tn), functools.partial(lambda i, j, n: (i, gate_off // tn + n * nj + j), n=n))
              for n in range(N_BRANCH)]
    return pl.pallas_call(
        _merge_body,
        grid=(t // tm, nj),
        in_specs=[ospec] * 4 + gspecs + [pl.BlockSpec((N_BRANCH, BRANCH_W, tn), lambda i, j: (0, 0, j))],
        out_specs=pl.BlockSpec((tm, tn), lambda i, j: (i, j)),
        out_shape=jax.ShapeDtypeStruct((t, d), BF16),
        compiler_params=_cparams(("parallel", "parallel")),
        name="gated_merge",
    )(*o_list, plain, plain, plain, plain, wb)


def _layer_norm(y, g, b):
    mu = jnp.mean(y, axis=-1, keepdims=True)
    yc = y - mu
    var = jnp.mean(yc * yc, axis=-1, keepdims=True)
    return yc * lax.rsqrt(var + LN_EPS) * g + b


def _outproj_ln_body(mg_ref, wo_ref, h_ref, g_ref, b_ref, *rest, with_router):
    if with_router:
        wr_ref, o_ref, ob_ref, lg_ref = rest
    else:
        o_ref, ob_ref = rest
    y = ALPHA * h_ref[...] + _dot(mg_ref[...], wo_ref[...])
    out = _layer_norm(y, g_ref[...], b_ref[...])
    o_ref[...] = out
    ob_ref[...] = out.astype(BF16)
    if with_router:
        hi, lo = _split2(out)
        lg_ref[...] = _dot(hi, wr_ref[0]) + _dot(lo, wr_ref[0]) + _dot(hi, wr_ref[1])


def _outproj_ln(merged, w_out, h, g, b, w_router=None, *, tm):
    t, d = h.shape
    with_router = w_router is not None
    row = lambda i: (i, 0)
    fix = lambda i: (0, 0)
    in_specs = [pl.BlockSpec((tm, d), row), pl.BlockSpec((d, d), fix, pipeline_mode=pl.Buffered(1)),
                pl.BlockSpec((tm, d), row),
                pl.BlockSpec((1, d), fix), pl.BlockSpec((1, d), fix)]
    out_specs = [pl.BlockSpec((tm, d), row), pl.BlockSpec((tm, d), row)]
    out_shape = [jax.ShapeDtypeStruct((t, d), F32), jax.ShapeDtypeStruct((t, d), BF16)]
    args = [merged, w_out, h, g, b]
    if with_router:
        in_specs.append(pl.BlockSpec((2, d, LANES), lambda i: (0, 0, 0)))
        out_specs.append(pl.BlockSpec((tm, LANES), row))
        out_shape.append(jax.ShapeDtypeStruct((t, LANES), F32))
        args.append(w_router)
    return pl.pallas_call(
        functools.partial(_outproj_ln_body, with_router=with_router),
        grid=(t // tm,), in_specs=in_specs, out_specs=out_specs, out_shape=out_shape,
        compiler_params=_cparams(("parallel",)), name="outproj_ln1",
    )(*args)


def _ple_ln_body(hb_ref, h_ref, f_ref, p_ref, wg_ref, wp_ref, g_ref, b_ref, o_ref, ob_ref):
    ple = _sigmoid(_dot(hb_ref[...], wg_ref[...])) * _dot(p_ref[...], wp_ref[...])
    out = _layer_norm(ALPHA * h_ref[...] + f_ref[...] + ple, g_ref[...], b_ref[...])
    o_ref[...] = out
    ob_ref[...] = out.astype(BF16)


def _ple_ln(h_bf, h, f, p_bf, w_gate, w_proj, g, b, *, tm):
    t, d = h.shape
    row = lambda i: (i, 0)
    fix = lambda i: (0, 0)
    return pl.pallas_call(
        _ple_ln_body,
        grid=(t // tm,),
        in_specs=[pl.BlockSpec((tm, d), row), pl.BlockSpec((tm, d), row), pl.BlockSpec((tm, d), row),
                  pl.BlockSpec((tm, PLE_DIM), row), pl.BlockSpec((d, d), fix, pipeline_mode=pl.Buffered(1)),
                  pl.BlockSpec((PLE_DIM, d), fix), pl.BlockSpec((1, d), fix), pl.BlockSpec((1, d), fix)],
        out_specs=[pl.BlockSpec((tm, d), row), pl.BlockSpec((tm, d), row)],
        out_shape=[jax.ShapeDtypeStruct((t, d), F32), jax.ShapeDtypeStruct((t, d), BF16)],
        compiler_params=_cparams(("parallel",)), name="ple_ln2",
    )(h_bf, h, f, p_bf, w_gate, w_proj, g, b)


def _swiglu_tile(x, wg, wu, wd):
    g = _dot(x, wg)
    u = _dot(x, wu)
    return _dot((g * _sigmoid(g) * u).astype(BF16), wd)


def _ffn_body(x_ref, wg_ref, wu_ref, wd_ref, o_ref):
    j = pl.program_id(1)
    y = _swiglu_tile(x_ref[...], wg_ref[...], wu_ref[...], wd_ref[...])

    @pl.when(j == 0)
    def _():
        o_ref[...] = y

    @pl.when(j > 0)
    def _():
        o_ref[...] += y


def _ffn(x_bf, wg, wu, wd, *, tm, tf):
    t, d = x_bf.shape
    f = wg.shape[1]
    return pl.pallas_call(
        _ffn_body,
        grid=(t // tm, f // tf),
        in_specs=[pl.BlockSpec((tm, d), lambda i, j: (i, 0)),
                  pl.BlockSpec((d, tf), lambda i, j: (0, j)),
                  pl.BlockSpec((d, tf), lambda i, j: (0, j)),
                  pl.BlockSpec((tf, d), lambda i, j: (j, 0))],
        out_specs=pl.BlockSpec((tm, d), lambda i, j: (i, 0)),
        out_shape=jax.ShapeDtypeStruct((t, d), F32),
        compiler_params=_cparams(("parallel", "arbitrary")), name="ffn_swiglu",
    )(x_bf, wg, wu, wd)


def _moe_ffn_body(te_ref, nt_ref, x_ref, wg_ref, wu_ref, wd_ref, o_ref):
    i = pl.program_id(0)
    j = pl.program_id(1)
    active = i < nt_ref[0]

    @pl.when(active)
    def _():
        y = _swiglu_tile(x_ref[...], wg_ref[0], wu_ref[0], wd_ref[0])

        @pl.when(j == 0)
        def _():
            o_ref[...] = y

        @pl.when(j > 0)
        def _():
            o_ref[...] += y

    @pl.when(jnp.logical_not(active) & (j == 0))
    def _():
        o_ref[...] = jnp.zeros(o_ref.shape, F32)


def _moe_ffn(tile_expert, n_tiles, x_sorted, wg, wu, wd, *, tm, tf):
    r, d = x_sorted.shape
    f = wg.shape[2]
    nj = f // tf

    def jj(i, j, nt):
        return jnp.where(i < nt[0], j, nj - 1)

    return pl.pallas_call(
        _moe_ffn_body,
        grid_spec=pltpu.PrefetchScalarGridSpec(
            num_scalar_prefetch=2, grid=(r // tm, nj),
            in_specs=[pl.BlockSpec((tm, d), lambda i, j, te, nt: (i, 0)),
                      pl.BlockSpec((1, d, tf), lambda i, j, te, nt: (te[i], 0, jj(i, j, nt))),
                      pl.BlockSpec((1, d, tf), lambda i, j, te, nt: (te[i], 0, jj(i, j, nt))),
                      pl.BlockSpec((1, tf, d), lambda i, j, te, nt: (te[i], jj(i, j, nt), 0))],
            out_specs=pl.BlockSpec((tm, d), lambda i, j, te, nt: (i, 0))),
        out_shape=jax.ShapeDtypeStruct((r, d), F32),
        compiler_params=_cparams(("arbitrary", "arbitrary")), name="moe_grouped_ffn",
    )(tile_expert, n_tiles, x_sorted, wg, wu, wd)


def _row_copy(src_ref, src_row, dst_ref, dst_row, sem):
    return pltpu.make_async_copy(src_ref.at[pl.ds(src_row, 1)], dst_ref.at[pl.ds(dst_row, 1)], sem)


def _gather_rows_body(idx_ref, src_ref, o_ref, buf_ref, sem, *, tm):
    base = pl.program_id(0) * tm

    def start(r, c):
        _row_copy(src_ref, idx_ref[base + r], buf_ref, r, sem).start()
        return c

    def wait(r, c):
        _row_copy(src_ref, 0, buf_ref, r, sem).wait()
        return c

    lax.fori_loop(0, tm, start, 0, unroll=8)
    lax.fori_loop(0, tm, wait, 0, unroll=8)
    o_ref[...] = buf_ref[...].astype(o_ref.dtype)


def _gather_rows(idx, src, n_rows, out_dtype, *, tm):
    d = src.shape[1]
    return pl.pallas_call(
        functools.partial(_gather_rows_body, tm=tm),
        grid_spec=pltpu.PrefetchScalarGridSpec(
            num_scalar_prefetch=1, grid=(n_rows // tm,),
            in_specs=[pl.BlockSpec(memory_space=pl.ANY)],
            out_specs=pl.BlockSpec((tm, d), lambda i, idx: (i, 0)),
            scratch_shapes=[pltpu.VMEM((tm, d), src.dtype), pltpu.SemaphoreType.DMA(())]),
        out_shape=jax.ShapeDtypeStruct((n_rows, d), out_dtype),
        compiler_params=_cparams(("arbitrary",)), name="moe_gather_rows",
    )(idx, src)


def _combine_body(idx_ref, src_ref, w_ref, o_ref, a_ref, b_ref, sem, *, tm):
    base = pl.program_id(0) * tm

    def start(r, c):
        _row_copy(src_ref, idx_ref[2 * (base + r)], a_ref, r, sem.at[0]).start()
        _row_copy(src_ref, idx_ref[2 * (base + r) + 1], b_ref, r, sem.at[1]).start()
        return c

    def wait(r, c):
        _row_copy(src_ref, 0, a_ref, r, sem.at[0]).wait()
        _row_copy(src_ref, 0, b_ref, r, sem.at[1]).wait()
        return c

    lax.fori_loop(0, tm, start, 0, unroll=8)
    lax.fori_loop(0, tm, wait, 0, unroll=8)
    w = w_ref[...]
    o_ref[...] = a_ref[...] * w[:, 0:1] + b_ref[...] * w[:, 1:2]


def _combine_pairs(pos, y_sorted, pair_w, n_tokens, *, tm):
    d = y_sorted.shape[1]
    return pl.pallas_call(
        functools.partial(_combine_body, tm=tm),
        grid_spec=pltpu.PrefetchScalarGridSpec(
            num_scalar_prefetch=1, grid=(n_tokens // tm,),
            in_specs=[pl.BlockSpec(memory_space=pl.ANY),
                      pl.BlockSpec((tm, LANES), lambda i, idx: (i, 0))],
            out_specs=pl.BlockSpec((tm, d), lambda i, idx: (i, 0)),
            scratch_shapes=[pltpu.VMEM((tm, d), F32), pltpu.VMEM((tm, d), F32),
                            pltpu.SemaphoreType.DMA((2,))]),
        out_shape=jax.ShapeDtypeStruct((n_tokens, d), F32),
        compiler_params=_cparams(("arbitrary",)), name="moe_combine",
    )(pos, y_sorted, pair_w)


def _col_slices():
    out, off = {}, 0
    for name, width in IN_SPLITS:
        out[name] = (off, width)
        off += width
    return out


def _gather_cols(w, names, pad_to=None):
    cs = _col_slices()
    parts = []
    for n in names:
        col = w[:, cs[n][0]:cs[n][0] + cs[n][1]]
        parts.append(col * Q_FOLD[n] if n in Q_FOLD else col)
    width = sum(cs[n][1] for n in names)
    if pad_to is not None and pad_to > width:
        parts.append(jnp.zeros((w.shape[0], pad_to - width), w.dtype))
    return jnp.concatenate(parts, axis=1).astype(BF16)


def _selection_map_t(n_cmp_pad, n_sel):
    ci = np.arange(n_cmp_pad)[:, None] * NSA_CMP_D
    sj = np.arange(n_sel)[None, :] * NSA_SEL_L
    ov = np.clip(np.minimum(ci + NSA_CMP_L, sj + NSA_SEL_L) - np.maximum(ci, sj), 0, None)
    return np.ascontiguousarray((ov / NSA_CMP_D).astype(np.float32).T)


def _gate_expand_matrices():
    e = np.zeros((4, 3, LANES, LANES), np.float32)
    for j in range(4):
        for hh in range(2):
            for c in range(3):
                e[j, c, (2 * j + hh) * 3 + c, hh * HEAD_DIM:(hh + 1) * HEAD_DIM] = 1.0
    return e


def _split_bits(x, n):
    parts = []
    r = x
    for _ in range(n):
        hi = lax.bitcast_convert_type(
            lax.bitcast_convert_type(r, jnp.uint32) & jnp.uint32(0xFFFF0000), F32)
        parts.append(hi.astype(BF16))
        r = r - hi
    return parts


def _tile(n, pref):
    return pref if n % pref == 0 else n


def _vt_heads(x3, tkc):
    b, s, c = x3.shape
    nh, nc = c // HEAD_DIM, s // tkc
    v = x3.reshape(b, nc, tkc, nh, HEAD_DIM).transpose(0, 3, 1, 4, 2)
    ones = jnp.ones((b, nh, nc, 1, tkc), v.dtype)
    zeros = jnp.zeros((b, nh, nc, ATT_VROWS - HEAD_DIM - 1, tkc), v.dtype)
    return jnp.concatenate([v, ones, zeros], axis=3)


def _token_mixer(h, h_bf, layer, b, s, tabs, w_in, cmp_pe, cmp_w1, cmp_b1, cmp_w2, sinks, fox_bf,
                 diff_lambda, diff_gain, w_branch, w_out, ln_g, ln_b, w_router):
    t = b * s
    (cos64, sin64), (cos32, sin32) = tabs
    tm = _tile(t, 1024)
    r64 = _proj(h_bf, _gather_cols(w_in, SEG_ROPE64), BF16, tm, 512, rope=(cos64, sin64, HEAD_DIM // 2))
    r32 = _proj(h_bf, _gather_cols(w_in, SEG_ROPE32), BF16, tm, 512, rope=(cos32, sin32, DIFF_SUB // 2))
    plain = _proj(h_bf, _gather_cols(w_in, SEG_PLAIN), BF16, tm, 512)
    gates = _proj(h_bf, _gather_cols(w_in, SEG_GATES), BF16, tm, 512)
    small =_proj(h_bf, _gather_cols(w_in, SEG_SMALL, pad_to=LANES), F32, tm, LANES)
    r64_3, r32_3, plain_3, small_3 = (a.reshape(b, s, -1) for a in (r64, r32, plain, small))
    tkc = min(ATT_TKC, s // 2)
    tq = 2 * tkc

    ncp = s // NSA_CMP_D
    n_sel = s // NSA_SEL_L
    topn = min(NSA_TOPN, n_sel)

    def cmp_blocks(x2d):
        c = x2d.reshape(b, s, 2, HEAD_DIM).transpose(0, 2, 1, 3).reshape(b * 2, ncp, NSA_CMP_D * HEAD_DIM)
        nxt = jnp.concatenate([c[:, 1:], jnp.zeros_like(c[:, :1])], axis=1)
        return jnp.concatenate([c, nxt], axis=-1)

    xk = cmp_blocks(r64[:, 512:640])
    xv = cmp_blocks(plain[:, 1024:1152])
    cmp_kv = _nsa_compress(jnp.stack([xk, xv]), cmp_pe.reshape(2, 1, -1), cmp_w1.astype(BF16),
                           cmp_b1.reshape(2, 1, -1), cmp_w2.astype(BF16))
    kc = cmp_kv[0].astype(BF16).reshape(b, 2, ncp, HEAD_DIM)
    vc = cmp_kv[1].astype(BF16).reshape(b, 2, ncp, HEAD_DIM)
    kc4 = jnp.tile(kc, (1, 1, 1, 4))
    vct = vc.transpose(0, 1, 3, 2)
    selt = jnp.asarray(_selection_map_t(ncp, n_sel), BF16)
    o_cmp, mneg_t = _nsa_cmp_topk(r64_3, 0, kc4, vct, selt, tq=_tile(s, 256), topn=topn)
    vt_all = _vt_heads(plain_3[..., 1152:2560], tkc)
    o_win = _tattn(r64_3, 0, r64_3, 6, vt_all, 2, nmaps=2, qsel="gqa", vmap=(0, 0), mode="band",
                   window=NSA_WIN, fin="win", name="nsa_window_attention")
    e_mat = jnp.asarray(_gate_expand_matrices(), BF16)
    block_id = jax.nn.one_hot(jnp.arange(s) // NSA_SEL_L, HEAD_DIM, dtype=BF16)
    block_id = jnp.concatenate([block_id, block_id], axis=1)
    nsa_specs = [pl.BlockSpec((1, 1, tq, LANES), lambda bi, u, qi, *_: (bi, u // 2, qi, 0)),
                 pl.BlockSpec((s, LANES), lambda bi, u, qi, *_: (0, 0)),
                 pl.BlockSpec((1, 1, tq, LANES), lambda bi, u, qi, *_: (bi, u // 2, qi, u % 2)),
                 pl.BlockSpec((1, tq, LANES), lambda bi, u, qi, *_: (bi, qi, u)),
                 pl.BlockSpec((1, tq, LANES), lambda bi, u, qi, *_: (bi, qi, 0)),
                 pl.BlockSpec((1, 3, LANES, LANES), lambda bi, u, qi, *_: (u, 0, 0, 0))]
    o_a = _tattn(r64_3, 0, r64_3, 5, vt_all, 0, nmaps=2, qsel="gqa", vmap=(0, 0), mode="causal", fin="nsa",
                 bias="nsa", name="nsa_selected_attention", extras=(mneg_t, block_id, o_cmp, o_win, small_3, e_mat),
                 extra_specs=nsa_specs)

    o_b = _tattn(r64_3, 7, r64_3, 11, vt_all, 4, nmaps=2, qsel="gqa", vmap=(0, 0), mode="band",
                 window=SWA_WIN, fin="swa", name="swa_attention", scalars=sinks.astype(F32) * LOG2E)

    f_logit = (small[:, 24:32] + fox_bf[None, :]).reshape(b, s, 8).transpose(0, 2, 1)
    cum = _cum_log_forget(f_logit.reshape(b * 8, s // LANES, LANES)).reshape(b, 8, s)
    hi, mid, lo_piece = (piece[..., None] for piece in _split_bits(cum * LOG2E, 3))
    slot = (jnp.arange(LANES) % HEAD_DIM)[None, None, None, :]
    ck3 = jnp.where(slot == 0, hi, jnp.where(slot == 1, mid, jnp.where(slot == 2, lo_piece, jnp.zeros((), BF16))))
    o_c = _tattn(plain_3, 0, plain_3, 4, vt_all, 6, nmaps=2, qsel="pair", vmap=(0, 1), mode="causal",
                 fin="fox", bias="fox", name="fox_attention", extras=(ck3,),
                 extra_specs=[pl.BlockSpec((1, 2, s, LANES), lambda bi, u, qi, *_: (bi, u, 0, 0))])

    lam_init = 0.8 - 0.6 * math.exp(-0.3 * layer)
    lf = diff_lambda.astype(F32)
    lam = jnp.exp(jnp.sum(lf[0] * lf[1])) - jnp.exp(jnp.sum(lf[2] * lf[3])) + lam_init
    lam_arr = jnp.stack([lam, jnp.asarray(1.0 - lam_init, F32)]).astype(F32)
    gain_t = jnp.broadcast_to(jnp.tile(diff_gain.astype(F32), 2)[:, None], (LANES, tq))
    o_d = _tattn(r32_3, 0, r32_3, 4, vt_all, 14, nmaps=4, qsel="pair", vmap=(0, 0, 1, 1), mode="causal",
                 fin="diff", name="diff_attention", scalars=lam_arr, extras=(gain_t,),
                 extra_specs=[pl.BlockSpec((LANES, tq), lambda bi, u, qi, *_: (0, 0))])

    o_list = [o.reshape(t, BRANCH_W) for o in (o_a, o_b, o_c, o_d)]
    merged = _merge(o_list, gates, 0, w_branch.astype(BF16), tm=tm, tn=512)
    return _outproj_ln(merged, w_out.astype(BF16), h, ln_g.reshape(1, -1), ln_b.reshape(1, -1),
                       w_router, tm=_tile(t, 256))


def _moe_layer(h1, logits_pad, b_router, wg, wu, wd, *, tm, tf):
    t, d = h1.shape
    logits = logits_pad[:, :N_EXPERTS] + b_router.astype(F32)[None, :]
    top_v, top_i = lax.top_k(logits, TOP_K)
    top_w = jax.nn.softmax(top_v, axis=-1)
    flat_e = top_i.reshape(-1)
    onehot = jax.nn.one_hot(flat_e, N_EXPERTS, dtype=jnp.int32)
    rank = jnp.sum((jnp.cumsum(onehot, axis=0) - onehot) * onehot, axis=1)
    cnt = jnp.sum(onehot, axis=0)
    padded = ((cnt + tm - 1) // tm) * tm
    ends = jnp.cumsum(padded)
    starts = ends - padded
    pos = (starts[flat_e] + rank).astype(jnp.int32)
    n_rows = TOP_K * t + N_EXPERTS * tm
    row_token = jnp.zeros((n_rows,), jnp.int32).at[pos].set(jnp.arange(TOP_K * t, dtype=jnp.int32) // TOP_K)
    tile_start = jnp.arange(n_rows // tm, dtype=jnp.int32) * tm
    tile_expert = jnp.minimum(jnp.sum(tile_start[:, None] >= ends[None, :], axis=1), N_EXPERTS - 1)
    n_tiles = (ends[-1] // tm).astype(jnp.int32).reshape(1)
    x_sorted = _gather_rows(row_token, h1, n_rows, BF16, tm=512)
    y_sorted = _moe_ffn(tile_expert.astype(jnp.int32), n_tiles, x_sorted, wg, wu, wd, tm=tm, tf=tf)
    pair_w = jnp.pad(top_w.astype(F32), ((0, 0), (0, LANES - TOP_K)))
    return _combine_pairs(pos, y_sorted, pair_w, t, tm=_tile(t, 256))


def kernel(x, p, positions, w_in, nsa_cmp_pe, nsa_cmp_w1, nsa_cmp_b1, nsa_cmp_w2, swa_sinks, fox_bf,
           diff_lambda, diff_gain, w_branch, w_out, ln1_g, ln1_b, ffn_wg, ffn_wu, ffn_wd, moe_router,
           moe_router_b, moe_wg, moe_wu, moe_wd, ple_proj, ple_gate, ln2_g, ln2_b):
    b, s, d = x.shape
    t = b * s
    tabs = (_rope_tabs(positions, HEAD_DIM), _rope_tabs(positions, DIFF_SUB))
    h = x.reshape(t, d).astype(F32)
    h_bf = h.astype(BF16)
    for i in range(DEPTH):
        is_moe = i % 2 == 1
        w_router = None
        if is_moe:
            wr = jnp.zeros((d, LANES), F32).at[:, :N_EXPERTS].set(moe_router[i // 2].astype(F32))
            w_router = jnp.stack(_split_bits(wr, 2))
        res = _token_mixer(h, h_bf, i, b, s, tabs, w_in[i], nsa_cmp_pe[i], nsa_cmp_w1[i], nsa_cmp_b1[i],
                           nsa_cmp_w2[i], swa_sinks[i], fox_bf[i], diff_lambda[i], diff_gain[i],
                           w_branch[i], w_out[i], ln1_g[i], ln1_b[i], w_router)
        h1, h1_bf = res[0], res[1]
        if not is_moe:
            fpad = (-D_FF) % 512
            wg = jnp.pad(ffn_wg[i // 2].astype(BF16), ((0, 0), (0, fpad)))
            wu = jnp.pad(ffn_wu[i // 2].astype(BF16), ((0, 0), (0, fpad)))
            wd = jnp.pad(ffn_wd[i // 2].astype(BF16), ((0, fpad), (0, 0)))
            f = _ffn(h1_bf, wg, wu, wd, tm=_tile(t, 1024), tf=512)
        else:
            f = _moe_layer(h1, res[2], moe_router_b[i // 2], moe_wg[i // 2].astype(BF16),
                           moe_wu[i // 2].astype(BF16), moe_wd[i // 2].astype(BF16),
                           tm=_tile(t, 512), tf=1024)
        h, h_bf = _ple_ln(h1_bf, h1, f, p[i].reshape(t, PLE_DIM).astype(BF16), ple_gate[i].astype(BF16),
                          ple_proj[i].astype(BF16), ln2_g[i].reshape(1, -1), ln2_b[i].reshape(1, -1),
                          tm=_tile(t, 512))
    return h.reshape(b, s, d).astype(x.dtype)
```

```python
import functools
import math

import numpy as np
import jax
import jax.numpy as jnp
from jax import lax
from jax.experimental import pallas as pl
from jax.experimental.pallas import tpu as pltpu

F32 = jnp.float32
BF16 = jnp.bfloat16

D_MODEL = 2048
DEPTH = 2
HEAD_DIM = 64
ROPE_THETA = 10000.0
PLE_DIM = 256
LN_EPS = 1e-5
NSA_CMP_L = 32
NSA_CMP_D = 16
NSA_SEL_L = 64
NSA_TOPN = 16
NSA_WIN = 512
NSA_CMP_HIDDEN = 256
NSA_FORCE = 1e9
SWA_WIN = 128
DIFF_SUB = HEAD_DIM // 2
N_BRANCH = 4
BRANCH_W = 8 * HEAD_DIM
D_FF = 5504
N_EXPERTS = 8
TOP_K = 2
D_FF_EXPERT = 7168
ALPHA = (2.0 * DEPTH) ** 0.25

IN_SPLITS = (
    ("a_q", 512), ("a_kc", 128), ("a_vc", 128), ("a_ks", 128), ("a_vs", 128),
    ("a_kw", 128), ("a_vw", 128), ("a_g", 24),
    ("b_q", 512), ("b_k", 128), ("b_v", 128),
    ("c_q", 512), ("c_k", 512), ("c_v", 512), ("c_f", 8),
    ("d_q", 512), ("d_k", 512), ("d_v", 512),
    ("merge_gate", N_BRANCH * D_MODEL),
)
SEG_ROPE64 = ("a_q", "a_kc", "a_ks", "a_kw", "b_q", "b_k")
SEG_ROPE32 = ("d_q", "d_k")
SEG_PLAIN = ("c_q", "c_k", "a_vc", "a_vs", "a_vw", "b_v", "c_v", "d_v")
SEG_GATES = ("merge_gate",)
SEG_SMALL = ("a_g", "c_f")

LANES = 128
NEG = -1e30
LOG2E = math.log2(math.e)
VMEM_LIMIT = 56 * 1024 * 1024
ATT_TKC = 256
ATT_TQ = 2 * ATT_TKC
ATT_VROWS = 80
ROW_DMA_GROUP = 8

Q_FOLD = {"a_q": HEAD_DIM ** -0.5 * LOG2E, "b_q": HEAD_DIM ** -0.5 * LOG2E,
          "c_q": HEAD_DIM ** -0.5 * LOG2E, "d_q": DIFF_SUB ** -0.5 * LOG2E}


def _cparams(sem):
    return pltpu.CompilerParams(dimension_semantics=sem, vmem_limit_bytes=VMEM_LIMIT)


def _sigmoid(x):
    return 1.0 / (1.0 + jnp.exp(-x))


def _dot(a, b):
    return jnp.dot(a, b, preferred_element_type=F32)


def _dot_nt(a, b):
    return lax.dot_general(a, b, (((1,), (1,)), ((), ())), preferred_element_type=F32)


def _split2(x):
    hi = x.astype(BF16)
    lo = (x - hi.astype(F32)).astype(BF16)
    return hi, lo


def _split3(x):
    hi = x.astype(BF16)
    r = x - hi.astype(F32)
    mid = r.astype(BF16)
    lo = (r - mid.astype(F32)).astype(BF16)
    return hi, mid, lo


def _proj_body(x_ref, w_ref, *rest, rope_half):
    if rope_half:
        cos_ref, sin_ref, o_ref = rest
    else:
        (o_ref,) = rest
    acc = _dot(x_ref[...], w_ref[...])
    if not rope_half:
        o_ref[...] = acc.astype(o_ref.dtype)
        return
    cos = cos_ref[...]
    sin = sin_ref[...]
    lane = lax.broadcasted_iota(jnp.int32, cos.shape, 1)
    first = (lane % (2 * rope_half)) < rope_half
    for c in range(acc.shape[1] // LANES):
        a = acc[:, c * LANES:(c + 1) * LANES]
        rot = jnp.where(first, pltpu.roll(a, LANES - rope_half, 1), pltpu.roll(a, rope_half, 1))
        o_ref[:, c * LANES:(c + 1) * LANES] = (a * cos + rot * sin).astype(o_ref.dtype)


def _proj(x, w, out_dtype, tm, tn, rope=None):
    m, k = x.shape
    n = w.shape[1]
    in_specs = [pl.BlockSpec((tm, k), lambda i, j: (i, 0)),
                pl.BlockSpec((k, tn), lambda i, j: (0, j))]
    args = [x, w]
    rope_half = 0
    if rope is not None:
        cos_tab, sin_tab, rope_half = rope
        in_specs += [pl.BlockSpec((tm, LANES), lambda i, j: (i, 0)),
                     pl.BlockSpec((tm, LANES), lambda i, j: (i, 0))]
        args += [cos_tab, sin_tab]
    return pl.pallas_call(
        functools.partial(_proj_body, rope_half=rope_half),
        grid=(m // tm, n // tn),
        in_specs=in_specs,
        out_specs=pl.BlockSpec((tm, tn), lambda i, j: (i, j)),
        out_shape=jax.ShapeDtypeStruct((m, n), out_dtype),
        compiler_params=_cparams(("parallel", "parallel")),
        name="proj_rope" if rope_half else "proj",
    )(*args)


def _rope_tabs(positions, dim):
    inv = 1.0 / (ROPE_THETA ** (jnp.arange(0, dim, 2, dtype=F32) / dim))
    ang = positions.astype(F32).reshape(-1)[:, None] * inv
    c, s = jnp.cos(ang), jnp.sin(ang)
    reps = LANES // dim
    return (jnp.tile(jnp.concatenate([c, c], -1), (1, reps)),
            jnp.tile(jnp.concatenate([-s, s], -1), (1, reps)))


def _tattn_body(*refs, nmaps, qsel, vmap, tq, tkc, mode, window, bias, fin):
    refs = list(refs)
    sc_ref = refs.pop(0) if fin in ("diff", "swa") else None
    q_ref, k_ref, vt_ref = refs[:3]
    extras, o_ref = refs[3:-5], refs[-5]
    qm_ref, st_ref, m_ref, acc_ref = refs[-4:]
    bias_ref = None
    if bias is not None:
        bias_ref, extras = extras[0], extras[1:]
    u = pl.program_id(1)
    qi = pl.program_id(2)
    q0 = qi * tq
    last_chunk = k_ref.shape[1] // tkc - 1

    lane = lax.broadcasted_iota(jnp.int32, (tq, LANES), 1)
    klane = lax.broadcasted_iota(jnp.int32, (tkc, LANES), 1)
    q = q_ref[0].astype(F32)
    if qsel == "gqa":
        lo_lane = (u // 2) * HEAD_DIM
        q_rolled = pltpu.roll(q, HEAD_DIM, 1)
        in_group = (lane >= lo_lane) & (lane < lo_lane + HEAD_DIM)
        k_in_group = (klane >= lo_lane) & (klane < lo_lane + HEAD_DIM)
        fill = bias_ref[0, 0].astype(F32) if bias == "nsa" else 0.0
        for e in range(2):
            q_e = jnp.where(lo_lane == e * HEAD_DIM, q, q_rolled)
            qm_ref[e] = jnp.where(in_group, q_e, fill).T.astype(BF16)
    else:
        width = LANES // nmaps
        fill = jnp.where(lane % HEAD_DIM < 3, -1.0, 0.0) if bias == "fox" else 0.0
        for mp in range(nmaps):
            qm_ref[mp] = jnp.where((lane >= mp * width) & (lane < (mp + 1) * width), q, fill).T.astype(BF16)

    m_ref[...] = jnp.full(m_ref.shape, NEG, F32)
    acc_ref[...] = jnp.zeros(acc_ref.shape, F32)

    def qk(c, buf, q_lo=0, q_hi=tq):
        cc = jnp.clip(c, 0, last_chunk)
        off = pl.multiple_of(cc * tkc, tkc)
        kc = k_ref[0, pl.ds(off, tkc), :]
        if bias == "nsa":
            kc = jnp.where(k_in_group, kc, extras[0][pl.ds(off, tkc), :])
        for mp in range(nmaps):
            kc_mp = kc
            if bias == "fox":
                kc_mp = jnp.where((klane >= mp * HEAD_DIM) & (klane < (mp + 1) * HEAD_DIM), kc,
                                  bias_ref[0, mp, pl.ds(off, tkc), :])
            st_ref[buf, mp, :, q_lo:q_hi] = _dot(kc_mp, qm_ref[mp, :, q_lo:q_hi])

    def soft(c, buf, masked, q_lo=0, q_hi=tq):
        for mp in range(nmaps):
            st = st_ref[buf, mp, :, q_lo:q_hi]
            if masked:
                key = c * tkc + lax.broadcasted_iota(jnp.int32, st.shape, 0)
                t_pos = q0 + q_lo + lax.broadcasted_iota(jnp.int32, st.shape, 1)
                keep = key <= t_pos
                if mode == "band":
                    keep = keep & (t_pos - key < window) & (key >= 0)
                st = jnp.where(keep, st, NEG)
            m_old = m_ref[mp, :, q_lo:q_hi]
            m8 = jnp.max(st.reshape(tkc // 8, 8, q_hi - q_lo), axis=0)
            m_new = jnp.maximum(m_old, jnp.max(m8, axis=0, keepdims=True))
            p = jnp.exp2(st - m_new).astype(BF16)
            acc_ref[mp, :, q_lo:q_hi] = (jnp.exp2(m_old - m_new) * acc_ref[mp, :, q_lo:q_hi]
                                         + _dot(vt_ref[0, vmap[mp], jnp.maximum(c, 0)], p))
            m_ref[mp, :, q_lo:q_hi] = m_new

    def pair(pidx, carry, masked):
        c0 = 2 * pidx
        qk(c0 + 1, 1)
        soft(c0, 0, masked)
        qk(c0 + 2, 0)
        soft(c0 + 1, 1, masked)
        return carry

    if mode == "causal":
        qk(0, 0)
        lax.fori_loop(0, qi, functools.partial(pair, masked=False), 0)
        qk(2 * qi + 1, 1, q_lo=tkc)
        soft(2 * qi, 0, True, q_hi=tkc)
        soft(2 * qi, 0, False, q_lo=tkc)
        soft(2 * qi + 1, 1, True, q_lo=tkc)
    else:
        sched = []
        for d in range(-((window - 1 + tkc - 1) // tkc), 2):
            hi_lane = min(tq, d * tkc + tkc - 1 + window)
            sched.append((2 * qi + d, max(0, d * tkc), -(-hi_lane // LANES) * LANES))
        qk(sched[0][0], 0, sched[0][1], sched[0][2])
        for i, (c, lo_lane_q, hi_lane_q) in enumerate(sched):
            if i + 1 < len(sched):
                qk(sched[i + 1][0], (i + 1) % 2, sched[i + 1][1], sched[i + 1][2])
            soft(c, i % 2, True, lo_lane_q, hi_lane_q)

    def normed(mp):
        acc = acc_ref[mp]
        num, l_i = acc[:HEAD_DIM], acc[HEAD_DIM:HEAD_DIM + 1]
        if fin == "swa":
            m_i = m_ref[mp]
            sk = sc_ref[2 * u + mp]
            m_f = jnp.maximum(m_i, sk)
            corr = jnp.exp2(m_i - m_f)
            return num * (corr / (l_i * corr + jnp.exp2(sk - m_f)))
        return num * (1.0 / l_i)

    if fin == "diff":
        lam = sc_ref[0]
        halves = []
        for hh in range(2):
            o = normed(2 * hh) - lam * normed(2 * hh + 1)
            ms = jnp.mean(o * o, axis=0, keepdims=True)
            halves.append(o * lax.rsqrt(ms + LN_EPS))
        ot = jnp.concatenate(halves, axis=0) * extras[0][...] * sc_ref[1]
    else:
        ot = jnp.concatenate([normed(0), normed(1)], axis=0)
    o = ot.T
    if fin == "nsa":
        ocmp_ref, owin_ref, sm_ref, e_ref = extras[1:]
        hi, lo_part = _split2(sm_ref[0])
        gates = [_sigmoid(_dot(hi, e_ref[0, c]) + _dot(lo_part, e_ref[0, c])) for c in range(3)]
        o = gates[0] * ocmp_ref[0, 0] + gates[1] * o + gates[2] * owin_ref[0].astype(F32)
    o_ref[0] = o.astype(o_ref.dtype)


def _tattn(q_arr, q_blk, k_arr, k_blk, vt_all, v_head, *, nmaps, qsel, vmap, mode, fin, name, window=0,
           bias=None, scalars=None, extras=(), extra_specs=()):
    b, s, _ = q_arr.shape
    nc, vrows, tkc = vt_all.shape[2], vt_all.shape[3], vt_all.shape[4]
    tq = 2 * tkc
    nu = 4
    if qsel == "gqa":
        kspec = pl.BlockSpec((1, s, LANES), lambda bi, u, qi, *_: (bi, 0, k_blk))
        vspec = pl.BlockSpec((1, 1, nc, vrows, tkc), lambda bi, u, qi, *_: (bi, v_head + u // 2, 0, 0, 0))
    else:
        kspec = pl.BlockSpec((1, s, LANES), lambda bi, u, qi, *_: (bi, 0, k_blk + u))
        vspec = pl.BlockSpec((1, 2, nc, vrows, tkc), lambda bi, u, qi, *_: (bi, v_head // 2 + u, 0, 0, 0))
    in_specs = [pl.BlockSpec((1, tq, LANES), lambda bi, u, qi, *_: (bi, qi, q_blk + u)), kspec, vspec]
    in_specs += list(extra_specs)
    body = functools.partial(_tattn_body, nmaps=nmaps, qsel=qsel, vmap=vmap, tq=tq, tkc=tkc, mode=mode,
                             window=window, bias=bias, fin=fin)
    args = ([] if scalars is None else [scalars]) + [q_arr, k_arr, vt_all] + list(extras)
    return pl.pallas_call(
        body,
        grid_spec=pltpu.PrefetchScalarGridSpec(
            num_scalar_prefetch=0 if scalars is None else 1, grid=(b, nu, s // tq),
            in_specs=in_specs,
            out_specs=pl.BlockSpec((1, tq, LANES), lambda bi, u, qi, *_: (bi, qi, u)),
            scratch_shapes=[pltpu.VMEM((nmaps, LANES, tq), BF16), pltpu.VMEM((2, nmaps, tkc, tq), F32),
                            pltpu.VMEM((nmaps, 1, tq), F32), pltpu.VMEM((nmaps, vrows, tq), F32)]),
        out_shape=jax.ShapeDtypeStruct((b, s, nu * LANES), BF16),
        compiler_params=_cparams(("parallel", "parallel", "parallel")),
        name=name,
    )(*args)


def _gelu_tanh(x):
    return 0.5 * x * (1.0 + jnp.tanh(math.sqrt(2.0 / math.pi) * (x + 0.044715 * (x * x * x))))


def _compress_body(x_ref, pe_ref, w1_ref, b1_ref, w2_ref, o_ref):
    x = (x_ref[0, 0].astype(F32) + pe_ref[0]).astype(BF16)
    hid = _gelu_tanh(_dot(x, w1_ref[0]) + b1_ref[0])
    o_ref[0, 0] = _dot(hid.astype(BF16), w2_ref[0])


def _nsa_compress(x, pe, w1, b1, w2):
    _, nb, ncp, ld = x.shape
    hid = w1.shape[-1]
    return pl.pallas_call(
        _compress_body,
        grid=(2, nb),
        in_specs=[pl.BlockSpec((1, 1, ncp, ld), lambda t, i: (t, i, 0, 0)),
                  pl.BlockSpec((1, 1, ld), lambda t, i: (t, 0, 0)),
                  pl.BlockSpec((1, ld, hid), lambda t, i: (t, 0, 0)),
                  pl.BlockSpec((1, 1, hid), lambda t, i: (t, 0, 0)),
                  pl.BlockSpec((1, hid, HEAD_DIM), lambda t, i: (t, 0, 0))],
        out_specs=pl.BlockSpec((1, 1, ncp, HEAD_DIM), lambda t, i: (t, i, 0, 0)),
        out_shape=jax.ShapeDtypeStruct((2, nb, ncp, HEAD_DIM), F32),
        compiler_params=_cparams(("parallel", "parallel")),
        name="nsa_compress",
    )(x, pe, w1, b1, w2)


def _cmp_topk_body(q_ref, kc_ref, vct_ref, selt_ref, o_ref, mt_ref, *, tq, ncp, nsel, topn):
    qi = pl.program_id(2)
    q = q_ref[0]
    kc4 = kc_ref[0, 0]
    vct = vct_ref[0, 0]
    lane = lax.broadcasted_iota(jnp.int32, kc4.shape, 1)
    ci = lax.broadcasted_iota(jnp.int32, (ncp, tq), 0)
    tpos = qi * tq + lax.broadcasted_iota(jnp.int32, (ncp, tq), 1)
    cmask = ci * NSA_CMP_D + (NSA_CMP_L - 1) <= tpos
    psum = jnp.zeros((ncp, tq), F32)
    rows = []
    for a in range(4):
        kcm = jnp.where((lane >= a * HEAD_DIM) & (lane < (a + 1) * HEAD_DIM), kc4, jnp.zeros_like(kc4))
        st = jnp.where(cmask, _dot_nt(kcm, q), NEG)
        m = jnp.max(st, axis=0, keepdims=True)
        e = jnp.where(cmask, jnp.exp2(st - m), 0.0)
        l = jnp.sum(e, axis=0, keepdims=True)
        p = e * jnp.where(l > 0.0, 1.0 / l, 0.0)
        psum = psum + p
        rows.append(_dot(vct, p.astype(BF16)))
    hi, lo = _split2(psum)
    selt = selt_ref[...]
    imp = _dot(selt, hi) + _dot(selt, lo)
    blk = lax.broadcasted_iota(jnp.int32, (nsel, tq), 0)
    cur = (qi * tq + lax.broadcasted_iota(jnp.int32, (nsel, tq), 1)) // NSA_SEL_L
    forced = (blk == 0) | (blk == cur) | (blk == cur - 1)
    imp = jnp.where(forced, NSA_FORCE, jnp.where(blk > cur, -NSA_FORCE, imp))
    cnt = jnp.zeros((nsel, tq), jnp.int32)
    for jp in range(nsel):
        v = imp[jp:jp + 1, :]
        tie = jnp.where(blk > jp, 1, 0)
        cnt = cnt + jnp.where(v > imp, 1, jnp.where(v == imp, tie, 0))
    mneg = jnp.where(cnt < topn, 0.0, NEG)
    if nsel < HEAD_DIM:
        mneg = jnp.concatenate([mneg, jnp.zeros((HEAD_DIM - nsel, tq), F32)], axis=0)
    mt_ref[0, 0] = jnp.concatenate([mneg, mneg], axis=0).T.astype(mt_ref.dtype)
    o_ref[0, 0] = jnp.concatenate(rows, axis=0).T


def _nsa_cmp_topk(q_arr, q_off256, kc4, vct, selt, *, tq, topn):
    b, s, _ = q_arr.shape
    ncp = kc4.shape[2]
    nsel = selt.shape[0]
    body = functools.partial(_cmp_topk_body, tq=tq, ncp=ncp, nsel=nsel, topn=topn)
    return pl.pallas_call(
        body,
        grid=(b, 2, s // tq),
        in_specs=[pl.BlockSpec((1, tq, 2 * LANES), lambda bi, g, qi: (bi, qi, q_off256 + g)),
                  pl.BlockSpec((1, 1, ncp, 2 * LANES), lambda bi, g, qi: (bi, g, 0, 0)),
                  pl.BlockSpec((1, 1, HEAD_DIM, ncp), lambda bi, g, qi: (bi, g, 0, 0)),
                  pl.BlockSpec((nsel, ncp), lambda bi, g, qi: (0, 0))],
        out_specs=[pl.BlockSpec((1, 1, tq, 2 * LANES), lambda bi, g, qi: (bi, g, qi, 0)),
                   pl.BlockSpec((1, 1, tq, LANES), lambda bi, g, qi: (bi, g, qi, 0))],
        out_shape=[jax.ShapeDtypeStruct((b, 2, s, 2 * LANES), F32),
                   jax.ShapeDtypeStruct((b, 2, s, LANES), BF16)],
        compiler_params=_cparams(("parallel", "parallel", "parallel")),
        name="nsa_cmp_topk",
    )(q_arr, kc4, vct, selt)


def _cumgate_body(x_ref, o_ref):
    x = x_ref[0]
    r = x.shape[0]
    ls = jnp.minimum(x, 0.0) - jnp.log1p(jnp.exp(-jnp.abs(x)))
    i0 = lax.broadcasted_iota(jnp.int32, (LANES, LANES), 0)
    i1 = lax.broadcasted_iota(jnp.int32, (LANES, LANES), 1)
    upper = jnp.where(i0 <= i1, 1.0, 0.0).astype(BF16)
    ones = jnp.ones((LANES, LANES), BF16)
    r0 = lax.broadcasted_iota(jnp.int32, (r, r), 0)
    r1 = lax.broadcasted_iota(jnp.int32, (r, r), 1)
    strict = jnp.where(r1 < r0, 1.0, 0.0).astype(BF16)
    parts = _split3(ls)
    intra = sum(_dot(pp, upper) for pp in parts)
    rowtot = sum(_dot(pp, ones) for pp in parts)
    off = sum(_dot(strict, pp) for pp in _split3(rowtot))
    o_ref[0] = intra + off


def _cum_log_forget(x):
    n, r, _ = x.shape
    return pl.pallas_call(
        _cumgate_body,
        grid=(n,),
        in_specs=[pl.BlockSpec((1, r, LANES), lambda i: (i, 0, 0))],
        out_specs=pl.BlockSpec((1, r, LANES), lambda i: (i, 0, 0)),
        out_shape=jax.ShapeDtypeStruct((n, r, LANES), F32),
        compiler_params=_cparams(("parallel",)),
        name="cum_log_forget",
    )(x)


def _merge_body(oa_ref, ob_ref, oc_ref, od_ref, g0_ref, g1_ref, g2_ref, g3_ref, wb_ref, o_ref):
    acc = None
    for n, (o_r, g_r) in enumerate(((oa_ref, g0_ref), (ob_ref, g1_ref), (oc_ref, g2_ref), (od_ref, g3_ref))):
        term = _sigmoid(g_r[...].astype(F32)) * _dot(o_r[...], wb_ref[n])
        acc = term if acc is None else acc + term
    o_ref[...] = acc.astype(o_ref.dtype)


def _merge(o_list, plain, gate_off, wb, *, tm, tn):
    t = plain.shape[0]
    d = wb.shape[-1]
    nj = d // tn
    ospec = pl.BlockSpec((tm, BRANCH_W), lambda i, j: (i, 0))
    gspecs = [pl.BlockSpec((tm, tn), functools.partial(lambda i, j, n: (i, gate_off // tn + n * nj + j), n=n))
              for n in range(N_BRANCH)]
    return pl.pallas_call(
        _merge_body,
        grid=(t // tm, nj),
        in_specs=[ospec] * 4 + gspecs + [pl.BlockSpec((N_BRANCH, BRANCH_W, tn), lambda i, j: (0, 0, j))],
        out_specs=pl.BlockSpec((tm, tn), lambda i, j: (i, j)),
        out_shape=jax.ShapeDtypeStruct((t, d), BF16),
        compiler_params=_cparams(("parallel", "parallel")),
        name="gated_merge",
    )(*o_list, plain, plain, plain, plain, wb)


def _layer_norm(y, g, b):
    mu = jnp.mean(y, axis=-1, keepdims=True)
    yc = y - mu
    var = jnp.mean(yc * yc, axis=-1, keepdims=True)
    return yc * lax.rsqrt(var + LN_EPS) * g + b


def _outproj_ln_body(mg_ref, wo_ref, h_ref, g_ref, b_ref, *rest, with_router):
    if with_router:
        wr_ref, o_ref, ob_ref, lg_ref = rest
    else:
        o_ref, ob_ref = rest
    y = ALPHA * h_ref[...] + _dot(mg_ref[...], wo_ref[...])
    out = _layer_norm(y, g_ref[...], b_ref[...])
    o_ref[...] = out
    ob_ref[...] = out.astype(BF16)
    if with_router:
        hi, lo = _split2(out)
        lg_ref[...] = _dot(hi, wr_ref[0]) + _dot(lo, wr_ref[0]) + _dot(hi, wr_ref[1])


def _outproj_ln(merged, w_out, h, g, b, w_router=None, *, tm):
    t, d = h.shape
    with_router = w_router is not None
    row = lambda i: (i, 0)
    fix = lambda i: (0, 0)
    in_specs = [pl.BlockSpec((tm, d), row), pl.BlockSpec((d, d), fix, pipeline_mode=pl.Buffered(1)),
                pl.BlockSpec((tm, d), row),
                pl.BlockSpec((1, d), fix), pl.BlockSpec((1, d), fix)]
    out_specs = [pl.BlockSpec((tm, d), row), pl.BlockSpec((tm, d), row)]
    out_shape = [jax.ShapeDtypeStruct((t, d), F32), jax.ShapeDtypeStruct((t, d), BF16)]
    args = [merged, w_out, h, g, b]
    if with_router:
        in_specs.append(pl.BlockSpec((2, d, LANES), lambda i: (0, 0, 0)))
        out_specs.append(pl.BlockSpec((tm, LANES), row))
        out_shape.append(jax.ShapeDtypeStruct((t, LANES), F32))
        args.append(w_router)
    return pl.pallas_call(
        functools.partial(_outproj_ln_body, with_router=with_router),
        grid=(t // tm,), in_specs=in_specs, out_specs=out_specs, out_shape=out_shape,
        compiler_params=_cparams(("parallel",)), name="outproj_ln1",
    )(*args)


def _ple_ln_body(hb_ref, h_ref, f_ref, p_ref, wg_ref, wp_ref, g_ref, b_ref, o_ref, ob_ref):
    ple = _sigmoid(_dot(hb_ref[...], wg_ref[...])) * _dot(p_ref[...], wp_ref[...])
    out = _layer_norm(ALPHA * h_ref[...] + f_ref[...] + ple, g_ref[...], b_ref[...])
    o_ref[...] = out
    ob_ref[...] = out.astype(BF16)


def _ple_ln(h_bf, h, f, p_bf, w_gate, w_proj, g, b, *, tm):
    t, d = h.shape
    row = lambda i: (i, 0)
    fix = lambda i: (0, 0)
    return pl.pallas_call(
        _ple_ln_body,
        grid=(t // tm,),
        in_specs=[pl.BlockSpec((tm, d), row), pl.BlockSpec((tm, d), row), pl.BlockSpec((tm, d), row),
                  pl.BlockSpec((tm, PLE_DIM), row), pl.BlockSpec((d, d), fix, pipeline_mode=pl.Buffered(1)),
                  pl.BlockSpec((PLE_DIM, d), fix), pl.BlockSpec((1, d), fix), pl.BlockSpec((1, d), fix)],
        out_specs=[pl.BlockSpec((tm, d), row), pl.BlockSpec((tm, d), row)],
        out_shape=[jax.ShapeDtypeStruct((t, d), F32), jax.ShapeDtypeStruct((t, d), BF16)],
        compiler_params=_cparams(("parallel",)), name="ple_ln2",
    )(h_bf, h, f, p_bf, w_gate, w_proj, g, b)


def _swiglu_tile(x, wg, wu, wd):
    g = _dot(x, wg)
    u = _dot(x, wu)
    return _dot((g * _sigmoid(g) * u).astype(BF16), wd)


def _ffn_body(x_ref, wg_ref, wu_ref, wd_ref, o_ref):
    j = pl.program_id(1)
    y = _swiglu_tile(x_ref[...], wg_ref[...], wu_ref[...], wd_ref[...])

    @pl.when(j == 0)
    def _():
        o_ref[...] = y

    @pl.when(j > 0)
    def _():
        o_ref[...] += y


def _ffn(x_bf, wg, wu, wd, *, tm, tf):
    t, d = x_bf.shape
    f = wg.shape[1]
    return pl.pallas_call(
        _ffn_body,
        grid=(t // tm, f // tf),
        in_specs=[pl.BlockSpec((tm, d), lambda i, j: (i, 0)),
                  pl.BlockSpec((d, tf), lambda i, j: (0, j)),
                  pl.BlockSpec((d, tf), lambda i, j: (0, j)),
                  pl.BlockSpec((tf, d), lambda i, j: (j, 0))],
        out_specs=pl.BlockSpec((tm, d), lambda i, j: (i, 0)),
        out_shape=jax.ShapeDtypeStruct((t, d), F32),
        compiler_params=_cparams(("parallel", "arbitrary")), name="ffn_swiglu",
    )(x_bf, wg, wu, wd)


def _moe_ffn_body(te_ref, nt_ref, x_ref, wg_ref, wu_ref, wd_ref, o_ref):
    i = pl.program_id(0)
    j = pl.program_id(1)
    active = i < nt_ref[0]

    @pl.when(active)
    def _():
        y = _swiglu_tile(x_ref[...], wg_ref[0], wu_ref[0], wd_ref[0])

        @pl.when(j == 0)
        def _():
            o_ref[...] = y

        @pl.when(j > 0)
        def _():
            o_ref[...] += y

    @pl.when(jnp.logical_not(active) & (j == 0))
    def _():
        o_ref[...] = jnp.zeros(o_ref.shape, F32)


def _moe_ffn(tile_expert, n_tiles, x_sorted, wg, wu, wd, *, tm, tf):
    r, d = x_sorted.shape
    f = wg.shape[2]
    nj = f // tf

    def jj(i, j, nt):
        return jnp.where(i < nt[0], j, nj - 1)

    return pl.pallas_call(
        _moe_ffn_body,
        grid_spec=pltpu.PrefetchScalarGridSpec(
            num_scalar_prefetch=2, grid=(r // tm, nj),
            in_specs=[pl.BlockSpec((tm, d), lambda i, j, te, nt: (i, 0)),
                      pl.BlockSpec((1, d, tf), lambda i, j, te, nt: (te[i], 0, jj(i, j, nt))),
                      pl.BlockSpec((1, d, tf), lambda i, j, te, nt: (te[i], 0, jj(i, j, nt))),
                      pl.BlockSpec((1, tf, d), lambda i, j, te, nt: (te[i], jj(i, j, nt), 0))],
            out_specs=pl.BlockSpec((tm, d), lambda i, j, te, nt: (i, 0))),
        out_shape=jax.ShapeDtypeStruct((r, d), F32),
        compiler_params=_cparams(("arbitrary", "arbitrary")), name="moe_grouped_ffn",
    )(tile_expert, n_tiles, x_sorted, wg, wu, wd)


def _row_copy(src_ref, src_row, dst_ref, dst_row, sem):
    return pltpu.make_async_copy(src_ref.at[pl.ds(src_row, 1)], dst_ref.at[pl.ds(dst_row, 1)], sem)


def _gather_rows_body(idx_ref, src_ref, o_ref, buf_ref, sem, *, tm):
    base = pl.program_id(0) * tm

    def start(g, c):
        for k in range(ROW_DMA_GROUP):
            r = g * ROW_DMA_GROUP + k
            _row_copy(src_ref, idx_ref[base + r], buf_ref, r, sem).start(priority=k % 2)
        return c

    def wait(r, c):
        _row_copy(src_ref, 0, buf_ref, r, sem).wait()
        return c

    lax.fori_loop(0, tm // ROW_DMA_GROUP, start, 0)
    lax.fori_loop(0, tm, wait, 0, unroll=8)
    o_ref[...] = buf_ref[...].astype(o_ref.dtype)


def _gather_rows(idx, src, n_rows, out_dtype, *, tm):
    d = src.shape[1]
    return pl.pallas_call(
        functools.partial(_gather_rows_body, tm=tm),
        grid_spec=pltpu.PrefetchScalarGridSpec(
            num_scalar_prefetch=1, grid=(n_rows // tm,),
            in_specs=[pl.BlockSpec(memory_space=pl.ANY)],
            out_specs=pl.BlockSpec((tm, d), lambda i, idx: (i, 0)),
            scratch_shapes=[pltpu.VMEM((tm, d), src.dtype), pltpu.SemaphoreType.DMA(())]),
        out_shape=jax.ShapeDtypeStruct((n_rows, d), out_dtype),
        compiler_params=_cparams(("arbitrary",)), name="moe_gather_rows",
    )(idx, src)


def _combine_body(idx_ref, src_ref, w_ref, o_ref, a_ref, b_ref, sem, *, tm):
    base = pl.program_id(0) * tm

    def start(g, c):
        for k in range(ROW_DMA_GROUP):
            r = g * ROW_DMA_GROUP + k
            _row_copy(src_ref, idx_ref[2 * (base + r)], a_ref, r, sem.at[0]).start(priority=0)
            _row_copy(src_ref, idx_ref[2 * (base + r) + 1], b_ref, r, sem.at[1]).start(priority=1)
        return c

    def wait(r, c):
        _row_copy(src_ref, 0, a_ref, r, sem.at[0]).wait()
        _row_copy(src_ref, 0, b_ref, r, sem.at[1]).wait()
        return c

    lax.fori_loop(0, tm // ROW_DMA_GROUP, start, 0)
    lax.fori_loop(0, tm, wait, 0, unroll=8)
    w = w_ref[...]
    o_ref[...] = a_ref[...] * w[:, 0:1] + b_ref[...] * w[:, 1:2]


def _combine_pairs(pos, y_sorted, pair_w, n_tokens, *, tm):
    d = y_sorted.shape[1]
    return pl.pallas_call(
        functools.partial(_combine_body, tm=tm),
        grid_spec=pltpu.PrefetchScalarGridSpec(
            num_scalar_prefetch=1, grid=(n_tokens // tm,),
            in_specs=[pl.BlockSpec(memory_space=pl.ANY),
                      pl.BlockSpec((tm, LANES), lambda i, idx: (i, 0))],
            out_specs=pl.BlockSpec((tm, d), lambda i, idx: (i, 0)),
            scratch_shapes=[pltpu.VMEM((tm, d), F32), pltpu.VMEM((tm, d), F32),
                            pltpu.SemaphoreType.DMA((2,))]),
        out_shape=jax.ShapeDtypeStruct((n_tokens, d), F32),
        compiler_params=_cparams(("arbitrary",)), name="moe_combine",
    )(pos, y_sorted, pair_w)


def _col_slices():
    out, off = {}, 0
    for name, width in IN_SPLITS:
        out[name] = (off, width)
        off += width
    return out


def _gather_cols(w, names, pad_to=None):
    cs = _col_slices()
    parts = []
    for n in names:
        col = w[:, cs[n][0]:cs[n][0] + cs[n][1]]
        parts.append(col * Q_FOLD[n] if n in Q_FOLD else col)
    width = sum(cs[n][1] for n in names)
    if pad_to is not None and pad_to > width:
        parts.append(jnp.zeros((w.shape[0], pad_to - width), w.dtype))
    return jnp.concatenate(parts, axis=1).astype(BF16)


def _selection_map_t(n_cmp_pad, n_sel):
    ci = np.arange(n_cmp_pad)[:, None] * NSA_CMP_D
    sj = np.arange(n_sel)[None, :] * NSA_SEL_L
    ov = np.clip(np.minimum(ci + NSA_CMP_L, sj + NSA_SEL_L) - np.maximum(ci, sj), 0, None)
    return np.ascontiguousarray((ov / NSA_CMP_D).astype(np.float32).T)


def _gate_expand_matrices():
    e = np.zeros((4, 3, LANES, LANES), np.float32)
    for j in range(4):
        for hh in range(2):
            for c in range(3):
                e[j, c, (2 * j + hh) * 3 + c, hh * HEAD_DIM:(hh + 1) * HEAD_DIM] = 1.0
    return e


def _split_bits(x, n):
    parts = []
    r = x
    for _ in range(n):
        hi = lax.bitcast_convert_type(
            lax.bitcast_convert_type(r, jnp.uint32) & jnp.uint32(0xFFFF0000), F32)
        parts.append(hi.astype(BF16))
        r = r - hi
    return parts


def _tile(n, pref):
    return pref if n % pref == 0 else n


def _vt_heads(x3, tkc):
    b, s, c = x3.shape
    nh, nc = c // HEAD_DIM, s // tkc
    v = x3.reshape(b, nc, tkc, nh, HEAD_DIM).transpose(0, 3, 1, 4, 2)
    ones = jnp.ones((b, nh, nc, 1, tkc), v.dtype)
    zeros = jnp.zeros((b, nh, nc, ATT_VROWS - HEAD_DIM - 1, tkc), v.dtype)
    return jnp.concatenate([v, ones, zeros], axis=3)


def _token_mixer(h, h_bf, layer, b, s, tabs, w_in, cmp_pe, cmp_w1, cmp_b1, cmp_w2, sinks, fox_bf,
                 diff_lambda, diff_gain, w_branch, w_out, ln_g, ln_b, w_router):
    t = b * s
    (cos64, sin64), (cos32, sin32) = tabs
    tm = _tile(t, 1024)
    r64 = _proj(h_bf, _gather_cols(w_in, SEG_ROPE64), BF16, tm, 512, rope=(cos64, sin64, HEAD_DIM // 2))
    r32 = _proj(h_bf, _gather_cols(w_in, SEG_ROPE32), BF16, tm, 512, rope=(cos32, sin32, DIFF_SUB // 2))
    plain = _proj(h_bf, _gather_cols(w_in, SEG_PLAIN), BF16, tm, 512)
    gates = _proj(h_bf, _gather_cols(w_in, SEG_GATES), BF16, tm, 512)
    small =_proj(h_bf, _gather_cols(w_in, SEG_SMALL, pad_to=LANES), F32, tm, LANES)
    r64_3, r32_3, plain_3, small_3 = (a.reshape(b, s, -1) for a in (r64, r32, plain, small))
    tkc = min(ATT_TKC, s // 2)
    tq = 2 * tkc

    ncp = s // NSA_CMP_D
    n_sel = s // NSA_SEL_L
    topn = min(NSA_TOPN, n_sel)

    def cmp_blocks(x2d):
        c = x2d.reshape(b, s, 2, HEAD_DIM).transpose(0, 2, 1, 3).reshape(b * 2, ncp, NSA_CMP_D * HEAD_DIM)
        nxt = jnp.concatenate([c[:, 1:], jnp.zeros_like(c[:, :1])], axis=1)
        return jnp.concatenate([c, nxt], axis=-1)

    xk = cmp_blocks(r64[:, 512:640])
    xv = cmp_blocks(plain[:, 1024:1152])
    cmp_kv = _nsa_compress(jnp.stack([xk, xv]), cmp_pe.reshape(2, 1, -1), cmp_w1.astype(BF16),
                           cmp_b1.reshape(2, 1, -1), cmp_w2.astype(BF16))
    kc = cmp_kv[0].astype(BF16).reshape(b, 2, ncp, HEAD_DIM)
    vc = cmp_kv[1].astype(BF16).reshape(b, 2, ncp, HEAD_DIM)
    kc4 = jnp.tile(kc, (1, 1, 1, 4))
    vct = vc.transpose(0, 1, 3, 2)
    selt = jnp.asarray(_selection_map_t(ncp, n_sel), BF16)
    o_cmp, mneg_t = _nsa_cmp_topk(r64_3, 0, kc4, vct, selt, tq=_tile(s, 256), topn=topn)
    vt_all = _vt_heads(plain_3[..., 1152:2560], tkc)
    o_win = _tattn(r64_3, 0, r64_3, 6, vt_all, 2, nmaps=2, qsel="gqa", vmap=(0, 0), mode="band",
                   window=NSA_WIN, fin="win", name="nsa_window_attention")
    e_mat = jnp.asarray(_gate_expand_matrices(), BF16)
    block_id = jax.nn.one_hot(jnp.arange(s) // NSA_SEL_L, HEAD_DIM, dtype=BF16)
    block_id = jnp.concatenate([block_id, block_id], axis=1)
    nsa_specs = [pl.BlockSpec((1, 1, tq, LANES), lambda bi, u, qi, *_: (bi, u // 2, qi, 0)),
                 pl.BlockSpec((s, LANES), lambda bi, u, qi, *_: (0, 0)),
                 pl.BlockSpec((1, 1, tq, LANES), lambda bi, u, qi, *_: (bi, u // 2, qi, u % 2)),
                 pl.BlockSpec((1, tq, LANES), lambda bi, u, qi, *_: (bi, qi, u)),
                 pl.BlockSpec((1, tq, LANES), lambda bi, u, qi, *_: (bi, qi, 0)),
                 pl.BlockSpec((1, 3, LANES, LANES), lambda bi, u, qi, *_: (u, 0, 0, 0))]
    o_a = _tattn(r64_3, 0, r64_3, 5, vt_all, 0, nmaps=2, qsel="gqa", vmap=(0, 0), mode="causal", fin="nsa",
                 bias="nsa", name="nsa_selected_attention", extras=(mneg_t, block_id, o_cmp, o_win, small_3, e_mat),
                 extra_specs=nsa_specs)

    o_b = _tattn(r64_3, 7, r64_3, 11, vt_all, 4, nmaps=2, qsel="gqa", vmap=(0, 0), mode="band",
                 window=SWA_WIN, fin="swa", name="swa_attention", scalars=sinks.astype(F32) * LOG2E)

    f_logit = (small[:, 24:32] + fox_bf[None, :]).reshape(b, s, 8).transpose(0, 2, 1)
    cum = _cum_log_forget(f_logit.reshape(b * 8, s // LANES, LANES)).reshape(b, 8, s)
    hi, mid, lo_piece = (piece[..., None] for piece in _split_bits(cum * LOG2E, 3))
    slot = (jnp.arange(LANES) % HEAD_DIM)[None, None, None, :]
    ck3 = jnp.where(slot == 0, hi, jnp.where(slot == 1, mid, jnp.where(slot == 2, lo_piece, jnp.zeros((), BF16))))
    o_c = _tattn(plain_3, 0, plain_3, 4, vt_all, 6, nmaps=2, qsel="pair", vmap=(0, 1), mode="causal",
                 fin="fox", bias="fox", name="fox_attention", extras=(ck3,),
                 extra_specs=[pl.BlockSpec((1, 2, s, LANES), lambda bi, u, qi, *_: (bi, u, 0, 0))])

    lam_init = 0.8 - 0.6 * math.exp(-0.3 * layer)
    lf = diff_lambda.astype(F32)
    lam = jnp.exp(jnp.sum(lf[0] * lf[1])) - jnp.exp(jnp.sum(lf[2] * lf[3])) + lam_init
    lam_arr = jnp.stack([lam, jnp.asarray(1.0 - lam_init, F32)]).astype(F32)
    gain_t = jnp.broadcast_to(jnp.tile(diff_gain.astype(F32), 2)[:, None], (LANES, tq))
    o_d = _tattn(r32_3, 0, r32_3, 4, vt_all, 14, nmaps=4, qsel="pair", vmap=(0, 0, 1, 1), mode="causal",
                 fin="diff", name="diff_attention", scalars=lam_arr, extras=(gain_t,),
                 extra_specs=[pl.BlockSpec((LANES, tq), lambda bi, u, qi, *_: (0, 0))])

    o_list = [o.reshape(t, BRANCH_W) for o in (o_a, o_b, o_c, o_d)]
    merged = _merge(o_list, gates, 0, w_branch.astype(BF16), tm=tm, tn=512)
    return _outproj_ln(merged, w_out.astype(BF16), h, ln_g.reshape(1, -1), ln_b.reshape(1, -1),
                       w_router, tm=_tile(t, 256))


def _moe_layer(h1, logits_pad, b_router, wg, wu, wd, *, tm, tf):
    t, d = h1.shape
    logits = logits_pad[:, :N_EXPERTS] + b_router.astype(F32)[None, :]
    top_v, top_i = lax.top_k(logits, TOP_K)
    top_w = jax.nn.softmax(top_v, axis=-1)
    flat_e = top_i.reshape(-1)
    onehot = jax.nn.one_hot(flat_e, N_EXPERTS, dtype=jnp.int32)
    rank = jnp.sum((jnp.cumsum(onehot, axis=0) - onehot) * onehot, axis=1)
    cnt = jnp.sum(onehot, axis=0)
    padded = ((cnt + tm - 1) // tm) * tm
    ends = jnp.cumsum(padded)
    starts = ends - padded
    pos = (starts[flat_e] + rank).astype(jnp.int32)
    n_rows = TOP_K * t + N_EXPERTS * tm
    row_token = jnp.zeros((n_rows,), jnp.int32).at[pos].set(jnp.arange(TOP_K * t, dtype=jnp.int32) // TOP_K)
    tile_start = jnp.arange(n_rows // tm, dtype=jnp.int32) * tm
    tile_expert = jnp.minimum(jnp.sum(tile_start[:, None] >= ends[None, :], axis=1), N_EXPERTS - 1)
    n_tiles = (ends[-1] // tm).astype(jnp.int32).reshape(1)
    x_sorted = _gather_rows(row_token, h1, n_rows, BF16, tm=512)
    y_sorted = _moe_ffn(tile_expert.astype(jnp.int32), n_tiles, x_sorted, wg, wu, wd, tm=tm, tf=tf)
    pair_w = jnp.pad(top_w.astype(F32), ((0, 0), (0, LANES - TOP_K)))
    return _combine_pairs(pos, y_sorted, pair_w, t, tm=_tile(t, 256))


def kernel(x, p, positions, w_in, nsa_cmp_pe, nsa_cmp_w1, nsa_cmp_b1, nsa_cmp_w2, swa_sinks, fox_bf,
           diff_lambda, diff_gain, w_branch, w_out, ln1_g, ln1_b, ffn_wg, ffn_wu, ffn_wd, moe_router,
           moe_router_b, moe_wg, moe_wu, moe_wd, ple_proj, ple_gate, ln2_g, ln2_b):
    b, s, d = x.shape
    t = b * s
    tabs = (_rope_tabs(positions, HEAD_DIM), _rope_tabs(positions, DIFF_SUB))
    h = x.reshape(t, d).astype(F32)
    h_bf = h.astype(BF16)
    for i in range(DEPTH):
        is_moe = i % 2 == 1
        w_router = None
        if is_moe:
            wr = jnp.zeros((d, LANES), F32).at[:, :N_EXPERTS].set(moe_router[i // 2].astype(F32))
            w_router = jnp.stack(_split_bits(wr, 2))
        res = _token_mixer(h, h_bf, i, b, s, tabs, w_in[i], nsa_cmp_pe[i], nsa_cmp_w1[i], nsa_cmp_b1[i],
                           nsa_cmp_w2[i], swa_sinks[i], fox_bf[i], diff_lambda[i], diff_gain[i],
                           w_branch[i], w_out[i], ln1_g[i], ln1_b[i], w_router)
        h1, h1_bf = res[0], res[1]
        if not is_moe:
            fpad = (-D_FF) % 512
            wg = jnp.pad(ffn_wg[i // 2].astype(BF16), ((0, 0), (0, fpad)))
            wu = jnp.pad(ffn_wu[i // 2].astype(BF16), ((0, 0), (0, fpad)))
            wd = jnp.pad(ffn_wd[i // 2].astype(BF16), ((0, fpad), (0, 0)))
            f = _ffn(h1_bf, wg, wu, wd, tm=_tile(t, 1024), tf=512)
        else:
            f = _moe_layer(h1, res[2], moe_router_b[i // 2], moe_wg[i // 2].astype(BF16),
                           moe_wu[i // 2].astype(BF16), moe_wd[i // 2].astype(BF16),
                           tm=_tile(t, 512), tf=1024)
        h, h_bf = _ple_ln(h1_bf, h1, f, p[i].reshape(t, PLE_DIM).astype(BF16), ple_gate[i].astype(BF16),
                          ple_proj[i].astype(BF16), ln2_g[i].reshape(1, -1), ln2_b[i].reshape(1, -1),
                          tm=_tile(t, 512))
    return h.reshape(b, s, d).astype(x.dtype)
```

```python
import functools
import math

import numpy as np
import jax
import jax.numpy as jnp
from jax import lax
from jax.experimental import pallas as pl
from jax.experimental.pallas import tpu as pltpu

F32 = jnp.float32
BF16 = jnp.bfloat16

D_MODEL = 2048
DEPTH = 2
HEAD_DIM = 64
ROPE_THETA = 10000.0
PLE_DIM = 256
LN_EPS = 1e-5
NSA_CMP_L = 32
NSA_CMP_D = 16
NSA_SEL_L = 64
NSA_TOPN = 16
NSA_WIN = 512
NSA_CMP_HIDDEN = 256
NSA_FORCE = 1e9
SWA_WIN = 128
DIFF_SUB = HEAD_DIM // 2
N_BRANCH = 4
BRANCH_W = 8 * HEAD_DIM
D_FF = 5504
N_EXPERTS = 8
TOP_K = 2
D_FF_EXPERT = 7168
ALPHA = (2.0 * DEPTH) ** 0.25

IN_SPLITS = (
    ("a_q", 512), ("a_kc", 128), ("a_vc", 128), ("a_ks", 128), ("a_vs", 128),
    ("a_kw", 128), ("a_vw", 128), ("a_g", 24),
    ("b_q", 512), ("b_k", 128), ("b_v", 128),
    ("c_q", 512), ("c_k", 512), ("c_v", 512), ("c_f", 8),
    ("d_q", 512), ("d_k", 512), ("d_v", 512),
    ("merge_gate", N_BRANCH * D_MODEL),
)
SEG_ROPE64 = ("a_q", "a_kc", "a_ks", "a_kw", "b_q", "b_k")
SEG_ROPE32 = ("d_q", "d_k")
SEG_PLAIN = ("c_q", "c_k", "a_vc", "a_vs", "a_vw", "b_v", "c_v", "d_v")
SEG_GATES = ("merge_gate",)
SEG_SMALL = ("a_g", "c_f")

LANES = 128
NEG = -1e30
LOG2E = math.log2(math.e)
VMEM_LIMIT = 56 * 1024 * 1024
ATT_TKC = 256
ATT_TQ = 2 * ATT_TKC
ATT_VROWS = 80

Q_FOLD = {"a_q": HEAD_DIM ** -0.5 * LOG2E, "b_q": HEAD_DIM ** -0.5 * LOG2E,
          "c_q": HEAD_DIM ** -0.5 * LOG2E, "d_q": DIFF_SUB ** -0.5 * LOG2E}


def _cparams(sem):
    return pltpu.CompilerParams(dimension_semantics=sem, vmem_limit_bytes=VMEM_LIMIT)


def _sigmoid(x):
    return 1.0 / (1.0 + jnp.exp(-x))


def _dot(a, b):
    return jnp.dot(a, b, preferred_element_type=F32)


def _dot_nt(a, b):
    return lax.dot_general(a, b, (((1,), (1,)), ((), ())), preferred_element_type=F32)


def _split2(x):
    hi = x.astype(BF16)
    lo = (x - hi.astype(F32)).astype(BF16)
    return hi, lo


def _split3(x):
    hi = x.astype(BF16)
    r = x - hi.astype(F32)
    mid = r.astype(BF16)
    lo = (r - mid.astype(F32)).astype(BF16)
    return hi, mid, lo


def _proj_body(x_ref, w_ref, *rest, rope_half):
    if rope_half:
        cos_ref, sin_ref, o_ref = rest
    else:
        (o_ref,) = rest
    acc = _dot(x_ref[...], w_ref[...])
    if not rope_half:
        o_ref[...] = acc.astype(o_ref.dtype)
        return
    cos = cos_ref[...]
    sin = sin_ref[...]
    lane = lax.broadcasted_iota(jnp.int32, cos.shape, 1)
    first = (lane % (2 * rope_half)) < rope_half
    for c in range(acc.shape[1] // LANES):
        a = acc[:, c * LANES:(c + 1) * LANES]
        rot = jnp.where(first, pltpu.roll(a, LANES - rope_half, 1), pltpu.roll(a, rope_half, 1))
        o_ref[:, c * LANES:(c + 1) * LANES] = (a * cos + rot * sin).astype(o_ref.dtype)


def _proj(x, w, out_dtype, tm, tn, rope=None):
    m, k = x.shape
    n = w.shape[1]
    in_specs = [pl.BlockSpec((tm, k), lambda i, j: (i, 0)),
                pl.BlockSpec((k, tn), lambda i, j: (0, j))]
    args = [x, w]
    rope_half = 0
    if rope is not None:
        cos_tab, sin_tab, rope_half = rope
        in_specs += [pl.BlockSpec((tm, LANES), lambda i, j: (i, 0)),
                     pl.BlockSpec((tm, LANES), lambda i, j: (i, 0))]
        args += [cos_tab, sin_tab]
    return pl.pallas_call(
        functools.partial(_proj_body, rope_half=rope_half),
        grid=(m // tm, n // tn),
        in_specs=in_specs,
        out_specs=pl.BlockSpec((tm, tn), lambda i, j: (i, j)),
        out_shape=jax.ShapeDtypeStruct((m, n), out_dtype),
        compiler_params=_cparams(("parallel", "parallel")),
        name="proj_rope" if rope_half else "proj",
    )(*args)


def _rope_tabs(positions, dim):
    inv = 1.0 / (ROPE_THETA ** (jnp.arange(0, dim, 2, dtype=F32) / dim))
    ang = positions.astype(F32).reshape(-1)[:, None] * inv
    c, s = jnp.cos(ang), jnp.sin(ang)
    reps = LANES // dim
    return (jnp.tile(jnp.concatenate([c, c], -1), (1, reps)),
            jnp.tile(jnp.concatenate([-s, s], -1), (1, reps)))


def _tattn_body(*refs, nmaps, qsel, vmap, tq, tkc, mode, window, bias, fin):
    refs = list(refs)
    sc_ref = refs.pop(0) if fin in ("diff", "swa") else None
    q_ref, k_ref, vt_ref = refs[:3]
    extras, o_ref = refs[3:-5], refs[-5]
    qm_ref, st_ref, m_ref, acc_ref = refs[-4:]
    bias_ref = None
    if bias is not None:
        bias_ref, extras = extras[0], extras[1:]
    u = pl.program_id(1)
    qi = pl.program_id(2)
    q0 = qi * tq
    last_chunk = k_ref.shape[1] // tkc - 1

    lane = lax.broadcasted_iota(jnp.int32, (tq, LANES), 1)
    klane = lax.broadcasted_iota(jnp.int32, (tkc, LANES), 1)
    q = q_ref[0].astype(F32)
    if qsel == "gqa":
        lo_lane = (u // 2) * HEAD_DIM
        q_rolled = pltpu.roll(q, HEAD_DIM, 1)
        in_group = (lane >= lo_lane) & (lane < lo_lane + HEAD_DIM)
        k_in_group = (klane >= lo_lane) & (klane < lo_lane + HEAD_DIM)
        fill = bias_ref[0, 0].astype(F32) if bias == "nsa" else 0.0
        for e in range(2):
            q_e = jnp.where(lo_lane == e * HEAD_DIM, q, q_rolled)
            qm_ref[e] = jnp.where(in_group, q_e, fill).T.astype(BF16)
    else:
        width = LANES // nmaps
        fill = jnp.where(lane % HEAD_DIM < 3, -1.0, 0.0) if bias == "fox" else 0.0
        for mp in range(nmaps):
            qm_ref[mp] = jnp.where((lane >= mp * width) & (lane < (mp + 1) * width), q, fill).T.astype(BF16)

    m_ref[...] = jnp.full(m_ref.shape, NEG, F32)
    acc_ref[...] = jnp.zeros(acc_ref.shape, F32)

    def qk(c, buf, q_lo=0, q_hi=tq):
        cc = jnp.clip(c, 0, last_chunk)
        off = pl.multiple_of(cc * tkc, tkc)
        kc = k_ref[0, pl.ds(off, tkc), :]
        if bias == "nsa":
            kc = jnp.where(k_in_group, kc, extras[0][pl.ds(off, tkc), :])
        for mp in range(nmaps):
            kc_mp = kc
            if bias == "fox":
                kc_mp = jnp.where((klane >= mp * HEAD_DIM) & (klane < (mp + 1) * HEAD_DIM), kc,
                                  bias_ref[0, mp, pl.ds(off, tkc), :])
            st_ref[buf, mp, :, q_lo:q_hi] = _dot(kc_mp, qm_ref[mp, :, q_lo:q_hi])

    def soft(c, buf, masked, q_lo=0, q_hi=tq):
        for mp in range(nmaps):
            st = st_ref[buf, mp, :, q_lo:q_hi]
            if masked:
                key = c * tkc + lax.broadcasted_iota(jnp.int32, st.shape, 0)
                t_pos = q0 + q_lo + lax.broadcasted_iota(jnp.int32, st.shape, 1)
                keep = key <= t_pos
                if mode == "band":
                    keep = keep & (t_pos - key < window) & (key >= 0)
                st = jnp.where(keep, st, NEG)
            m_old = m_ref[mp, :, q_lo:q_hi]
            m8 = jnp.max(st.reshape(tkc // 8, 8, q_hi - q_lo), axis=0)
            m_new = jnp.maximum(m_old, jnp.max(m8, axis=0, keepdims=True))
            p = jnp.exp2(st - m_new).astype(BF16)
            acc_ref[mp, :, q_lo:q_hi] = (jnp.exp2(m_old - m_new) * acc_ref[mp, :, q_lo:q_hi]
                                         + _dot(vt_ref[0, vmap[mp], jnp.maximum(c, 0)], p))
            m_ref[mp, :, q_lo:q_hi] = m_new

    def pair(pidx, carry, masked):
        c0 = 2 * pidx
        qk(c0 + 1, 1)
        soft(c0, 0, masked)
        qk(c0 + 2, 0)
        soft(c0 + 1, 1, masked)
        return carry

    if mode == "causal":
        qk(0, 0)
        lax.fori_loop(0, qi, functools.partial(pair, masked=False), 0)
        qk(2 * qi + 1, 1, q_lo=tkc)
        soft(2 * qi, 0, True, q_hi=tkc)
        soft(2 * qi, 0, False, q_lo=tkc)
        soft(2 * qi + 1, 1, True, q_lo=tkc)
    else:
        sched = []
        for d in range(-((window - 1 + tkc - 1) // tkc), 2):
            hi_lane = min(tq, d * tkc + tkc - 1 + window)
            sched.append((2 * qi + d, max(0, d * tkc), -(-hi_lane // LANES) * LANES))
        qk(sched[0][0], 0, sched[0][1], sched[0][2])
        for i, (c, lo_lane_q, hi_lane_q) in enumerate(sched):
            if i + 1 < len(sched):
                qk(sched[i + 1][0], (i + 1) % 2, sched[i + 1][1], sched[i + 1][2])
            soft(c, i % 2, True, lo_lane_q, hi_lane_q)

    def normed(mp):
        acc = acc_ref[mp]
        num, l_i = acc[:HEAD_DIM], acc[HEAD_DIM:HEAD_DIM + 1]
        if fin == "swa":
            m_i = m_ref[mp]
            sk = sc_ref[2 * u + mp]
            m_f = jnp.maximum(m_i, sk)
            corr = jnp.exp2(m_i - m_f)
            return num * (corr / (l_i * corr + jnp.exp2(sk - m_f)))
        return num * (1.0 / l_i)

    if fin == "diff":
        lam = sc_ref[0]
        halves = []
        for hh in range(2):
            o = normed(2 * hh) - lam * normed(2 * hh + 1)
            ms = jnp.mean(o * o, axis=0, keepdims=True)
            halves.append(o * lax.rsqrt(ms + LN_EPS))
        ot = jnp.concatenate(halves, axis=0) * extras[0][...] * sc_ref[1]
    else:
        ot = jnp.concatenate([normed(0), normed(1)], axis=0)
    o = ot.T
    if fin == "nsa":
        ocmp_ref, owin_ref, sm_ref, e_ref = extras[1:]
        hi, lo_part = _split2(sm_ref[0])
        gates = [_sigmoid(_dot(hi, e_ref[0, c]) + _dot(lo_part, e_ref[0, c])) for c in range(3)]
        o = gates[0] * ocmp_ref[0, 0] + gates[1] * o + gates[2] * owin_ref[0].astype(F32)
    o_ref[0] = o.astype(o_ref.dtype)


def _tattn(q_arr, q_blk, k_arr, k_blk, vt_all, v_head, *, nmaps, qsel, vmap, mode, fin, name, window=0,
           bias=None, scalars=None, extras=(), extra_specs=()):
    b, s, _ = q_arr.shape
    nc, vrows, tkc = vt_all.shape[2], vt_all.shape[3], vt_all.shape[4]
    tq = 2 * tkc
    nu = 4
    if qsel == "gqa":
        kspec = pl.BlockSpec((1, s, LANES), lambda bi, u, qi, *_: (bi, 0, k_blk))
        vspec = pl.BlockSpec((1, 1, nc, vrows, tkc), lambda bi, u, qi, *_: (bi, v_head + u // 2, 0, 0, 0))
    else:
        kspec = pl.BlockSpec((1, s, LANES), lambda bi, u, qi, *_: (bi, 0, k_blk + u))
        vspec = pl.BlockSpec((1, 2, nc, vrows, tkc), lambda bi, u, qi, *_: (bi, v_head // 2 + u, 0, 0, 0))
    in_specs = [pl.BlockSpec((1, tq, LANES), lambda bi, u, qi, *_: (bi, qi, q_blk + u)), kspec, vspec]
    in_specs += list(extra_specs)
    body = functools.partial(_tattn_body, nmaps=nmaps, qsel=qsel, vmap=vmap, tq=tq, tkc=tkc, mode=mode,
                             window=window, bias=bias, fin=fin)
    args = ([] if scalars is None else [scalars]) + [q_arr, k_arr, vt_all] + list(extras)
    return pl.pallas_call(
        body,
        grid_spec=pltpu.PrefetchScalarGridSpec(
            num_scalar_prefetch=0 if scalars is None else 1, grid=(b, nu, s // tq),
            in_specs=in_specs,
            out_specs=pl.BlockSpec((1, tq, LANES), lambda bi, u, qi, *_: (bi, qi, u)),
            scratch_shapes=[pltpu.VMEM((nmaps, LANES, tq), BF16), pltpu.VMEM((2, nmaps, tkc, tq), F32),
                            pltpu.VMEM((nmaps, 1, tq), F32), pltpu.VMEM((nmaps, vrows, tq), F32)]),
        out_shape=jax.ShapeDtypeStruct((b, s, nu * LANES), BF16),
        compiler_params=_cparams(("parallel", "parallel", "parallel")),
        name=name,
    )(*args)


def _gelu_tanh(x):
    return 0.5 * x * (1.0 + jnp.tanh(math.sqrt(2.0 / math.pi) * (x + 0.044715 * (x * x * x))))


def _compress_body(x_ref, pe_ref, w1_ref, b1_ref, w2_ref, o_ref):
    x = (x_ref[0, 0].astype(F32) + pe_ref[0]).astype(BF16)
    hid = _gelu_tanh(_dot(x, w1_ref[0]) + b1_ref[0])
    o_ref[0, 0] = _dot(hid.astype(BF16), w2_ref[0])


def _nsa_compress(x, pe, w1, b1, w2):
    _, nb, ncp, ld = x.shape
    hid = w1.shape[-1]
    return pl.pallas_call(
        _compress_body,
        grid=(2, nb),
        in_specs=[pl.BlockSpec((1, 1, ncp, ld), lambda t, i: (t, i, 0, 0)),
                  pl.BlockSpec((1, 1, ld), lambda t, i: (t, 0, 0)),
                  pl.BlockSpec((1, ld, hid), lambda t, i: (t, 0, 0)),
                  pl.BlockSpec((1, 1, hid), lambda t, i: (t, 0, 0)),
                  pl.BlockSpec((1, hid, HEAD_DIM), lambda t, i: (t, 0, 0))],
        out_specs=pl.BlockSpec((1, 1, ncp, HEAD_DIM), lambda t, i: (t, i, 0, 0)),
        out_shape=jax.ShapeDtypeStruct((2, nb, ncp, HEAD_DIM), F32),
        compiler_params=_cparams(("parallel", "parallel")),
        name="nsa_compress",
    )(x, pe, w1, b1, w2)


def _cmp_topk_body(q_ref, kc_ref, vct_ref, selt_ref, o_ref, mt_ref, *, tq, ncp, nsel, topn):
    qi = pl.program_id(2)
    q = q_ref[0]
    kc4 = kc_ref[0, 0]
    vct = vct_ref[0, 0]
    lane = lax.broadcasted_iota(jnp.int32, kc4.shape, 1)
    ci = lax.broadcasted_iota(jnp.int32, (ncp, tq), 0)
    tpos = qi * tq + lax.broadcasted_iota(jnp.int32, (ncp, tq), 1)
    cmask = ci * NSA_CMP_D + (NSA_CMP_L - 1) <= tpos
    psum = jnp.zeros((ncp, tq), F32)
    rows = []
    for a in range(4):
        kcm = jnp.where((lane >= a * HEAD_DIM) & (lane < (a + 1) * HEAD_DIM), kc4, jnp.zeros_like(kc4))
        st = jnp.where(cmask, _dot_nt(kcm, q), NEG)
        m = jnp.max(st, axis=0, keepdims=True)
        e = jnp.where(cmask, jnp.exp2(st - m), 0.0)
        l = jnp.sum(e, axis=0, keepdims=True)
        p = e * jnp.where(l > 0.0, 1.0 / l, 0.0)
        psum = psum + p
        rows.append(_dot(vct, p.astype(BF16)))
    hi, lo = _split2(psum)
    selt = selt_ref[...]
    imp = _dot(selt, hi) + _dot(selt, lo)
    blk = lax.broadcasted_iota(jnp.int32, (nsel, tq), 0)
    cur = (qi * tq + lax.broadcasted_iota(jnp.int32, (nsel, tq), 1)) // NSA_SEL_L
    forced = (blk == 0) | (blk == cur) | (blk == cur - 1)
    imp = jnp.where(forced, NSA_FORCE, jnp.where(blk > cur, -NSA_FORCE, imp))
    cnt = jnp.zeros((nsel, tq), jnp.int32)
    for jp in range(nsel):
        v = imp[jp:jp + 1, :]
        tie = jnp.where(blk > jp, 1, 0)
        cnt = cnt + jnp.where(v > imp, 1, jnp.where(v == imp, tie, 0))
    mneg = jnp.where(cnt < topn, 0.0, NEG)
    if nsel < HEAD_DIM:
        mneg = jnp.concatenate([mneg, jnp.zeros((HEAD_DIM - nsel, tq), F32)], axis=0)
    mt_ref[0, 0] = jnp.concatenate([mneg, mneg], axis=0).T.astype(mt_ref.dtype)
    o_ref[0, 0] = jnp.concatenate(rows, axis=0).T


def _nsa_cmp_topk(q_arr, q_off256, kc4, vct, selt, *, tq, topn):
    b, s, _ = q_arr.shape
    ncp = kc4.shape[2]
    nsel = selt.shape[0]
    body = functools.partial(_cmp_topk_body, tq=tq, ncp=ncp, nsel=nsel, topn=topn)
    return pl.pallas_call(
        body,
        grid=(b, 2, s // tq),
        in_specs=[pl.BlockSpec((1, tq, 2 * LANES), lambda bi, g, qi: (bi, qi, q_off256 + g)),
                  pl.BlockSpec((1, 1, ncp, 2 * LANES), lambda bi, g, qi: (bi, g, 0, 0)),
                  pl.BlockSpec((1, 1, HEAD_DIM, ncp), lambda bi, g, qi: (bi, g, 0, 0)),
                  pl.BlockSpec((nsel, ncp), lambda bi, g, qi: (0, 0))],
        out_specs=[pl.BlockSpec((1, 1, tq, 2 * LANES), lambda bi, g, qi: (bi, g, qi, 0)),
                   pl.BlockSpec((1, 1, tq, LANES), lambda bi, g, qi: (bi, g, qi, 0))],
        out_shape=[jax.ShapeDtypeStruct((b, 2, s, 2 * LANES), F32),
                   jax.ShapeDtypeStruct((b, 2, s, LANES), BF16)],
        compiler_params=_cparams(("parallel", "parallel", "parallel")),
        name="nsa_cmp_topk",
    )(q_arr, kc4, vct, selt)


def _cumgate_body(x_ref, o_ref):
    x = x_ref[0]
    r = x.shape[0]
    ls = jnp.minimum(x, 0.0) - jnp.log1p(jnp.exp(-jnp.abs(x)))
    i0 = lax.broadcasted_iota(jnp.int32, (LANES, LANES), 0)
    i1 = lax.broadcasted_iota(jnp.int32, (LANES, LANES), 1)
    upper = jnp.where(i0 <= i1, 1.0, 0.0).astype(BF16)
    ones = jnp.ones((LANES, LANES), BF16)
    r0 = lax.broadcasted_iota(jnp.int32, (r, r), 0)
    r1 = lax.broadcasted_iota(jnp.int32, (r, r), 1)
    strict = jnp.where(r1 < r0, 1.0, 0.0).astype(BF16)
    parts = _split3(ls)
    intra = sum(_dot(pp, upper) for pp in parts)
    rowtot = sum(_dot(pp, ones) for pp in parts)
    off = sum(_dot(strict, pp) for pp in _split3(rowtot))
    o_ref[0] = intra + off


def _cum_log_forget(x):
    n, r, _ = x.shape
    return pl.pallas_call(
        _cumgate_body,
        grid=(n,),
        in_specs=[pl.BlockSpec((1, r, LANES), lambda i: (i, 0, 0))],
        out_specs=pl.BlockSpec((1, r, LANES), lambda i: (i, 0, 0)),
        out_shape=jax.ShapeDtypeStruct((n, r, LANES), F32),
        compiler_params=_cparams(("parallel",)),
        name="cum_log_forget",
    )(x)


def _merge_body(oa_ref, ob_ref, oc_ref, od_ref, g0_ref, g1_ref, g2_ref, g3_ref, wb_ref, o_ref):
    acc = None
    for n, (o_r, g_r) in enumerate(((oa_ref, g0_ref), (ob_ref, g1_ref), (oc_ref, g2_ref), (od_ref, g3_ref))):
        term = _sigmoid(g_r[...].astype(F32)) * _dot(o_r[...], wb_ref[n])
        acc = term if acc is None else acc + term
    o_ref[...] = acc.astype(o_ref.dtype)


def _merge(o_list, plain, gate_off, wb, *, tm, tn):
    t = plain.shape[0]
    d = wb.shape[-1]
    nj = d // tn
    ospec = pl.BlockSpec((tm, BRANCH_W), lambda i, j: (i, 0))
    gspecs = [pl.BlockSpec((tm, tn), functools.partial(lambda i, j, n: (i, gate_off // tn + n * nj + j), n=n))
              for n in range(N_BRANCH)]
    return pl.pallas_call(
        _merge_body,
        grid=(t // tm, nj),
        in_specs=[ospec] * 4 + gspecs + [pl.BlockSpec((N_BRANCH, BRANCH_W, tn), lambda i, j: (0, 0, j))],
        out_specs=pl.BlockSpec((tm, tn), lambda i, j: (i, j)),
        out_shape=jax.ShapeDtypeStruct((t, d), BF16),
        compiler_params=_cparams(("parallel", "parallel")),
        name="gated_merge",
    )(*o_list, plain, plain, plain, plain, wb)


def _layer_norm(y, g, b):
    mu = jnp.mean(y, axis=-1, keepdims=True)
    yc = y - mu
    var = jnp.mean(yc * yc, axis=-1, keepdims=True)
    return yc * lax.rsqrt(var + LN_EPS) * g + b


def _outproj_ln_body(mg_ref, wo_ref, h_ref, g_ref, b_ref, *rest, with_router):
    if with_router:
        wr_ref, o_ref, ob_ref, lg_ref = rest
    else:
        o_ref, ob_ref = rest
    y = ALPHA * h_ref[...] + _dot(mg_ref[...], wo_ref[...])
    out = _layer_norm(y, g_ref[...], b_ref[...])
    o_ref[...] = out
    ob_ref[...] = out.astype(BF16)
    if with_router:
        hi, lo = _split2(out)
        lg_ref[...] = _dot(hi, wr_ref[0]) + _dot(lo, wr_ref[0]) + _dot(hi, wr_ref[1])


def _outproj_ln(merged, w_out, h, g, b, w_router=None, *, tm):
    t, d = h.shape
    with_router = w_router is not None
    row = lambda i: (i, 0)
    fix = lambda i: (0, 0)
    in_specs = [pl.BlockSpec((tm, d), row), pl.BlockSpec((d, d), fix, pipeline_mode=pl.Buffered(1)),
                pl.BlockSpec((tm, d), row),
                pl.BlockSpec((1, d), fix), pl.BlockSpec((1, d), fix)]
    out_specs = [pl.BlockSpec((tm, d), row), pl.BlockSpec((tm, d), row)]
    out_shape = [jax.ShapeDtypeStruct((t, d), F32), jax.ShapeDtypeStruct((t, d), BF16)]
    args = [merged, w_out, h, g, b]
    if with_router:
        in_specs.append(pl.BlockSpec((2, d, LANES), lambda i: (0, 0, 0)))
        out_specs.append(pl.BlockSpec((tm, LANES), row))
        out_shape.append(jax.ShapeDtypeStruct((t, LANES), F32))
        args.append(w_router)
    return pl.pallas_call(
        functools.partial(_outproj_ln_body, with_router=with_router),
        grid=(t // tm,), in_specs=in_specs, out_specs=out_specs, out_shape=out_shape,
        compiler_params=_cparams(("parallel",)), name="outproj_ln1",
    )(*args)


def _ple_ln_body(hb_ref, h_ref, f_ref, p_ref, wg_ref, wp_ref, g_ref, b_ref, o_ref, ob_ref):
    ple = _sigmoid(_dot(hb_ref[...], wg_ref[...])) * _dot(p_ref[...], wp_ref[...])
    out = _layer_norm(ALPHA * h_ref[...] + f_ref[...] + ple, g_ref[...], b_ref[...])
    o_ref[...] = out
    ob_ref[...] = out.astype(BF16)


def _ple_ln(h_bf, h, f, p_bf, w_gate, w_proj, g, b, *, tm):
    t, d = h.shape
    row = lambda i: (i, 0)
    fix = lambda i: (0, 0)
    return pl.pallas_call(
        _ple_ln_body,
        grid=(t // tm,),
        in_specs=[pl.BlockSpec((tm, d), row), pl.BlockSpec((tm, d), row), pl.BlockSpec((tm, d), row),
                  pl.BlockSpec((tm, PLE_DIM), row), pl.BlockSpec((d, d), fix, pipeline_mode=pl.Buffered(1)),
                  pl.BlockSpec((PLE_DIM, d), fix), pl.BlockSpec((1, d), fix), pl.BlockSpec((1, d), fix)],
        out_specs=[pl.BlockSpec((tm, d), row), pl.BlockSpec((tm, d), row)],
        out_shape=[jax.ShapeDtypeStruct((t, d), F32), jax.ShapeDtypeStruct((t, d), BF16)],
        compiler_params=_cparams(("parallel",)), name="ple_ln2",
    )(h_bf, h, f, p_bf, w_gate, w_proj, g, b)


def _swiglu_tile(x, wg, wu, wd):
    g = _dot(x, wg)
    u = _dot(x, wu)
    return _dot((g * _sigmoid(g) * u).astype(BF16), wd)


def _ffn_body(x_ref, wg_ref, wu_ref, wd_ref, o_ref):
    j = pl.program_id(1)
    y = _swiglu_tile(x_ref[...], wg_ref[...], wu_ref[...], wd_ref[...])

    @pl.when(j == 0)
    def _():
        o_ref[...] = y

    @pl.when(j > 0)
    def _():
        o_ref[...] += y


def _ffn(x_bf, wg, wu, wd, *, tm, tf):
    t, d = x_bf.shape
    f = wg.shape[1]
    return pl.pallas_call(
        _ffn_body,
        grid=(t // tm, f // tf),
        in_specs=[pl.BlockSpec((tm, d), lambda i, j: (i, 0)),
                  pl.BlockSpec((d, tf), lambda i, j: (0, j)),
                  pl.BlockSpec((d, tf), lambda i, j: (0, j)),
                  pl.BlockSpec((tf, d), lambda i, j: (j, 0))],
        out_specs=pl.BlockSpec((tm, d), lambda i, j: (i, 0)),
        out_shape=jax.ShapeDtypeStruct((t, d), F32),
        compiler_params=_cparams(("parallel", "arbitrary")), name="ffn_swiglu",
    )(x_bf, wg, wu, wd)


def _moe_ffn_body(te_ref, nt_ref, x_ref, wg_ref, wu_ref, wd_ref, o_ref):
    i = pl.program_id(0)
    j = pl.program_id(1)
    active = i < nt_ref[0]

    @pl.when(active)
    def _():
        y = _swiglu_tile(x_ref[...], wg_ref[0], wu_ref[0], wd_ref[0])

        @pl.when(j == 0)
        def _():
            o_ref[...] = y

        @pl.when(j > 0)
        def _():
            o_ref[...] += y

    @pl.when(jnp.logical_not(active) & (j == 0))
    def _():
        o_ref[...] = jnp.zeros(o_ref.shape, F32)


def _moe_ffn(tile_expert, n_tiles, x_sorted, wg, wu, wd, *, tm, tf):
    r, d = x_sorted.shape
    f = wg.shape[2]
    nj = f // tf

    def jj(i, j, nt):
        return jnp.where(i < nt[0], j, nj - 1)

    return pl.pallas_call(
        _moe_ffn_body,
        grid_spec=pltpu.PrefetchScalarGridSpec(
            num_scalar_prefetch=2, grid=(r // tm, nj),
            in_specs=[pl.BlockSpec((tm, d), lambda i, j, te, nt: (i, 0)),
                      pl.BlockSpec((1, d, tf), lambda i, j, te, nt: (te[i], 0, jj(i, j, nt))),
                      pl.BlockSpec((1, d, tf), lambda i, j, te, nt: (te[i], 0, jj(i, j, nt))),
                      pl.BlockSpec((1, tf, d), lambda i, j, te, nt: (te[i], jj(i, j, nt), 0))],
            out_specs=pl.BlockSpec((tm, d), lambda i, j, te, nt: (i, 0))),
        out_shape=jax.ShapeDtypeStruct((r, d), F32),
        compiler_params=_cparams(("arbitrary", "arbitrary")), name="moe_grouped_ffn",
    )(tile_expert, n_tiles, x_sorted, wg, wu, wd)


def _row_copy(src_ref, src_row, dst_ref, dst_row, sem):
    return pltpu.make_async_copy(src_ref.at[pl.ds(src_row, 1)], dst_ref.at[pl.ds(dst_row, 1)], sem)


def _gather_rows_body(idx_ref, src_ref, o_ref, buf_ref, sem, *, tm):
    base = pl.program_id(0) * tm

    half = tm // 2

    def start(r, c):
        _row_copy(src_ref, idx_ref[base + r], buf_ref, r, sem.at[0]).start()
        _row_copy(src_ref, idx_ref[base + half + r], buf_ref, half + r, sem.at[1]).start()
        return c

    def wait(r, c):
        _row_copy(src_ref, 0, buf_ref, r, sem.at[0]).wait()
        _row_copy(src_ref, 0, buf_ref, half + r, sem.at[1]).wait()
        return c

    lax.fori_loop(0, half, start, 0, unroll=8)
    lax.fori_loop(0, half, wait, 0, unroll=8)
    o_ref[...] = buf_ref[...].astype(o_ref.dtype)


def _gather_rows(idx, src, n_rows, out_dtype, *, tm):
    d = src.shape[1]
    return pl.pallas_call(
        functools.partial(_gather_rows_body, tm=tm),
        grid_spec=pltpu.PrefetchScalarGridSpec(
            num_scalar_prefetch=1, grid=(n_rows // tm,),
            in_specs=[pl.BlockSpec(memory_space=pl.ANY)],
            out_specs=pl.BlockSpec((tm, d), lambda i, idx: (i, 0)),
            scratch_shapes=[pltpu.VMEM((tm, d), src.dtype), pltpu.SemaphoreType.DMA((2,))]),
        out_shape=jax.ShapeDtypeStruct((n_rows, d), out_dtype),
        compiler_params=_cparams(("arbitrary",)), name="moe_gather_rows",
    )(idx, src)


def _combine_body(idx_ref, src_ref, w_ref, o_ref, a_ref, b_ref, sem, *, tm):
    base = pl.program_id(0) * tm

    def start(r, c):
        _row_copy(src_ref, idx_ref[2 * (base + r)], a_ref, r, sem.at[0]).start()
        _row_copy(src_ref, idx_ref[2 * (base + r) + 1], b_ref, r, sem.at[1]).start()
        return c

    def wait(r, c):
        _row_copy(src_ref, 0, a_ref, r, sem.at[0]).wait()
        _row_copy(src_ref, 0, b_ref, r, sem.at[1]).wait()
        return c

    lax.fori_loop(0, tm, start, 0, unroll=8)
    lax.fori_loop(0, tm, wait, 0, unroll=8)
    w = w_ref[...]
    o_ref[...] = a_ref[...] * w[:, 0:1] + b_ref[...] * w[:, 1:2]


def _combine_pairs(pos, y_sorted, pair_w, n_tokens, *, tm):
    d = y_sorted.shape[1]
    return pl.pallas_call(
        functools.partial(_combine_body, tm=tm),
        grid_spec=pltpu.PrefetchScalarGridSpec(
            num_scalar_prefetch=1, grid=(n_tokens // tm,),
            in_specs=[pl.BlockSpec(memory_space=pl.ANY),
                      pl.BlockSpec((tm, LANES), lambda i, idx: (i, 0))],
            out_specs=pl.BlockSpec((tm, d), lambda i, idx: (i, 0)),
            scratch_shapes=[pltpu.VMEM((tm, d), F32), pltpu.VMEM((tm, d), F32),
                            pltpu.SemaphoreType.DMA((2,))]),
        out_shape=jax.ShapeDtypeStruct((n_tokens, d), F32),
        compiler_params=_cparams(("arbitrary",)), name="moe_combine",
    )(pos, y_sorted, pair_w)


def _col_slices():
    out, off = {}, 0
    for name, width in IN_SPLITS:
        out[name] = (off, width)
        off += width
    return out


def _gather_cols(w, names, pad_to=None):
    cs = _col_slices()
    parts = []
    for n in names:
        col = w[:, cs[n][0]:cs[n][0] + cs[n][1]]
        parts.append(col * Q_FOLD[n] if n in Q_FOLD else col)
    width = sum(cs[n][1] for n in names)
    if pad_to is not None and pad_to > width:
        parts.append(jnp.zeros((w.shape[0], pad_to - width), w.dtype))
    return jnp.concatenate(parts, axis=1).astype(BF16)


def _selection_map_t(n_cmp_pad, n_sel):
    ci = np.arange(n_cmp_pad)[:, None] * NSA_CMP_D
    sj = np.arange(n_sel)[None, :] * NSA_SEL_L
    ov = np.clip(np.minimum(ci + NSA_CMP_L, sj + NSA_SEL_L) - np.maximum(ci, sj), 0, None)
    return np.ascontiguousarray((ov / NSA_CMP_D).astype(np.float32).T)


def _gate_expand_matrices():
    e = np.zeros((4, 3, LANES, LANES), np.float32)
    for j in range(4):
        for hh in range(2):
            for c in range(3):
                e[j, c, (2 * j + hh) * 3 + c, hh * HEAD_DIM:(hh + 1) * HEAD_DIM] = 1.0
    return e


def _split_bits(x, n):
    parts = []
    r = x
    for _ in range(n):
        hi = lax.bitcast_convert_type(
            lax.bitcast_convert_type(r, jnp.uint32) & jnp.uint32(0xFFFF0000), F32)
        parts.append(hi.astype(BF16))
        r = r - hi
    return parts


def _tile(n, pref):
    return pref if n % pref == 0 else n


def _vt_heads(x3, tkc):
    b, s, c = x3.shape
    nh, nc = c // HEAD_DIM, s // tkc
    v = x3.reshape(b, nc, tkc, nh, HEAD_DIM).transpose(0, 3, 1, 4, 2)
    ones = jnp.ones((b, nh, nc, 1, tkc), v.dtype)
    zeros = jnp.zeros((b, nh, nc, ATT_VROWS - HEAD_DIM - 1, tkc), v.dtype)
    return jnp.concatenate([v, ones, zeros], axis=3)


def _token_mixer(h, h_bf, layer, b, s, tabs, w_in, cmp_pe, cmp_w1, cmp_b1, cmp_w2, sinks, fox_bf,
                 diff_lambda, diff_gain, w_branch, w_out, ln_g, ln_b, w_router):
    t = b * s
    (cos64, sin64), (cos32, sin32) = tabs
    tm = _tile(t, 1024)
    r64 = _proj(h_bf, _gather_cols(w_in, SEG_ROPE64), BF16, tm, 512, rope=(cos64, sin64, HEAD_DIM // 2))
    r32 = _proj(h_bf, _gather_cols(w_in, SEG_ROPE32), BF16, tm, 512, rope=(cos32, sin32, DIFF_SUB // 2))
    plain = _proj(h_bf, _gather_cols(w_in, SEG_PLAIN), BF16, tm, 512)
    gates = _proj(h_bf, _gather_cols(w_in, SEG_GATES), BF16, tm, 1024)
    small =_proj(h_bf, _gather_cols(w_in, SEG_SMALL, pad_to=LANES), F32, tm, LANES)
    r64_3, r32_3, plain_3, small_3 = (a.reshape(b, s, -1) for a in (r64, r32, plain, small))
    tkc = min(ATT_TKC, s // 2)
    tq = 2 * tkc

    ncp = s // NSA_CMP_D
    n_sel = s // NSA_SEL_L
    topn = min(NSA_TOPN, n_sel)

    def cmp_blocks(x2d):
        c = x2d.reshape(b, s, 2, HEAD_DIM).transpose(0, 2, 1, 3).reshape(b * 2, ncp, NSA_CMP_D * HEAD_DIM)
        nxt = jnp.concatenate([c[:, 1:], jnp.zeros_like(c[:, :1])], axis=1)
        return jnp.concatenate([c, nxt], axis=-1)

    xk = cmp_blocks(r64[:, 512:640])
    xv = cmp_blocks(plain[:, 1024:1152])
    cmp_kv = _nsa_compress(jnp.stack([xk, xv]), cmp_pe.reshape(2, 1, -1), cmp_w1.astype(BF16),
                           cmp_b1.reshape(2, 1, -1), cmp_w2.astype(BF16))
    kc = cmp_kv[0].astype(BF16).reshape(b, 2, ncp, HEAD_DIM)
    vc = cmp_kv[1].astype(BF16).reshape(b, 2, ncp, HEAD_DIM)
    kc4 = jnp.tile(kc, (1, 1, 1, 4))
    vct = vc.transpose(0, 1, 3, 2)
    selt = jnp.asarray(_selection_map_t(ncp, n_sel), BF16)
    o_cmp, mneg_t = _nsa_cmp_topk(r64_3, 0, kc4, vct, selt, tq=_tile(s, 256), topn=topn)
    vt_all = _vt_heads(plain_3[..., 1152:2560], tkc)
    o_win = _tattn(r64_3, 0, r64_3, 6, vt_all, 2, nmaps=2, qsel="gqa", vmap=(0, 0), mode="band",
                   window=NSA_WIN, fin="win", name="nsa_window_attention")
    e_mat = jnp.asarray(_gate_expand_matrices(), BF16)
    block_id = jax.nn.one_hot(jnp.arange(s) // NSA_SEL_L, HEAD_DIM, dtype=BF16)
    block_id = jnp.concatenate([block_id, block_id], axis=1)
    nsa_specs = [pl.BlockSpec((1, 1, tq, LANES), lambda bi, u, qi, *_: (bi, u // 2, qi, 0)),
                 pl.BlockSpec((s, LANES), lambda bi, u, qi, *_: (0, 0)),
                 pl.BlockSpec((1, 1, tq, LANES), lambda bi, u, qi, *_: (bi, u // 2, qi, u % 2)),
                 pl.BlockSpec((1, tq, LANES), lambda bi, u, qi, *_: (bi, qi, u)),
                 pl.BlockSpec((1, tq, LANES), lambda bi, u, qi, *_: (bi, qi, 0)),
                 pl.BlockSpec((1, 3, LANES, LANES), lambda bi, u, qi, *_: (u, 0, 0, 0))]
    o_a = _tattn(r64_3, 0, r64_3, 5, vt_all, 0, nmaps=2, qsel="gqa", vmap=(0, 0), mode="causal", fin="nsa",
                 bias="nsa", name="nsa_selected_attention", extras=(mneg_t, block_id, o_cmp, o_win, small_3, e_mat),
                 extra_specs=nsa_specs)

    o_b = _tattn(r64_3, 7, r64_3, 11, vt_all, 4, nmaps=2, qsel="gqa", vmap=(0, 0), mode="band",
                 window=SWA_WIN, fin="swa", name="swa_attention", scalars=sinks.astype(F32) * LOG2E)

    f_logit = (small[:, 24:32] + fox_bf[None, :]).reshape(b, s, 8).transpose(0, 2, 1)
    cum = _cum_log_forget(f_logit.reshape(b * 8, s // LANES, LANES)).reshape(b, 8, s)
    hi, mid, lo_piece = (piece[..., None] for piece in _split_bits(cum * LOG2E, 3))
    slot = (jnp.arange(LANES) % HEAD_DIM)[None, None, None, :]
    ck3 = jnp.where(slot == 0, hi, jnp.where(slot == 1, mid, jnp.where(slot == 2, lo_piece, jnp.zeros((), BF16))))
    o_c = _tattn(plain_3, 0, plain_3, 4, vt_all, 6, nmaps=2, qsel="pair", vmap=(0, 1), mode="causal",
                 fin="fox", bias="fox", name="fox_attention", extras=(ck3,),
                 extra_specs=[pl.BlockSpec((1, 2, s, LANES), lambda bi, u, qi, *_: (bi, u, 0, 0))])

    lam_init = 0.8 - 0.6 * math.exp(-0.3 * layer)
    lf = diff_lambda.astype(F32)
    lam = jnp.exp(jnp.sum(lf[0] * lf[1])) - jnp.exp(jnp.sum(lf[2] * lf[3])) + lam_init
    lam_arr = jnp.stack([lam, jnp.asarray(1.0 - lam_init, F32)]).astype(F32)
    gain_t = jnp.broadcast_to(jnp.tile(diff_gain.astype(F32), 2)[:, None], (LANES, tq))
    o_d = _tattn(r32_3, 0, r32_3, 4, vt_all, 14, nmaps=4, qsel="pair", vmap=(0, 0, 1, 1), mode="causal",
                 fin="diff", name="diff_attention", scalars=lam_arr, extras=(gain_t,),
                 extra_specs=[pl.BlockSpec((LANES, tq), lambda bi, u, qi, *_: (0, 0))])

    o_list = [o.reshape(t, BRANCH_W) for o in (o_a, o_b, o_c, o_d)]
    merged = _merge(o_list, gates, 0, w_branch.astype(BF16), tm=tm, tn=512)
    return _outproj_ln(merged, w_out.astype(BF16), h, ln_g.reshape(1, -1), ln_b.reshape(1, -1),
                       w_router, tm=_tile(t, 256))


def _moe_layer(h1, logits_pad, b_router, wg, wu, wd, *, tm, tf):
    t, d = h1.shape
    logits = logits_pad[:, :N_EXPERTS] + b_router.astype(F32)[None, :]
    top_v, top_i = lax.top_k(logits, TOP_K)
    top_w = jax.nn.softmax(top_v, axis=-1)
    flat_e = top_i.reshape(-1)
    onehot = jax.nn.one_hot(flat_e, N_EXPERTS, dtype=jnp.int32)
    rank = jnp.sum((jnp.cumsum(onehot, axis=0) - onehot) * onehot, axis=1)
    cnt = jnp.sum(onehot, axis=0)
    padded = ((cnt + tm - 1) // tm) * tm
    ends = jnp.cumsum(padded)
    starts = ends - padded
    pos = (starts[flat_e] + rank).astype(jnp.int32)
    n_rows = TOP_K * t + N_EXPERTS * tm
    row_token = jnp.zeros((n_rows,), jnp.int32).at[pos].set(jnp.arange(TOP_K * t, dtype=jnp.int32) // TOP_K)
    tile_start = jnp.arange(n_rows // tm, dtype=jnp.int32) * tm
    tile_expert = jnp.minimum(jnp.sum(tile_start[:, None] >= ends[None, :], axis=1), N_EXPERTS - 1)
    n_tiles = (ends[-1] // tm).astype(jnp.int32).reshape(1)
    x_sorted = _gather_rows(row_token, h1, n_rows, BF16, tm=512)
    y_sorted = _moe_ffn(tile_expert.astype(jnp.int32), n_tiles, x_sorted, wg, wu, wd, tm=tm, tf=tf)
    pair_w = jnp.pad(top_w.astype(F32), ((0, 0), (0, LANES - TOP_K)))
    return _combine_pairs(pos, y_sorted, pair_w, t, tm=_tile(t, 256))


def kernel(x, p, positions, w_in, nsa_cmp_pe, nsa_cmp_w1, nsa_cmp_b1, nsa_cmp_w2, swa_sinks, fox_bf,
           diff_lambda, diff_gain, w_branch, w_out, ln1_g, ln1_b, ffn_wg, ffn_wu, ffn_wd, moe_router,
           moe_router_b, moe_wg, moe_wu, moe_wd, ple_proj, ple_gate, ln2_g, ln2_b):
    b, s, d = x.shape
    t = b * s
    tabs = (_rope_tabs(positions, HEAD_DIM), _rope_tabs(positions, DIFF_SUB))
    h = x.reshape(t, d).astype(F32)
    h_bf = h.astype(BF16)
    for i in range(DEPTH):
        is_moe = i % 2 == 1
        w_router = None
        if is_moe:
            wr = jnp.zeros((d, LANES), F32).at[:, :N_EXPERTS].set(moe_router[i // 2].astype(F32))
            w_router = jnp.stack(_split_bits(wr, 2))
        res = _token_mixer(h, h_bf, i, b, s, tabs, w_in[i], nsa_cmp_pe[i], nsa_cmp_w1[i], nsa_cmp_b1[i],
                           nsa_cmp_w2[i], swa_sinks[i], fox_bf[i], diff_lambda[i], diff_gain[i],
                           w_branch[i], w_out[i], ln1_g[i], ln1_b[i], w_router)
        h1, h1_bf = res[0], res[1]
        if not is_moe:
            fpad = (-D_FF) % 512
            wg = jnp.pad(ffn_wg[i // 2].astype(BF16), ((0, 0), (0, fpad)))
            wu = jnp.pad(ffn_wu[i // 2].astype(BF16), ((0, 0), (0, fpad)))
            wd = jnp.pad(ffn_wd[i // 2].astype(BF16), ((0, fpad), (0, 0)))
            f = _ffn(h1_bf, wg, wu, wd, tm=_tile(t, 1024), tf=512)
        else:
            f = _moe_layer(h1, res[2], moe_router_b[i // 2], moe_wg[i // 2].astype(BF16),
                           moe_wu[i // 2].astype(BF16), moe_wd[i // 2].astype(BF16),
                           tm=_tile(t, 512), tf=1024)
        h, h_bf = _ple_ln(h1_bf, h1, f, p[i].reshape(t, PLE_DIM).astype(BF16), ple_gate[i].astype(BF16),
                          ple_proj[i].astype(BF16), ln2_g[i].reshape(1, -1), ln2_b[i].reshape(1, -1),
                          tm=_tile(t, 512))
    return h.reshape(b, s, d).astype(x.dtype)
```

```python
import functools
import math

import numpy as np
import jax
import jax.numpy as jnp
from jax import lax
from jax.experimental import pallas as pl
from jax.experimental.pallas import tpu as pltpu

F32 = jnp.float32
BF16 = jnp.bfloat16

D_MODEL = 2048
DEPTH = 2
HEAD_DIM = 64
ROPE_THETA = 10000.0
PLE_DIM = 256
LN_EPS = 1e-5
NSA_CMP_L = 32
NSA_CMP_D = 16
NSA_SEL_L = 64
NSA_TOPN = 16
NSA_WIN = 512
NSA_CMP_HIDDEN = 256
NSA_FORCE = 1e9
SWA_WIN = 128
DIFF_SUB = HEAD_DIM // 2
N_BRANCH = 4
BRANCH_W = 8 * HEAD_DIM
D_FF = 5504
N_EXPERTS = 8
TOP_K = 2
D_FF_EXPERT = 7168
ALPHA = (2.0 * DEPTH) ** 0.25

IN_SPLITS = (
    ("a_q", 512), ("a_kc", 128), ("a_vc", 128), ("a_ks", 128), ("a_vs", 128),
    ("a_kw", 128), ("a_vw", 128), ("a_g", 24),
    ("b_q", 512), ("b_k", 128), ("b_v", 128),
    ("c_q", 512), ("c_k", 512), ("c_v", 512), ("c_f", 8),
    ("d_q", 512), ("d_k", 512), ("d_v", 512),
    ("merge_gate", N_BRANCH * D_MODEL),
)
SEG_ROPE64 = ("a_q", "a_kc", "a_ks", "a_kw", "b_q", "b_k")
SEG_ROPE32 = ("d_q", "d_k")
SEG_PLAIN = ("c_q", "c_k", "a_vc", "a_vs", "a_vw", "b_v", "c_v", "d_v")
SEG_GATES = ("merge_gate",)
SEG_SMALL = ("a_g", "c_f")

LANES = 128
NEG = -1e30
LOG2E = math.log2(math.e)
VMEM_LIMIT = 56 * 1024 * 1024
ATT_TKC = 256
ATT_TQ = 2 * ATT_TKC
ATT_VROWS = 80

Q_FOLD = {"a_q": HEAD_DIM ** -0.5 * LOG2E, "b_q": HEAD_DIM ** -0.5 * LOG2E,
          "c_q": HEAD_DIM ** -0.5 * LOG2E, "d_q": DIFF_SUB ** -0.5 * LOG2E}


def _cparams(sem):
    return pltpu.CompilerParams(dimension_semantics=sem, vmem_limit_bytes=VMEM_LIMIT)


def _sigmoid(x):
    return 1.0 / (1.0 + jnp.exp(-x))


def _dot(a, b):
    return jnp.dot(a, b, preferred_element_type=F32)


def _dot_nt(a, b):
    return lax.dot_general(a, b, (((1,), (1,)), ((), ())), preferred_element_type=F32)


def _split2(x):
    hi = x.astype(BF16)
    lo = (x - hi.astype(F32)).astype(BF16)
    return hi, lo


def _split3(x):
    hi = x.astype(BF16)
    r = x - hi.astype(F32)
    mid = r.astype(BF16)
    lo = (r - mid.astype(F32)).astype(BF16)
    return hi, mid, lo


def _proj_body(x_ref, w_ref, *rest, rope_half):
    if rope_half:
        cos_ref, sin_ref, o_ref = rest
    else:
        (o_ref,) = rest
    acc = _dot(x_ref[...], w_ref[...])
    if not rope_half:
        o_ref[...] = acc.astype(o_ref.dtype)
        return
    cos = cos_ref[...]
    sin = sin_ref[...]
    lane = lax.broadcasted_iota(jnp.int32, cos.shape, 1)
    first = (lane % (2 * rope_half)) < rope_half
    for c in range(acc.shape[1] // LANES):
        a = acc[:, c * LANES:(c + 1) * LANES]
        rot = jnp.where(first, pltpu.roll(a, LANES - rope_half, 1), pltpu.roll(a, rope_half, 1))
        o_ref[:, c * LANES:(c + 1) * LANES] = (a * cos + rot * sin).astype(o_ref.dtype)


def _proj(x, w, out_dtype, tm, tn, rope=None):
    m, k = x.shape
    n = w.shape[1]
    in_specs = [pl.BlockSpec((tm, k), lambda i, j: (i, 0)),
                pl.BlockSpec((k, tn), lambda i, j: (0, j))]
    args = [x, w]
    rope_half = 0
    if rope is not None:
        cos_tab, sin_tab, rope_half = rope
        in_specs += [pl.BlockSpec((tm, LANES), lambda i, j: (i, 0)),
                     pl.BlockSpec((tm, LANES), lambda i, j: (i, 0))]
        args += [cos_tab, sin_tab]
    return pl.pallas_call(
        functools.partial(_proj_body, rope_half=rope_half),
        grid=(m // tm, n // tn),
        in_specs=in_specs,
        out_specs=pl.BlockSpec((tm, tn), lambda i, j: (i, j)),
        out_shape=jax.ShapeDtypeStruct((m, n), out_dtype),
        compiler_params=_cparams(("parallel", "parallel")),
        name="proj_rope" if rope_half else "proj",
    )(*args)


def _rope_tabs(positions, dim):
    inv = 1.0 / (ROPE_THETA ** (jnp.arange(0, dim, 2, dtype=F32) / dim))
    ang = positions.astype(F32).reshape(-1)[:, None] * inv
    c, s = jnp.cos(ang), jnp.sin(ang)
    reps = LANES // dim
    return (jnp.tile(jnp.concatenate([c, c], -1), (1, reps)),
            jnp.tile(jnp.concatenate([-s, s], -1), (1, reps)))


def _tattn_body(*refs, nmaps, qsel, vmap, tq, tkc, mode, window, bias, fin):
    refs = list(refs)
    sc_ref = refs.pop(0) if fin in ("diff", "swa") else None
    q_ref, k_ref, vt_ref = refs[:3]
    extras, o_ref = refs[3:-5], refs[-5]
    qm_ref, st_ref, m_ref, acc_ref = refs[-4:]
    bias_ref = None
    if bias is not None:
        bias_ref, extras = extras[0], extras[1:]
    u = pl.program_id(1)
    qi = pl.program_id(2)
    q0 = qi * tq
    last_chunk = k_ref.shape[1] // tkc - 1

    lane = lax.broadcasted_iota(jnp.int32, (tq, LANES), 1)
    klane = lax.broadcasted_iota(jnp.int32, (tkc, LANES), 1)
    q = q_ref[0].astype(F32)
    if qsel == "gqa":
        lo_lane = (u // 2) * HEAD_DIM
        q_rolled = pltpu.roll(q, HEAD_DIM, 1)
        in_group = (lane >= lo_lane) & (lane < lo_lane + HEAD_DIM)
        k_in_group = (klane >= lo_lane) & (klane < lo_lane + HEAD_DIM)
        fill = bias_ref[0, 0].astype(F32) if bias == "nsa" else 0.0
        for e in range(2):
            q_e = jnp.where(lo_lane == e * HEAD_DIM, q, q_rolled)
            qm_ref[e] = jnp.where(in_group, q_e, fill).T.astype(BF16)
    else:
        width = LANES // nmaps
        fill = jnp.where(lane % HEAD_DIM < 3, -1.0, 0.0) if bias == "fox" else 0.0
        for mp in range(nmaps):
            qm_ref[mp] = jnp.where((lane >= mp * width) & (lane < (mp + 1) * width), q, fill).T.astype(BF16)

    m_ref[...] = jnp.full(m_ref.shape, NEG, F32)
    acc_ref[...] = jnp.zeros(acc_ref.shape, F32)

    def qk(c, buf, q_lo=0, q_hi=tq):
        cc = jnp.clip(c, 0, last_chunk)
        off = pl.multiple_of(cc * tkc, tkc)
        kc = k_ref[0, pl.ds(off, tkc), :]
        if bias == "nsa":
            kc = jnp.where(k_in_group, kc, extras[0][pl.ds(off, tkc), :])
        for mp in range(nmaps):
            kc_mp = kc
            if bias == "fox":
                kc_mp = jnp.where((klane >= mp * HEAD_DIM) & (klane < (mp + 1) * HEAD_DIM), kc,
                                  bias_ref[0, mp, pl.ds(off, tkc), :])
            st_ref[buf, mp, :, q_lo:q_hi] = _dot(kc_mp, qm_ref[mp, :, q_lo:q_hi])

    def soft(c, buf, masked, q_lo=0, q_hi=tq):
        for mp in range(nmaps):
            st = st_ref[buf, mp, :, q_lo:q_hi]
            if masked:
                key = c * tkc + lax.broadcasted_iota(jnp.int32, st.shape, 0)
                t_pos = q0 + q_lo + lax.broadcasted_iota(jnp.int32, st.shape, 1)
                keep = key <= t_pos
                if mode == "band":
                    keep = keep & (t_pos - key < window) & (key >= 0)
                st = jnp.where(keep, st, NEG)
            m_old = m_ref[mp, :, q_lo:q_hi]
            m8 = jnp.max(st.reshape(tkc // 8, 8, q_hi - q_lo), axis=0)
            m_new = jnp.maximum(m_old, jnp.max(m8, axis=0, keepdims=True))
            p = jnp.exp2(st - m_new).astype(BF16)
            acc_ref[mp, :, q_lo:q_hi] = (jnp.exp2(m_old - m_new) * acc_ref[mp, :, q_lo:q_hi]
                                         + _dot(vt_ref[0, vmap[mp], jnp.maximum(c, 0)], p))
            m_ref[mp, :, q_lo:q_hi] = m_new

    def pair(pidx, carry, masked):
        c0 = 2 * pidx
        qk(c0 + 1, 1)
        soft(c0, 0, masked)
        qk(c0 + 2, 0)
        soft(c0 + 1, 1, masked)
        return carry

    if mode == "causal":
        qk(0, 0)
        lax.fori_loop(0, qi, functools.partial(pair, masked=False), 0)
        qk(2 * qi + 1, 1, q_lo=tkc)
        soft(2 * qi, 0, True, q_hi=tkc)
        soft(2 * qi, 0, False, q_lo=tkc)
        soft(2 * qi + 1, 1, True, q_lo=tkc)
    else:
        sched = []
        for d in range(-((window - 1 + tkc - 1) // tkc), 2):
            hi_lane = min(tq, d * tkc + tkc - 1 + window)
            sched.append((2 * qi + d, max(0, d * tkc), -(-hi_lane // LANES) * LANES))
        qk(sched[0][0], 0, sched[0][1], sched[0][2])
        for i, (c, lo_lane_q, hi_lane_q) in enumerate(sched):
            if i + 1 < len(sched):
                qk(sched[i + 1][0], (i + 1) % 2, sched[i + 1][1], sched[i + 1][2])
            soft(c, i % 2, True, lo_lane_q, hi_lane_q)

    def normed(mp):
        acc = acc_ref[mp]
        num, l_i = acc[:HEAD_DIM], acc[HEAD_DIM:HEAD_DIM + 1]
        if fin == "swa":
            m_i = m_ref[mp]
            sk = sc_ref[2 * u + mp]
            m_f = jnp.maximum(m_i, sk)
            corr = jnp.exp2(m_i - m_f)
            return num * (corr / (l_i * corr + jnp.exp2(sk - m_f)))
        return num * (1.0 / l_i)

    if fin == "diff":
        lam = sc_ref[0]
        halves = []
        for hh in range(2):
            o = normed(2 * hh) - lam * normed(2 * hh + 1)
            ms = jnp.mean(o * o, axis=0, keepdims=True)
            halves.append(o * lax.rsqrt(ms + LN_EPS))
        ot = jnp.concatenate(halves, axis=0) * extras[0][...] * sc_ref[1]
    else:
        ot = jnp.concatenate([normed(0), normed(1)], axis=0)
    o = ot.T
    if fin == "nsa":
        ocmp_ref, owin_ref, sm_ref, e_ref = extras[1:]
        hi, lo_part = _split2(sm_ref[0])
        gates = [_sigmoid(_dot(hi, e_ref[0, c]) + _dot(lo_part, e_ref[0, c])) for c in range(3)]
        o = gates[0] * ocmp_ref[0, 0] + gates[1] * o + gates[2] * owin_ref[0].astype(F32)
    o_ref[0] = o.astype(o_ref.dtype)


def _tattn(q_arr, q_blk, k_arr, k_blk, vt_all, v_head, *, nmaps, qsel, vmap, mode, fin, name, window=0,
           bias=None, scalars=None, extras=(), extra_specs=()):
    b, s, _ = q_arr.shape
    nc, vrows, tkc = vt_all.shape[2], vt_all.shape[3], vt_all.shape[4]
    tq = 2 * tkc
    nu = 4
    if qsel == "gqa":
        kspec = pl.BlockSpec((1, s, LANES), lambda bi, u, qi, *_: (bi, 0, k_blk))
        vspec = pl.BlockSpec((1, 1, nc, vrows, tkc), lambda bi, u, qi, *_: (bi, v_head + u // 2, 0, 0, 0))
    else:
        kspec = pl.BlockSpec((1, s, LANES), lambda bi, u, qi, *_: (bi, 0, k_blk + u))
        vspec = pl.BlockSpec((1, 2, nc, vrows, tkc), lambda bi, u, qi, *_: (bi, v_head // 2 + u, 0, 0, 0))
    in_specs = [pl.BlockSpec((1, tq, LANES), lambda bi, u, qi, *_: (bi, qi, q_blk + u)), kspec, vspec]
    in_specs += list(extra_specs)
    body = functools.partial(_tattn_body, nmaps=nmaps, qsel=qsel, vmap=vmap, tq=tq, tkc=tkc, mode=mode,
                             window=window, bias=bias, fin=fin)
    args = ([] if scalars is None else [scalars]) + [q_arr, k_arr, vt_all] + list(extras)
    return pl.pallas_call(
        body,
        grid_spec=pltpu.PrefetchScalarGridSpec(
            num_scalar_prefetch=0 if scalars is None else 1, grid=(b, nu, s // tq),
            in_specs=in_specs,
            out_specs=pl.BlockSpec((1, tq, LANES), lambda bi, u, qi, *_: (bi, qi, u)),
            scratch_shapes=[pltpu.VMEM((nmaps, LANES, tq), BF16), pltpu.VMEM((2, nmaps, tkc, tq), F32),
                            pltpu.VMEM((nmaps, 1, tq), F32), pltpu.VMEM((nmaps, vrows, tq), F32)]),
        out_shape=jax.ShapeDtypeStruct((b, s, nu * LANES), BF16),
        compiler_params=_cparams(("parallel", "parallel", "parallel")),
        name=name,
    )(*args)


def _gelu_tanh(x):
    return 0.5 * x * (1.0 + jnp.tanh(math.sqrt(2.0 / math.pi) * (x + 0.044715 * (x * x * x))))


def _compress_body(x_ref, pe_ref, w1_ref, b1_ref, w2_ref, o_ref):
    x = (x_ref[0, 0].astype(F32) + pe_ref[0]).astype(BF16)
    hid = _gelu_tanh(_dot(x, w1_ref[0]) + b1_ref[0])
    o_ref[0, 0] = _dot(hid.astype(BF16), w2_ref[0])


def _nsa_compress(x, pe, w1, b1, w2):
    _, nb, ncp, ld = x.shape
    hid = w1.shape[-1]
    return pl.pallas_call(
        _compress_body,
        grid=(2, nb),
        in_specs=[pl.BlockSpec((1, 1, ncp, ld), lambda t, i: (t, i, 0, 0)),
                  pl.BlockSpec((1, 1, ld), lambda t, i: (t, 0, 0)),
                  pl.BlockSpec((1, ld, hid), lambda t, i: (t, 0, 0)),
                  pl.BlockSpec((1, 1, hid), lambda t, i: (t, 0, 0)),
                  pl.BlockSpec((1, hid, HEAD_DIM), lambda t, i: (t, 0, 0))],
        out_specs=pl.BlockSpec((1, 1, ncp, HEAD_DIM), lambda t, i: (t, i, 0, 0)),
        out_shape=jax.ShapeDtypeStruct((2, nb, ncp, HEAD_DIM), F32),
        compiler_params=_cparams(("parallel", "parallel")),
        name="nsa_compress",
    )(x, pe, w1, b1, w2)


def _cmp_topk_body(q_ref, kc_ref, vct_ref, selt_ref, o_ref, mt_ref, *, tq, ncp, nsel, topn):
    qi = pl.program_id(2)
    q = q_ref[0]
    kc4 = kc_ref[0, 0]
    vct = vct_ref[0, 0]
    lane = lax.broadcasted_iota(jnp.int32, kc4.shape, 1)
    ci = lax.broadcasted_iota(jnp.int32, (ncp, tq), 0)
    tpos = qi * tq + lax.broadcasted_iota(jnp.int32, (ncp, tq), 1)
    cmask = ci * NSA_CMP_D + (NSA_CMP_L - 1) <= tpos
    psum = jnp.zeros((ncp, tq), F32)
    rows = []
    for a in range(4):
        kcm = jnp.where((lane >= a * HEAD_DIM) & (lane < (a + 1) * HEAD_DIM), kc4, jnp.zeros_like(kc4))
        st = jnp.where(cmask, _dot_nt(kcm, q), NEG)
        m = jnp.max(st, axis=0, keepdims=True)
        e = jnp.where(cmask, jnp.exp2(st - m), 0.0)
        l = jnp.sum(e, axis=0, keepdims=True)
        p = e * jnp.where(l > 0.0, 1.0 / l, 0.0)
        psum = psum + p
        rows.append(_dot(vct, p.astype(BF16)))
    hi, lo = _split2(psum)
    selt = selt_ref[...]
    imp = _dot(selt, hi) + _dot(selt, lo)
    blk = lax.broadcasted_iota(jnp.int32, (nsel, tq), 0)
    cur = (qi * tq + lax.broadcasted_iota(jnp.int32, (nsel, tq), 1)) // NSA_SEL_L
    forced = (blk == 0) | (blk == cur) | (blk == cur - 1)
    imp = jnp.where(forced, NSA_FORCE, jnp.where(blk > cur, -NSA_FORCE, imp))
    cnt = jnp.zeros((nsel, tq), jnp.int32)
    for jp in range(nsel):
        v = imp[jp:jp + 1, :]
        tie = jnp.where(blk > jp, 1, 0)
        cnt = cnt + jnp.where(v > imp, 1, jnp.where(v == imp, tie, 0))
    mneg = jnp.where(cnt < topn, 0.0, NEG)
    if nsel < HEAD_DIM:
        mneg = jnp.concatenate([mneg, jnp.zeros((HEAD_DIM - nsel, tq), F32)], axis=0)
    mt_ref[0, 0] = jnp.concatenate([mneg, mneg], axis=0).T.astype(mt_ref.dtype)
    o_ref[0, 0] = jnp.concatenate(rows, axis=0).T


def _nsa_cmp_topk(q_arr, q_off256, kc4, vct, selt, *, tq, topn):
    b, s, _ = q_arr.shape
    ncp = kc4.shape[2]
    nsel = selt.shape[0]
    body = functools.partial(_cmp_topk_body, tq=tq, ncp=ncp, nsel=nsel, topn=topn)
    return pl.pallas_call(
        body,
        grid=(b, 2, s // tq),
        in_specs=[pl.BlockSpec((1, tq, 2 * LANES), lambda bi, g, qi: (bi, qi, q_off256 + g)),
                  pl.BlockSpec((1, 1, ncp, 2 * LANES), lambda bi, g, qi: (bi, g, 0, 0)),
                  pl.BlockSpec((1, 1, HEAD_DIM, ncp), lambda bi, g, qi: (bi, g, 0, 0)),
                  pl.BlockSpec((nsel, ncp), lambda bi, g, qi: (0, 0))],
        out_specs=[pl.BlockSpec((1, 1, tq, 2 * LANES), lambda bi, g, qi: (bi, g, qi, 0)),
                   pl.BlockSpec((1, 1, tq, LANES), lambda bi, g, qi: (bi, g, qi, 0))],
        out_shape=[jax.ShapeDtypeStruct((b, 2, s, 2 * LANES), F32),
                   jax.ShapeDtypeStruct((b, 2, s, LANES), BF16)],
        compiler_params=_cparams(("parallel", "parallel", "parallel")),
        name="nsa_cmp_topk",
    )(q_arr, kc4, vct, selt)


def _cumgate_body(x_ref, o_ref):
    x = x_ref[0]
    r = x.shape[0]
    ls = jnp.minimum(x, 0.0) - jnp.log1p(jnp.exp(-jnp.abs(x)))
    i0 = lax.broadcasted_iota(jnp.int32, (LANES, LANES), 0)
    i1 = lax.broadcasted_iota(jnp.int32, (LANES, LANES), 1)
    upper = jnp.where(i0 <= i1, 1.0, 0.0).astype(BF16)
    ones = jnp.ones((LANES, LANES), BF16)
    r0 = lax.broadcasted_iota(jnp.int32, (r, r), 0)
    r1 = lax.broadcasted_iota(jnp.int32, (r, r), 1)
    strict = jnp.where(r1 < r0, 1.0, 0.0).astype(BF16)
    parts = _split3(ls)
    intra = sum(_dot(pp, upper) for pp in parts)
    rowtot = sum(_dot(pp, ones) for pp in parts)
    off = sum(_dot(strict, pp) for pp in _split3(rowtot))
    o_ref[0] = intra + off


def _cum_log_forget(x):
    n, r, _ = x.shape
    return pl.pallas_call(
        _cumgate_body,
        grid=(n,),
        in_specs=[pl.BlockSpec((1, r, LANES), lambda i: (i, 0, 0))],
        out_specs=pl.BlockSpec((1, r, LANES), lambda i: (i, 0, 0)),
        out_shape=jax.ShapeDtypeStruct((n, r, LANES), F32),
        compiler_params=_cparams(("parallel",)),
        name="cum_log_forget",
    )(x)


def _merge_body(oa_ref, ob_ref, oc_ref, od_ref, g0_ref, g1_ref, g2_ref, g3_ref, wb_ref, o_ref):
    acc = None
    for n, (o_r, g_r) in enumerate(((oa_ref, g0_ref), (ob_ref, g1_ref), (oc_ref, g2_ref), (od_ref, g3_ref))):
        gate = 0.5 * jnp.tanh(0.5 * g_r[...].astype(F32)) + 0.5
        term = gate * _dot(o_r[...], wb_ref[n])
        acc = term if acc is None else acc + term
    o_ref[...] = acc.astype(o_ref.dtype)


def _merge(o_list, plain, gate_off, wb, *, tm, tn):
    t = plain.shape[0]
    d = wb.shape[-1]
    nj = d // tn
    ospec = pl.BlockSpec((tm, BRANCH_W), lambda i, j: (i, 0))
    gspecs = [pl.BlockSpec((tm, tn), functools.partial(lambda i, j, n: (i, gate_off // tn + n * nj + j), n=n))
              for n in range(N_BRANCH)]
    return pl.pallas_call(
        _merge_body,
        grid=(t // tm, nj),
        in_specs=[ospec] * 4 + gspecs + [pl.BlockSpec((N_BRANCH, BRANCH_W, tn), lambda i, j: (0, 0, j))],
        out_specs=pl.BlockSpec((tm, tn), lambda i, j: (i, j)),
        out_shape=jax.ShapeDtypeStruct((t, d), BF16),
        compiler_params=_cparams(("parallel", "parallel")),
        name="gated_merge",
    )(*o_list, plain, plain, plain, plain, wb)


def _layer_norm(y, g, b):
    mu = jnp.mean(y, axis=-1, keepdims=True)
    yc = y - mu
    var = jnp.mean(yc * yc, axis=-1, keepdims=True)
    return yc * lax.rsqrt(var + LN_EPS) * g + b


def _outproj_ln_body(mg_ref, wo_ref, h_ref, g_ref, b_ref, *rest, with_router):
    if with_router:
        wr_ref, o_ref, ob_ref, lg_ref = rest
    else:
        o_ref, ob_ref = rest
    y = ALPHA * h_ref[...] + _dot(mg_ref[...], wo_ref[...])
    out = _layer_norm(y, g_ref[...], b_ref[...])
    o_ref[...] = out
    ob_ref[...] = out.astype(BF16)
    if with_router:
        hi, lo = _split2(out)
        lg_ref[...] = _dot(hi, wr_ref[0]) + _dot(lo, wr_ref[0]) + _dot(hi, wr_ref[1])


def _outproj_ln(merged, w_out, h, g, b, w_router=None, *, tm):
    t, d = h.shape
    with_router = w_router is not None
    row = lambda i: (i, 0)
    fix = lambda i: (0, 0)
    in_specs = [pl.BlockSpec((tm, d), row), pl.BlockSpec((d, d), fix, pipeline_mode=pl.Buffered(1)),
                pl.BlockSpec((tm, d), row),
                pl.BlockSpec((1, d), fix), pl.BlockSpec((1, d), fix)]
    out_specs = [pl.BlockSpec((tm, d), row), pl.BlockSpec((tm, d), row)]
    out_shape = [jax.ShapeDtypeStruct((t, d), F32), jax.ShapeDtypeStruct((t, d), BF16)]
    args = [merged, w_out, h, g, b]
    if with_router:
        in_specs.append(pl.BlockSpec((2, d, LANES), lambda i: (0, 0, 0)))
        out_specs.append(pl.BlockSpec((tm, LANES), row))
        out_shape.append(jax.ShapeDtypeStruct((t, LANES), F32))
        args.append(w_router)
    return pl.pallas_call(
        functools.partial(_outproj_ln_body, with_router=with_router),
        grid=(t // tm,), in_specs=in_specs, out_specs=out_specs, out_shape=out_shape,
        compiler_params=_cparams(("parallel",)), name="outproj_ln1",
    )(*args)


def _ple_ln_body(hb_ref, h_ref, f_ref, p_ref, wg_ref, wp_ref, g_ref, b_ref, o_ref, ob_ref):
    ple = _sigmoid(_dot(hb_ref[...], wg_ref[...])) * _dot(p_ref[...], wp_ref[...])
    out = _layer_norm(ALPHA * h_ref[...] + f_ref[...] + ple, g_ref[...], b_ref[...])
    o_ref[...] = out
    ob_ref[...] = out.astype(BF16)


def _ple_ln(h_bf, h, f, p_bf, w_gate, w_proj, g, b, *, tm):
    t, d = h.shape
    row = lambda i: (i, 0)
    fix = lambda i: (0, 0)
    return pl.pallas_call(
        _ple_ln_body,
        grid=(t // tm,),
        in_specs=[pl.BlockSpec((tm, d), row), pl.BlockSpec((tm, d), row), pl.BlockSpec((tm, d), row),
                  pl.BlockSpec((tm, PLE_DIM), row), pl.BlockSpec((d, d), fix, pipeline_mode=pl.Buffered(1)),
                  pl.BlockSpec((PLE_DIM, d), fix), pl.BlockSpec((1, d), fix), pl.BlockSpec((1, d), fix)],
        out_specs=[pl.BlockSpec((tm, d), row), pl.BlockSpec((tm, d), row)],
        out_shape=[jax.ShapeDtypeStruct((t, d), F32), jax.ShapeDtypeStruct((t, d), BF16)],
        compiler_params=_cparams(("parallel",)), name="ple_ln2",
    )(h_bf, h, f, p_bf, w_gate, w_proj, g, b)


def _swiglu_tile(x, wg, wu, wd):
    g = _dot(x, wg)
    u = _dot(x, wu)
    return _dot((g * _sigmoid(g) * u).astype(BF16), wd)


def _ffn_body(x_ref, wg_ref, wu_ref, wd_ref, o_ref):
    j = pl.program_id(1)
    y = _swiglu_tile(x_ref[...], wg_ref[...], wu_ref[...], wd_ref[...])

    @pl.when(j == 0)
    def _():
        o_ref[...] = y

    @pl.when(j > 0)
    def _():
        o_ref[...] += y


def _ffn(x_bf, wg, wu, wd, *, tm, tf):
    t, d = x_bf.shape
    f = wg.shape[1]
    return pl.pallas_call(
        _ffn_body,
        grid=(t // tm, f // tf),
        in_specs=[pl.BlockSpec((tm, d), lambda i, j: (i, 0)),
                  pl.BlockSpec((d, tf), lambda i, j: (0, j)),
                  pl.BlockSpec((d, tf), lambda i, j: (0, j)),
                  pl.BlockSpec((tf, d), lambda i, j: (j, 0))],
        out_specs=pl.BlockSpec((tm, d), lambda i, j: (i, 0)),
        out_shape=jax.ShapeDtypeStruct((t, d), F32),
        compiler_params=_cparams(("parallel", "arbitrary")), name="ffn_swiglu",
    )(x_bf, wg, wu, wd)


def _moe_ffn_body(te_ref, nt_ref, x_ref, wg_ref, wu_ref, wd_ref, o_ref):
    i = pl.program_id(0)
    j = pl.program_id(1)
    active = i < nt_ref[0]

    @pl.when(active)
    def _():
        y = _swiglu_tile(x_ref[...], wg_ref[0], wu_ref[0], wd_ref[0])

        @pl.when(j == 0)
        def _():
            o_ref[...] = y

        @pl.when(j > 0)
        def _():
            o_ref[...] += y

    @pl.when(jnp.logical_not(active) & (j == 0))
    def _():
        o_ref[...] = jnp.zeros(o_ref.shape, F32)


def _moe_ffn(tile_expert, n_tiles, x_sorted, wg, wu, wd, *, tm, tf):
    r, d = x_sorted.shape
    f = wg.shape[2]
    nj = f // tf

    def jj(i, j, nt):
        return jnp.where(i < nt[0], j, nj - 1)

    return pl.pallas_call(
        _moe_ffn_body,
        grid_spec=pltpu.PrefetchScalarGridSpec(
            num_scalar_prefetch=2, grid=(r // tm, nj),
            in_specs=[pl.BlockSpec((tm, d), lambda i, j, te, nt: (i, 0)),
                      pl.BlockSpec((1, d, tf), lambda i, j, te, nt: (te[i], 0, jj(i, j, nt))),
                      pl.BlockSpec((1, d, tf), lambda i, j, te, nt: (te[i], 0, jj(i, j, nt))),
                      pl.BlockSpec((1, tf, d), lambda i, j, te, nt: (te[i], jj(i, j, nt), 0))],
            out_specs=pl.BlockSpec((tm, d), lambda i, j, te, nt: (i, 0))),
        out_shape=jax.ShapeDtypeStruct((r, d), F32),
        compiler_params=_cparams(("arbitrary", "arbitrary")), name="moe_grouped_ffn",
    )(tile_expert, n_tiles, x_sorted, wg, wu, wd)


def _row_copy(src_ref, src_row, dst_ref, dst_row, sem):
    return pltpu.make_async_copy(src_ref.at[pl.ds(src_row, 1)], dst_ref.at[pl.ds(dst_row, 1)], sem)


def _gather_rows_body(idx_ref, src_ref, o_ref, buf_ref, sem, *, tm):
    base = pl.program_id(0) * tm

    half = tm // 2

    def start(r, c):
        _row_copy(src_ref, idx_ref[base + r], buf_ref, r, sem.at[0]).start()
        _row_copy(src_ref, idx_ref[base + half + r], buf_ref, half + r, sem.at[1]).start()
        return c

    def wait(r, c):
        _row_copy(src_ref, 0, buf_ref, r, sem.at[0]).wait()
        _row_copy(src_ref, 0, buf_ref, half + r, sem.at[1]).wait()
        return c

    lax.fori_loop(0, half, start, 0, unroll=8)
    lax.fori_loop(0, half, wait, 0, unroll=8)
    o_ref[...] = buf_ref[...].astype(o_ref.dtype)


def _gather_rows(idx, src, n_rows, out_dtype, *, tm):
    d = src.shape[1]
    return pl.pallas_call(
        functools.partial(_gather_rows_body, tm=tm),
        grid_spec=pltpu.PrefetchScalarGridSpec(
            num_scalar_prefetch=1, grid=(n_rows // tm,),
            in_specs=[pl.BlockSpec(memory_space=pl.ANY)],
            out_specs=pl.BlockSpec((tm, d), lambda i, idx: (i, 0)),
            scratch_shapes=[pltpu.VMEM((tm, d), src.dtype), pltpu.SemaphoreType.DMA((2,))]),
        out_shape=jax.ShapeDtypeStruct((n_rows, d), out_dtype),
        compiler_params=_cparams(("arbitrary",)), name="moe_gather_rows",
    )(idx, src)


def _combine_body(idx_ref, src_ref, w_ref, o_ref, a_ref, b_ref, sem, *, tm):
    base = pl.program_id(0) * tm

    def start(r, c):
        _row_copy(src_ref, idx_ref[2 * (base + r)], a_ref, r, sem.at[0]).start()
        _row_copy(src_ref, idx_ref[2 * (base + r) + 1], b_ref, r, sem.at[1]).start()
        return c

    def wait(r, c):
        _row_copy(src_ref, 0, a_ref, r, sem.at[0]).wait()
        _row_copy(src_ref, 0, b_ref, r, sem.at[1]).wait()
        return c

    lax.fori_loop(0, tm, start, 0, unroll=8)
    lax.fori_loop(0, tm, wait, 0, unroll=8)
    w = w_ref[...]
    o_ref[...] = a_ref[...] * w[:, 0:1] + b_ref[...] * w[:, 1:2]


def _combine_pairs(pos, y_sorted, pair_w, n_tokens, *, tm):
    d = y_sorted.shape[1]
    return pl.pallas_call(
        functools.partial(_combine_body, tm=tm),
        grid_spec=pltpu.PrefetchScalarGridSpec(
            num_scalar_prefetch=1, grid=(n_tokens // tm,),
            in_specs=[pl.BlockSpec(memory_space=pl.ANY),
                      pl.BlockSpec((tm, LANES), lambda i, idx: (i, 0))],
            out_specs=pl.BlockSpec((tm, d), lambda i, idx: (i, 0)),
            scratch_shapes=[pltpu.VMEM((tm, d), F32), pltpu.VMEM((tm, d), F32),
                            pltpu.SemaphoreType.DMA((2,))]),
        out_shape=jax.ShapeDtypeStruct((n_tokens, d), F32),
        compiler_params=_cparams(("arbitrary",)), name="moe_combine",
    )(pos, y_sorted, pair_w)


def _col_slices():
    out, off = {}, 0
    for name, width in IN_SPLITS:
        out[name] = (off, width)
        off += width
    return out


def _gather_cols(w, names, pad_to=None):
    cs = _col_slices()
    parts = []
    for n in names:
        col = w[:, cs[n][0]:cs[n][0] + cs[n][1]]
        parts.append(col * Q_FOLD[n] if n in Q_FOLD else col)
    width = sum(cs[n][1] for n in names)
    if pad_to is not None and pad_to > width:
        parts.append(jnp.zeros((w.shape[0], pad_to - width), w.dtype))
    return jnp.concatenate(parts, axis=1).astype(BF16)


def _selection_map_t(n_cmp_pad, n_sel):
    ci = np.arange(n_cmp_pad)[:, None] * NSA_CMP_D
    sj = np.arange(n_sel)[None, :] * NSA_SEL_L
    ov = np.clip(np.minimum(ci + NSA_CMP_L, sj + NSA_SEL_L) - np.maximum(ci, sj), 0, None)
    return np.ascontiguousarray((ov / NSA_CMP_D).astype(np.float32).T)


def _gate_expand_matrices():
    e = np.zeros((4, 3, LANES, LANES), np.float32)
    for j in range(4):
        for hh in range(2):
            for c in range(3):
                e[j, c, (2 * j + hh) * 3 + c, hh * HEAD_DIM:(hh + 1) * HEAD_DIM] = 1.0
    return e


def _split_bits(x, n):
    parts = []
    r = x
    for _ in range(n):
        hi = lax.bitcast_convert_type(
            lax.bitcast_convert_type(r, jnp.uint32) & jnp.uint32(0xFFFF0000), F32)
        parts.append(hi.astype(BF16))
        r = r - hi
    return parts


def _tile(n, pref):
    return pref if n % pref == 0 else n


def _vt_heads(x3, tkc):
    b, s, c = x3.shape
    nh, nc = c // HEAD_DIM, s // tkc
    v = x3.reshape(b, nc, tkc, nh, HEAD_DIM).transpose(0, 3, 1, 4, 2)
    ones = jnp.ones((b, nh, nc, 1, tkc), v.dtype)
    zeros = jnp.zeros((b, nh, nc, ATT_VROWS - HEAD_DIM - 1, tkc), v.dtype)
    return jnp.concatenate([v, ones, zeros], axis=3)


def _token_mixer(h, h_bf, layer, b, s, tabs, w_in, cmp_pe, cmp_w1, cmp_b1, cmp_w2, sinks, fox_bf,
                 diff_lambda, diff_gain, w_branch, w_out, ln_g, ln_b, w_router):
    t = b * s
    (cos64, sin64), (cos32, sin32) = tabs
    tm = _tile(t, 1024)
    r64 = _proj(h_bf, _gather_cols(w_in, SEG_ROPE64), BF16, tm, 512, rope=(cos64, sin64, HEAD_DIM // 2))
    r32 = _proj(h_bf, _gather_cols(w_in, SEG_ROPE32), BF16, tm, 512, rope=(cos32, sin32, DIFF_SUB // 2))
    plain = _proj(h_bf, _gather_cols(w_in, SEG_PLAIN), BF16, tm, 1280)
    gates = _proj(h_bf, _gather_cols(w_in, SEG_GATES), BF16, tm, 1024)
    small =_proj(h_bf, _gather_cols(w_in, SEG_SMALL, pad_to=LANES), F32, tm, LANES)
    r64_3, r32_3, plain_3, small_3 = (a.reshape(b, s, -1) for a in (r64, r32, plain, small))
    tkc = min(ATT_TKC, s // 2)
    tq = 2 * tkc

    ncp = s // NSA_CMP_D
    n_sel = s // NSA_SEL_L
    topn = min(NSA_TOPN, n_sel)

    def cmp_blocks(x2d):
        c = x2d.reshape(b, s, 2, HEAD_DIM).transpose(0, 2, 1, 3).reshape(b * 2, ncp, NSA_CMP_D * HEAD_DIM)
        nxt = jnp.concatenate([c[:, 1:], jnp.zeros_like(c[:, :1])], axis=1)
        return jnp.concatenate([c, nxt], axis=-1)

    xk = cmp_blocks(r64[:, 512:640])
    xv = cmp_blocks(plain[:, 1024:1152])
    cmp_kv = _nsa_compress(jnp.stack([xk, xv]), cmp_pe.reshape(2, 1, -1), cmp_w1.astype(BF16),
                           cmp_b1.reshape(2, 1, -1), cmp_w2.astype(BF16))
    kc = cmp_kv[0].astype(BF16).reshape(b, 2, ncp, HEAD_DIM)
    vc = cmp_kv[1].astype(BF16).reshape(b, 2, ncp, HEAD_DIM)
    kc4 = jnp.tile(kc, (1, 1, 1, 4))
    vct = vc.transpose(0, 1, 3, 2)
    selt = jnp.asarray(_selection_map_t(ncp, n_sel), BF16)
    o_cmp, mneg_t = _nsa_cmp_topk(r64_3, 0, kc4, vct, selt, tq=_tile(s, 256), topn=topn)
    vt_all = _vt_heads(plain_3[..., 1152:2560], tkc)
    o_win = _tattn(r64_3, 0, r64_3, 6, vt_all, 2, nmaps=2, qsel="gqa", vmap=(0, 0), mode="band",
                   window=NSA_WIN, fin="win", name="nsa_window_attention")
    e_mat = jnp.asarray(_gate_expand_matrices(), BF16)
    block_id = jax.nn.one_hot(jnp.arange(s) // NSA_SEL_L, HEAD_DIM, dtype=BF16)
    block_id = jnp.concatenate([block_id, block_id], axis=1)
    nsa_specs = [pl.BlockSpec((1, 1, tq, LANES), lambda bi, u, qi, *_: (bi, u // 2, qi, 0)),
                 pl.BlockSpec((s, LANES), lambda bi, u, qi, *_: (0, 0)),
                 pl.BlockSpec((1, 1, tq, LANES), lambda bi, u, qi, *_: (bi, u // 2, qi, u % 2)),
                 pl.BlockSpec((1, tq, LANES), lambda bi, u, qi, *_: (bi, qi, u)),
                 pl.BlockSpec((1, tq, LANES), lambda bi, u, qi, *_: (bi, qi, 0)),
                 pl.BlockSpec((1, 3, LANES, LANES), lambda bi, u, qi, *_: (u, 0, 0, 0))]
    o_a = _tattn(r64_3, 0, r64_3, 5, vt_all, 0, nmaps=2, qsel="gqa", vmap=(0, 0), mode="causal", fin="nsa",
                 bias="nsa", name="nsa_selected_attention", extras=(mneg_t, block_id, o_cmp, o_win, small_3, e_mat),
                 extra_specs=nsa_specs)

    o_b = _tattn(r64_3, 7, r64_3, 11, vt_all, 4, nmaps=2, qsel="gqa", vmap=(0, 0), mode="band",
                 window=SWA_WIN, fin="swa", name="swa_attention", scalars=sinks.astype(F32) * LOG2E)

    f_logit = (small[:, 24:32] + fox_bf[None, :]).reshape(b, s, 8).transpose(0, 2, 1)
    cum = _cum_log_forget(f_logit.reshape(b * 8, s // LANES, LANES)).reshape(b, 8, s)
    hi, mid, lo_piece = (piece[..., None] for piece in _split_bits(cum * LOG2E, 3))
    slot = (jnp.arange(LANES) % HEAD_DIM)[None, None, None, :]
    ck3 = jnp.where(slot == 0, hi, jnp.where(slot == 1, mid, jnp.where(slot == 2, lo_piece, jnp.zeros((), BF16))))
    o_c = _tattn(plain_3, 0, plain_3, 4, vt_all, 6, nmaps=2, qsel="pair", vmap=(0, 1), mode="causal",
                 fin="fox", bias="fox", name="fox_attention", extras=(ck3,),
                 extra_specs=[pl.BlockSpec((1, 2, s, LANES), lambda bi, u, qi, *_: (bi, u, 0, 0))])

    lam_init = 0.8 - 0.6 * math.exp(-0.3 * layer)
    lf = diff_lambda.astype(F32)
    lam = jnp.exp(jnp.sum(lf[0] * lf[1])) - jnp.exp(jnp.sum(lf[2] * lf[3])) + lam_init
    lam_arr = jnp.stack([lam, jnp.asarray(1.0 - lam_init, F32)]).astype(F32)
    gain_t = jnp.broadcast_to(jnp.tile(diff_gain.astype(F32), 2)[:, None], (LANES, tq))
    o_d = _tattn(r32_3, 0, r32_3, 4, vt_all, 14, nmaps=4, qsel="pair", vmap=(0, 0, 1, 1), mode="causal",
                 fin="diff", name="diff_attention", scalars=lam_arr, extras=(gain_t,),
                 extra_specs=[pl.BlockSpec((LANES, tq), lambda bi, u, qi, *_: (0, 0))])

    o_list = [o.reshape(t, BRANCH_W) for o in (o_a, o_b, o_c, o_d)]
    merged = _merge(o_list, gates, 0, w_branch.astype(BF16), tm=tm, tn=512)
    return _outproj_ln(merged, w_out.astype(BF16), h, ln_g.reshape(1, -1), ln_b.reshape(1, -1),
                       w_router, tm=_tile(t, 256))


def _moe_layer(h1, logits_pad, b_router, wg, wu, wd, *, tm, tf):
    t, d = h1.shape
    logits = logits_pad[:, :N_EXPERTS] + b_router.astype(F32)[None, :]
    top_v, top_i = lax.top_k(logits, TOP_K)
    top_w = jax.nn.softmax(top_v, axis=-1)
    flat_e = top_i.reshape(-1)
    onehot = jax.nn.one_hot(flat_e, N_EXPERTS, dtype=jnp.int32)
    rank = jnp.sum((jnp.cumsum(onehot, axis=0) - onehot) * onehot, axis=1)
    cnt = jnp.sum(onehot, axis=0)
    padded = ((cnt + tm - 1) // tm) * tm
    ends = jnp.cumsum(padded)
    starts = ends - padded
    pos = (starts[flat_e] + rank).astype(jnp.int32)
    n_rows = TOP_K * t + N_EXPERTS * tm
    row_token = jnp.zeros((n_rows,), jnp.int32).at[pos].set(jnp.arange(TOP_K * t, dtype=jnp.int32) // TOP_K)
    tile_start = jnp.arange(n_rows // tm, dtype=jnp.int32) * tm
    tile_expert = jnp.minimum(jnp.sum(tile_start[:, None] >= ends[None, :], axis=1), N_EXPERTS - 1)
    n_tiles = (ends[-1] // tm).astype(jnp.int32).reshape(1)
    x_sorted = _gather_rows(row_token, h1, n_rows, BF16, tm=512)
    y_sorted = _moe_ffn(tile_expert.astype(jnp.int32), n_tiles, x_sorted, wg, wu, wd, tm=tm, tf=tf)
    pair_w = jnp.pad(top_w.astype(F32), ((0, 0), (0, LANES - TOP_K)))
    return _combine_pairs(pos, y_sorted, pair_w, t, tm=_tile(t, 256))


def kernel(x, p, positions, w_in, nsa_cmp_pe, nsa_cmp_w1, nsa_cmp_b1, nsa_cmp_w2, swa_sinks, fox_bf,
           diff_lambda, diff_gain, w_branch, w_out, ln1_g, ln1_b, ffn_wg, ffn_wu, ffn_wd, moe_router,
           moe_router_b, moe_wg, moe_wu, moe_wd, ple_proj, ple_gate, ln2_g, ln2_b):
    b, s, d = x.shape
    t = b * s
    tabs = (_rope_tabs(positions, HEAD_DIM), _rope_tabs(positions, DIFF_SUB))
    h = x.reshape(t, d).astype(F32)
    h_bf = h.astype(BF16)
    for i in range(DEPTH):
        is_moe = i % 2 == 1
        w_router = None
        if is_moe:
            wr = jnp.zeros((d, LANES), F32).at[:, :N_EXPERTS].set(moe_router[i // 2].astype(F32))
            w_router = jnp.stack(_split_bits(wr, 2))
        res = _token_mixer(h, h_bf, i, b, s, tabs, w_in[i], nsa_cmp_pe[i], nsa_cmp_w1[i], nsa_cmp_b1[i],
                           nsa_cmp_w2[i], swa_sinks[i], fox_bf[i], diff_lambda[i], diff_gain[i],
                           w_branch[i], w_out[i], ln1_g[i], ln1_b[i], w_router)
        h1, h1_bf = res[0], res[1]
        if not is_moe:
            fpad = (-D_FF) % 512
            wg = jnp.pad(ffn_wg[i // 2].astype(BF16), ((0, 0), (0, fpad)))
            wu = jnp.pad(ffn_wu[i // 2].astype(BF16), ((0, 0), (0, fpad)))
            wd = jnp.pad(ffn_wd[i // 2].astype(BF16), ((0, fpad), (0, 0)))
            f = _ffn(h1_bf, wg, wu, wd, tm=_tile(t, 1024), tf=512)
        else:
            f = _moe_layer(h1, res[2], moe_router_b[i // 2], moe_wg[i // 2].astype(BF16),
                           moe_wu[i // 2].astype(BF16), moe_wd[i // 2].astype(BF16),
                           tm=_tile(t, 512), tf=1024)
        h, h_bf = _ple_ln(h1_bf, h1, f, p[i].reshape(t, PLE_DIM).astype(BF16), ple_gate[i].astype(BF16),
                          ple_proj[i].astype(BF16), ln2_g[i].reshape(1, -1), ln2_b[i].reshape(1, -1),
                          tm=_tile(t, 512))
    return h.reshape(b, s, d).astype(x.dtype)
```

```python
import functools
import math

import numpy as np
import jax
import jax.numpy as jnp
from jax import lax
from jax.experimental import pallas as pl
from jax.experimental.pallas import tpu as pltpu

F32 = jnp.float32
BF16 = jnp.bfloat16

D_MODEL = 2048
DEPTH = 2
HEAD_DIM = 64
ROPE_THETA = 10000.0
PLE_DIM = 256
LN_EPS = 1e-5
NSA_CMP_L = 32
NSA_CMP_D = 16
NSA_SEL_L = 64
NSA_TOPN = 16
NSA_WIN = 512
NSA_CMP_HIDDEN = 256
NSA_FORCE = 1e9
SWA_WIN = 128
DIFF_SUB = HEAD_DIM // 2
N_BRANCH = 4
BRANCH_W = 8 * HEAD_DIM
D_FF = 5504
N_EXPERTS = 8
TOP_K = 2
D_FF_EXPERT = 7168
ALPHA = (2.0 * DEPTH) ** 0.25

IN_SPLITS = (
    ("a_q", 512), ("a_kc", 128), ("a_vc", 128), ("a_ks", 128), ("a_vs", 128),
    ("a_kw", 128), ("a_vw", 128), ("a_g", 24),
    ("b_q", 512), ("b_k", 128), ("b_v", 128),
    ("c_q", 512), ("c_k", 512), ("c_v", 512), ("c_f", 8),
    ("d_q", 512), ("d_k", 512), ("d_v", 512),
    ("merge_gate", N_BRANCH * D_MODEL),
)
SEG_ROPE64 = ("a_q", "a_kc", "a_ks", "a_kw", "b_q", "b_k")
SEG_ROPE32 = ("d_q", "d_k")
SEG_PLAIN = ("c_q", "c_k", "a_vc", "a_vs", "a_vw", "b_v", "c_v", "d_v")
SEG_GATES = ("merge_gate",)
SEG_SMALL = ("a_g", "c_f")

LANES = 128
NEG = -1e30
LOG2E = math.log2(math.e)
VMEM_LIMIT = 56 * 1024 * 1024
ATT_TKC = 256
ATT_TQ = 2 * ATT_TKC
ATT_VROWS = 80

Q_FOLD = {"a_q": HEAD_DIM ** -0.5 * LOG2E, "b_q": HEAD_DIM ** -0.5 * LOG2E,
          "c_q": HEAD_DIM ** -0.5 * LOG2E, "d_q": DIFF_SUB ** -0.5 * LOG2E}


def _cparams(sem):
    return pltpu.CompilerParams(dimension_semantics=sem, vmem_limit_bytes=VMEM_LIMIT)


def _sigmoid(x):
    return 1.0 / (1.0 + jnp.exp(-x))


def _dot(a, b):
    return jnp.dot(a, b, preferred_element_type=F32)


def _dot_nt(a, b):
    return lax.dot_general(a, b, (((1,), (1,)), ((), ())), preferred_element_type=F32)


def _split2(x):
    hi = x.astype(BF16)
    lo = (x - hi.astype(F32)).astype(BF16)
    return hi, lo


def _split3(x):
    hi = x.astype(BF16)
    r = x - hi.astype(F32)
    mid = r.astype(BF16)
    lo = (r - mid.astype(F32)).astype(BF16)
    return hi, mid, lo


def _proj_body(x_ref, w_ref, *rest, rope_half):
    if rope_half:
        cos_ref, sin_ref, o_ref = rest
    else:
        (o_ref,) = rest
    acc = _dot(x_ref[...], w_ref[...])
    if not rope_half:
        o_ref[...] = acc.astype(o_ref.dtype)
        return
    cos = cos_ref[...]
    sin = sin_ref[...]
    lane = lax.broadcasted_iota(jnp.int32, cos.shape, 1)
    first = (lane % (2 * rope_half)) < rope_half
    for c in range(acc.shape[1] // LANES):
        a = acc[:, c * LANES:(c + 1) * LANES]
        rot = jnp.where(first, pltpu.roll(a, LANES - rope_half, 1), pltpu.roll(a, rope_half, 1))
        o_ref[:, c * LANES:(c + 1) * LANES] = (a * cos + rot * sin).astype(o_ref.dtype)


def _proj(x, w, out_dtype, tm, tn, rope=None):
    m, k = x.shape
    n = w.shape[1]
    in_specs = [pl.BlockSpec((tm, k), lambda i, j: (i, 0)),
                pl.BlockSpec((k, tn), lambda i, j: (0, j))]
    args = [x, w]
    rope_half = 0
    if rope is not None:
        cos_tab, sin_tab, rope_half = rope
        in_specs += [pl.BlockSpec((tm, LANES), lambda i, j: (i, 0)),
                     pl.BlockSpec((tm, LANES), lambda i, j: (i, 0))]
        args += [cos_tab, sin_tab]
    return pl.pallas_call(
        functools.partial(_proj_body, rope_half=rope_half),
        grid=(m // tm, n // tn),
        in_specs=in_specs,
        out_specs=pl.BlockSpec((tm, tn), lambda i, j: (i, j)),
        out_shape=jax.ShapeDtypeStruct((m, n), out_dtype),
        compiler_params=_cparams(("parallel", "parallel")),
        name="proj_rope" if rope_half else "proj",
    )(*args)


def _rope_tabs(positions, dim):
    inv = 1.0 / (ROPE_THETA ** (jnp.arange(0, dim, 2, dtype=F32) / dim))
    ang = positions.astype(F32).reshape(-1)[:, None] * inv
    c, s = jnp.cos(ang), jnp.sin(ang)
    reps = LANES // dim
    return (jnp.tile(jnp.concatenate([c, c], -1), (1, reps)),
            jnp.tile(jnp.concatenate([-s, s], -1), (1, reps)))


def _tattn_body(*refs, nmaps, qsel, vmap, tq, tkc, mode, window, bias, fin):
    refs = list(refs)
    sc_ref = refs.pop(0) if fin in ("diff", "swa") else None
    q_ref, k_ref, vt_ref = refs[:3]
    extras, o_ref = refs[3:-5], refs[-5]
    qm_ref, st_ref, m_ref, acc_ref = refs[-4:]
    bias_ref = None
    if bias is not None:
        bias_ref, extras = extras[0], extras[1:]
    u = pl.program_id(1)
    qi = pl.program_id(2)
    q0 = qi * tq
    last_chunk = k_ref.shape[1] // tkc - 1

    lane = lax.broadcasted_iota(jnp.int32, (tq, LANES), 1)
    klane = lax.broadcasted_iota(jnp.int32, (tkc, LANES), 1)
    q = q_ref[0].astype(F32)
    if qsel == "gqa":
        lo_lane = (u // 2) * HEAD_DIM
        q_rolled = pltpu.roll(q, HEAD_DIM, 1)
        in_group = (lane >= lo_lane) & (lane < lo_lane + HEAD_DIM)
        k_in_group = (klane >= lo_lane) & (klane < lo_lane + HEAD_DIM)
        fill = bias_ref[0, 0].astype(F32) if bias == "nsa" else 0.0
        for e in range(2):
            q_e = jnp.where(lo_lane == e * HEAD_DIM, q, q_rolled)
            qm_ref[e] = jnp.where(in_group, q_e, fill).T.astype(BF16)
    else:
        width = LANES // nmaps
        fill = jnp.where(lane % HEAD_DIM < 3, -1.0, 0.0) if bias == "fox" else 0.0
        for mp in range(nmaps):
            qm_ref[mp] = jnp.where((lane >= mp * width) & (lane < (mp + 1) * width), q, fill).T.astype(BF16)

    m_ref[...] = jnp.full(m_ref.shape, NEG, F32)
    acc_ref[...] = jnp.zeros(acc_ref.shape, F32)

    def qk(c, buf, q_lo=0, q_hi=tq, maps=None):
        cc = jnp.clip(c, 0, last_chunk)
        off = pl.multiple_of(cc * tkc, tkc)
        kc = k_ref[0, pl.ds(off, tkc), :]
        if bias == "nsa":
            kc = jnp.where(k_in_group, kc, extras[0][pl.ds(off, tkc), :])
        for mp in (range(nmaps) if maps is None else maps):
            kc_mp = kc
            if bias == "fox":
                kc_mp = jnp.where((klane >= mp * HEAD_DIM) & (klane < (mp + 1) * HEAD_DIM), kc,
                                  bias_ref[0, mp, pl.ds(off, tkc), :])
            st_ref[buf, mp, :, q_lo:q_hi] = _dot(kc_mp, qm_ref[mp, :, q_lo:q_hi])

    def soft(c, buf, masked, q_lo=0, q_hi=tq, maps=None):
        for mp in (range(nmaps) if maps is None else maps):
            st = st_ref[buf, mp, :, q_lo:q_hi]
            if masked:
                key = c * tkc + lax.broadcasted_iota(jnp.int32, st.shape, 0)
                t_pos = q0 + q_lo + lax.broadcasted_iota(jnp.int32, st.shape, 1)
                keep = key <= t_pos
                if mode == "band":
                    keep = keep & (t_pos - key < window) & (key >= 0)
                st = jnp.where(keep, st, NEG)
            m_old = m_ref[mp, :, q_lo:q_hi]
            m8 = jnp.max(st.reshape(tkc // 8, 8, q_hi - q_lo), axis=0)
            m_new = jnp.maximum(m_old, jnp.max(m8, axis=0, keepdims=True))
            p = jnp.exp2(st - m_new).astype(BF16)
            acc_ref[mp, :, q_lo:q_hi] = (jnp.exp2(m_old - m_new) * acc_ref[mp, :, q_lo:q_hi]
                                         + _dot(vt_ref[0, vmap[mp], jnp.maximum(c, 0)], p))
            m_ref[mp, :, q_lo:q_hi] = m_new

    def pair(pidx, carry, masked):
        c0 = 2 * pidx
        for mp in range(nmaps):
            qk(c0 + 1, 1, maps=(mp,))
            soft(c0, 0, masked, maps=(mp,))
        for mp in range(nmaps):
            qk(c0 + 2, 0, maps=(mp,))
            soft(c0 + 1, 1, masked, maps=(mp,))
        return carry

    if mode == "causal":
        qk(0, 0)
        lax.fori_loop(0, qi, functools.partial(pair, masked=False), 0)
        qk(2 * qi + 1, 1, q_lo=tkc)
        soft(2 * qi, 0, True, q_hi=tkc)
        soft(2 * qi, 0, False, q_lo=tkc)
        soft(2 * qi + 1, 1, True, q_lo=tkc)
    else:
        sched = []
        for d in range(-((window - 1 + tkc - 1) // tkc), 2):
            hi_lane = min(tq, d * tkc + tkc - 1 + window)
            sched.append((2 * qi + d, max(0, d * tkc), -(-hi_lane // LANES) * LANES))
        qk(sched[0][0], 0, sched[0][1], sched[0][2])
        for i, (c, lo_lane_q, hi_lane_q) in enumerate(sched):
            if i + 1 < len(sched):
                qk(sched[i + 1][0], (i + 1) % 2, sched[i + 1][1], sched[i + 1][2])
            soft(c, i % 2, True, lo_lane_q, hi_lane_q)

    def normed(mp):
        acc = acc_ref[mp]
        num, l_i = acc[:HEAD_DIM], acc[HEAD_DIM:HEAD_DIM + 1]
        if fin == "swa":
            m_i = m_ref[mp]
            sk = sc_ref[2 * u + mp]
            m_f = jnp.maximum(m_i, sk)
            corr = jnp.exp2(m_i - m_f)
            return num * (corr / (l_i * corr + jnp.exp2(sk - m_f)))
        return num * (1.0 / l_i)

    if fin == "diff":
        lam = sc_ref[0]
        halves = []
        for hh in range(2):
            o = normed(2 * hh) - lam * normed(2 * hh + 1)
            ms = jnp.mean(o * o, axis=0, keepdims=True)
            halves.append(o * lax.rsqrt(ms + LN_EPS))
        ot = jnp.concatenate(halves, axis=0) * extras[0][...] * sc_ref[1]
    else:
        ot = jnp.concatenate([normed(0), normed(1)], axis=0)
    o = ot.T
    if fin == "nsa":
        ocmp_ref, owin_ref, sm_ref, e_ref = extras[1:]
        hi, lo_part = _split2(sm_ref[0])
        gates = [_sigmoid(_dot(hi, e_ref[0, c]) + _dot(lo_part, e_ref[0, c])) for c in range(3)]
        o = gates[0] * ocmp_ref[0, 0] + gates[1] * o + gates[2] * owin_ref[0].astype(F32)
    o_ref[0] = o.astype(o_ref.dtype)


def _tattn(q_arr, q_blk, k_arr, k_blk, vt_all, v_head, *, nmaps, qsel, vmap, mode, fin, name, window=0,
           bias=None, scalars=None, extras=(), extra_specs=()):
    b, s, _ = q_arr.shape
    nc, vrows, tkc = vt_all.shape[2], vt_all.shape[3], vt_all.shape[4]
    tq = 2 * tkc
    nu = 4
    if qsel == "gqa":
        kspec = pl.BlockSpec((1, s, LANES), lambda bi, u, qi, *_: (bi, 0, k_blk))
        vspec = pl.BlockSpec((1, 1, nc, vrows, tkc), lambda bi, u, qi, *_: (bi, v_head + u // 2, 0, 0, 0))
    else:
        kspec = pl.BlockSpec((1, s, LANES), lambda bi, u, qi, *_: (bi, 0, k_blk + u))
        vspec = pl.BlockSpec((1, 2, nc, vrows, tkc), lambda bi, u, qi, *_: (bi, v_head // 2 + u, 0, 0, 0))
    in_specs = [pl.BlockSpec((1, tq, LANES), lambda bi, u, qi, *_: (bi, qi, q_blk + u)), kspec, vspec]
    in_specs += list(extra_specs)
    body = functools.partial(_tattn_body, nmaps=nmaps, qsel=qsel, vmap=vmap, tq=tq, tkc=tkc, mode=mode,
                             window=window, bias=bias, fin=fin)
    args = ([] if scalars is None else [scalars]) + [q_arr, k_arr, vt_all] + list(extras)
    return pl.pallas_call(
        body,
        grid_spec=pltpu.PrefetchScalarGridSpec(
            num_scalar_prefetch=0 if scalars is None else 1, grid=(b, nu, s // tq),
            in_specs=in_specs,
            out_specs=pl.BlockSpec((1, tq, LANES), lambda bi, u, qi, *_: (bi, qi, u)),
            scratch_shapes=[pltpu.VMEM((nmaps, LANES, tq), BF16), pltpu.VMEM((2, nmaps, tkc, tq), F32),
                            pltpu.VMEM((nmaps, 1, tq), F32), pltpu.VMEM((nmaps, vrows, tq), F32)]),
        out_shape=jax.ShapeDtypeStruct((b, s, nu * LANES), BF16),
        compiler_params=_cparams(("parallel", "parallel", "parallel")),
        name=name,
    )(*args)


def _gelu_tanh(x):
    return 0.5 * x * (1.0 + jnp.tanh(math.sqrt(2.0 / math.pi) * (x + 0.044715 * (x * x * x))))


def _compress_body(x_ref, pe_ref, w1_ref, b1_ref, w2_ref, o_ref):
    x = (x_ref[0, 0].astype(F32) + pe_ref[0]).astype(BF16)
    hid = _gelu_tanh(_dot(x, w1_ref[0]) + b1_ref[0])
    o_ref[0, 0] = _dot(hid.astype(BF16), w2_ref[0])


def _nsa_compress(x, pe, w1, b1, w2):
    _, nb, ncp, ld = x.shape
    hid = w1.shape[-1]
    return pl.pallas_call(
        _compress_body,
        grid=(2, nb),
        in_specs=[pl.BlockSpec((1, 1, ncp, ld), lambda t, i: (t, i, 0, 0)),
                  pl.BlockSpec((1, 1, ld), lambda t, i: (t, 0, 0)),
                  pl.BlockSpec((1, ld, hid), lambda t, i: (t, 0, 0)),
                  pl.BlockSpec((1, 1, hid), lambda t, i: (t, 0, 0)),
                  pl.BlockSpec((1, hid, HEAD_DIM), lambda t, i: (t, 0, 0))],
        out_specs=pl.BlockSpec((1, 1, ncp, HEAD_DIM), lambda t, i: (t, i, 0, 0)),
        out_shape=jax.ShapeDtypeStruct((2, nb, ncp, HEAD_DIM), F32),
        compiler_params=_cparams(("parallel", "parallel")),
        name="nsa_compress",
    )(x, pe, w1, b1, w2)


def _cmp_topk_body(q_ref, kc_ref, vct_ref, selt_ref, o_ref, mt_ref, *, tq, ncp, nsel, topn):
    qi = pl.program_id(2)
    q = q_ref[0]
    kc4 = kc_ref[0, 0]
    vct = vct_ref[0, 0]
    lane = lax.broadcasted_iota(jnp.int32, kc4.shape, 1)
    ci = lax.broadcasted_iota(jnp.int32, (ncp, tq), 0)
    tpos = qi * tq + lax.broadcasted_iota(jnp.int32, (ncp, tq), 1)
    cmask = ci * NSA_CMP_D + (NSA_CMP_L - 1) <= tpos
    psum = jnp.zeros((ncp, tq), F32)
    rows = []
    for a in range(4):
        kcm = jnp.where((lane >= a * HEAD_DIM) & (lane < (a + 1) * HEAD_DIM), kc4, jnp.zeros_like(kc4))
        st = jnp.where(cmask, _dot_nt(kcm, q), NEG)
        m = jnp.max(st, axis=0, keepdims=True)
        e = jnp.where(cmask, jnp.exp2(st - m), 0.0)
        l = jnp.sum(e, axis=0, keepdims=True)
        p = e * jnp.where(l > 0.0, 1.0 / l, 0.0)
        psum = psum + p
        rows.append(_dot(vct, p.astype(BF16)))
    hi, lo = _split2(psum)
    selt = selt_ref[...]
    imp = _dot(selt, hi) + _dot(selt, lo)
    blk = lax.broadcasted_iota(jnp.int32, (nsel, tq), 0)
    cur = (qi * tq + lax.broadcasted_iota(jnp.int32, (nsel, tq), 1)) // NSA_SEL_L
    forced = (blk == 0) | (blk == cur) | (blk == cur - 1)
    imp = jnp.where(forced, NSA_FORCE, jnp.where(blk > cur, -NSA_FORCE, imp))
    cnt = jnp.zeros((nsel, tq), jnp.int32)
    for jp in range(nsel):
        v = imp[jp:jp + 1, :]
        tie = jnp.where(blk > jp, 1, 0)
        cnt = cnt + jnp.where(v > imp, 1, jnp.where(v == imp, tie, 0))
    mneg = jnp.where(cnt < topn, 0.0, NEG)
    if nsel < HEAD_DIM:
        mneg = jnp.concatenate([mneg, jnp.zeros((HEAD_DIM - nsel, tq), F32)], axis=0)
    mt_ref[0, 0] = jnp.concatenate([mneg, mneg], axis=0).T.astype(mt_ref.dtype)
    o_ref[0, 0] = jnp.concatenate(rows, axis=0).T


def _nsa_cmp_topk(q_arr, q_off256, kc4, vct, selt, *, tq, topn):
    b, s, _ = q_arr.shape
    ncp = kc4.shape[2]
    nsel = selt.shape[0]
    body = functools.partial(_cmp_topk_body, tq=tq, ncp=ncp, nsel=nsel, topn=topn)
    return pl.pallas_call(
        body,
        grid=(b, 2, s // tq),
        in_specs=[pl.BlockSpec((1, tq, 2 * LANES), lambda bi, g, qi: (bi, qi, q_off256 + g)),
                  pl.BlockSpec((1, 1, ncp, 2 * LANES), lambda bi, g, qi: (bi, g, 0, 0)),
                  pl.BlockSpec((1, 1, HEAD_DIM, ncp), lambda bi, g, qi: (bi, g, 0, 0)),
                  pl.BlockSpec((nsel, ncp), lambda bi, g, qi: (0, 0))],
        out_specs=[pl.BlockSpec((1, 1, tq, 2 * LANES), lambda bi, g, qi: (bi, g, qi, 0)),
                   pl.BlockSpec((1, 1, tq, LANES), lambda bi, g, qi: (bi, g, qi, 0))],
        out_shape=[jax.ShapeDtypeStruct((b, 2, s, 2 * LANES), F32),
                   jax.ShapeDtypeStruct((b, 2, s, LANES), BF16)],
        compiler_params=_cparams(("parallel", "parallel", "parallel")),
        name="nsa_cmp_topk",
    )(q_arr, kc4, vct, selt)


def _cumgate_body(x_ref, o_ref):
    x = x_ref[0]
    r = x.shape[0]
    ls = jnp.minimum(x, 0.0) - jnp.log1p(jnp.exp(-jnp.abs(x)))
    i0 = lax.broadcasted_iota(jnp.int32, (LANES, LANES), 0)
    i1 = lax.broadcasted_iota(jnp.int32, (LANES, LANES), 1)
    upper = jnp.where(i0 <= i1, 1.0, 0.0).astype(BF16)
    ones = jnp.ones((LANES, LANES), BF16)
    r0 = lax.broadcasted_iota(jnp.int32, (r, r), 0)
    r1 = lax.broadcasted_iota(jnp.int32, (r, r), 1)
    strict = jnp.where(r1 < r0, 1.0, 0.0).astype(BF16)
    parts = _split3(ls)
    intra = sum(_dot(pp, upper) for pp in parts)
    rowtot = sum(_dot(pp, ones) for pp in parts)
    off = sum(_dot(strict, pp) for pp in _split3(rowtot))
    o_ref[0] = intra + off


def _cum_log_forget(x):
    n, r, _ = x.shape
    return pl.pallas_call(
        _cumgate_body,
        grid=(n,),
        in_specs=[pl.BlockSpec((1, r, LANES), lambda i: (i, 0, 0))],
        out_specs=pl.BlockSpec((1, r, LANES), lambda i: (i, 0, 0)),
        out_shape=jax.ShapeDtypeStruct((n, r, LANES), F32),
        compiler_params=_cparams(("parallel",)),
        name="cum_log_forget",
    )(x)


def _merge_body(oa_ref, ob_ref, oc_ref, od_ref, g0_ref, g1_ref, g2_ref, g3_ref, wb_ref, o_ref):
    acc = None
    for n, (o_r, g_r) in enumerate(((oa_ref, g0_ref), (ob_ref, g1_ref), (oc_ref, g2_ref), (od_ref, g3_ref))):
        gate = 0.5 * jnp.tanh(0.5 * g_r[...].astype(F32)) + 0.5
        term = gate * _dot(o_r[...], wb_ref[n])
        acc = term if acc is None else acc + term
    o_ref[...] = acc.astype(o_ref.dtype)


def _merge(o_list, plain, gate_off, wb, *, tm, tn):
    t = plain.shape[0]
    d = wb.shape[-1]
    nj = d // tn
    ospec = pl.BlockSpec((tm, BRANCH_W), lambda i, j: (i, 0))
    gspecs = [pl.BlockSpec((tm, tn), functools.partial(lambda i, j, n: (i, gate_off // tn + n * nj + j), n=n))
              for n in range(N_BRANCH)]
    return pl.pallas_call(
        _merge_body,
        grid=(t // tm, nj),
        in_specs=[ospec] * 4 + gspecs + [pl.BlockSpec((N_BRANCH, BRANCH_W, tn), lambda i, j: (0, 0, j))],
        out_specs=pl.BlockSpec((tm, tn), lambda i, j: (i, j)),
        out_shape=jax.ShapeDtypeStruct((t, d), BF16),
        compiler_params=_cparams(("parallel", "parallel")),
        name="gated_merge",
    )(*o_list, plain, plain, plain, plain, wb)


def _layer_norm(y, g, b):
    mu = jnp.mean(y, axis=-1, keepdims=True)
    yc = y - mu
    var = jnp.mean(yc * yc, axis=-1, keepdims=True)
    return yc * lax.rsqrt(var + LN_EPS) * g + b


def _outproj_ln_body(mg_ref, wo_ref, h_ref, g_ref, b_ref, *rest, with_router):
    if with_router:
        wr_ref, o_ref, ob_ref, lg_ref = rest
    else:
        o_ref, ob_ref = rest
    y = ALPHA * h_ref[...] + _dot(mg_ref[...], wo_ref[...])
    out = _layer_norm(y, g_ref[...], b_ref[...])
    o_ref[...] = out
    ob_ref[...] = out.astype(BF16)
    if with_router:
        hi, lo = _split2(out)
        lg_ref[...] = _dot(hi, wr_ref[0]) + _dot(lo, wr_ref[0]) + _dot(hi, wr_ref[1])


def _outproj_ln(merged, w_out, h, g, b, w_router=None, *, tm):
    t, d = h.shape
    with_router = w_router is not None
    row = lambda i: (i, 0)
    fix = lambda i: (0, 0)
    in_specs = [pl.BlockSpec((tm, d), row), pl.BlockSpec((d, d), fix, pipeline_mode=pl.Buffered(1)),
                pl.BlockSpec((tm, d), row),
                pl.BlockSpec((1, d), fix), pl.BlockSpec((1, d), fix)]
    out_specs = [pl.BlockSpec((tm, d), row), pl.BlockSpec((tm, d), row)]
    out_shape = [jax.ShapeDtypeStruct((t, d), F32), jax.ShapeDtypeStruct((t, d), BF16)]
    args = [merged, w_out, h, g, b]
    if with_router:
        in_specs.append(pl.BlockSpec((2, d, LANES), lambda i: (0, 0, 0)))
        out_specs.append(pl.BlockSpec((tm, LANES), row))
        out_shape.append(jax.ShapeDtypeStruct((t, LANES), F32))
        args.append(w_router)
    return pl.pallas_call(
        functools.partial(_outproj_ln_body, with_router=with_router),
        grid=(t // tm,), in_specs=in_specs, out_specs=out_specs, out_shape=out_shape,
        compiler_params=_cparams(("parallel",)), name="outproj_ln1",
    )(*args)


def _ple_ln_body(hb_ref, h_ref, f_ref, p_ref, wg_ref, wp_ref, g_ref, b_ref, o_ref, ob_ref):
    ple = _sigmoid(_dot(hb_ref[...], wg_ref[...])) * _dot(p_ref[...], wp_ref[...])
    out = _layer_norm(ALPHA * h_ref[...] + f_ref[...] + ple, g_ref[...], b_ref[...])
    o_ref[...] = out
    ob_ref[...] = out.astype(BF16)


def _ple_ln(h_bf, h, f, p_bf, w_gate, w_proj, g, b, *, tm):
    t, d = h.shape
    row = lambda i: (i, 0)
    fix = lambda i: (0, 0)
    return pl.pallas_call(
        _ple_ln_body,
        grid=(t // tm,),
        in_specs=[pl.BlockSpec((tm, d), row), pl.BlockSpec((tm, d), row), pl.BlockSpec((tm, d), row),
                  pl.BlockSpec((tm, PLE_DIM), row), pl.BlockSpec((d, d), fix, pipeline_mode=pl.Buffered(1)),
                  pl.BlockSpec((PLE_DIM, d), fix), pl.BlockSpec((1, d), fix), pl.BlockSpec((1, d), fix)],
        out_specs=[pl.BlockSpec((tm, d), row), pl.BlockSpec((tm, d), row)],
        out_shape=[jax.ShapeDtypeStruct((t, d), F32), jax.ShapeDtypeStruct((t, d), BF16)],
        compiler_params=_cparams(("parallel",)), name="ple_ln2",
    )(h_bf, h, f, p_bf, w_gate, w_proj, g, b)


def _swiglu_tile(x, wg, wu, wd):
    g = _dot(x, wg)
    u = _dot(x, wu)
    return _dot((g * _sigmoid(g) * u).astype(BF16), wd)


def _ffn_body(x_ref, wg_ref, wu_ref, wd_ref, o_ref):
    j = pl.program_id(1)
    y = _swiglu_tile(x_ref[...], wg_ref[...], wu_ref[...], wd_ref[...])

    @pl.when(j == 0)
    def _():
        o_ref[...] = y

    @pl.when(j > 0)
    def _():
        o_ref[...] += y


def _ffn(x_bf, wg, wu, wd, *, tm, tf):
    t, d = x_bf.shape
    f = wg.shape[1]
    return pl.pallas_call(
        _ffn_body,
        grid=(t // tm, f // tf),
        in_specs=[pl.BlockSpec((tm, d), lambda i, j: (i, 0)),
                  pl.BlockSpec((d, tf), lambda i, j: (0, j)),
                  pl.BlockSpec((d, tf), lambda i, j: (0, j)),
                  pl.BlockSpec((tf, d), lambda i, j: (j, 0))],
        out_specs=pl.BlockSpec((tm, d), lambda i, j: (i, 0)),
        out_shape=jax.ShapeDtypeStruct((t, d), F32),
        compiler_params=_cparams(("parallel", "arbitrary")), name="ffn_swiglu",
    )(x_bf, wg, wu, wd)


def _moe_ffn_body(te_ref, nt_ref, x_ref, wg_ref, wu_ref, wd_ref, o_ref):
    i = pl.program_id(0)
    j = pl.program_id(1)
    active = i < nt_ref[0]

    @pl.when(active)
    def _():
        y = _swiglu_tile(x_ref[...], wg_ref[0], wu_ref[0], wd_ref[0])

        @pl.when(j == 0)
        def _():
            o_ref[...] = y

        @pl.when(j > 0)
        def _():
            o_ref[...] += y

    @pl.when(jnp.logical_not(active) & (j == 0))
    def _():
        o_ref[...] = jnp.zeros(o_ref.shape, F32)


def _moe_ffn(tile_expert, n_tiles, x_sorted, wg, wu, wd, *, tm, tf):
    r, d = x_sorted.shape
    f = wg.shape[2]
    nj = f // tf

    def jj(i, j, nt):
        return jnp.where(i < nt[0], j, nj - 1)

    return pl.pallas_call(
        _moe_ffn_body,
        grid_spec=pltpu.PrefetchScalarGridSpec(
            num_scalar_prefetch=2, grid=(r // tm, nj),
            in_specs=[pl.BlockSpec((tm, d), lambda i, j, te, nt: (i, 0)),
                      pl.BlockSpec((1, d, tf), lambda i, j, te, nt: (te[i], 0, jj(i, j, nt))),
                      pl.BlockSpec((1, d, tf), lambda i, j, te, nt: (te[i], 0, jj(i, j, nt))),
                      pl.BlockSpec((1, tf, d), lambda i, j, te, nt: (te[i], jj(i, j, nt), 0))],
            out_specs=pl.BlockSpec((tm, d), lambda i, j, te, nt: (i, 0))),
        out_shape=jax.ShapeDtypeStruct((r, d), F32),
        compiler_params=_cparams(("arbitrary", "arbitrary")), name="moe_grouped_ffn",
    )(tile_expert, n_tiles, x_sorted, wg, wu, wd)


def _row_copy(src_ref, src_row, dst_ref, dst_row, sem):
    return pltpu.make_async_copy(src_ref.at[pl.ds(src_row, 1)], dst_ref.at[pl.ds(dst_row, 1)], sem)


def _gather_rows_body(idx_ref, src_ref, o_ref, buf_ref, sem, *, tm):
    base = pl.program_id(0) * tm

    half = tm // 2

    def start(r, c):
        _row_copy(src_ref, idx_ref[base + r], buf_ref, r, sem.at[0]).start()
        _row_copy(src_ref, idx_ref[base + half + r], buf_ref, half + r, sem.at[1]).start()
        return c

    def wait(r, c):
        _row_copy(src_ref, 0, buf_ref, r, sem.at[0]).wait()
        _row_copy(src_ref, 0, buf_ref, half + r, sem.at[1]).wait()
        return c

    lax.fori_loop(0, half, start, 0, unroll=8)
    lax.fori_loop(0, half, wait, 0, unroll=8)
    o_ref[...] = buf_ref[...].astype(o_ref.dtype)


def _gather_rows(idx, src, n_rows, out_dtype, *, tm):
    d = src.shape[1]
    return pl.pallas_call(
        functools.partial(_gather_rows_body, tm=tm),
        grid_spec=pltpu.PrefetchScalarGridSpec(
            num_scalar_prefetch=1, grid=(n_rows // tm,),
            in_specs=[pl.BlockSpec(memory_space=pl.ANY)],
            out_specs=pl.BlockSpec((tm, d), lambda i, idx: (i, 0)),
            scratch_shapes=[pltpu.VMEM((tm, d), src.dtype), pltpu.SemaphoreType.DMA((2,))]),
        out_shape=jax.ShapeDtypeStruct((n_rows, d), out_dtype),
        compiler_params=_cparams(("arbitrary",)), name="moe_gather_rows",
    )(idx, src)


def _combine_body(idx_ref, src_ref, w_ref, o_ref, a_ref, b_ref, sem, *, tm):
    base = pl.program_id(0) * tm

    def start(r, c):
        _row_copy(src_ref, idx_ref[2 * (base + r)], a_ref, r, sem.at[0]).start()
        _row_copy(src_ref, idx_ref[2 * (base + r) + 1], b_ref, r, sem.at[1]).start()
        return c

    def wait(r, c):
        _row_copy(src_ref, 0, a_ref, r, sem.at[0]).wait()
        _row_copy(src_ref, 0, b_ref, r, sem.at[1]).wait()
        return c

    lax.fori_loop(0, tm, start, 0, unroll=8)
    lax.fori_loop(0, tm, wait, 0, unroll=8)
    w = w_ref[...]
    o_ref[...] = a_ref[...] * w[:, 0:1] + b_ref[...] * w[:, 1:2]


def _combine_pairs(pos, y_sorted, pair_w, n_tokens, *, tm):
    d = y_sorted.shape[1]
    return pl.pallas_call(
        functools.partial(_combine_body, tm=tm),
        grid_spec=pltpu.PrefetchScalarGridSpec(
            num_scalar_prefetch=1, grid=(n_tokens // tm,),
            in_specs=[pl.BlockSpec(memory_space=pl.ANY),
                      pl.BlockSpec((tm, LANES), lambda i, idx: (i, 0))],
            out_specs=pl.BlockSpec((tm, d), lambda i, idx: (i, 0)),
            scratch_shapes=[pltpu.VMEM((tm, d), F32), pltpu.VMEM((tm, d), F32),
                            pltpu.SemaphoreType.DMA((2,))]),
        out_shape=jax.ShapeDtypeStruct((n_tokens, d), F32),
        compiler_params=_cparams(("arbitrary",)), name="moe_combine",
    )(pos, y_sorted, pair_w)


def _col_slices():
    out, off = {}, 0
    for name, width in IN_SPLITS:
        out[name] = (off, width)
        off += width
    return out


def _gather_cols(w, names, pad_to=None):
    cs = _col_slices()
    parts = []
    for n in names:
        col = w[:, cs[n][0]:cs[n][0] + cs[n][1]]
        parts.append(col * Q_FOLD[n] if n in Q_FOLD else col)
    width = sum(cs[n][1] for n in names)
    if pad_to is not None and pad_to > width:
        parts.append(jnp.zeros((w.shape[0], pad_to - width), w.dtype))
    return jnp.concatenate(parts, axis=1).astype(BF16)


def _selection_map_t(n_cmp_pad, n_sel):
    ci = np.arange(n_cmp_pad)[:, None] * NSA_CMP_D
    sj = np.arange(n_sel)[None, :] * NSA_SEL_L
    ov = np.clip(np.minimum(ci + NSA_CMP_L, sj + NSA_SEL_L) - np.maximum(ci, sj), 0, None)
    return np.ascontiguousarray((ov / NSA_CMP_D).astype(np.float32).T)


def _gate_expand_matrices():
    e = np.zeros((4, 3, LANES, LANES), np.float32)
    for j in range(4):
        for hh in range(2):
            for c in range(3):
                e[j, c, (2 * j + hh) * 3 + c, hh * HEAD_DIM:(hh + 1) * HEAD_DIM] = 1.0
    return e


def _split_bits(x, n):
    parts = []
    r = x
    for _ in range(n):
        hi = lax.bitcast_convert_type(
            lax.bitcast_convert_type(r, jnp.uint32) & jnp.uint32(0xFFFF0000), F32)
        parts.append(hi.astype(BF16))
        r = r - hi
    return parts


def _tile(n, pref):
    return pref if n % pref == 0 else n


def _vt_heads(x3, tkc):
    b, s, c = x3.shape
    nh, nc = c // HEAD_DIM, s // tkc
    v = x3.reshape(b, nc, tkc, nh, HEAD_DIM).transpose(0, 3, 1, 4, 2)
    ones = jnp.ones((b, nh, nc, 1, tkc), v.dtype)
    zeros = jnp.zeros((b, nh, nc, ATT_VROWS - HEAD_DIM - 1, tkc), v.dtype)
    return jnp.concatenate([v, ones, zeros], axis=3)


def _token_mixer(h, h_bf, layer, b, s, tabs, w_in, cmp_pe, cmp_w1, cmp_b1, cmp_w2, sinks, fox_bf,
                 diff_lambda, diff_gain, w_branch, w_out, ln_g, ln_b, w_router):
    t = b * s
    (cos64, sin64), (cos32, sin32) = tabs
    tm = _tile(t, 1024)
    r64 = _proj(h_bf, _gather_cols(w_in, SEG_ROPE64), BF16, tm, 512, rope=(cos64, sin64, HEAD_DIM // 2))
    r32 = _proj(h_bf, _gather_cols(w_in, SEG_ROPE32), BF16, tm, 512, rope=(cos32, sin32, DIFF_SUB // 2))
    plain = _proj(h_bf, _gather_cols(w_in, SEG_PLAIN), BF16, tm, 1280)
    gates = _proj(h_bf, _gather_cols(w_in, SEG_GATES), BF16, tm, 1024)
    small =_proj(h_bf, _gather_cols(w_in, SEG_SMALL, pad_to=LANES), F32, tm, LANES)
    r64_3, r32_3, plain_3, small_3 = (a.reshape(b, s, -1) for a in (r64, r32, plain, small))
    tkc = min(ATT_TKC, s // 2)
    tq = 2 * tkc

    ncp = s // NSA_CMP_D
    n_sel = s // NSA_SEL_L
    topn = min(NSA_TOPN, n_sel)

    def cmp_blocks(x2d):
        c = x2d.reshape(b, s, 2, HEAD_DIM).transpose(0, 2, 1, 3).reshape(b * 2, ncp, NSA_CMP_D * HEAD_DIM)
        nxt = jnp.concatenate([c[:, 1:], jnp.zeros_like(c[:, :1])], axis=1)
        return jnp.concatenate([c, nxt], axis=-1)

    xk = cmp_blocks(r64[:, 512:640])
    xv = cmp_blocks(plain[:, 1024:1152])
    cmp_kv = _nsa_compress(jnp.stack([xk, xv]), cmp_pe.reshape(2, 1, -1), cmp_w1.astype(BF16),
                           cmp_b1.reshape(2, 1, -1), cmp_w2.astype(BF16))
    kc = cmp_kv[0].astype(BF16).reshape(b, 2, ncp, HEAD_DIM)
    vc = cmp_kv[1].astype(BF16).reshape(b, 2, ncp, HEAD_DIM)
    kc4 = jnp.tile(kc, (1, 1, 1, 4))
    vct = vc.transpose(0, 1, 3, 2)
    selt = jnp.asarray(_selection_map_t(ncp, n_sel), BF16)
    o_cmp, mneg_t = _nsa_cmp_topk(r64_3, 0, kc4, vct, selt, tq=_tile(s, 256), topn=topn)
    vt_all = _vt_heads(plain_3[..., 1152:2560], tkc)
    o_win = _tattn(r64_3, 0, r64_3, 6, vt_all, 2, nmaps=2, qsel="gqa", vmap=(0, 0), mode="band",
                   window=NSA_WIN, fin="win", name="nsa_window_attention")
    e_mat = jnp.asarray(_gate_expand_matrices(), BF16)
    block_id = jax.nn.one_hot(jnp.arange(s) // NSA_SEL_L, HEAD_DIM, dtype=BF16)
    block_id = jnp.concatenate([block_id, block_id], axis=1)
    nsa_specs = [pl.BlockSpec((1, 1, tq, LANES), lambda bi, u, qi, *_: (bi, u // 2, qi, 0)),
                 pl.BlockSpec((s, LANES), lambda bi, u, qi, *_: (0, 0)),
                 pl.BlockSpec((1, 1, tq, LANES), lambda bi, u, qi, *_: (bi, u // 2, qi, u % 2)),
                 pl.BlockSpec((1, tq, LANES), lambda bi, u, qi, *_: (bi, qi, u)),
                 pl.BlockSpec((1, tq, LANES), lambda bi, u, qi, *_: (bi, qi, 0)),
                 pl.BlockSpec((1, 3, LANES, LANES), lambda bi, u, qi, *_: (u, 0, 0, 0))]
    o_a = _tattn(r64_3, 0, r64_3, 5, vt_all, 0, nmaps=2, qsel="gqa", vmap=(0, 0), mode="causal", fin="nsa",
                 bias="nsa", name="nsa_selected_attention", extras=(mneg_t, block_id, o_cmp, o_win, small_3, e_mat),
                 extra_specs=nsa_specs)

    o_b = _tattn(r64_3, 7, r64_3, 11, vt_all, 4, nmaps=2, qsel="gqa", vmap=(0, 0), mode="band",
                 window=SWA_WIN, fin="swa", name="swa_attention", scalars=sinks.astype(F32) * LOG2E)

    f_logit = (small[:, 24:32] + fox_bf[None, :]).reshape(b, s, 8).transpose(0, 2, 1)
    cum = _cum_log_forget(f_logit.reshape(b * 8, s // LANES, LANES)).reshape(b, 8, s)
    hi, mid, lo_piece = (piece[..., None] for piece in _split_bits(cum * LOG2E, 3))
    slot = (jnp.arange(LANES) % HEAD_DIM)[None, None, None, :]
    ck3 = jnp.where(slot == 0, hi, jnp.where(slot == 1, mid, jnp.where(slot == 2, lo_piece, jnp.zeros((), BF16))))
    o_c = _tattn(plain_3, 0, plain_3, 4, vt_all, 6, nmaps=2, qsel="pair", vmap=(0, 1), mode="causal",
                 fin="fox", bias="fox", name="fox_attention", extras=(ck3,),
                 extra_specs=[pl.BlockSpec((1, 2, s, LANES), lambda bi, u, qi, *_: (bi, u, 0, 0))])

    lam_init = 0.8 - 0.6 * math.exp(-0.3 * layer)
    lf = diff_lambda.astype(F32)
    lam = jnp.exp(jnp.sum(lf[0] * lf[1])) - jnp.exp(jnp.sum(lf[2] * lf[3])) + lam_init
    lam_arr = jnp.stack([lam, jnp.asarray(1.0 - lam_init, F32)]).astype(F32)
    gain_t = jnp.broadcast_to(jnp.tile(diff_gain.astype(F32), 2)[:, None], (LANES, tq))
    o_d = _tattn(r32_3, 0, r32_3, 4, vt_all, 14, nmaps=4, qsel="pair", vmap=(0, 0, 1, 1), mode="causal",
                 fin="diff", name="diff_attention", scalars=lam_arr, extras=(gain_t,),
                 extra_specs=[pl.BlockSpec((LANES, tq), lambda bi, u, qi, *_: (0, 0))])

    o_list = [o.reshape(t, BRANCH_W) for o in (o_a, o_b, o_c, o_d)]
    merged = _merge(o_list, gates, 0, w_branch.astype(BF16), tm=tm, tn=512)
    return _outproj_ln(merged, w_out.astype(BF16), h, ln_g.reshape(1, -1), ln_b.reshape(1, -1),
                       w_router, tm=_tile(t, 256))


def _moe_layer(h1, logits_pad, b_router, wg, wu, wd, *, tm, tf):
    t, d = h1.shape
    logits = logits_pad[:, :N_EXPERTS] + b_router.astype(F32)[None, :]
    top_v, top_i = lax.top_k(logits, TOP_K)
    top_w = jax.nn.softmax(top_v, axis=-1)
    flat_e = top_i.reshape(-1)
    onehot = jax.nn.one_hot(flat_e, N_EXPERTS, dtype=jnp.int32)
    rank = jnp.sum((jnp.cumsum(onehot, axis=0) - onehot) * onehot, axis=1)
    cnt = jnp.sum(onehot, axis=0)
    padded = ((cnt + tm - 1) // tm) * tm
    ends = jnp.cumsum(padded)
    starts = ends - padded
    pos = (starts[flat_e] + rank).astype(jnp.int32)
    n_rows = TOP_K * t + N_EXPERTS * tm
    row_token = jnp.zeros((n_rows,), jnp.int32).at[pos].set(jnp.arange(TOP_K * t, dtype=jnp.int32) // TOP_K)
    tile_start = jnp.arange(n_rows // tm, dtype=jnp.int32) * tm
    tile_expert = jnp.minimum(jnp.sum(tile_start[:, None] >= ends[None, :], axis=1), N_EXPERTS - 1)
    n_tiles = (ends[-1] // tm).astype(jnp.int32).reshape(1)
    x_sorted = _gather_rows(row_token, h1, n_rows, BF16, tm=512)
    y_sorted = _moe_ffn(tile_expert.astype(jnp.int32), n_tiles, x_sorted, wg, wu, wd, tm=tm, tf=tf)
    pair_w = jnp.pad(top_w.astype(F32), ((0, 0), (0, LANES - TOP_K)))
    return _combine_pairs(pos, y_sorted, pair_w, t, tm=_tile(t, 256))


def kernel(x, p, positions, w_in, nsa_cmp_pe, nsa_cmp_w1, nsa_cmp_b1, nsa_cmp_w2, swa_sinks, fox_bf,
           diff_lambda, diff_gain, w_branch, w_out, ln1_g, ln1_b, ffn_wg, ffn_wu, ffn_wd, moe_router,
           moe_router_b, moe_wg, moe_wu, moe_wd, ple_proj, ple_gate, ln2_g, ln2_b):
    b, s, d = x.shape
    t = b * s
    tabs = (_rope_tabs(positions, HEAD_DIM), _rope_tabs(positions, DIFF_SUB))
    h = x.reshape(t, d).astype(F32)
    h_bf = h.astype(BF16)
    for i in range(DEPTH):
        is_moe = i % 2 == 1
        w_router = None
        if is_moe:
            wr = jnp.zeros((d, LANES), F32).at[:, :N_EXPERTS].set(moe_router[i // 2].astype(F32))
            w_router = jnp.stack(_split_bits(wr, 2))
        res = _token_mixer(h, h_bf, i, b, s, tabs, w_in[i], nsa_cmp_pe[i], nsa_cmp_w1[i], nsa_cmp_b1[i],
                           nsa_cmp_w2[i], swa_sinks[i], fox_bf[i], diff_lambda[i], diff_gain[i],
                           w_branch[i], w_out[i], ln1_g[i], ln1_b[i], w_router)
        h1, h1_bf = res[0], res[1]
        if not is_moe:
            fpad = (-D_FF) % 512
            wg = jnp.pad(ffn_wg[i // 2].astype(BF16), ((0, 0), (0, fpad)))
            wu = jnp.pad(ffn_wu[i // 2].astype(BF16), ((0, 0), (0, fpad)))
            wd = jnp.pad(ffn_wd[i // 2].astype(BF16), ((0, fpad), (0, 0)))
            f = _ffn(h1_bf, wg, wu, wd, tm=_tile(t, 1024), tf=512)
        else:
            f = _moe_layer(h1, res[2], moe_router_b[i // 2], moe_wg[i // 2].astype(BF16),
                           moe_wu[i // 2].astype(BF16), moe_wd[i // 2].astype(BF16),
                           tm=_tile(t, 512), tf=1024)
        h, h_bf = _ple_ln(h1_bf, h1, f, p[i].reshape(t, PLE_DIM).astype(BF16), ple_gate[i].astype(BF16),
                          ple_proj[i].astype(BF16), ln2_g[i].reshape(1, -1), ln2_b[i].reshape(1, -1),
                          tm=_tile(t, 512))
    return h.reshape(b, s, d).astype(x.dtype)
```
